```python
import math
import jax, jax.numpy as jnp
from jax import lax
import numpy as np

D_MODEL = 1024
BATCH = 8
SEQ = 2048
DEPTH = 1

FOX_HEADS = 8
FOX_HEAD_DIM = 64
FOX_WIDTH = FOX_HEADS * FOX_HEAD_DIM
MLA_HEADS = 8
MLA_NOPE_DIM = 64
MLA_ROPE_DIM = 32
MLA_V_DIM = 64
MLA_Q_LORA = 768
MLA_KV_LORA = 256
MLA_WIDTH = MLA_HEADS * MLA_V_DIM
ROPE_THETA = 10000.0
D_FF = ((8 * D_MODEL + 3 * 256 - 1) // (3 * 256)) * 256
Q_BLOCK = 128
NORM_EPS = 1e-6

IN_WIDTHS = (
    FOX_WIDTH,
    FOX_WIDTH,
    FOX_WIDTH,
    FOX_HEADS,
    MLA_Q_LORA,
    MLA_KV_LORA,
    MLA_ROPE_DIM,
    D_MODEL,
    D_MODEL,
)
D_IN = sum(IN_WIDTHS)

kernel_name = "hybrid_fox_mla_sandwich_adaln_block"


def rmsnorm(x, g):
    xf = x.astype(jnp.float32)
    y = xf * lax.rsqrt(jnp.mean(xf * xf, axis=-1, keepdims=True) + NORM_EPS)
    return (y * g.astype(jnp.float32)).astype(x.dtype)


def rope(x, cos, sin):
    x1, x2 = jnp.split(x, 2, axis=-1)
    return jnp.concatenate([x1 * cos - x2 * sin, x2 * cos + x1 * sin], axis=-1).astype(x.dtype)


def blocked_causal_attention(q, k, v, scale, log_decay=None):
    B, H, S, dk = q.shape
    nb = S // Q_BLOCK
    q_blocks = q.reshape(B, H, nb, Q_BLOCK, dk).transpose(2, 0, 1, 3, 4)
    d_blocks = None if log_decay is None else log_decay.reshape(B, H, nb, Q_BLOCK).transpose(2, 0, 1, 3)
    key_pos = jnp.arange(S)

    def one_block(args):
        blk, q_blk, d_blk = args
        s = jnp.einsum("bhqd,bhkd->bhqk", q_blk, k, preferred_element_type=jnp.float32) * scale
        if d_blk is not None:
            s = s + (d_blk[..., :, None] - log_decay[:, :, None, :])
        query_pos = blk * Q_BLOCK + jnp.arange(Q_BLOCK)
        s = jnp.where(key_pos[None, :] <= query_pos[:, None], s, -jnp.inf)
        p = jax.nn.softmax(s, axis=-1)
        return jnp.einsum("bhqk,bhkd->bhqd", p.astype(v.dtype), v)

    out = lax.map(one_block, (jnp.arange(nb), q_blocks, d_blocks))
    return out.transpose(1, 2, 0, 3, 4).reshape(B, H, S, v.shape[-1])


def setup_inputs(seed: int = 0) -> dict:
    key = jax.random.key(seed)
    ks = jax.random.split(key, 24)
    f32 = jnp.float32

    def normal(k, shape, fan_in):
        return jax.random.normal(k, shape, f32) * (fan_in ** -0.5)

    def gain(k, shape):
        return 1.0 + 0.05 * jax.random.normal(k, shape, f32)

    x = jax.random.normal(ks[0], (BATCH, SEQ, D_MODEL), f32)
    c = jax.random.normal(ks[1], (BATCH, D_MODEL), f32)
    offsets = jax.random.randint(ks[2], (BATCH, 1), 0, 1024, dtype=jnp.int32)
    positions = (offsets + jnp.arange(SEQ, dtype=jnp.int32)[None, :]).astype(jnp.int32)

    return {
        "x": x,
        "c": c,
        "positions": positions,
        "w_ada": normal(ks[3], (DEPTH, D_MODEL, 6 * D_MODEL), D_MODEL),
        "b_ada": 0.02 * jax.random.normal(ks[4], (DEPTH, 6 * D_MODEL), f32),
        "g_pre_mix": gain(ks[5], (DEPTH, D_MODEL)),
        "g_post_mix": gain(ks[6], (DEPTH, D_MODEL)),
        "g_pre_ffn": gain(ks[7], (DEPTH, D_MODEL)),
        "g_post_ffn": gain(ks[8], (DEPTH, D_MODEL)),
        "w_in": normal(ks[9], (DEPTH, D_MODEL, D_IN), D_MODEL),
        "b_forget": 3.0 + 0.5 * jax.random.normal(ks[10], (DEPTH, FOX_HEADS), f32),
        "g_q_lora": gain(ks[11], (DEPTH, MLA_Q_LORA)),
        "w_uq": normal(ks[12], (DEPTH, MLA_Q_LORA, MLA_HEADS * (MLA_NOPE_DIM + MLA_ROPE_DIM)), MLA_Q_LORA),
        "g_kv_lora": gain(ks[13], (DEPTH, MLA_KV_LORA)),
        "w_ukv": normal(ks[14], (DEPTH, MLA_KV_LORA, MLA_HEADS * (MLA_NOPE_DIM + MLA_V_DIM)), MLA_KV_LORA),
        "w_proj_fox": normal(ks[15], (DEPTH, FOX_WIDTH, D_MODEL), FOX_WIDTH),
        "w_proj_mla": normal(ks[16], (DEPTH, MLA_WIDTH, D_MODEL), MLA_WIDTH),
        "w_out": normal(ks[17], (DEPTH, D_MODEL, D_MODEL), D_MODEL),
        "w_ffn_in": normal(ks[18], (DEPTH, D_MODEL, 2 * D_FF), D_MODEL),
        "w_ffn_out": normal(ks[19], (DEPTH, D_FF, D_MODEL), D_FF),
    }


def reference(x, c, positions, w_ada, b_ada, g_pre_mix, g_post_mix, g_pre_ffn, g_post_ffn,
              w_in, b_forget, g_q_lora, w_uq, g_kv_lora, w_ukv, w_proj_fox, w_proj_mla,
              w_out, w_ffn_in, w_ffn_out):
    B, S, D = x.shape
    inv_freq = 1.0 / (ROPE_THETA ** (jnp.arange(0, MLA_ROPE_DIM, 2, dtype=jnp.float32) / MLA_ROPE_DIM))
    angles = positions.astype(jnp.float32)[..., None] * inv_freq
    cos, sin = jnp.cos(angles), jnp.sin(angles)
    split_points = [int(v) for v in np.cumsum(IN_WIDTHS)[:-1]]
    silu_c = jax.nn.silu(c)

    for l in range(DEPTH):
        mod = (silu_c @ w_ada[l] + b_ada[l])[:, None, :]
        shift_mix, scale_mix, gate_mix, shift_ffn, scale_ffn, gate_ffn = jnp.split(mod, 6, axis=-1)

        h = rmsnorm(x, g_pre_mix[l]) * (1.0 + scale_mix) + shift_mix
        proj = h @ w_in[l]
        (fq, fk, fv, f_logit, cq, ckv, k_rope_in, gate_fox, gate_mla) = jnp.split(proj, split_points, axis=-1)

        q_a = fq.reshape(B, S, FOX_HEADS, FOX_HEAD_DIM).transpose(0, 2, 1, 3)
        k_a = fk.reshape(B, S, FOX_HEADS, FOX_HEAD_DIM).transpose(0, 2, 1, 3)
        v_a = fv.reshape(B, S, FOX_HEADS, FOX_HEAD_DIM).transpose(0, 2, 1, 3)
        log_f = jax.nn.log_sigmoid((f_logit + b_forget[l]).astype(jnp.float32))
        cum_log_f = jnp.cumsum(log_f, axis=1).transpose(0, 2, 1)
        o_a = blocked_causal_attention(q_a, k_a, v_a, 1.0 / math.sqrt(FOX_HEAD_DIM), cum_log_f)
        o_a = o_a.transpose(0, 2, 1, 3).reshape(B, S, FOX_WIDTH)

        q_b = (rmsnorm(cq, g_q_lora[l]) @ w_uq[l]).reshape(B, S, MLA_HEADS, MLA_NOPE_DIM + MLA_ROPE_DIM)
        q_nope, q_pe = jnp.split(q_b, [MLA_NOPE_DIM], axis=-1)
        q_pe = rope(q_pe, cos[:, :, None, :], sin[:, :, None, :])
        q_b = jnp.concatenate([q_nope, q_pe], axis=-1).transpose(0, 2, 1, 3)
        kv_b = (rmsnorm(ckv, g_kv_lora[l]) @ w_ukv[l]).reshape(B, S, MLA_HEADS, MLA_NOPE_DIM + MLA_V_DIM)
        k_nope, v_b = jnp.split(kv_b, [MLA_NOPE_DIM], axis=-1)
        k_pe = rope(k_rope_in, cos, sin)
        k_pe = jnp.broadcast_to(k_pe[:, :, None, :], (B, S, MLA_HEADS, MLA_ROPE_DIM))
        k_b = jnp.concatenate([k_nope, k_pe], axis=-1).transpose(0, 2, 1, 3)
        v_b = v_b.transpose(0, 2, 1, 3)
        o_b = blocked_causal_attention(q_b, k_b, v_b, 1.0 / math.sqrt(MLA_NOPE_DIM + MLA_ROPE_DIM))
        o_b = o_b.transpose(0, 2, 1, 3).reshape(B, S, MLA_WIDTH)

        merged = (jax.nn.sigmoid(gate_fox) * (o_a @ w_proj_fox[l])
                  + jax.nn.sigmoid(gate_mla) * (o_b @ w_proj_mla[l]))
        y = merged @ w_out[l]
        x = x + gate_mix * rmsnorm(y, g_post_mix[l])

        h = rmsnorm(x, g_pre_ffn[l]) * (1.0 + scale_ffn) + shift_ffn
        g, u = jnp.split(h @ w_ffn_in[l], 2, axis=-1)
        y = (jax.nn.silu(g) * u) @ w_ffn_out[l]
        x = x + gate_ffn * rmsnorm(y, g_post_ffn[l])

    return x
```

```python
import functools
import math

import jax
import jax.numpy as jnp
import numpy as np
from jax import lax
from jax.experimental import pallas as pl
from jax.experimental.pallas import tpu as pltpu

D_MODEL = 1024
HEADS = 8
HEAD_DIM = 64
FOX_WIDTH = HEADS * HEAD_DIM
MLA_NOPE = 64
MLA_ROPE = 32
MLA_V = 64
MLA_Q_LORA = 768
MLA_KV_LORA = 256
D_FF = 2816
ROPE_THETA = 10000.0
NORM_EPS = 1e-6
IN_WIDTHS = (FOX_WIDTH, FOX_WIDTH, FOX_WIDTH, HEADS, MLA_Q_LORA, MLA_KV_LORA, MLA_ROPE, D_MODEL, D_MODEL)

LANES = 128
TOKEN_TILE = 512
ATTN_TILE = 256
FFN_CHUNK = 512
VMEM_LIMIT = 56 * 1024 * 1024

_C_QKV = 0
_C_MISC = 3 * FOX_WIDTH
_C_CQ = _C_MISC + 2 * LANES
_C_CKV = _C_CQ + MLA_Q_LORA
_C_GATE = _C_CKV + MLA_KV_LORA
_C_END = _C_GATE + 2 * D_MODEL

F32 = jnp.float32
BF16 = jnp.bfloat16


def _const_spec(shape):
    zeros = (0,) * len(shape)
    return pl.BlockSpec(shape, lambda *_: zeros, pipeline_mode=pl.Buffered(1))


def _rms(x):
    return x * lax.rsqrt(jnp.mean(x * x, axis=-1, keepdims=True) + NORM_EPS)


def _dot(a, b):
    return jnp.dot(a, b, preferred_element_type=F32)


def _adaln_kernel(c_ref, w_ref, b_ref, o_ref):
    c = c_ref[...]
    sc = c * jax.nn.sigmoid(c)
    o_ref[...] = jnp.dot(sc, w_ref[...], preferred_element_type=F32,
                         precision=lax.Precision.HIGHEST) + b_ref[...]


def _adaln_mod(c, w_ada, b_ada):
    bsz, d = c.shape
    n = w_ada.shape[1]
    tn = 1024
    return pl.pallas_call(
        _adaln_kernel,
        grid=(n // tn,),
        in_specs=[pl.BlockSpec((bsz, d), lambda j: (0, 0)),
                  pl.BlockSpec((d, tn), lambda j: (0, j)),
                  pl.BlockSpec((1, tn), lambda j: (0, j))],
        out_specs=pl.BlockSpec((bsz, tn), lambda j: (0, j)),
        out_shape=jax.ShapeDtypeStruct((bsz, n), F32),
        name="adaln_mod",
    )(c, w_ada, b_ada.reshape(1, n))


def _rope_kernel(pos_ref, invf_ref, cos_ref, sin_ref):
    ang = pos_ref[...] * invf_ref[...]
    cos_ref[...] = jnp.cos(ang)
    sin_ref[...] = jnp.sin(ang)


def _rope_tables(positions):
    bsz, seq = positions.shape
    half = MLA_ROPE // 2
    inv_freq = 1.0 / (ROPE_THETA ** (np.arange(0, MLA_ROPE, 2, dtype=np.float32) / MLA_ROPE))
    rows = seq * half // LANES
    pos_rep = jnp.repeat(positions.astype(F32), half, axis=1).reshape(bsz, rows, LANES)
    invf = jnp.asarray(np.tile(inv_freq.astype(np.float32), LANES // half)).reshape(1, 1, LANES)
    spec = pl.BlockSpec((1, rows, LANES), lambda b: (b, 0, 0))
    cos, sin = pl.pallas_call(
        _rope_kernel,
        grid=(bsz,),
        in_specs=[spec, pl.BlockSpec((1, 1, LANES), lambda b: (0, 0, 0))],
        out_specs=[spec, spec],
        out_shape=[jax.ShapeDtypeStruct((bsz, rows, LANES), F32)] * 2,
        name="rope_tables",
    )(pos_rep, invf)
    cos = cos.reshape(bsz, seq, half)
    sin = sin.reshape(bsz, seq, half)
    ones = jnp.ones((bsz, seq, MLA_NOPE), F32)
    zq = jnp.zeros((bsz, seq, MLA_NOPE), F32)
    zpad = jnp.zeros((bsz, seq, LANES - MLA_NOPE - MLA_ROPE), F32)
    cfull = jnp.concatenate([ones, cos, cos, zpad], axis=-1)
    sfull = jnp.concatenate([zq, sin, sin, zpad], axis=-1)
    return cfull, sfull


def _token_proj_kernel(x_ref, mod_ref, gpre_ref, bf_ref, gq_ref, gkv_ref, cos_ref, sin_ref,
                       wmain_ref, wq_ref, wkv_ref, sel_ref, tri_ref,
                       qf_ref, kf_ref, vf_ref, qm_ref, km_ref, vm_ref, gf_ref, gm_ref,
                       carry_ref):
    tm = x_ref.shape[1]

    @pl.when(pl.program_id(1) == 0)
    def _():
        carry_ref[...] = jnp.zeros_like(carry_ref)

    x = x_ref[0]
    shift = mod_ref[0, 0:1, :]
    scale = mod_ref[0, 1:2, :]
    h = (_rms(x) * gpre_ref[...] * (1.0 + scale) + shift).astype(BF16)

    for gi, ref in enumerate((gf_ref, gm_ref)):
        lo = _C_GATE + gi * D_MODEL
        g = _dot(h, wmain_ref[:, lo:lo + D_MODEL])
        ref[0] = jax.nn.sigmoid(g).astype(ref.dtype)

    lane = lax.broadcasted_iota(jnp.int32, (tm, LANES), 1)
    cfull = cos_ref[0]
    sfull = sin_ref[0]

    p_qkv = _dot(h, wmain_ref[:, _C_QKV:_C_QKV + 3 * FOX_WIDTH])
    vf_ref[0] = p_qkv[:, 2 * FOX_WIDTH:].astype(vf_ref.dtype)
    misc = _dot(h, wmain_ref[:, _C_MISC:_C_MISC + 2 * LANES])
    pa = misc[:, :LANES]
    pb = misc[:, LANES:]

    logit = pa + bf_ref[...]
    logf = jnp.minimum(logit, 0.0) - jnp.log(1.0 + jnp.exp(-jnp.abs(logit)))
    tri = tri_ref[...]
    l_hi = logf.astype(BF16)
    r1 = logf - l_hi.astype(F32)
    l_mid = r1.astype(BF16)
    l_lo = (r1 - l_mid.astype(F32)).astype(BF16)
    cum = _dot(tri, l_hi) + _dot(tri, l_mid) + _dot(tri, l_lo) + carry_ref[0:1, :]
    carry_ref[0:1, :] = cum[tm - 1:tm, :]

    a = cum * math.sqrt(HEAD_DIM)
    a_hi = a.astype(BF16).astype(F32)
    r1 = a - a_hi
    a_mid = r1.astype(BF16).astype(F32)
    a_lo = (r1 - a_mid).astype(BF16).astype(F32)
    z = jnp.where(lane < 8, a_hi,
                  jnp.where(lane < 16, a_mid,
                            jnp.where(lane < 24, a_lo,
                                      jnp.where(lane == 24, 1.0, 0.0))))
    aug = _dot(z.astype(BF16), sel_ref[...])
    for hh in range(HEADS):
        pair = hh // 2
        keep = (lane < HEAD_DIM) if hh % 2 == 0 else (lane >= HEAD_DIM)
        xq = p_qkv[:, pair * LANES:(pair + 1) * LANES]
        xk = p_qkv[:, FOX_WIDTH + pair * LANES:FOX_WIDTH + (pair + 1) * LANES]
        aq = aug[:, hh * LANES:(hh + 1) * LANES]
        ak = aug[:, (HEADS + hh) * LANES:(HEADS + hh + 1) * LANES]
        qf_ref[0, hh] = jnp.where(keep, xq, aq).astype(qf_ref.dtype)
        kf_ref[0, hh] = jnp.where(keep, xk, ak).astype(kf_ref.dtype)

    cq = _dot(h, wmain_ref[:, _C_CQ:_C_CQ + MLA_Q_LORA])
    nq = (_rms(cq) * gq_ref[...]).astype(BF16)
    qq = _dot(nq, wq_ref[...])
    for hh in range(HEADS):
        qa = qq[:, hh * LANES:(hh + 1) * LANES]
        qb = qq[:, (HEADS + hh) * LANES:(HEADS + hh + 1) * LANES]
        qm_ref[0, hh] = (qa * cfull + qb * sfull).astype(qm_ref.dtype)

    ckv = _dot(h, wmain_ref[:, _C_CKV:_C_CKV + MLA_KV_LORA])
    nkv = (_rms(ckv) * gkv_ref[...]).astype(BF16)
    kv = _dot(nkv, wkv_ref[...])
    vm_ref[0] = kv[:, HEADS * LANES:].astype(vm_ref.dtype)
    in_rope = (lane >= MLA_NOPE) & (lane < MLA_NOPE + MLA_ROPE)
    kpe = jnp.where(in_rope, pa * cfull + pb * sfull, 0.0)
    for hh in range(HEADS):
        km_ref[0, hh] = (kv[:, hh * LANES:(hh + 1) * LANES] + kpe).astype(km_ref.dtype)


def _rot_half_cols(w):
    half = w.shape[1] // 2
    return jnp.concatenate([-w[:, half:], w[:, :half]], axis=1)


def _prep_proj_weights(w_in, w_uq, w_ukv, b_forget):
    d = w_in.shape[0]
    sp = np.cumsum(IN_WIDTHS)[:-1]
    w_fq, w_fk, w_fv, w_f, w_cq, w_ckv, w_kr, w_gf, w_gm = jnp.split(w_in, [int(v) for v in sp], axis=1)
    zeros = lambda n: jnp.zeros((d, n), w_in.dtype)
    misc_a = jnp.concatenate([w_f, w_f, w_f, zeros(MLA_NOPE - 3 * HEADS), w_kr,
                              zeros(LANES - MLA_NOPE - MLA_ROPE)], axis=1)
    misc_b = jnp.concatenate([zeros(MLA_NOPE), _rot_half_cols(w_kr),
                              zeros(LANES - MLA_NOPE - MLA_ROPE)], axis=1)
    w_main = jnp.concatenate([w_fq, w_fk, w_fv, misc_a, misc_b, w_cq, w_ckv, w_gf, w_gm], axis=1).astype(BF16)

    r = w_uq.shape[0]
    uq = w_uq.reshape(r, HEADS, MLA_NOPE + MLA_ROPE)
    nope, rope = uq[:, :, :MLA_NOPE], uq[:, :, MLA_NOPE:]
    half = MLA_ROPE // 2
    rope_rot = jnp.concatenate([-rope[:, :, half:], rope[:, :, :half]], axis=2)
    pad = jnp.zeros((r, HEADS, LANES - MLA_NOPE - MLA_ROPE), w_uq.dtype)
    w_a = jnp.concatenate([nope, rope, pad], axis=2).reshape(r, HEADS * LANES)
    w_b = jnp.concatenate([jnp.zeros_like(nope), rope_rot, pad], axis=2).reshape(r, HEADS * LANES)
    w_q = jnp.concatenate([w_a, w_b], axis=1).astype(BF16)

    rk = w_ukv.shape[0]
    ukv = w_ukv.reshape(rk, HEADS, MLA_NOPE + MLA_V)
    k_nope, v = ukv[:, :, :MLA_NOPE], ukv[:, :, MLA_NOPE:]
    w_kn = jnp.concatenate([k_nope, jnp.zeros((rk, HEADS, LANES - MLA_NOPE), w_ukv.dtype)], axis=2)
    w_kv = jnp.concatenate([w_kn.reshape(rk, HEADS * LANES), v.reshape(rk, HEADS * MLA_V)], axis=1).astype(BF16)

    bf3 = jnp.concatenate([b_forget, b_forget, b_forget,
                           jnp.zeros((LANES - 3 * HEADS,), b_forget.dtype)]).reshape(1, LANES)
    return w_main, w_q, w_kv, bf3


def _decay_selector():
    sel = np.zeros((LANES, 2 * HEADS * LANES), np.float32)
    for hh in range(HEADS):
        base_q = hh * LANES + (HEAD_DIM if hh % 2 == 0 else 0)
        base_k = (HEADS + hh) * LANES + (HEAD_DIM if hh % 2 == 0 else 0)
        for piece in range(3):
            sel[piece * HEADS + hh, base_q + piece] = 1.0
            sel[3 * HEADS, base_q + 3 + piece] = 1.0
            sel[3 * HEADS, base_k + piece] = 1.0
            sel[piece * HEADS + hh, base_k + 3 + piece] = -1.0
    return jnp.asarray(sel, BF16)


def _token_proj(x, mod, g_pre, cfull, sfull, w_main, w_q, w_kv, bf3, g_q, g_kv):
    bsz, seq, d = x.shape
    tm = TOKEN_TILE
    sel = _decay_selector()
    tri = jnp.asarray(np.tril(np.ones((tm, tm), np.float32)), BF16)
    row = lambda b, i: (b, i, 0)
    head = lambda b, i: (b, 0, i, 0)
    head_shape = jax.ShapeDtypeStruct((bsz, HEADS, seq, LANES), BF16)
    head_spec = pl.BlockSpec((1, HEADS, tm, LANES), head)
    v_shape = jax.ShapeDtypeStruct((bsz, seq, FOX_WIDTH), BF16)
    v_spec = pl.BlockSpec((1, tm, FOX_WIDTH), row)
    g_shape = jax.ShapeDtypeStruct((bsz, seq, d), BF16)
    g_spec = pl.BlockSpec((1, tm, d), row)
    return pl.pallas_call(
        _token_proj_kernel,
        grid=(bsz, seq // tm),
        in_specs=[pl.BlockSpec((1, tm, d), row),
                  pl.BlockSpec((1, 6, d), lambda b, i: (b, 0, 0)),
                  _const_spec((1, d)), _const_spec((1, LANES)),
                  _const_spec((1, MLA_Q_LORA)), _const_spec((1, MLA_KV_LORA)),
                  pl.BlockSpec((1, tm, LANES), row), pl.BlockSpec((1, tm, LANES), row),
                  _const_spec(w_main.shape), _const_spec(w_q.shape), _const_spec(w_kv.shape),
                  _const_spec(sel.shape), _const_spec(tri.shape)],
        out_specs=[head_spec, head_spec, v_spec, head_spec, head_spec, v_spec, g_spec, g_spec],
        out_shape=[head_shape, head_shape, v_shape, head_shape, head_shape, v_shape, g_shape, g_shape],
        scratch_shapes=[pltpu.VMEM((8, LANES), F32)],
        compiler_params=pltpu.CompilerParams(dimension_semantics=("arbitrary", "arbitrary"),
                                             vmem_limit_bytes=VMEM_LIMIT),
        name="token_proj",
    )(x, mod, g_pre.reshape(1, d), bf3, g_q.reshape(1, -1), g_kv.reshape(1, -1), cfull, sfull,
      w_main, w_q, w_kv, sel, tri)


def _attn_kernel(q_ref, k_ref, v_ref, o_ref, *, exp2_scale, tile):
    seq = v_ref.shape[1]
    nq = seq // tile
    lane = lax.broadcasted_iota(jnp.int32, (tile, LANES), 1)
    rows = lax.broadcasted_iota(jnp.int32, (tile, tile), 0)
    cols = lax.broadcasted_iota(jnp.int32, (tile, tile), 1)
    causal = cols <= rows
    nt = (((1,), (1,)), ((), ()))

    def scores(q, k):
        return lax.dot_general(q, k, nt, preferred_element_type=F32)

    def q_block(qi, _):
        q0 = pl.multiple_of(qi * tile, tile)
        qs = [q_ref[0, hh, pl.ds(q0, tile), :] for hh in range(2)]
        vd = v_ref[0, pl.ds(q0, tile), :]
        state = []
        for hh in range(2):
            s = jnp.where(causal, scores(qs[hh], k_ref[0, hh, pl.ds(q0, tile), :]), -jnp.inf)
            m = jnp.max(s, axis=1, keepdims=True)
            p = jnp.exp2((s - m) * exp2_scale)
            l = jnp.sum(p, axis=1, keepdims=True)
            acc = _dot(p.astype(BF16), vd)
            state += [m, l, acc]

        def kv_block(j, st):
            k0 = pl.multiple_of(j * tile, tile)
            vj = v_ref[0, pl.ds(k0, tile), :]
            out = []
            for hh in range(2):
                m, l, acc = st[3 * hh:3 * hh + 3]
                s = scores(qs[hh], k_ref[0, hh, pl.ds(k0, tile), :])
                m_new = jnp.maximum(m, jnp.max(s, axis=1, keepdims=True))
                alpha = jnp.exp2((m - m_new) * exp2_scale)
                p = jnp.exp2((s - m_new) * exp2_scale)
                l = alpha * l + jnp.sum(p, axis=1, keepdims=True)
                acc = alpha * acc + _dot(p.astype(BF16), vj)
                out += [m_new, l, acc]
            return tuple(out)

        st = lax.fori_loop(0, qi, kv_block, tuple(state))
        o0 = st[2] * (1.0 / st[1])
        o1 = st[5] * (1.0 / st[4])
        o_ref[0, pl.ds(q0, tile), :] = jnp.where(lane < HEAD_DIM, o0, o1).astype(o_ref.dtype)
        return 0

    lax.fori_loop(0, nq, q_block, 0)


def _causal_attn(q, k, v, softmax_scale):
    bsz, heads, seq, _ = q.shape
    qk_spec = pl.BlockSpec((1, 2, seq, LANES), lambda b, p: (b, p, 0, 0))
    v_spec = pl.BlockSpec((1, seq, LANES), lambda b, p: (b, 0, p))
    kern = functools.partial(_attn_kernel, exp2_scale=softmax_scale * math.log2(math.e), tile=ATTN_TILE)
    return pl.pallas_call(
        kern,
        grid=(bsz, heads // 2),
        in_specs=[qk_spec, qk_spec, v_spec],
        out_specs=v_spec,
        out_shape=jax.ShapeDtypeStruct(v.shape, BF16),
        compiler_params=pltpu.CompilerParams(dimension_semantics=("arbitrary", "arbitrary"),
                                             vmem_limit_bytes=VMEM_LIMIT),
        name="causal_attn",
    )(q, k, v)


def _merge_kernel(x_ref, oa_ref, ob_ref, gf_ref, gm_ref, mod_ref, gpost_ref,
                  wpf_ref, wpm_ref, wout_ref, o_ref):
    pa = _dot(oa_ref[0], wpf_ref[...])
    pb = _dot(ob_ref[0], wpm_ref[...])
    merged = gf_ref[0].astype(F32) * pa + gm_ref[0].astype(F32) * pb
    y = _dot(merged.astype(BF16), wout_ref[...])
    gate = mod_ref[0, 2:3, :]
    o_ref[0] = x_ref[0] + gate * (_rms(y) * gpost_ref[...])


def _merge_out(x, o_a, o_b, gf, gm, mod, g_post, w_pf, w_pm, w_out):
    bsz, seq, d = x.shape
    tm = TOKEN_TILE
    row = lambda b, i: (b, i, 0)
    return pl.pallas_call(
        _merge_kernel,
        grid=(bsz, seq // tm),
        in_specs=[pl.BlockSpec((1, tm, d), row),
                  pl.BlockSpec((1, tm, FOX_WIDTH), row), pl.BlockSpec((1, tm, FOX_WIDTH), row),
                  pl.BlockSpec((1, tm, d), row), pl.BlockSpec((1, tm, d), row),
                  pl.BlockSpec((1, 6, d), lambda b, i: (b, 0, 0)),
                  _const_spec((1, d)),
                  _const_spec(w_pf.shape), _const_spec(w_pm.shape), _const_spec(w_out.shape)],
        out_specs=pl.BlockSpec((1, tm, d), row),
        out_shape=jax.ShapeDtypeStruct(x.shape, F32),
        compiler_params=pltpu.CompilerParams(dimension_semantics=("arbitrary", "arbitrary"),
                                             vmem_limit_bytes=VMEM_LIMIT),
        name="merge_out",
    )(x, o_a, o_b, gf, gm, mod, g_post.reshape(1, d), w_pf, w_pm, w_out)


def _ffn_kernel(x_ref, mod_ref, gpre_ref, gpost_ref, win_ref, wout_ref, o_ref):
    x = x_ref[0]
    shift = mod_ref[0, 3:4, :]
    scale = mod_ref[0, 4:5, :]
    gate = mod_ref[0, 5:6, :]
    h = (_rms(x) * gpre_ref[...] * (1.0 + scale) + shift).astype(BF16)
    y = None
    for lo in range(0, D_FF, FFN_CHUNK):
        w = min(FFN_CHUNK, D_FF - lo)
        g = _dot(h, win_ref[:, lo:lo + w])
        u = _dot(h, win_ref[:, D_FF + lo:D_FF + lo + w])
        act = (g * jax.nn.sigmoid(g) * u).astype(BF16)
        part = _dot(act, wout_ref[lo:lo + w, :])
        y = part if y is None else y + part
    o_ref[0] = x + gate * (_rms(y) * gpost_ref[...])


def _ffn(x, mod, g_pre, g_post, w_in, w_out):
    bsz, seq, d = x.shape
    tm = TOKEN_TILE
    row = lambda b, i: (b, i, 0)
    return pl.pallas_call(
        _ffn_kernel,
        grid=(bsz, seq // tm),
        in_specs=[pl.BlockSpec((1, tm, d), row),
                  pl.BlockSpec((1, 6, d), lambda b, i: (b, 0, 0)),
                  _const_spec((1, d)), _const_spec((1, d)),
                  _const_spec(w_in.shape), _const_spec(w_out.shape)],
        out_specs=pl.BlockSpec((1, tm, d), row),
        out_shape=jax.ShapeDtypeStruct(x.shape, F32),
        compiler_params=pltpu.CompilerParams(dimension_semantics=("arbitrary", "arbitrary"),
                                             vmem_limit_bytes=VMEM_LIMIT),
        name="ffn",
    )(x, mod, g_pre.reshape(1, d), g_post.reshape(1, d), w_in, w_out)


def kernel(x, c, positions, w_ada, b_ada, g_pre_mix, g_post_mix, g_pre_ffn, g_post_ffn, w_in, b_forget,
           g_q_lora, w_uq, g_kv_lora, w_ukv, w_proj_fox, w_proj_mla, w_out, w_ffn_in, w_ffn_out):
    bsz, seq, d = x.shape
    depth = w_ada.shape[0]
    cfull, sfull = _rope_tables(positions)
    for l in range(depth):
        mod = _adaln_mod(c, w_ada[l], b_ada[l]).reshape(bsz, 6, d)
        w_main, w_q, w_kv, bf3 = _prep_proj_weights(w_in[l], w_uq[l], w_ukv[l], b_forget[l])
        qf, kf, vf, qm, km, vm, gf, gm = _token_proj(
            x, mod, g_pre_mix[l], cfull, sfull, w_main, w_q, w_kv, bf3, g_q_lora[l], g_kv_lora[l])
        o_a = _causal_attn(qf, kf, vf, 1.0 / math.sqrt(HEAD_DIM))
        o_b = _causal_attn(qm, km, vm, 1.0 / math.sqrt(MLA_NOPE + MLA_ROPE))
        x = _merge_out(x, o_a, o_b, gf, gm, mod, g_post_mix[l],
                       w_proj_fox[l].astype(BF16), w_proj_mla[l].astype(BF16), w_out[l].astype(BF16))
        x = _ffn(x, mod, g_pre_ffn[l], g_post_ffn[l], w_ffn_in[l].astype(BF16), w_ffn_out[l].astype(BF16))
    return x
```

```python
import functools
import math

import jax
import jax.numpy as jnp
import numpy as np
from jax import lax
from jax.experimental import pallas as pl
from jax.experimental.pallas import tpu as pltpu

D_MODEL = 1024
HEADS = 8
HEAD_DIM = 64
FOX_WIDTH = HEADS * HEAD_DIM
MLA_NOPE = 64
MLA_ROPE = 32
MLA_V = 64
MLA_Q_LORA = 768
MLA_KV_LORA = 256
D_FF = 2816
ROPE_THETA = 10000.0
NORM_EPS = 1e-6
IN_WIDTHS = (FOX_WIDTH, FOX_WIDTH, FOX_WIDTH, HEADS, MLA_Q_LORA, MLA_KV_LORA, MLA_ROPE, D_MODEL, D_MODEL)

LANES = 128
TOKEN_TILE = 512
ATTN_TILE = 512
ATTN_CHUNK = 32
FFN_CHUNK = 512
VMEM_LIMIT = 56 * 1024 * 1024

_C_QKV = 0
_C_MISC = 3 * FOX_WIDTH
_C_CQ = _C_MISC + 2 * LANES
_C_CKV = _C_CQ + MLA_Q_LORA
_C_GATE = _C_CKV + MLA_KV_LORA
_C_END = _C_GATE + 2 * D_MODEL

F32 = jnp.float32
BF16 = jnp.bfloat16


def _const_spec(shape):
    zeros = (0,) * len(shape)
    return pl.BlockSpec(shape, lambda *_: zeros, pipeline_mode=pl.Buffered(1))


def _rms(x):
    return x * lax.rsqrt(jnp.mean(x * x, axis=-1, keepdims=True) + NORM_EPS)


def _dot(a, b):
    return jnp.dot(a, b, preferred_element_type=F32)


def _adaln_kernel(c_ref, w_ref, b_ref, o_ref):
    c = c_ref[...]
    sc = c * jax.nn.sigmoid(c)
    o_ref[...] = jnp.dot(sc, w_ref[...], preferred_element_type=F32,
                         precision=lax.Precision.HIGHEST) + b_ref[...]


def _adaln_mod(c, w_ada, b_ada):
    bsz, d = c.shape
    n = w_ada.shape[1]
    tn = 1024
    return pl.pallas_call(
        _adaln_kernel,
        grid=(n // tn,),
        in_specs=[pl.BlockSpec((bsz, d), lambda j: (0, 0)),
                  pl.BlockSpec((d, tn), lambda j: (0, j)),
                  pl.BlockSpec((1, tn), lambda j: (0, j))],
        out_specs=pl.BlockSpec((bsz, tn), lambda j: (0, j)),
        out_shape=jax.ShapeDtypeStruct((bsz, n), F32),
        name="adaln_mod",
    )(c, w_ada, b_ada.reshape(1, n))


def _rope_kernel(pos_ref, invf_ref, cos_ref, sin_ref):
    ang = pos_ref[...] * invf_ref[...]
    cos_ref[...] = jnp.cos(ang)
    sin_ref[...] = jnp.sin(ang)


def _rope_tables(positions):
    bsz, seq = positions.shape
    half = MLA_ROPE // 2
    inv_freq = 1.0 / (ROPE_THETA ** (np.arange(0, MLA_ROPE, 2, dtype=np.float32) / MLA_ROPE))
    rows = seq * half // LANES
    pos_rep = jnp.repeat(positions.astype(F32), half, axis=1).reshape(bsz, rows, LANES)
    invf = jnp.asarray(np.tile(inv_freq.astype(np.float32), LANES // half)).reshape(1, 1, LANES)
    spec = pl.BlockSpec((1, rows, LANES), lambda b: (b, 0, 0))
    cos, sin = pl.pallas_call(
        _rope_kernel,
        grid=(bsz,),
        in_specs=[spec, pl.BlockSpec((1, 1, LANES), lambda b: (0, 0, 0))],
        out_specs=[spec, spec],
        out_shape=[jax.ShapeDtypeStruct((bsz, rows, LANES), F32)] * 2,
        name="rope_tables",
    )(pos_rep, invf)
    cos = cos.reshape(bsz, seq, half)
    sin = sin.reshape(bsz, seq, half)
    ones = jnp.ones((bsz, seq, MLA_NOPE), F32)
    zq = jnp.zeros((bsz, seq, MLA_NOPE), F32)
    zpad = jnp.zeros((bsz, seq, LANES - MLA_NOPE - MLA_ROPE), F32)
    cfull = jnp.concatenate([ones, cos, cos, zpad], axis=-1)
    sfull = jnp.concatenate([zq, sin, sin, zpad], axis=-1)
    return cfull, sfull


def _token_proj_kernel(x_ref, mod_ref, gpre_ref, bf_ref, gq_ref, gkv_ref, cos_ref, sin_ref,
                       wmain_ref, wq_ref, wkv_ref, sel_ref, tri_ref,
                       qf_ref, kf_ref, vf_ref, qm_ref, km_ref, vm_ref, gf_ref, gm_ref,
                       carry_ref):
    tm = x_ref.shape[1]

    @pl.when(pl.program_id(1) == 0)
    def _():
        carry_ref[...] = jnp.zeros_like(carry_ref)

    x = x_ref[0]
    shift = mod_ref[0, 0:1, :]
    scale = mod_ref[0, 1:2, :]
    h = (_rms(x) * gpre_ref[...] * (1.0 + scale) + shift).astype(BF16)

    for gi, ref in enumerate((gf_ref, gm_ref)):
        lo = _C_GATE + gi * D_MODEL
        g = _dot(h, wmain_ref[:, lo:lo + D_MODEL])
        ref[0] = jax.nn.sigmoid(g).astype(ref.dtype)

    lane = lax.broadcasted_iota(jnp.int32, (tm, LANES), 1)
    cfull = cos_ref[0]
    sfull = sin_ref[0]

    p_qkv = _dot(h, wmain_ref[:, _C_QKV:_C_QKV + 3 * FOX_WIDTH])
    vf_ref[0] = p_qkv[:, 2 * FOX_WIDTH:].astype(vf_ref.dtype)
    misc = _dot(h, wmain_ref[:, _C_MISC:_C_MISC + 2 * LANES])
    pa = misc[:, :LANES]
    pb = misc[:, LANES:]

    logit = pa + bf_ref[...]
    logf = jnp.minimum(logit, 0.0) - jnp.log(1.0 + jnp.exp(-jnp.abs(logit)))
    tri = tri_ref[...]
    l_hi = logf.astype(BF16)
    r1 = logf - l_hi.astype(F32)
    l_mid = r1.astype(BF16)
    l_lo = (r1 - l_mid.astype(F32)).astype(BF16)
    cum = _dot(tri, l_hi) + _dot(tri, l_mid) + _dot(tri, l_lo) + carry_ref[0:1, :]
    carry_ref[0:1, :] = cum[tm - 1:tm, :]

    a = cum * math.sqrt(HEAD_DIM)
    a_hi = a.astype(BF16).astype(F32)
    r1 = a - a_hi
    a_mid = r1.astype(BF16).astype(F32)
    a_lo = (r1 - a_mid).astype(BF16).astype(F32)
    z = jnp.where(lane < 8, a_hi,
                  jnp.where(lane < 16, a_mid,
                            jnp.where(lane < 24, a_lo,
                                      jnp.where(lane == 24, 1.0, 0.0))))
    aug = _dot(z.astype(BF16), sel_ref[...])
    for hh in range(HEADS):
        pair = hh // 2
        keep = (lane < HEAD_DIM) if hh % 2 == 0 else (lane >= HEAD_DIM)
        xq = p_qkv[:, pair * LANES:(pair + 1) * LANES]
        xk = p_qkv[:, FOX_WIDTH + pair * LANES:FOX_WIDTH + (pair + 1) * LANES]
        aq = aug[:, hh * LANES:(hh + 1) * LANES]
        ak = aug[:, (HEADS + hh) * LANES:(HEADS + hh + 1) * LANES]
        qf_ref[0, hh] = jnp.where(keep, xq, aq).astype(qf_ref.dtype)
        kf_ref[0, hh] = jnp.where(keep, xk, ak).astype(kf_ref.dtype)

    cq = _dot(h, wmain_ref[:, _C_CQ:_C_CQ + MLA_Q_LORA])
    nq = (_rms(cq) * gq_ref[...]).astype(BF16)
    qq = _dot(nq, wq_ref[...])
    for hh in range(HEADS):
        qa = qq[:, hh * LANES:(hh + 1) * LANES]
        qb = qq[:, (HEADS + hh) * LANES:(HEADS + hh + 1) * LANES]
        qm_ref[0, hh] = (qa * cfull + qb * sfull).astype(qm_ref.dtype)

    ckv = _dot(h, wmain_ref[:, _C_CKV:_C_CKV + MLA_KV_LORA])
    nkv = (_rms(ckv) * gkv_ref[...]).astype(BF16)
    kv = _dot(nkv, wkv_ref[...])
    vm_ref[0] = kv[:, HEADS * LANES:].astype(vm_ref.dtype)
    in_rope = (lane >= MLA_NOPE) & (lane < MLA_NOPE + MLA_ROPE)
    kpe = jnp.where(in_rope, pa * cfull + pb * sfull, 0.0)
    for hh in range(HEADS):
        km_ref[0, hh] = (kv[:, hh * LANES:(hh + 1) * LANES] + kpe).astype(km_ref.dtype)


def _rot_half_cols(w):
    half = w.shape[1] // 2
    return jnp.concatenate([-w[:, half:], w[:, :half]], axis=1)


def _prep_proj_weights(w_in, w_uq, w_ukv, b_forget):
    d = w_in.shape[0]
    sp = np.cumsum(IN_WIDTHS)[:-1]
    w_fq, w_fk, w_fv, w_f, w_cq, w_ckv, w_kr, w_gf, w_gm = jnp.split(w_in, [int(v) for v in sp], axis=1)
    zeros = lambda n: jnp.zeros((d, n), w_in.dtype)
    misc_a = jnp.concatenate([w_f, w_f, w_f, zeros(MLA_NOPE - 3 * HEADS), w_kr,
                              zeros(LANES - MLA_NOPE - MLA_ROPE)], axis=1)
    misc_b = jnp.concatenate([zeros(MLA_NOPE), _rot_half_cols(w_kr),
                              zeros(LANES - MLA_NOPE - MLA_ROPE)], axis=1)
    w_main = jnp.concatenate([w_fq, w_fk, w_fv, misc_a, misc_b, w_cq, w_ckv, w_gf, w_gm], axis=1).astype(BF16)

    r = w_uq.shape[0]
    uq = w_uq.reshape(r, HEADS, MLA_NOPE + MLA_ROPE)
    nope, rope = uq[:, :, :MLA_NOPE], uq[:, :, MLA_NOPE:]
    half = MLA_ROPE // 2
    rope_rot = jnp.concatenate([-rope[:, :, half:], rope[:, :, :half]], axis=2)
    pad = jnp.zeros((r, HEADS, LANES - MLA_NOPE - MLA_ROPE), w_uq.dtype)
    w_a = jnp.concatenate([nope, rope, pad], axis=2).reshape(r, HEADS * LANES)
    w_b = jnp.concatenate([jnp.zeros_like(nope), rope_rot, pad], axis=2).reshape(r, HEADS * LANES)
    w_q = jnp.concatenate([w_a, w_b], axis=1).astype(BF16)

    rk = w_ukv.shape[0]
    ukv = w_ukv.reshape(rk, HEADS, MLA_NOPE + MLA_V)
    k_nope, v = ukv[:, :, :MLA_NOPE], ukv[:, :, MLA_NOPE:]
    w_kn = jnp.concatenate([k_nope, jnp.zeros((rk, HEADS, LANES - MLA_NOPE), w_ukv.dtype)], axis=2)
    w_kv = jnp.concatenate([w_kn.reshape(rk, HEADS * LANES), v.reshape(rk, HEADS * MLA_V)], axis=1).astype(BF16)

    bf3 = jnp.concatenate([b_forget, b_forget, b_forget,
                           jnp.zeros((LANES - 3 * HEADS,), b_forget.dtype)]).reshape(1, LANES)
    return w_main, w_q, w_kv, bf3


def _decay_selector():
    sel = np.zeros((LANES, 2 * HEADS * LANES), np.float32)
    for hh in range(HEADS):
        base_q = hh * LANES + (HEAD_DIM if hh % 2 == 0 else 0)
        base_k = (HEADS + hh) * LANES + (HEAD_DIM if hh % 2 == 0 else 0)
        for piece in range(3):
            sel[piece * HEADS + hh, base_q + piece] = 1.0
            sel[3 * HEADS, base_q + 3 + piece] = 1.0
            sel[3 * HEADS, base_k + piece] = 1.0
            sel[piece * HEADS + hh, base_k + 3 + piece] = -1.0
    return jnp.asarray(sel, BF16)


def _token_proj(x, mod, g_pre, cfull, sfull, w_main, w_q, w_kv, bf3, g_q, g_kv):
    bsz, seq, d = x.shape
    tm = TOKEN_TILE
    sel = _decay_selector()
    tri = jnp.asarray(np.tril(np.ones((tm, tm), np.float32)), BF16)
    row = lambda b, i: (b, i, 0)
    head = lambda b, i: (b, 0, i, 0)
    head_shape = jax.ShapeDtypeStruct((bsz, HEADS, seq, LANES), BF16)
    head_spec = pl.BlockSpec((1, HEADS, tm, LANES), head)
    v_shape = jax.ShapeDtypeStruct((bsz, seq, FOX_WIDTH), BF16)
    v_spec = pl.BlockSpec((1, tm, FOX_WIDTH), row)
    g_shape = jax.ShapeDtypeStruct((bsz, seq, d), BF16)
    g_spec = pl.BlockSpec((1, tm, d), row)
    return pl.pallas_call(
        _token_proj_kernel,
        grid=(bsz, seq // tm),
        in_specs=[pl.BlockSpec((1, tm, d), row),
                  pl.BlockSpec((1, 6, d), lambda b, i: (b, 0, 0)),
                  _const_spec((1, d)), _const_spec((1, LANES)),
                  _const_spec((1, MLA_Q_LORA)), _const_spec((1, MLA_KV_LORA)),
                  pl.BlockSpec((1, tm, LANES), row), pl.BlockSpec((1, tm, LANES), row),
                  _const_spec(w_main.shape), _const_spec(w_q.shape), _const_spec(w_kv.shape),
                  _const_spec(sel.shape), _const_spec(tri.shape)],
        out_specs=[head_spec, head_spec, v_spec, head_spec, head_spec, v_spec, g_spec, g_spec],
        out_shape=[head_shape, head_shape, v_shape, head_shape, head_shape, v_shape, g_shape, g_shape],
        scratch_shapes=[pltpu.VMEM((8, LANES), F32)],
        compiler_params=pltpu.CompilerParams(dimension_semantics=("arbitrary", "arbitrary"),
                                             vmem_limit_bytes=VMEM_LIMIT),
        name="token_proj",
    )(x, mod, g_pre.reshape(1, d), bf3, g_q.reshape(1, -1), g_kv.reshape(1, -1), cfull, sfull,
      w_main, w_q, w_kv, sel, tri)


def _attn_kernel(q_ref, k_ref, v_ref, o_ref, *, exp2_scale, tile, chunk):
    seq = v_ref.shape[1]
    nq = seq // tile
    lane = lax.broadcasted_iota(jnp.int32, (tile, LANES), 1)
    rows = lax.broadcasted_iota(jnp.int32, (tile, tile), 0)
    cols = lax.broadcasted_iota(jnp.int32, (tile, tile), 1)
    causal = cols <= rows
    nt = (((1,), (1,)), ((), ()))
    sum_lane = (HEAD_DIM, 0)
    one_cols = [jnp.where(lane == sum_lane[hh], 1.0, 0.0).astype(BF16) for hh in range(2)]
    keep = (lane < HEAD_DIM, lane >= HEAD_DIM)

    def values(v, hh):
        return jnp.where(keep[hh], v, one_cols[hh])

    def scores(q, k, masked):
        s = lax.dot_general(q, k, nt, preferred_element_type=F32)
        return jnp.where(causal, s, -jnp.inf) if masked else s

    def softmax_pv(s, vh, m_old, acc_old):
        m_parts, p_parts = [], []
        for c0 in range(0, tile, chunk):
            cols_ = [s[c0:c0 + chunk, k0:k0 + LANES] for k0 in range(0, tile, LANES)]
            mx = functools.reduce(jnp.maximum, cols_)
            mc = jnp.broadcast_to(jnp.max(mx, axis=1, keepdims=True), (chunk, LANES))
            if m_old is not None:
                mc = jnp.maximum(mc, m_old[c0:c0 + chunk])
            p_parts.append(jnp.concatenate(
                [jnp.exp2((cb - mc) * exp2_scale).astype(BF16) for cb in cols_], axis=1))
            m_parts.append(mc)
        m_new = jnp.concatenate(m_parts, axis=0)
        pv = _dot(jnp.concatenate(p_parts, axis=0), vh)
        if m_old is None:
            return [m_new, pv]
        alpha = jnp.exp2((m_old - m_new) * exp2_scale)
        return [m_new, alpha * acc_old + pv]

    def q_block(qi, _):
        q0 = pl.multiple_of(qi * tile, tile)
        qs = [q_ref[0, hh, pl.ds(q0, tile), :] for hh in range(2)]
        vd = v_ref[0, pl.ds(q0, tile), :]
        ss = [scores(qs[hh], k_ref[0, hh, pl.ds(q0, tile), :], True) for hh in range(2)]
        state = []
        for hh in range(2):
            state += softmax_pv(ss[hh], values(vd, hh), None, None)

        def kv_block(j, st):
            k0 = pl.multiple_of(j * tile, tile)
            vj = v_ref[0, pl.ds(k0, tile), :]
            sj = [scores(qs[hh], k_ref[0, hh, pl.ds(k0, tile), :], False) for hh in range(2)]
            out = []
            for hh in range(2):
                out += softmax_pv(sj[hh], values(vj, hh), st[2 * hh], st[2 * hh + 1])
            return tuple(out)

        st = lax.fori_loop(0, qi, kv_block, tuple(state))
        outs = []
        for hh in range(2):
            acc = st[2 * hh + 1]
            outs.append(acc * (1.0 / acc[:, sum_lane[hh]:sum_lane[hh] + 1]))
        o_ref[0, pl.ds(q0, tile), :] = jnp.where(keep[0], outs[0], outs[1]).astype(o_ref.dtype)
        return 0

    lax.fori_loop(0, nq, q_block, 0)


def _causal_attn(q, k, v, softmax_scale):
    bsz, heads, seq, _ = q.shape
    qk_spec = pl.BlockSpec((1, 2, seq, LANES), lambda b, p: (b, p, 0, 0))
    v_spec = pl.BlockSpec((1, seq, LANES), lambda b, p: (b, 0, p))
    kern = functools.partial(_attn_kernel, exp2_scale=softmax_scale * math.log2(math.e),
                             tile=ATTN_TILE, chunk=ATTN_CHUNK)
    return pl.pallas_call(
        kern,
        grid=(bsz, heads // 2),
        in_specs=[qk_spec, qk_spec, v_spec],
        out_specs=v_spec,
        out_shape=jax.ShapeDtypeStruct(v.shape, BF16),
        compiler_params=pltpu.CompilerParams(dimension_semantics=("arbitrary", "arbitrary"),
                                             vmem_limit_bytes=VMEM_LIMIT),
        name="causal_attn",
    )(q, k, v)


def _merge_kernel(x_ref, oa_ref, ob_ref, gf_ref, gm_ref, mod_ref, gpost_ref,
                  wpf_ref, wpm_ref, wout_ref, o_ref):
    pa = _dot(oa_ref[0], wpf_ref[...])
    pb = _dot(ob_ref[0], wpm_ref[...])
    merged = gf_ref[0].astype(F32) * pa + gm_ref[0].astype(F32) * pb
    y = _dot(merged.astype(BF16), wout_ref[...])
    gate = mod_ref[0, 2:3, :]
    o_ref[0] = x_ref[0] + gate * (_rms(y) * gpost_ref[...])


def _merge_out(x, o_a, o_b, gf, gm, mod, g_post, w_pf, w_pm, w_out):
    bsz, seq, d = x.shape
    tm = TOKEN_TILE
    row = lambda b, i: (b, i, 0)
    return pl.pallas_call(
        _merge_kernel,
        grid=(bsz, seq // tm),
        in_specs=[pl.BlockSpec((1, tm, d), row),
                  pl.BlockSpec((1, tm, FOX_WIDTH), row), pl.BlockSpec((1, tm, FOX_WIDTH), row),
                  pl.BlockSpec((1, tm, d), row), pl.BlockSpec((1, tm, d), row),
                  pl.BlockSpec((1, 6, d), lambda b, i: (b, 0, 0)),
                  _const_spec((1, d)),
                  _const_spec(w_pf.shape), _const_spec(w_pm.shape), _const_spec(w_out.shape)],
        out_specs=pl.BlockSpec((1, tm, d), row),
        out_shape=jax.ShapeDtypeStruct(x.shape, F32),
        compiler_params=pltpu.CompilerParams(dimension_semantics=("arbitrary", "arbitrary"),
                                             vmem_limit_bytes=VMEM_LIMIT),
        name="merge_out",
    )(x, o_a, o_b, gf, gm, mod, g_post.reshape(1, d), w_pf, w_pm, w_out)


def _ffn_kernel(x_ref, mod_ref, gpre_ref, gpost_ref, win_ref, wout_ref, o_ref):
    x = x_ref[0]
    shift = mod_ref[0, 3:4, :]
    scale = mod_ref[0, 4:5, :]
    gate = mod_ref[0, 5:6, :]
    h = (_rms(x) * gpre_ref[...] * (1.0 + scale) + shift).astype(BF16)
    y = None
    for lo in range(0, D_FF, FFN_CHUNK):
        w = min(FFN_CHUNK, D_FF - lo)
        g = _dot(h, win_ref[:, lo:lo + w])
        u = _dot(h, win_ref[:, D_FF + lo:D_FF + lo + w])
        act = (g * jax.nn.sigmoid(g) * u).astype(BF16)
        part = _dot(act, wout_ref[lo:lo + w, :])
        y = part if y is None else y + part
    o_ref[0] = x + gate * (_rms(y) * gpost_ref[...])


def _ffn(x, mod, g_pre, g_post, w_in, w_out):
    bsz, seq, d = x.shape
    tm = TOKEN_TILE
    row = lambda b, i: (b, i, 0)
    return pl.pallas_call(
        _ffn_kernel,
        grid=(bsz, seq // tm),
        in_specs=[pl.BlockSpec((1, tm, d), row),
                  pl.BlockSpec((1, 6, d), lambda b, i: (b, 0, 0)),
                  _const_spec((1, d)), _const_spec((1, d)),
                  _const_spec(w_in.shape), _const_spec(w_out.shape)],
        out_specs=pl.BlockSpec((1, tm, d), row),
        out_shape=jax.ShapeDtypeStruct(x.shape, F32),
        compiler_params=pltpu.CompilerParams(dimension_semantics=("arbitrary", "arbitrary"),
                                             vmem_limit_bytes=VMEM_LIMIT),
        name="ffn",
    )(x, mod, g_pre.reshape(1, d), g_post.reshape(1, d), w_in, w_out)


def kernel(x, c, positions, w_ada, b_ada, g_pre_mix, g_post_mix, g_pre_ffn, g_post_ffn, w_in, b_forget,
           g_q_lora, w_uq, g_kv_lora, w_ukv, w_proj_fox, w_proj_mla, w_out, w_ffn_in, w_ffn_out):
    bsz, seq, d = x.shape
    depth = w_ada.shape[0]
    cfull, sfull = _rope_tables(positions)
    for l in range(depth):
        mod = _adaln_mod(c, w_ada[l], b_ada[l]).reshape(bsz, 6, d)
        w_main, w_q, w_kv, bf3 = _prep_proj_weights(w_in[l], w_uq[l], w_ukv[l], b_forget[l])
        qf, kf, vf, qm, km, vm, gf, gm = _token_proj(
            x, mod, g_pre_mix[l], cfull, sfull, w_main, w_q, w_kv, bf3, g_q_lora[l], g_kv_lora[l])
        o_a = _causal_attn(qf, kf, vf, 1.0 / math.sqrt(HEAD_DIM))
        o_b = _causal_attn(qm, km, vm, 1.0 / math.sqrt(MLA_NOPE + MLA_ROPE))
        x = _merge_out(x, o_a, o_b, gf, gm, mod, g_post_mix[l],
                       w_proj_fox[l].astype(BF16), w_proj_mla[l].astype(BF16), w_out[l].astype(BF16))
        x = _ffn(x, mod, g_pre_ffn[l], g_post_ffn[l], w_ffn_in[l].astype(BF16), w_ffn_out[l].astype(BF16))
    return x
```

```python
import functools
import math

import jax
import jax.numpy as jnp
import numpy as np
from jax import lax
from jax.experimental import pallas as pl
from jax.experimental.pallas import tpu as pltpu

D_MODEL = 1024
HEADS = 8
HEAD_DIM = 64
FOX_WIDTH = HEADS * HEAD_DIM
MLA_NOPE = 64
MLA_ROPE = 32
MLA_V = 64
MLA_Q_LORA = 768
MLA_KV_LORA = 256
D_FF = 2816
ROPE_THETA = 10000.0
NORM_EPS = 1e-6
IN_WIDTHS = (FOX_WIDTH, FOX_WIDTH, FOX_WIDTH, HEADS, MLA_Q_LORA, MLA_KV_LORA, MLA_ROPE, D_MODEL, D_MODEL)

LANES = 128
TOKEN_TILE = 512
ATTN_TILE = 512
ATTN_CHUNK = 32
FFN_CHUNK = 512
VMEM_LIMIT = 56 * 1024 * 1024

_C_QKV = 0
_C_MISC = 3 * FOX_WIDTH
_C_CQ = _C_MISC + 2 * LANES
_C_CKV = _C_CQ + MLA_Q_LORA
_C_GATE = _C_CKV + MLA_KV_LORA
_C_END = _C_GATE + 2 * D_MODEL

F32 = jnp.float32
BF16 = jnp.bfloat16


def _const_spec(shape):
    zeros = (0,) * len(shape)
    return pl.BlockSpec(shape, lambda *_: zeros, pipeline_mode=pl.Buffered(1))


def _rms(x):
    return x * lax.rsqrt(jnp.mean(x * x, axis=-1, keepdims=True) + NORM_EPS)


def _dot(a, b):
    return jnp.dot(a, b, preferred_element_type=F32)


def _adaln_kernel(c_ref, w_ref, b_ref, o_ref):
    c = c_ref[...]
    sc = c * jax.nn.sigmoid(c)
    o_ref[...] = jnp.dot(sc, w_ref[...], preferred_element_type=F32,
                         precision=lax.Precision.HIGHEST) + b_ref[...]


def _adaln_mod(c, w_ada, b_ada):
    bsz, d = c.shape
    n = w_ada.shape[1]
    tn = 1024
    return pl.pallas_call(
        _adaln_kernel,
        grid=(n // tn,),
        in_specs=[pl.BlockSpec((bsz, d), lambda j: (0, 0)),
                  pl.BlockSpec((d, tn), lambda j: (0, j)),
                  pl.BlockSpec((1, tn), lambda j: (0, j))],
        out_specs=pl.BlockSpec((bsz, tn), lambda j: (0, j)),
        out_shape=jax.ShapeDtypeStruct((bsz, n), F32),
        name="adaln_mod",
    )(c, w_ada, b_ada.reshape(1, n))


def _rope_kernel(pos_ref, invf_ref, cos_ref, sin_ref):
    ang = pos_ref[...] * invf_ref[...]
    cos_ref[...] = jnp.cos(ang)
    sin_ref[...] = jnp.sin(ang)


def _rope_tables(positions):
    bsz, seq = positions.shape
    half = MLA_ROPE // 2
    inv_freq = 1.0 / (ROPE_THETA ** (np.arange(0, MLA_ROPE, 2, dtype=np.float32) / MLA_ROPE))
    rows = seq * half // LANES
    pos_rep = jnp.repeat(positions.astype(F32), half, axis=1).reshape(bsz, rows, LANES)
    invf = jnp.asarray(np.tile(inv_freq.astype(np.float32), LANES // half)).reshape(1, 1, LANES)
    spec = pl.BlockSpec((1, rows, LANES), lambda b: (b, 0, 0))
    cos, sin = pl.pallas_call(
        _rope_kernel,
        grid=(bsz,),
        in_specs=[spec, pl.BlockSpec((1, 1, LANES), lambda b: (0, 0, 0))],
        out_specs=[spec, spec],
        out_shape=[jax.ShapeDtypeStruct((bsz, rows, LANES), F32)] * 2,
        name="rope_tables",
    )(pos_rep, invf)
    cos = cos.reshape(bsz, seq, half)
    sin = sin.reshape(bsz, seq, half)
    ones = jnp.ones((bsz, seq, MLA_NOPE), F32)
    zq = jnp.zeros((bsz, seq, MLA_NOPE), F32)
    zpad = jnp.zeros((bsz, seq, LANES - MLA_NOPE - MLA_ROPE), F32)
    cfull = jnp.concatenate([ones, cos, cos, zpad], axis=-1)
    sfull = jnp.concatenate([zq, sin, sin, zpad], axis=-1)
    return cfull, sfull


def _token_proj_kernel(x_ref, mod_ref, gpre_ref, bf_ref, gq_ref, gkv_ref, cos_ref, sin_ref,
                       wmain_ref, wq_ref, wkv_ref, sel_ref, tri_ref,
                       qf_ref, kf_ref, vf_ref, qm_ref, km_ref, vm_ref, gf_ref, gm_ref,
                       carry_ref):
    tm = x_ref.shape[1]

    @pl.when(pl.program_id(1) == 0)
    def _():
        carry_ref[...] = jnp.zeros_like(carry_ref)

    x = x_ref[0]
    shift = mod_ref[0, 0:1, :]
    scale = mod_ref[0, 1:2, :]
    h = (_rms(x) * gpre_ref[...] * (1.0 + scale) + shift).astype(BF16)

    for gi, ref in enumerate((gf_ref, gm_ref)):
        lo = _C_GATE + gi * D_MODEL
        g = _dot(h, wmain_ref[:, lo:lo + D_MODEL])
        ref[0] = jax.nn.sigmoid(g).astype(ref.dtype)

    lane = lax.broadcasted_iota(jnp.int32, (tm, LANES), 1)
    cfull = cos_ref[0]
    sfull = sin_ref[0]

    p_qkv = _dot(h, wmain_ref[:, _C_QKV:_C_QKV + 3 * FOX_WIDTH])
    vf_ref[0] = p_qkv[:, 2 * FOX_WIDTH:].astype(vf_ref.dtype)
    misc = _dot(h, wmain_ref[:, _C_MISC:_C_MISC + 2 * LANES])
    pa = misc[:, :LANES]
    pb = misc[:, LANES:]

    logit = pa + bf_ref[...]
    logf = jnp.minimum(logit, 0.0) - jnp.log(1.0 + jnp.exp(-jnp.abs(logit)))
    tri = tri_ref[...]
    l_hi = logf.astype(BF16)
    r1 = logf - l_hi.astype(F32)
    l_mid = r1.astype(BF16)
    l_lo = (r1 - l_mid.astype(F32)).astype(BF16)
    cum = _dot(tri, l_hi) + _dot(tri, l_mid) + _dot(tri, l_lo) + carry_ref[0:1, :]
    carry_ref[0:1, :] = cum[tm - 1:tm, :]

    a = cum * math.sqrt(HEAD_DIM)
    a_hi = a.astype(BF16).astype(F32)
    r1 = a - a_hi
    a_mid = r1.astype(BF16).astype(F32)
    a_lo = (r1 - a_mid).astype(BF16).astype(F32)
    z = jnp.where(lane < 8, a_hi,
                  jnp.where(lane < 16, a_mid,
                            jnp.where(lane < 24, a_lo,
                                      jnp.where(lane == 24, 1.0, 0.0))))
    aug = _dot(z.astype(BF16), sel_ref[...])
    for hh in range(HEADS):
        pair = hh // 2
        keep = (lane < HEAD_DIM) if hh % 2 == 0 else (lane >= HEAD_DIM)
        xq = p_qkv[:, pair * LANES:(pair + 1) * LANES]
        xk = p_qkv[:, FOX_WIDTH + pair * LANES:FOX_WIDTH + (pair + 1) * LANES]
        aq = aug[:, hh * LANES:(hh + 1) * LANES]
        ak = aug[:, (HEADS + hh) * LANES:(HEADS + hh + 1) * LANES]
        qf_ref[0, hh] = jnp.where(keep, xq, aq).astype(qf_ref.dtype)
        kf_ref[0, hh] = jnp.where(keep, xk, ak).astype(kf_ref.dtype)

    cq = _dot(h, wmain_ref[:, _C_CQ:_C_CQ + MLA_Q_LORA])
    nq = (_rms(cq) * gq_ref[...]).astype(BF16)
    qq = _dot(nq, wq_ref[...])
    for hh in range(HEADS):
        qa = qq[:, hh * LANES:(hh + 1) * LANES]
        qb = qq[:, (HEADS + hh) * LANES:(HEADS + hh + 1) * LANES]
        qm_ref[0, hh] = (qa * cfull + qb * sfull).astype(qm_ref.dtype)

    ckv = _dot(h, wmain_ref[:, _C_CKV:_C_CKV + MLA_KV_LORA])
    nkv = (_rms(ckv) * gkv_ref[...]).astype(BF16)
    kv = _dot(nkv, wkv_ref[...])
    vm_ref[0] = kv[:, HEADS * LANES:].astype(vm_ref.dtype)
    in_rope = (lane >= MLA_NOPE) & (lane < MLA_NOPE + MLA_ROPE)
    kpe = jnp.where(in_rope, pa * cfull + pb * sfull, 0.0)
    for hh in range(HEADS):
        km_ref[0, hh] = (kv[:, hh * LANES:(hh + 1) * LANES] + kpe).astype(km_ref.dtype)


def _rot_half_cols(w):
    half = w.shape[1] // 2
    return jnp.concatenate([-w[:, half:], w[:, :half]], axis=1)


def _prep_proj_weights(w_in, w_uq, w_ukv, b_forget):
    d = w_in.shape[0]
    sp = np.cumsum(IN_WIDTHS)[:-1]
    w_fq, w_fk, w_fv, w_f, w_cq, w_ckv, w_kr, w_gf, w_gm = jnp.split(w_in, [int(v) for v in sp], axis=1)
    zeros = lambda n: jnp.zeros((d, n), w_in.dtype)
    misc_a = jnp.concatenate([w_f, w_f, w_f, zeros(MLA_NOPE - 3 * HEADS), w_kr,
                              zeros(LANES - MLA_NOPE - MLA_ROPE)], axis=1)
    misc_b = jnp.concatenate([zeros(MLA_NOPE), _rot_half_cols(w_kr),
                              zeros(LANES - MLA_NOPE - MLA_ROPE)], axis=1)
    w_main = jnp.concatenate([w_fq, w_fk, w_fv, misc_a, misc_b, w_cq, w_ckv, w_gf, w_gm], axis=1).astype(BF16)

    r = w_uq.shape[0]
    uq = w_uq.reshape(r, HEADS, MLA_NOPE + MLA_ROPE)
    nope, rope = uq[:, :, :MLA_NOPE], uq[:, :, MLA_NOPE:]
    half = MLA_ROPE // 2
    rope_rot = jnp.concatenate([-rope[:, :, half:], rope[:, :, :half]], axis=2)
    pad = jnp.zeros((r, HEADS, LANES - MLA_NOPE - MLA_ROPE), w_uq.dtype)
    w_a = jnp.concatenate([nope, rope, pad], axis=2).reshape(r, HEADS * LANES)
    w_b = jnp.concatenate([jnp.zeros_like(nope), rope_rot, pad], axis=2).reshape(r, HEADS * LANES)
    w_q = jnp.concatenate([w_a, w_b], axis=1).astype(BF16)

    rk = w_ukv.shape[0]
    ukv = w_ukv.reshape(rk, HEADS, MLA_NOPE + MLA_V)
    k_nope, v = ukv[:, :, :MLA_NOPE], ukv[:, :, MLA_NOPE:]
    w_kn = jnp.concatenate([k_nope, jnp.zeros((rk, HEADS, LANES - MLA_NOPE), w_ukv.dtype)], axis=2)
    w_kv = jnp.concatenate([w_kn.reshape(rk, HEADS * LANES), v.reshape(rk, HEADS * MLA_V)], axis=1).astype(BF16)

    bf3 = jnp.concatenate([b_forget, b_forget, b_forget,
                           jnp.zeros((LANES - 3 * HEADS,), b_forget.dtype)]).reshape(1, LANES)
    return w_main, w_q, w_kv, bf3


def _decay_selector():
    sel = np.zeros((LANES, 2 * HEADS * LANES), np.float32)
    for hh in range(HEADS):
        base_q = hh * LANES + (HEAD_DIM if hh % 2 == 0 else 0)
        base_k = (HEADS + hh) * LANES + (HEAD_DIM if hh % 2 == 0 else 0)
        for piece in range(3):
            sel[piece * HEADS + hh, base_q + piece] = 1.0
            sel[3 * HEADS, base_q + 3 + piece] = 1.0
            sel[3 * HEADS, base_k + piece] = 1.0
            sel[piece * HEADS + hh, base_k + 3 + piece] = -1.0
    return jnp.asarray(sel, BF16)


def _token_proj(x, mod, g_pre, cfull, sfull, w_main, w_q, w_kv, bf3, g_q, g_kv):
    bsz, seq, d = x.shape
    tm = TOKEN_TILE
    sel = _decay_selector()
    tri = jnp.asarray(np.tril(np.ones((tm, tm), np.float32)), BF16)
    row = lambda b, i: (b, i, 0)
    head = lambda b, i: (b, 0, i, 0)
    head_shape = jax.ShapeDtypeStruct((bsz, HEADS, seq, LANES), BF16)
    head_spec = pl.BlockSpec((1, HEADS, tm, LANES), head)
    v_shape = jax.ShapeDtypeStruct((bsz, seq, FOX_WIDTH), BF16)
    v_spec = pl.BlockSpec((1, tm, FOX_WIDTH), row)
    g_shape = jax.ShapeDtypeStruct((bsz, seq, d), BF16)
    g_spec = pl.BlockSpec((1, tm, d), row)
    return pl.pallas_call(
        _token_proj_kernel,
        grid=(bsz, seq // tm),
        in_specs=[pl.BlockSpec((1, tm, d), row),
                  pl.BlockSpec((1, 6, d), lambda b, i: (b, 0, 0)),
                  _const_spec((1, d)), _const_spec((1, LANES)),
                  _const_spec((1, MLA_Q_LORA)), _const_spec((1, MLA_KV_LORA)),
                  pl.BlockSpec((1, tm, LANES), row), pl.BlockSpec((1, tm, LANES), row),
                  _const_spec(w_main.shape), _const_spec(w_q.shape), _const_spec(w_kv.shape),
                  _const_spec(sel.shape), _const_spec(tri.shape)],
        out_specs=[head_spec, head_spec, v_spec, head_spec, head_spec, v_spec, g_spec, g_spec],
        out_shape=[head_shape, head_shape, v_shape, head_shape, head_shape, v_shape, g_shape, g_shape],
        scratch_shapes=[pltpu.VMEM((8, LANES), F32)],
        compiler_params=pltpu.CompilerParams(dimension_semantics=("arbitrary", "arbitrary"),
                                             vmem_limit_bytes=VMEM_LIMIT),
        name="token_proj",
    )(x, mod, g_pre.reshape(1, d), bf3, g_q.reshape(1, -1), g_kv.reshape(1, -1), cfull, sfull,
      w_main, w_q, w_kv, sel, tri)


def _attn_kernel(q_ref, k_ref, v_ref, o_ref, sa_ref, sb_ref, m_ref, alpha_ref, acc_ref,
                 *, exp2_scale, tile, chunk):
    seq = v_ref.shape[1]
    nq = seq // tile
    lane = lax.broadcasted_iota(jnp.int32, (tile, LANES), 1)
    rows = lax.broadcasted_iota(jnp.int32, (tile, tile), 0)
    cols = lax.broadcasted_iota(jnp.int32, (tile, tile), 1)
    causal = cols <= rows
    nt = (((1,), (1,)), ((), ()))
    sum_lane = (HEAD_DIM, 0)
    one_cols = [jnp.where(lane == sum_lane[hh], 1.0, 0.0).astype(BF16) for hh in range(2)]
    keep = (lane < HEAD_DIM, lane >= HEAD_DIM)

    def values(v, hh):
        return jnp.where(keep[hh], v, one_cols[hh])

    def issue_scores(qi, j, s_out):
        for hh in range(2):
            s_out[hh] = lax.dot_general(q_ref[0, hh, qi * tile:(qi + 1) * tile, :],
                                        k_ref[0, hh, j * tile:(j + 1) * tile, :],
                                        nt, preferred_element_type=F32)

    def softmax_pv(s_in, hh, vh, masked):
        p_parts = []
        for c0 in range(0, tile, chunk):
            cols_ = []
            for k0 in range(0, tile, LANES):
                if masked and k0 > c0 + chunk - 1:
                    continue
                cb = s_in[hh, c0:c0 + chunk, k0:k0 + LANES]
                if masked and k0 + LANES - 1 > c0:
                    cb = jnp.where(causal[c0:c0 + chunk, k0:k0 + LANES], cb, -jnp.inf)
                cols_.append(cb)
            mx = functools.reduce(jnp.maximum, cols_)
            m_old = m_ref[hh, c0:c0 + chunk, :]
            mc = jnp.maximum(m_old, jnp.broadcast_to(jnp.max(mx, axis=1, keepdims=True), (chunk, LANES)))
            dead = [jnp.zeros((chunk, LANES), BF16)] * (tile // LANES - len(cols_))
            p_parts.append(jnp.concatenate(
                [jnp.exp2((cb - mc) * exp2_scale).astype(BF16) for cb in cols_] + dead, axis=1))
            m_ref[hh, c0:c0 + chunk, :] = mc
            alpha_ref[hh, c0:c0 + chunk, :] = jnp.exp2((m_old - mc) * exp2_scale)
        pv = _dot(jnp.concatenate(p_parts, axis=0), vh)
        acc_ref[hh] = alpha_ref[hh] * acc_ref[hh] + pv

    def reset_state():
        m_ref[...] = jnp.full(m_ref.shape, -jnp.inf, F32)
        acc_ref[...] = jnp.zeros(acc_ref.shape, F32)

    blocks = [(qi, j) for qi in range(nq) for j in range(qi + 1)]
    bufs = (sa_ref, sb_ref)
    reset_state()
    issue_scores(0, 0, bufs[0])
    for t, (qi, j) in enumerate(blocks):
        s_in, s_out = bufs[t % 2], bufs[(t + 1) % 2]
        if t + 1 < len(blocks):
            issue_scores(*blocks[t + 1], s_out)
        vj = v_ref[0, j * tile:(j + 1) * tile, :]
        for hh in range(2):
            softmax_pv(s_in, hh, values(vj, hh), j == qi)
        if j == qi:
            outs = []
            for hh in range(2):
                acc = acc_ref[hh]
                outs.append(acc * (1.0 / acc[:, sum_lane[hh]:sum_lane[hh] + 1]))
            o_ref[0, qi * tile:(qi + 1) * tile, :] = jnp.where(keep[0], outs[0], outs[1]).astype(o_ref.dtype)
            if qi + 1 < nq:
                reset_state()


def _causal_attn(q, k, v, softmax_scale):
    bsz, heads, seq, _ = q.shape
    qk_spec = pl.BlockSpec((1, 2, seq, LANES), lambda b, p: (b, p, 0, 0))
    v_spec = pl.BlockSpec((1, seq, LANES), lambda b, p: (b, 0, p))
    kern = functools.partial(_attn_kernel, exp2_scale=softmax_scale * math.log2(math.e),
                             tile=ATTN_TILE, chunk=ATTN_CHUNK)
    return pl.pallas_call(
        kern,
        grid=(bsz, heads // 2),
        in_specs=[qk_spec, qk_spec, v_spec],
        out_specs=v_spec,
        out_shape=jax.ShapeDtypeStruct(v.shape, BF16),
        scratch_shapes=[pltpu.VMEM((2, ATTN_TILE, ATTN_TILE), F32),
                        pltpu.VMEM((2, ATTN_TILE, ATTN_TILE), F32),
                        pltpu.VMEM((2, ATTN_TILE, LANES), F32),
                        pltpu.VMEM((2, ATTN_TILE, LANES), F32),
                        pltpu.VMEM((2, ATTN_TILE, LANES), F32)],
        compiler_params=pltpu.CompilerParams(dimension_semantics=("arbitrary", "arbitrary"),
                                             vmem_limit_bytes=VMEM_LIMIT),
        name="causal_attn",
    )(q, k, v)


def _merge_kernel(x_ref, oa_ref, ob_ref, gf_ref, gm_ref, mod_ref, gpost_ref,
                  wpf_ref, wpm_ref, wout_ref, o_ref):
    pa = _dot(oa_ref[0], wpf_ref[...])
    pb = _dot(ob_ref[0], wpm_ref[...])
    merged = gf_ref[0].astype(F32) * pa + gm_ref[0].astype(F32) * pb
    y = _dot(merged.astype(BF16), wout_ref[...])
    gate = mod_ref[0, 2:3, :]
    o_ref[0] = x_ref[0] + gate * (_rms(y) * gpost_ref[...])


def _merge_out(x, o_a, o_b, gf, gm, mod, g_post, w_pf, w_pm, w_out):
    bsz, seq, d = x.shape
    tm = TOKEN_TILE
    row = lambda b, i: (b, i, 0)
    return pl.pallas_call(
        _merge_kernel,
        grid=(bsz, seq // tm),
        in_specs=[pl.BlockSpec((1, tm, d), row),
                  pl.BlockSpec((1, tm, FOX_WIDTH), row), pl.BlockSpec((1, tm, FOX_WIDTH), row),
                  pl.BlockSpec((1, tm, d), row), pl.BlockSpec((1, tm, d), row),
                  pl.BlockSpec((1, 6, d), lambda b, i: (b, 0, 0)),
                  _const_spec((1, d)),
                  _const_spec(w_pf.shape), _const_spec(w_pm.shape), _const_spec(w_out.shape)],
        out_specs=pl.BlockSpec((1, tm, d), row),
        out_shape=jax.ShapeDtypeStruct(x.shape, F32),
        compiler_params=pltpu.CompilerParams(dimension_semantics=("arbitrary", "arbitrary"),
                                             vmem_limit_bytes=VMEM_LIMIT),
        name="merge_out",
    )(x, o_a, o_b, gf, gm, mod, g_post.reshape(1, d), w_pf, w_pm, w_out)


def _ffn_kernel(x_ref, mod_ref, gpre_ref, gpost_ref, win_ref, wout_ref, o_ref):
    x = x_ref[0]
    shift = mod_ref[0, 3:4, :]
    scale = mod_ref[0, 4:5, :]
    gate = mod_ref[0, 5:6, :]
    h = (_rms(x) * gpre_ref[...] * (1.0 + scale) + shift).astype(BF16)
    y = None
    for lo in range(0, D_FF, FFN_CHUNK):
        w = min(FFN_CHUNK, D_FF - lo)
        g = _dot(h, win_ref[:, lo:lo + w])
        u = _dot(h, win_ref[:, D_FF + lo:D_FF + lo + w])
        act = (g * jax.nn.sigmoid(g) * u).astype(BF16)
        part = _dot(act, wout_ref[lo:lo + w, :])
        y = part if y is None else y + part
    o_ref[0] = x + gate * (_rms(y) * gpost_ref[...])


def _ffn(x, mod, g_pre, g_post, w_in, w_out):
    bsz, seq, d = x.shape
    tm = TOKEN_TILE
    row = lambda b, i: (b, i, 0)
    return pl.pallas_call(
        _ffn_kernel,
        grid=(bsz, seq // tm),
        in_specs=[pl.BlockSpec((1, tm, d), row),
                  pl.BlockSpec((1, 6, d), lambda b, i: (b, 0, 0)),
                  _const_spec((1, d)), _const_spec((1, d)),
                  _const_spec(w_in.shape), _const_spec(w_out.shape)],
        out_specs=pl.BlockSpec((1, tm, d), row),
        out_shape=jax.ShapeDtypeStruct(x.shape, F32),
        compiler_params=pltpu.CompilerParams(dimension_semantics=("arbitrary", "arbitrary"),
                                             vmem_limit_bytes=VMEM_LIMIT),
        name="ffn",
    )(x, mod, g_pre.reshape(1, d), g_post.reshape(1, d), w_in, w_out)


def kernel(x, c, positions, w_ada, b_ada, g_pre_mix, g_post_mix, g_pre_ffn, g_post_ffn, w_in, b_forget,
           g_q_lora, w_uq, g_kv_lora, w_ukv, w_proj_fox, w_proj_mla, w_out, w_ffn_in, w_ffn_out):
    bsz, seq, d = x.shape
    depth = w_ada.shape[0]
    cfull, sfull = _rope_tables(positions)
    for l in range(depth):
        mod = _adaln_mod(c, w_ada[l], b_ada[l]).reshape(bsz, 6, d)
        w_main, w_q, w_kv, bf3 = _prep_proj_weights(w_in[l], w_uq[l], w_ukv[l], b_forget[l])
        qf, kf, vf, qm, km, vm, gf, gm = _token_proj(
            x, mod, g_pre_mix[l], cfull, sfull, w_main, w_q, w_kv, bf3, g_q_lora[l], g_kv_lora[l])
        o_a = _causal_attn(qf, kf, vf, 1.0 / math.sqrt(HEAD_DIM))
        o_b = _causal_attn(qm, km, vm, 1.0 / math.sqrt(MLA_NOPE + MLA_ROPE))
        x = _merge_out(x, o_a, o_b, gf, gm, mod, g_post_mix[l],
                       w_proj_fox[l].astype(BF16), w_proj_mla[l].astype(BF16), w_out[l].astype(BF16))
        x = _ffn(x, mod, g_pre_ffn[l], g_post_ffn[l], w_ffn_in[l].astype(BF16), w_ffn_out[l].astype(BF16))
    return x
```

```python
import functools
import math

import jax
import jax.numpy as jnp
import numpy as np
from jax import lax
from jax.experimental import pallas as pl
from jax.experimental.pallas import tpu as pltpu

D_MODEL = 1024
HEADS = 8
HEAD_DIM = 64
FOX_WIDTH = HEADS * HEAD_DIM
MLA_NOPE = 64
MLA_ROPE = 32
MLA_V = 64
MLA_Q_LORA = 768
MLA_KV_LORA = 256
D_FF = 2816
ROPE_THETA = 10000.0
NORM_EPS = 1e-6
IN_WIDTHS = (FOX_WIDTH, FOX_WIDTH, FOX_WIDTH, HEADS, MLA_Q_LORA, MLA_KV_LORA, MLA_ROPE, D_MODEL, D_MODEL)

LANES = 128
TOKEN_TILE = 512
ATTN_TILE = 512
ATTN_CHUNK = 32
FFN_CHUNK = 512
VMEM_LIMIT = 56 * 1024 * 1024

_EXP2_FOX = math.log2(math.e) / math.sqrt(HEAD_DIM)
_EXP2_MLA = math.log2(math.e) / math.sqrt(MLA_NOPE + MLA_ROPE)

F32 = jnp.float32
BF16 = jnp.bfloat16


def _const_spec(shape):
    zeros = (0,) * len(shape)
    return pl.BlockSpec(shape, lambda *_: zeros, pipeline_mode=pl.Buffered(1))


def _rms(x):
    return x * lax.rsqrt(jnp.mean(x * x, axis=-1, keepdims=True) + NORM_EPS)


def _dot(a, b):
    return jnp.dot(a, b, preferred_element_type=F32)


def _adaln_kernel(c_ref, w_ref, b_ref, o_ref):
    c = c_ref[...]
    sc = c * jax.nn.sigmoid(c)
    o_ref[...] = jnp.dot(sc, w_ref[...], preferred_element_type=F32,
                         precision=lax.Precision.HIGHEST) + b_ref[...]


def _adaln_mod(c, w_ada, b_ada):
    bsz, d = c.shape
    n = w_ada.shape[1]
    tn = 1024
    return pl.pallas_call(
        _adaln_kernel,
        grid=(n // tn,),
        in_specs=[pl.BlockSpec((bsz, d), lambda j: (0, 0)),
                  pl.BlockSpec((d, tn), lambda j: (0, j)),
                  pl.BlockSpec((1, tn), lambda j: (0, j))],
        out_specs=pl.BlockSpec((bsz, tn), lambda j: (0, j)),
        out_shape=jax.ShapeDtypeStruct((bsz, n), F32),
        name="adaln_mod",
    )(c, w_ada, b_ada.reshape(1, n))


def _rope_kernel(pos_ref, invf_ref, cos_ref, sin_ref):
    ang = pos_ref[...] * invf_ref[...]
    cos_ref[...] = jnp.cos(ang)
    sin_ref[...] = jnp.sin(ang)


def _rope_tables(positions):
    bsz, seq = positions.shape
    half = MLA_ROPE // 2
    inv_freq = 1.0 / (ROPE_THETA ** (np.arange(0, MLA_ROPE, 2, dtype=np.float32) / MLA_ROPE))
    rows = seq * half // LANES
    pos_rep = jnp.repeat(positions.astype(F32), half, axis=1).reshape(bsz, rows, LANES)
    invf = jnp.asarray(np.tile(inv_freq.astype(np.float32), LANES // half)).reshape(1, 1, LANES)
    spec = pl.BlockSpec((1, rows, LANES), lambda b: (b, 0, 0))
    cos, sin = pl.pallas_call(
        _rope_kernel,
        grid=(bsz,),
        in_specs=[spec, pl.BlockSpec((1, 1, LANES), lambda b: (0, 0, 0))],
        out_specs=[spec, spec],
        out_shape=[jax.ShapeDtypeStruct((bsz, rows, LANES), F32)] * 2,
        name="rope_tables",
    )(pos_rep, invf)
    cos = cos.reshape(bsz, seq, half)
    sin = sin.reshape(bsz, seq, half)
    ones = jnp.ones((bsz, seq, MLA_NOPE), F32)
    zq = jnp.zeros((bsz, seq, MLA_NOPE), F32)
    zpad = jnp.zeros((bsz, seq, LANES - MLA_NOPE - MLA_ROPE), F32)
    cfull = jnp.concatenate([ones, cos, cos, zpad], axis=-1)
    ssign = jnp.concatenate([zq, -sin, sin, zpad], axis=-1)
    return cfull, ssign


def _cumsum_rows(x):
    n = x.shape[0]
    row = lax.broadcasted_iota(jnp.int32, x.shape, 0)
    d = 1
    while d < n:
        x = x + jnp.where(row >= d, pltpu.roll(x, d, axis=0), 0.0)
        d *= 2
    return x


def _rot_half_lanes(x, lane):
    half = MLA_ROPE // 2
    return jnp.where(lane < MLA_NOPE + half, pltpu.roll(x, LANES - half, axis=1), pltpu.roll(x, half, axis=1))


def _token_proj_kernel(x_ref, mod_ref, gpre_ref, bf_ref, gq_ref, gkv_ref, cos_ref, sin_ref,
                       wqkv_ref, wmisc_ref, wcq_ref, wckv_ref, wg_ref, wq_ref, wkv_ref, sel_ref,
                       qf_ref, kf_ref, vf_ref, qm_ref, km_ref, vm_ref, gf_ref, gm_ref,
                       carry_ref):
    tm = x_ref.shape[1]

    @pl.when(pl.program_id(1) == 0)
    def _():
        carry_ref[...] = jnp.zeros_like(carry_ref)

    x = x_ref[0]
    shift = mod_ref[0, 0:1, :]
    scale = mod_ref[0, 1:2, :]
    h = (_rms(x) * gpre_ref[...] * (1.0 + scale) + shift).astype(BF16)

    for gi, ref in enumerate((gf_ref, gm_ref)):
        g = _dot(h, wg_ref[:, gi * D_MODEL:(gi + 1) * D_MODEL])
        ref[0] = jax.nn.sigmoid(g).astype(ref.dtype)

    lane = lax.broadcasted_iota(jnp.int32, (tm, LANES), 1)
    cfull = cos_ref[0]
    ssign = sin_ref[0]

    p_qkv = _dot(h, wqkv_ref[...])
    vf_ref[0] = p_qkv[:, 2 * FOX_WIDTH:].astype(vf_ref.dtype)
    misc = _dot(h, wmisc_ref[...])

    logit = misc + bf_ref[...]
    logf = jnp.minimum(logit, 0.0) - jnp.log(1.0 + jnp.exp(-jnp.abs(logit)))
    cum = _cumsum_rows(logf) + carry_ref[0:1, :]
    carry_ref[0:1, :] = cum[tm - 1:tm, :]

    a = cum * math.log2(math.e)
    a_hi = a.astype(BF16).astype(F32)
    r1 = a - a_hi
    a_mid = r1.astype(BF16).astype(F32)
    a_lo = (r1 - a_mid).astype(BF16).astype(F32)
    z = jnp.where(lane < 8, a_hi,
                  jnp.where(lane < 16, a_mid,
                            jnp.where(lane < 24, a_lo,
                                      jnp.where(lane == 24, 1.0, 0.0))))
    aug = _dot(z.astype(BF16), sel_ref[...])
    pairs = HEADS // 2
    for hh in range(HEADS):
        pair = hh // 2
        keep = (lane < HEAD_DIM) if hh % 2 == 0 else (lane >= HEAD_DIM)
        xq = p_qkv[:, pair * LANES:(pair + 1) * LANES]
        xk = p_qkv[:, FOX_WIDTH + pair * LANES:FOX_WIDTH + (pair + 1) * LANES]
        aq = aug[:, pair * LANES:(pair + 1) * LANES]
        ak = aug[:, (pairs + pair) * LANES:(pairs + pair + 1) * LANES]
        qf_ref[0, hh] = jnp.where(keep, xq * _EXP2_FOX, aq).astype(qf_ref.dtype)
        kf_ref[0, hh] = jnp.where(keep, xk, ak).astype(kf_ref.dtype)

    cq = _dot(h, wcq_ref[...])
    nq = (_rms(cq) * gq_ref[...]).astype(BF16)
    qq = _dot(nq, wq_ref[...])
    for hh in range(HEADS):
        qa = qq[:, hh * LANES:(hh + 1) * LANES]
        qm_ref[0, hh] = ((qa * cfull + _rot_half_lanes(qa, lane) * ssign) * _EXP2_MLA).astype(qm_ref.dtype)

    ckv = _dot(h, wckv_ref[...])
    nkv = (_rms(ckv) * gkv_ref[...]).astype(BF16)
    kv = _dot(nkv, wkv_ref[...])
    vm_ref[0] = kv[:, HEADS * LANES:].astype(vm_ref.dtype)
    in_rope = (lane >= MLA_NOPE) & (lane < MLA_NOPE + MLA_ROPE)
    kpe = jnp.where(in_rope, misc * cfull + _rot_half_lanes(misc, lane) * ssign, 0.0)
    for hh in range(HEADS):
        km_ref[0, hh] = (kv[:, hh * LANES:(hh + 1) * LANES] + kpe).astype(km_ref.dtype)


def _prep_proj_weights(w_in, w_uq, w_ukv, b_forget):
    d = w_in.shape[0]
    sp = np.cumsum(IN_WIDTHS)[:-1]
    w_fq, w_fk, w_fv, w_f, w_cq, w_ckv, w_kr, w_gf, w_gm = jnp.split(w_in, [int(v) for v in sp], axis=1)
    zeros = lambda n: jnp.zeros((d, n), w_in.dtype)
    w_misc = jnp.concatenate([w_f, w_f, w_f, zeros(MLA_NOPE - 3 * HEADS), w_kr,
                              zeros(LANES - MLA_NOPE - MLA_ROPE)], axis=1).astype(BF16)
    w_qkv = w_in[:, :3 * FOX_WIDTH].astype(BF16)
    w_g = w_in[:, -2 * D_MODEL:].astype(BF16)

    r = w_uq.shape[0]
    uq = w_uq.reshape(r, HEADS, MLA_NOPE + MLA_ROPE)
    pad = jnp.zeros((r, HEADS, LANES - MLA_NOPE - MLA_ROPE), w_uq.dtype)
    w_q = jnp.concatenate([uq, pad], axis=2).reshape(r, HEADS * LANES).astype(BF16)

    rk = w_ukv.shape[0]
    ukv = w_ukv.reshape(rk, HEADS, MLA_NOPE + MLA_V)
    k_nope, v = ukv[:, :, :MLA_NOPE], ukv[:, :, MLA_NOPE:]
    w_kn = jnp.concatenate([k_nope, jnp.zeros((rk, HEADS, LANES - MLA_NOPE), w_ukv.dtype)], axis=2)
    w_kv = jnp.concatenate([w_kn.reshape(rk, HEADS * LANES), v.reshape(rk, HEADS * MLA_V)], axis=1).astype(BF16)

    bf3 = jnp.concatenate([b_forget, b_forget, b_forget,
                           jnp.zeros((LANES - 3 * HEADS,), b_forget.dtype)]).reshape(1, LANES)
    return (w_qkv, w_misc, w_cq.astype(BF16), w_ckv.astype(BF16), w_g, w_q, w_kv), bf3


def _decay_selector():
    pairs = HEADS // 2
    sel = np.zeros((LANES, 2 * pairs * LANES), np.float32)
    for hh in range(HEADS):
        base_q = (hh // 2) * LANES + (HEAD_DIM if hh % 2 == 0 else 0)
        base_k = (pairs + hh // 2) * LANES + (HEAD_DIM if hh % 2 == 0 else 0)
        for piece in range(3):
            sel[piece * HEADS + hh, base_q + piece] = 1.0
            sel[3 * HEADS, base_q + 3 + piece] = 1.0
            sel[3 * HEADS, base_k + piece] = 1.0
            sel[piece * HEADS + hh, base_k + 3 + piece] = -1.0
    return jnp.asarray(sel, BF16)


def _token_proj(x, mod, g_pre, cfull, ssign, weights, bf3, g_q, g_kv):
    bsz, seq, d = x.shape
    tm = TOKEN_TILE
    sel = _decay_selector()
    row = lambda b, i: (b, i, 0)
    head = lambda b, i: (b, 0, i, 0)
    head_shape = jax.ShapeDtypeStruct((bsz, HEADS, seq, LANES), BF16)
    head_spec = pl.BlockSpec((1, HEADS, tm, LANES), head)
    v_shape = jax.ShapeDtypeStruct((bsz, seq, FOX_WIDTH), BF16)
    v_spec = pl.BlockSpec((1, tm, FOX_WIDTH), row)
    g_shape = jax.ShapeDtypeStruct((bsz, seq, d), BF16)
    g_spec = pl.BlockSpec((1, tm, d), row)
    return pl.pallas_call(
        _token_proj_kernel,
        grid=(bsz, seq // tm),
        in_specs=[pl.BlockSpec((1, tm, d), row),
                  pl.BlockSpec((1, 6, d), lambda b, i: (b, 0, 0)),
                  _const_spec((1, d)), _const_spec((1, LANES)),
                  _const_spec((1, MLA_Q_LORA)), _const_spec((1, MLA_KV_LORA)),
                  pl.BlockSpec((1, tm, LANES), row), pl.BlockSpec((1, tm, LANES), row)]
                 + [_const_spec(w.shape) for w in weights] + [_const_spec(sel.shape)],
        out_specs=[head_spec, head_spec, v_spec, head_spec, head_spec, v_spec, g_spec, g_spec],
        out_shape=[head_shape, head_shape, v_shape, head_shape, head_shape, v_shape, g_shape, g_shape],
        scratch_shapes=[pltpu.VMEM((8, LANES), F32)],
        compiler_params=pltpu.CompilerParams(dimension_semantics=("arbitrary", "arbitrary"),
                                             vmem_limit_bytes=VMEM_LIMIT),
        name="token_proj",
    )(x, mod, g_pre.reshape(1, d), bf3, g_q.reshape(1, -1), g_kv.reshape(1, -1), cfull, ssign,
      *weights, sel)


def _attn_kernel(q_ref, k_ref, v_ref, o_ref, sa_ref, sb_ref, m_ref, alpha_ref, acc_ref,
                 *, tile, chunk):
    seq = v_ref.shape[1]
    nq = seq // tile
    half = tile // 2
    lane = lax.broadcasted_iota(jnp.int32, (tile, LANES), 1)
    rows = lax.broadcasted_iota(jnp.int32, (tile, tile), 0)
    cols = lax.broadcasted_iota(jnp.int32, (tile, tile), 1)
    causal = cols <= rows
    nt = (((1,), (1,)), ((), ()))
    sum_lane = (HEAD_DIM, 0)
    one_cols = [jnp.where(lane == sum_lane[hh], 1.0, 0.0).astype(BF16) for hh in range(2)]
    keep = (lane < HEAD_DIM, lane >= HEAD_DIM)

    def values(v, hh):
        return jnp.where(keep[hh], v, one_cols[hh])

    def issue_scores(qi, j, s_out):
        for hh in range(2):
            q = q_ref[0, hh, qi * tile:(qi + 1) * tile, :]
            k = k_ref[0, hh, j * tile:(j + 1) * tile, :]
            if j == qi:
                s_out[hh, :half, :half] = lax.dot_general(q[:half], k[:half], nt, preferred_element_type=F32)
                s_out[hh, half:, :] = lax.dot_general(q[half:], k, nt, preferred_element_type=F32)
            else:
                s_out[hh] = lax.dot_general(q, k, nt, preferred_element_type=F32)

    def softmax_pv(s_in, hh, vh, masked):
        p_parts = []
        for c0 in range(0, tile, chunk):
            width = half if (masked and c0 < half) else tile
            cols_ = []
            for k0 in range(0, width, LANES):
                if masked and k0 > c0 + chunk - 1:
                    continue
                cb = s_in[hh, c0:c0 + chunk, k0:k0 + LANES]
                if masked and k0 + LANES - 1 > c0:
                    cb = jnp.where(causal[c0:c0 + chunk, k0:k0 + LANES], cb, -jnp.inf)
                cols_.append(cb)
            mx = functools.reduce(jnp.maximum, cols_)
            m_old = m_ref[hh, c0:c0 + chunk, :]
            mc = jnp.maximum(m_old, jnp.broadcast_to(jnp.max(mx, axis=1, keepdims=True), (chunk, LANES)))
            dead = [jnp.zeros((chunk, LANES), BF16)] * (width // LANES - len(cols_))
            p_parts.append(jnp.concatenate(
                [jnp.exp2(cb - mc).astype(BF16) for cb in cols_] + dead, axis=1))
            m_ref[hh, c0:c0 + chunk, :] = mc
            alpha_ref[hh, c0:c0 + chunk, :] = jnp.exp2(m_old - mc)
        if masked:
            n_top = half // chunk
            pv = jnp.concatenate([_dot(jnp.concatenate(p_parts[:n_top], axis=0), vh[:half]),
                                  _dot(jnp.concatenate(p_parts[n_top:], axis=0), vh)], axis=0)
        else:
            pv = _dot(jnp.concatenate(p_parts, axis=0), vh)
        acc_ref[hh] = alpha_ref[hh] * acc_ref[hh] + pv

    def reset_state():
        m_ref[...] = jnp.full(m_ref.shape, -jnp.inf, F32)
        acc_ref[...] = jnp.zeros(acc_ref.shape, F32)

    blocks = [(qi, j) for qi in range(nq) for j in range(qi + 1)]
    bufs = (sa_ref, sb_ref)
    reset_state()
    issue_scores(0, 0, bufs[0])
    for t, (qi, j) in enumerate(blocks):
        s_in, s_out = bufs[t % 2], bufs[(t + 1) % 2]
        if t + 1 < len(blocks):
            issue_scores(*blocks[t + 1], s_out)
        vj = v_ref[0, j * tile:(j + 1) * tile, :]
        for hh in range(2):
            softmax_pv(s_in, hh, values(vj, hh), j == qi)
        if j == qi:
            outs = []
            for hh in range(2):
                acc = acc_ref[hh]
                outs.append(acc * (1.0 / acc[:, sum_lane[hh]:sum_lane[hh] + 1]))
            o_ref[0, qi * tile:(qi + 1) * tile, :] = jnp.where(keep[0], outs[0], outs[1]).astype(o_ref.dtype)
            if qi + 1 < nq:
                reset_state()


def _causal_attn(q, k, v):
    bsz, heads, seq, _ = q.shape
    qk_spec = pl.BlockSpec((1, 2, seq, LANES), lambda b, p: (b, p, 0, 0))
    v_spec = pl.BlockSpec((1, seq, LANES), lambda b, p: (b, 0, p))
    kern = functools.partial(_attn_kernel, tile=ATTN_TILE, chunk=ATTN_CHUNK)
    return pl.pallas_call(
        kern,
        grid=(bsz, heads // 2),
        in_specs=[qk_spec, qk_spec, v_spec],
        out_specs=v_spec,
        out_shape=jax.ShapeDtypeStruct(v.shape, BF16),
        scratch_shapes=[pltpu.VMEM((2, ATTN_TILE, ATTN_TILE), F32),
                        pltpu.VMEM((2, ATTN_TILE, ATTN_TILE), F32),
                        pltpu.VMEM((2, ATTN_TILE, LANES), F32),
                        pltpu.VMEM((2, ATTN_TILE, LANES), F32),
                        pltpu.VMEM((2, ATTN_TILE, LANES), F32)],
        compiler_params=pltpu.CompilerParams(dimension_semantics=("arbitrary", "arbitrary"),
                                             vmem_limit_bytes=VMEM_LIMIT),
        name="causal_attn",
    )(q, k, v)


def _merge_kernel(x_ref, oa_ref, ob_ref, gf_ref, gm_ref, mod_ref, gpost_ref,
                  wpf_ref, wpm_ref, wout_ref, o_ref):
    pa = _dot(oa_ref[0], wpf_ref[...])
    pb = _dot(ob_ref[0], wpm_ref[...])
    merged = gf_ref[0].astype(F32) * pa + gm_ref[0].astype(F32) * pb
    y = _dot(merged.astype(BF16), wout_ref[...])
    gate = mod_ref[0, 2:3, :]
    o_ref[0] = x_ref[0] + gate * (_rms(y) * gpost_ref[...])


def _merge_out(x, o_a, o_b, gf, gm, mod, g_post, w_pf, w_pm, w_out):
    bsz, seq, d = x.shape
    tm = TOKEN_TILE
    row = lambda b, i: (b, i, 0)
    return pl.pallas_call(
        _merge_kernel,
        grid=(bsz, seq // tm),
        in_specs=[pl.BlockSpec((1, tm, d), row),
                  pl.BlockSpec((1, tm, FOX_WIDTH), row), pl.BlockSpec((1, tm, FOX_WIDTH), row),
                  pl.BlockSpec((1, tm, d), row), pl.BlockSpec((1, tm, d), row),
                  pl.BlockSpec((1, 6, d), lambda b, i: (b, 0, 0)),
                  _const_spec((1, d)),
                  _const_spec(w_pf.shape), _const_spec(w_pm.shape), _const_spec(w_out.shape)],
        out_specs=pl.BlockSpec((1, tm, d), row),
        out_shape=jax.ShapeDtypeStruct(x.shape, F32),
        compiler_params=pltpu.CompilerParams(dimension_semantics=("arbitrary", "arbitrary"),
                                             vmem_limit_bytes=VMEM_LIMIT),
        name="merge_out",
    )(x, o_a, o_b, gf, gm, mod, g_post.reshape(1, d), w_pf, w_pm, w_out)


def _ffn_kernel(x_ref, mod_ref, gpre_ref, gpost_ref, win_ref, wout_ref, o_ref):
    x = x_ref[0]
    shift = mod_ref[0, 3:4, :]
    scale = mod_ref[0, 4:5, :]
    gate = mod_ref[0, 5:6, :]
    h = (_rms(x) * gpre_ref[...] * (1.0 + scale) + shift).astype(BF16)
    y = None
    for lo in range(0, D_FF, FFN_CHUNK):
        w = min(FFN_CHUNK, D_FF - lo)
        g = _dot(h, win_ref[:, lo:lo + w])
        u = _dot(h, win_ref[:, D_FF + lo:D_FF + lo + w])
        act = (g * jax.nn.sigmoid(g) * u).astype(BF16)
        part = _dot(act, wout_ref[lo:lo + w, :])
        y = part if y is None else y + part
    o_ref[0] = x + gate * (_rms(y) * gpost_ref[...])


def _ffn(x, mod, g_pre, g_post, w_in, w_out):
    bsz, seq, d = x.shape
    tm = TOKEN_TILE
    row = lambda b, i: (b, i, 0)
    return pl.pallas_call(
        _ffn_kernel,
        grid=(bsz, seq // tm),
        in_specs=[pl.BlockSpec((1, tm, d), row),
                  pl.BlockSpec((1, 6, d), lambda b, i: (b, 0, 0)),
                  _const_spec((1, d)), _const_spec((1, d)),
                  _const_spec(w_in.shape), _const_spec(w_out.shape)],
        out_specs=pl.BlockSpec((1, tm, d), row),
        out_shape=jax.ShapeDtypeStruct(x.shape, F32),
        compiler_params=pltpu.CompilerParams(dimension_semantics=("arbitrary", "arbitrary"),
                                             vmem_limit_bytes=VMEM_LIMIT),
        name="ffn",
    )(x, mod, g_pre.reshape(1, d), g_post.reshape(1, d), w_in, w_out)


def kernel(x, c, positions, w_ada, b_ada, g_pre_mix, g_post_mix, g_pre_ffn, g_post_ffn, w_in, b_forget,
           g_q_lora, w_uq, g_kv_lora, w_ukv, w_proj_fox, w_proj_mla, w_out, w_ffn_in, w_ffn_out):
    bsz, seq, d = x.shape
    depth = w_ada.shape[0]
    cfull, ssign = _rope_tables(positions)
    for l in range(depth):
        mod = _adaln_mod(c, w_ada[l], b_ada[l]).reshape(bsz, 6, d)
        weights, bf3 = _prep_proj_weights(w_in[l], w_uq[l], w_ukv[l], b_forget[l])
        qf, kf, vf, qm, km, vm, gf, gm = _token_proj(
            x, mod, g_pre_mix[l], cfull, ssign, weights, bf3, g_q_lora[l], g_kv_lora[l])
        o_a = _causal_attn(qf, kf, vf)
        o_b = _causal_attn(qm, km, vm)
        x = _merge_out(x, o_a, o_b, gf, gm, mod, g_post_mix[l],
                       w_proj_fox[l].astype(BF16), w_proj_mla[l].astype(BF16), w_out[l].astype(BF16))
        x = _ffn(x, mod, g_pre_ffn[l], g_post_ffn[l], w_ffn_in[l].astype(BF16), w_ffn_out[l].astype(BF16))
    return x
```

```python
import functools
import math

import jax
import jax.numpy as jnp
import numpy as np
from jax import lax
from jax.experimental import pallas as pl
from jax.experimental.pallas import tpu as pltpu

D_MODEL = 1024
HEADS = 8
HEAD_DIM = 64
FOX_WIDTH = HEADS * HEAD_DIM
MLA_NOPE = 64
MLA_ROPE = 32
MLA_V = 64
MLA_Q_LORA = 768
MLA_KV_LORA = 256
D_FF = 2816
ROPE_THETA = 10000.0
NORM_EPS = 1e-6
IN_WIDTHS = (FOX_WIDTH, FOX_WIDTH, FOX_WIDTH, HEADS, MLA_Q_LORA, MLA_KV_LORA, MLA_ROPE, D_MODEL, D_MODEL)

LANES = 128
TOKEN_TILE = 512
ATTN_TILE = 512
ATTN_CHUNK = 32
FFN_CHUNK = 512
VMEM_LIMIT = 56 * 1024 * 1024

_EXP2_FOX = math.log2(math.e) / math.sqrt(HEAD_DIM)
_EXP2_MLA = math.log2(math.e) / math.sqrt(MLA_NOPE + MLA_ROPE)

F32 = jnp.float32
BF16 = jnp.bfloat16


def _const_spec(shape):
    zeros = (0,) * len(shape)
    return pl.BlockSpec(shape, lambda *_: zeros, pipeline_mode=pl.Buffered(1))


def _rms(x):
    return x * lax.rsqrt(jnp.mean(x * x, axis=-1, keepdims=True) + NORM_EPS)


def _dot(a, b):
    return jnp.dot(a, b, preferred_element_type=F32)


def _adaln_kernel(c_ref, w_ref, b_ref, o_ref):
    c = c_ref[...]
    sc = c * jax.nn.sigmoid(c)
    o_ref[...] = jnp.dot(sc, w_ref[...], preferred_element_type=F32,
                         precision=lax.Precision.HIGHEST) + b_ref[...]


def _adaln_mod(c, w_ada, b_ada):
    bsz, d = c.shape
    n = w_ada.shape[1]
    tn = 1024
    return pl.pallas_call(
        _adaln_kernel,
        grid=(n // tn,),
        in_specs=[pl.BlockSpec((bsz, d), lambda j: (0, 0)),
                  pl.BlockSpec((d, tn), lambda j: (0, j)),
                  pl.BlockSpec((1, tn), lambda j: (0, j))],
        out_specs=pl.BlockSpec((bsz, tn), lambda j: (0, j)),
        out_shape=jax.ShapeDtypeStruct((bsz, n), F32),
        name="adaln_mod",
    )(c, w_ada, b_ada.reshape(1, n))


def _rope_kernel(pos_ref, invf_ref, cos_ref, sin_ref):
    ang = pos_ref[...] * invf_ref[...]
    cos_ref[...] = jnp.cos(ang)
    sin_ref[...] = jnp.sin(ang)


def _rope_tables(positions):
    bsz, seq = positions.shape
    half = MLA_ROPE // 2
    inv_freq = 1.0 / (ROPE_THETA ** (np.arange(0, MLA_ROPE, 2, dtype=np.float32) / MLA_ROPE))
    rows = seq * half // LANES
    pos_rep = jnp.repeat(positions.astype(F32), half, axis=1).reshape(bsz, rows, LANES)
    invf = jnp.asarray(np.tile(inv_freq.astype(np.float32), LANES // half)).reshape(1, 1, LANES)
    spec = pl.BlockSpec((1, rows, LANES), lambda b: (b, 0, 0))
    cos, sin = pl.pallas_call(
        _rope_kernel,
        grid=(bsz,),
        in_specs=[spec, pl.BlockSpec((1, 1, LANES), lambda b: (0, 0, 0))],
        out_specs=[spec, spec],
        out_shape=[jax.ShapeDtypeStruct((bsz, rows, LANES), F32)] * 2,
        name="rope_tables",
    )(pos_rep, invf)
    cos = cos.reshape(bsz, seq, half)
    sin = sin.reshape(bsz, seq, half)
    ones = jnp.ones((bsz, seq, MLA_NOPE), F32)
    zq = jnp.zeros((bsz, seq, MLA_NOPE), F32)
    zpad = jnp.zeros((bsz, seq, LANES - MLA_NOPE - MLA_ROPE), F32)
    cfull = jnp.concatenate([ones, cos, cos, zpad], axis=-1)
    ssign = jnp.concatenate([zq, -sin, sin, zpad], axis=-1)
    return cfull, ssign


def _cumsum_rows(x):
    n = x.shape[0]
    row = lax.broadcasted_iota(jnp.int32, x.shape, 0)
    d = 1
    while d < n:
        x = x + jnp.where(row >= d, pltpu.roll(x, d, axis=0), 0.0)
        d *= 2
    return x


def _rot_half_lanes(x, lane):
    half = MLA_ROPE // 2
    return jnp.where(lane < MLA_NOPE + half, pltpu.roll(x, LANES - half, axis=1), pltpu.roll(x, half, axis=1))


def _token_proj_kernel(x_ref, mod_ref, gpre_ref, bf_ref, gq_ref, gkv_ref, cos_ref, sin_ref,
                       wqkv_ref, wmisc_ref, wcq_ref, wckv_ref, wg_ref, wq_ref, wkv_ref, sel_ref,
                       qf_ref, kf_ref, vf_ref, qm_ref, km_ref, vm_ref, gf_ref, gm_ref,
                       carry_ref):
    tm = x_ref.shape[1]

    @pl.when(pl.program_id(1) == 0)
    def _():
        carry_ref[...] = jnp.zeros_like(carry_ref)

    x = x_ref[0]
    shift = mod_ref[0, 0:1, :]
    scale = mod_ref[0, 1:2, :]
    h = (_rms(x) * gpre_ref[...] * (1.0 + scale) + shift).astype(BF16)

    lane = lax.broadcasted_iota(jnp.int32, (tm, LANES), 1)
    cfull = cos_ref[0]
    ssign = sin_ref[0]

    misc = _dot(h, wmisc_ref[...])
    logit = misc + bf_ref[...]
    logf = jnp.minimum(logit, 0.0) - jnp.log(1.0 + jnp.exp(-jnp.abs(logit)))
    cum = _cumsum_rows(logf) + carry_ref[0:1, :]
    carry_ref[0:1, :] = cum[tm - 1:tm, :]

    for gi, ref in enumerate((gf_ref, gm_ref)):
        g = _dot(h, wg_ref[:, gi * D_MODEL:(gi + 1) * D_MODEL])
        ref[0] = jax.nn.sigmoid(g).astype(ref.dtype)

    cq = _dot(h, wcq_ref[...])
    nq = (_rms(cq) * gq_ref[...]).astype(BF16)
    qq = _dot(nq, wq_ref[...])
    for hh in range(HEADS):
        qa = qq[:, hh * LANES:(hh + 1) * LANES]
        qm_ref[0, hh] = ((qa * cfull + _rot_half_lanes(qa, lane) * ssign) * _EXP2_MLA).astype(qm_ref.dtype)

    ckv = _dot(h, wckv_ref[...])
    nkv = (_rms(ckv) * gkv_ref[...]).astype(BF16)
    kv = _dot(nkv, wkv_ref[...])
    vm_ref[0] = kv[:, HEADS * LANES:].astype(vm_ref.dtype)
    in_rope = (lane >= MLA_NOPE) & (lane < MLA_NOPE + MLA_ROPE)
    kpe = jnp.where(in_rope, misc * cfull + _rot_half_lanes(misc, lane) * ssign, 0.0)
    for hh in range(HEADS):
        km_ref[0, hh] = (kv[:, hh * LANES:(hh + 1) * LANES] + kpe).astype(km_ref.dtype)

    p_qkv = _dot(h, wqkv_ref[...])
    vf_ref[0] = p_qkv[:, 2 * FOX_WIDTH:].astype(vf_ref.dtype)

    a = cum * math.log2(math.e)
    a_hi = a.astype(BF16).astype(F32)
    r1 = a - a_hi
    a_mid = r1.astype(BF16).astype(F32)
    a_lo = (r1 - a_mid).astype(BF16).astype(F32)
    z = jnp.where(lane < 8, a_hi,
                  jnp.where(lane < 16, a_mid,
                            jnp.where(lane < 24, a_lo,
                                      jnp.where(lane == 24, 1.0, 0.0))))
    aug = _dot(z.astype(BF16), sel_ref[...])
    pairs = HEADS // 2
    for hh in range(HEADS):
        pair = hh // 2
        keep = (lane < HEAD_DIM) if hh % 2 == 0 else (lane >= HEAD_DIM)
        xq = p_qkv[:, pair * LANES:(pair + 1) * LANES]
        xk = p_qkv[:, FOX_WIDTH + pair * LANES:FOX_WIDTH + (pair + 1) * LANES]
        aq = aug[:, pair * LANES:(pair + 1) * LANES]
        ak = aug[:, (pairs + pair) * LANES:(pairs + pair + 1) * LANES]
        qf_ref[0, hh] = jnp.where(keep, xq * _EXP2_FOX, aq).astype(qf_ref.dtype)
        kf_ref[0, hh] = jnp.where(keep, xk, ak).astype(kf_ref.dtype)


def _prep_proj_weights(w_in, w_uq, w_ukv, b_forget):
    d = w_in.shape[0]
    sp = np.cumsum(IN_WIDTHS)[:-1]
    w_fq, w_fk, w_fv, w_f, w_cq, w_ckv, w_kr, w_gf, w_gm = jnp.split(w_in, [int(v) for v in sp], axis=1)
    zeros = lambda n: jnp.zeros((d, n), w_in.dtype)
    w_misc = jnp.concatenate([w_f, w_f, w_f, zeros(MLA_NOPE - 3 * HEADS), w_kr,
                              zeros(LANES - MLA_NOPE - MLA_ROPE)], axis=1).astype(BF16)
    w_qkv = w_in[:, :3 * FOX_WIDTH].astype(BF16)
    w_g = w_in[:, -2 * D_MODEL:].astype(BF16)

    r = w_uq.shape[0]
    uq = w_uq.reshape(r, HEADS, MLA_NOPE + MLA_ROPE)
    pad = jnp.zeros((r, HEADS, LANES - MLA_NOPE - MLA_ROPE), w_uq.dtype)
    w_q = jnp.concatenate([uq, pad], axis=2).reshape(r, HEADS * LANES).astype(BF16)

    rk = w_ukv.shape[0]
    ukv = w_ukv.reshape(rk, HEADS, MLA_NOPE + MLA_V)
    k_nope, v = ukv[:, :, :MLA_NOPE], ukv[:, :, MLA_NOPE:]
    w_kn = jnp.concatenate([k_nope, jnp.zeros((rk, HEADS, LANES - MLA_NOPE), w_ukv.dtype)], axis=2)
    w_kv = jnp.concatenate([w_kn.reshape(rk, HEADS * LANES), v.reshape(rk, HEADS * MLA_V)], axis=1).astype(BF16)

    bf3 = jnp.concatenate([b_forget, b_forget, b_forget,
                           jnp.zeros((LANES - 3 * HEADS,), b_forget.dtype)]).reshape(1, LANES)
    return (w_qkv, w_misc, w_cq.astype(BF16), w_ckv.astype(BF16), w_g, w_q, w_kv), bf3


def _decay_selector():
    pairs = HEADS // 2
    sel = np.zeros((LANES, 2 * pairs * LANES), np.float32)
    for hh in range(HEADS):
        base_q = (hh // 2) * LANES + (HEAD_DIM if hh % 2 == 0 else 0)
        base_k = (pairs + hh // 2) * LANES + (HEAD_DIM if hh % 2 == 0 else 0)
        for piece in range(3):
            sel[piece * HEADS + hh, base_q + piece] = 1.0
            sel[3 * HEADS, base_q + 3 + piece] = 1.0
            sel[3 * HEADS, base_k + piece] = 1.0
            sel[piece * HEADS + hh, base_k + 3 + piece] = -1.0
    return jnp.asarray(sel, BF16)


def _token_proj(x, mod, g_pre, cfull, ssign, weights, bf3, g_q, g_kv):
    bsz, seq, d = x.shape
    tm = TOKEN_TILE
    sel = _decay_selector()
    row = lambda b, i: (b, i, 0)
    head = lambda b, i: (b, 0, i, 0)
    head_shape = jax.ShapeDtypeStruct((bsz, HEADS, seq, LANES), BF16)
    head_spec = pl.BlockSpec((1, HEADS, tm, LANES), head)
    v_shape = jax.ShapeDtypeStruct((bsz, seq, FOX_WIDTH), BF16)
    v_spec = pl.BlockSpec((1, tm, FOX_WIDTH), row)
    g_shape = jax.ShapeDtypeStruct((bsz, seq, d), BF16)
    g_spec = pl.BlockSpec((1, tm, d), row)
    return pl.pallas_call(
        _token_proj_kernel,
        grid=(bsz, seq // tm),
        in_specs=[pl.BlockSpec((1, tm, d), row),
                  pl.BlockSpec((1, 6, d), lambda b, i: (b, 0, 0)),
                  _const_spec((1, d)), _const_spec((1, LANES)),
                  _const_spec((1, MLA_Q_LORA)), _const_spec((1, MLA_KV_LORA)),
                  pl.BlockSpec((1, tm, LANES), row), pl.BlockSpec((1, tm, LANES), row)]
                 + [_const_spec(w.shape) for w in weights] + [_const_spec(sel.shape)],
        out_specs=[head_spec, head_spec, v_spec, head_spec, head_spec, v_spec, g_spec, g_spec],
        out_shape=[head_shape, head_shape, v_shape, head_shape, head_shape, v_shape, g_shape, g_shape],
        scratch_shapes=[pltpu.VMEM((8, LANES), F32)],
        compiler_params=pltpu.CompilerParams(dimension_semantics=("arbitrary", "arbitrary"),
                                             vmem_limit_bytes=VMEM_LIMIT),
        name="token_proj",
    )(x, mod, g_pre.reshape(1, d), bf3, g_q.reshape(1, -1), g_kv.reshape(1, -1), cfull, ssign,
      *weights, sel)


def _attn_kernel(q_ref, k_ref, v_ref, o_ref, sa_ref, sb_ref, m_ref, alpha_ref, acc_ref,
                 *, tile, chunk):
    seq = v_ref.shape[1]
    nq = seq // tile
    half = tile // 2
    lane = lax.broadcasted_iota(jnp.int32, (tile, LANES), 1)
    rows = lax.broadcasted_iota(jnp.int32, (tile, tile), 0)
    cols = lax.broadcasted_iota(jnp.int32, (tile, tile), 1)
    causal = cols <= rows
    nt = (((1,), (1,)), ((), ()))
    sum_lane = (HEAD_DIM, 0)
    one_cols = [jnp.where(lane == sum_lane[hh], 1.0, 0.0).astype(BF16) for hh in range(2)]
    keep = (lane < HEAD_DIM, lane >= HEAD_DIM)

    def values(v, hh):
        return jnp.where(keep[hh], v, one_cols[hh])

    def issue_scores(qi, j, s_out):
        for hh in range(2):
            q = q_ref[0, hh, qi * tile:(qi + 1) * tile, :]
            k = k_ref[0, hh, j * tile:(j + 1) * tile, :]
            if j == qi:
                s_out[hh, :half, :half] = lax.dot_general(q[:half], k[:half], nt, preferred_element_type=F32)
                s_out[hh, half:, :] = lax.dot_general(q[half:], k, nt, preferred_element_type=F32)
            else:
                s_out[hh] = lax.dot_general(q, k, nt, preferred_element_type=F32)

    def softmax_pv(s_in, hh, vh, masked):
        p_parts = []
        for c0 in range(0, tile, chunk):
            width = half if (masked and c0 < half) else tile
            cols_ = []
            for k0 in range(0, width, LANES):
                if masked and k0 > c0 + chunk - 1:
                    continue
                cb = s_in[hh, c0:c0 + chunk, k0:k0 + LANES]
                if masked and k0 + LANES - 1 > c0:
                    cb = jnp.where(causal[c0:c0 + chunk, k0:k0 + LANES], cb, -jnp.inf)
                cols_.append(cb)
            mx = functools.reduce(jnp.maximum, cols_)
            m_old = m_ref[hh, c0:c0 + chunk, :]
            mc = jnp.maximum(m_old, jnp.broadcast_to(jnp.max(mx, axis=1, keepdims=True), (chunk, LANES)))
            dead = [jnp.zeros((chunk, LANES), BF16)] * (width // LANES - len(cols_))
            p_parts.append(jnp.concatenate(
                [jnp.exp2(cb - mc).astype(BF16) for cb in cols_] + dead, axis=1))
            m_ref[hh, c0:c0 + chunk, :] = mc
            alpha_ref[hh, c0:c0 + chunk, :] = jnp.exp2(m_old - mc)
        if masked:
            n_top = half // chunk
            pv = jnp.concatenate([_dot(jnp.concatenate(p_parts[:n_top], axis=0), vh[:half]),
                                  _dot(jnp.concatenate(p_parts[n_top:], axis=0), vh)], axis=0)
        else:
            pv = _dot(jnp.concatenate(p_parts, axis=0), vh)
        acc_ref[hh] = alpha_ref[hh] * acc_ref[hh] + pv

    def reset_state():
        m_ref[...] = jnp.full(m_ref.shape, -jnp.inf, F32)
        acc_ref[...] = jnp.zeros(acc_ref.shape, F32)

    blocks = [(qi, j) for qi in range(nq) for j in range(qi + 1)]
    bufs = (sa_ref, sb_ref)
    reset_state()
    issue_scores(0, 0, bufs[0])
    for t, (qi, j) in enumerate(blocks):
        s_in, s_out = bufs[t % 2], bufs[(t + 1) % 2]
        if t + 1 < len(blocks):
            issue_scores(*blocks[t + 1], s_out)
        vj = v_ref[0, j * tile:(j + 1) * tile, :]
        for hh in range(2):
            softmax_pv(s_in, hh, values(vj, hh), j == qi)
        if j == qi:
            outs = []
            for hh in range(2):
                acc = acc_ref[hh]
                outs.append(acc * (1.0 / acc[:, sum_lane[hh]:sum_lane[hh] + 1]))
            o_ref[0, qi * tile:(qi + 1) * tile, :] = jnp.where(keep[0], outs[0], outs[1]).astype(o_ref.dtype)
            if qi + 1 < nq:
                reset_state()


def _causal_attn(q, k, v):
    bsz, heads, seq, _ = q.shape
    qk_spec = pl.BlockSpec((1, 2, seq, LANES), lambda b, p: (b, p, 0, 0))
    v_spec = pl.BlockSpec((1, seq, LANES), lambda b, p: (b, 0, p))
    kern = functools.partial(_attn_kernel, tile=ATTN_TILE, chunk=ATTN_CHUNK)
    return pl.pallas_call(
        kern,
        grid=(bsz, heads // 2),
        in_specs=[qk_spec, qk_spec, v_spec],
        out_specs=v_spec,
        out_shape=jax.ShapeDtypeStruct(v.shape, BF16),
        scratch_shapes=[pltpu.VMEM((2, ATTN_TILE, ATTN_TILE), F32),
                        pltpu.VMEM((2, ATTN_TILE, ATTN_TILE), F32),
                        pltpu.VMEM((2, ATTN_TILE, LANES), F32),
                        pltpu.VMEM((2, ATTN_TILE, LANES), F32),
                        pltpu.VMEM((2, ATTN_TILE, LANES), F32)],
        compiler_params=pltpu.CompilerParams(dimension_semantics=("arbitrary", "arbitrary"),
                                             vmem_limit_bytes=VMEM_LIMIT),
        name="causal_attn",
    )(q, k, v)


def _mix_ffn_kernel(x_ref, oa_ref, ob_ref, gf_ref, gm_ref, mod_ref, gpost_mix_ref, gpre_ffn_ref,
                    gpost_ffn_ref, wpf_ref, wpm_ref, wout_ref, win_ref, wdown_ref, o_ref):
    pa = _dot(oa_ref[0], wpf_ref[...])
    pb = _dot(ob_ref[0], wpm_ref[...])
    merged = gf_ref[0].astype(F32) * pa + gm_ref[0].astype(F32) * pb
    y = _dot(merged.astype(BF16), wout_ref[...])
    x = x_ref[0] + mod_ref[0, 2:3, :] * (_rms(y) * gpost_mix_ref[...])

    shift = mod_ref[0, 3:4, :]
    scale = mod_ref[0, 4:5, :]
    gate = mod_ref[0, 5:6, :]
    h = (_rms(x) * gpre_ffn_ref[...] * (1.0 + scale) + shift).astype(BF16)
    y = None
    for lo in range(0, D_FF, FFN_CHUNK):
        w = min(FFN_CHUNK, D_FF - lo)
        g = _dot(h, win_ref[:, lo:lo + w])
        u = _dot(h, win_ref[:, D_FF + lo:D_FF + lo + w])
        act = (g * jax.nn.sigmoid(g) * u).astype(BF16)
        part = _dot(act, wdown_ref[lo:lo + w, :])
        y = part if y is None else y + part
    o_ref[0] = x + gate * (_rms(y) * gpost_ffn_ref[...])


def _mix_ffn(x, o_a, o_b, gf, gm, mod, g_post_mix, g_pre_ffn, g_post_ffn, w_pf, w_pm, w_out, w_in, w_down):
    bsz, seq, d = x.shape
    tm = TOKEN_TILE
    row = lambda b, i: (b, i, 0)
    weights = (w_pf, w_pm, w_out, w_in, w_down)
    return pl.pallas_call(
        _mix_ffn_kernel,
        grid=(bsz, seq // tm),
        in_specs=[pl.BlockSpec((1, tm, d), row),
                  pl.BlockSpec((1, tm, FOX_WIDTH), row), pl.BlockSpec((1, tm, FOX_WIDTH), row),
                  pl.BlockSpec((1, tm, d), row), pl.BlockSpec((1, tm, d), row),
                  pl.BlockSpec((1, 6, d), lambda b, i: (b, 0, 0)),
                  _const_spec((1, d)), _const_spec((1, d)), _const_spec((1, d))]
                 + [_const_spec(w.shape) for w in weights],
        out_specs=pl.BlockSpec((1, tm, d), row),
        out_shape=jax.ShapeDtypeStruct(x.shape, F32),
        compiler_params=pltpu.CompilerParams(dimension_semantics=("arbitrary", "arbitrary"),
                                             vmem_limit_bytes=VMEM_LIMIT),
        name="mix_ffn",
    )(x, o_a, o_b, gf, gm, mod, g_post_mix.reshape(1, d), g_pre_ffn.reshape(1, d), g_post_ffn.reshape(1, d),
      *weights)


def kernel(x, c, positions, w_ada, b_ada, g_pre_mix, g_post_mix, g_pre_ffn, g_post_ffn, w_in, b_forget,
           g_q_lora, w_uq, g_kv_lora, w_ukv, w_proj_fox, w_proj_mla, w_out, w_ffn_in, w_ffn_out):
    bsz, seq, d = x.shape
    depth = w_ada.shape[0]
    cfull, ssign = _rope_tables(positions)
    for l in range(depth):
        mod = _adaln_mod(c, w_ada[l], b_ada[l]).reshape(bsz, 6, d)
        weights, bf3 = _prep_proj_weights(w_in[l], w_uq[l], w_ukv[l], b_forget[l])
        qf, kf, vf, qm, km, vm, gf, gm = _token_proj(
            x, mod, g_pre_mix[l], cfull, ssign, weights, bf3, g_q_lora[l], g_kv_lora[l])
        o_a = _causal_attn(qf, kf, vf)
        o_b = _causal_attn(qm, km, vm)
        x = _mix_ffn(x, o_a, o_b, gf, gm, mod, g_post_mix[l], g_pre_ffn[l], g_post_ffn[l],
                     w_proj_fox[l].astype(BF16), w_proj_mla[l].astype(BF16), w_out[l].astype(BF16),
                     w_ffn_in[l].astype(BF16), w_ffn_out[l].astype(BF16))
    return x
```

```python
import functools
import math

import jax
import jax.numpy as jnp
import numpy as np
from jax import lax
from jax.experimental import pallas as pl
from jax.experimental.pallas import tpu as pltpu

D_MODEL = 1024
HEADS = 8
HEAD_DIM = 64
FOX_WIDTH = HEADS * HEAD_DIM
MLA_NOPE = 64
MLA_ROPE = 32
MLA_V = 64
MLA_Q_LORA = 768
MLA_KV_LORA = 256
D_FF = 2816
ROPE_THETA = 10000.0
NORM_EPS = 1e-6
IN_WIDTHS = (FOX_WIDTH, FOX_WIDTH, FOX_WIDTH, HEADS, MLA_Q_LORA, MLA_KV_LORA, MLA_ROPE, D_MODEL, D_MODEL)

LANES = 128
TOKEN_TILE = 512
ATTN_TILE = 512
FFN_CHUNK = 512
VMEM_LIMIT = 56 * 1024 * 1024

_EXP2_FOX = math.log2(math.e) / math.sqrt(HEAD_DIM)
_EXP2_MLA = math.log2(math.e) / math.sqrt(MLA_NOPE + MLA_ROPE)

F32 = jnp.float32
BF16 = jnp.bfloat16


def _const_spec(shape):
    zeros = (0,) * len(shape)
    return pl.BlockSpec(shape, lambda *_: zeros, pipeline_mode=pl.Buffered(1))


def _rms(x):
    return x * lax.rsqrt(jnp.mean(x * x, axis=-1, keepdims=True) + NORM_EPS)


def _dot(a, b):
    return jnp.dot(a, b, preferred_element_type=F32)


def _adaln_kernel(c_ref, w_ref, b_ref, o_ref):
    c = c_ref[...]
    sc = c * jax.nn.sigmoid(c)
    o_ref[...] = jnp.dot(sc, w_ref[...], preferred_element_type=F32,
                         precision=lax.Precision.HIGHEST) + b_ref[...]


def _adaln_mod(c, w_ada, b_ada):
    bsz, d = c.shape
    n = w_ada.shape[1]
    tn = 1024
    return pl.pallas_call(
        _adaln_kernel,
        grid=(n // tn,),
        in_specs=[pl.BlockSpec((bsz, d), lambda j: (0, 0)),
                  pl.BlockSpec((d, tn), lambda j: (0, j)),
                  pl.BlockSpec((1, tn), lambda j: (0, j))],
        out_specs=pl.BlockSpec((bsz, tn), lambda j: (0, j)),
        out_shape=jax.ShapeDtypeStruct((bsz, n), F32),
        name="adaln_mod",
    )(c, w_ada, b_ada.reshape(1, n))


def _rope_kernel(pos_ref, invf_ref, cos_ref, sin_ref):
    ang = pos_ref[...] * invf_ref[...]
    cos_ref[...] = jnp.cos(ang)
    sin_ref[...] = jnp.sin(ang)


def _rope_tables(positions):
    bsz, seq = positions.shape
    half = MLA_ROPE // 2
    inv_freq = 1.0 / (ROPE_THETA ** (np.arange(0, MLA_ROPE, 2, dtype=np.float32) / MLA_ROPE))
    rows = seq * half // LANES
    pos_rep = jnp.repeat(positions.astype(F32), half, axis=1).reshape(bsz, rows, LANES)
    invf = jnp.asarray(np.tile(inv_freq.astype(np.float32), LANES // half)).reshape(1, 1, LANES)
    spec = pl.BlockSpec((1, rows, LANES), lambda b: (b, 0, 0))
    cos, sin = pl.pallas_call(
        _rope_kernel,
        grid=(bsz,),
        in_specs=[spec, pl.BlockSpec((1, 1, LANES), lambda b: (0, 0, 0))],
        out_specs=[spec, spec],
        out_shape=[jax.ShapeDtypeStruct((bsz, rows, LANES), F32)] * 2,
        name="rope_tables",
    )(pos_rep, invf)
    cos = cos.reshape(bsz, seq, half)
    sin = sin.reshape(bsz, seq, half)
    ones = jnp.ones((bsz, seq, MLA_NOPE), F32)
    zq = jnp.zeros((bsz, seq, MLA_NOPE), F32)
    zpad = jnp.zeros((bsz, seq, LANES - MLA_NOPE - MLA_ROPE), F32)
    cfull = jnp.concatenate([ones, cos, cos, zpad], axis=-1)
    ssign = jnp.concatenate([zq, -sin, sin, zpad], axis=-1)
    return cfull, ssign


def _cumsum_rows(x):
    n = x.shape[0]
    row = lax.broadcasted_iota(jnp.int32, x.shape, 0)
    d = 1
    while d < n:
        x = x + jnp.where(row >= d, pltpu.roll(x, d, axis=0), 0.0)
        d *= 2
    return x


def _rot_half_lanes(x, lane):
    half = MLA_ROPE // 2
    return jnp.where(lane < MLA_NOPE + half, pltpu.roll(x, LANES - half, axis=1), pltpu.roll(x, half, axis=1))


def _store_values_transposed(vt_ref, v):
    for pair in range(HEADS // 2):
        vt_ref[0, pair] = v[:, pair * LANES:(pair + 1) * LANES].T.astype(vt_ref.dtype)


def _token_proj_kernel(x_ref, mod_ref, gpre_ref, bf_ref, gq_ref, gkv_ref, cos_ref, sin_ref,
                       wqkv_ref, wmisc_ref, wcq_ref, wckv_ref, wg_ref, wq_ref, wkv_ref, sel_ref,
                       qf_ref, kf_ref, vf_ref, qm_ref, km_ref, vm_ref, gf_ref, gm_ref,
                       carry_ref):
    tm = x_ref.shape[1]

    @pl.when(pl.program_id(1) == 0)
    def _():
        carry_ref[...] = jnp.zeros_like(carry_ref)

    x = x_ref[0]
    shift = mod_ref[0, 0:1, :]
    scale = mod_ref[0, 1:2, :]
    h = (_rms(x) * gpre_ref[...] * (1.0 + scale) + shift).astype(BF16)

    lane = lax.broadcasted_iota(jnp.int32, (tm, LANES), 1)
    cfull = cos_ref[0]
    ssign = sin_ref[0]

    misc = _dot(h, wmisc_ref[...])
    logit = misc + bf_ref[...]
    logf = jnp.minimum(logit, 0.0) - jnp.log(1.0 + jnp.exp(-jnp.abs(logit)))
    cum = _cumsum_rows(logf) + carry_ref[0:1, :]
    carry_ref[0:1, :] = cum[tm - 1:tm, :]

    for gi, ref in enumerate((gf_ref, gm_ref)):
        g = _dot(h, wg_ref[:, gi * D_MODEL:(gi + 1) * D_MODEL])
        ref[0] = jax.nn.sigmoid(g).astype(ref.dtype)

    cq = _dot(h, wcq_ref[...])
    nq = (_rms(cq) * gq_ref[...]).astype(BF16)
    qq = _dot(nq, wq_ref[...])
    for hh in range(HEADS):
        qa = qq[:, hh * LANES:(hh + 1) * LANES]
        qm_ref[0, hh] = ((qa * cfull + _rot_half_lanes(qa, lane) * ssign) * _EXP2_MLA).astype(qm_ref.dtype)

    ckv = _dot(h, wckv_ref[...])
    nkv = (_rms(ckv) * gkv_ref[...]).astype(BF16)
    kv = _dot(nkv, wkv_ref[...])
    _store_values_transposed(vm_ref, kv[:, HEADS * LANES:])
    in_rope = (lane >= MLA_NOPE) & (lane < MLA_NOPE + MLA_ROPE)
    kpe = jnp.where(in_rope, misc * cfull + _rot_half_lanes(misc, lane) * ssign, 0.0)
    for hh in range(HEADS):
        km_ref[0, hh] = (kv[:, hh * LANES:(hh + 1) * LANES] + kpe).astype(km_ref.dtype)

    p_qkv = _dot(h, wqkv_ref[...])
    _store_values_transposed(vf_ref, p_qkv[:, 2 * FOX_WIDTH:])

    a = cum * math.log2(math.e)
    a_hi = a.astype(BF16).astype(F32)
    r1 = a - a_hi
    a_mid = r1.astype(BF16).astype(F32)
    a_lo = (r1 - a_mid).astype(BF16).astype(F32)
    z = jnp.where(lane < 8, a_hi,
                  jnp.where(lane < 16, a_mid,
                            jnp.where(lane < 24, a_lo,
                                      jnp.where(lane == 24, 1.0, 0.0))))
    aug = _dot(z.astype(BF16), sel_ref[...])
    pairs = HEADS // 2
    for hh in range(HEADS):
        pair = hh // 2
        keep = (lane < HEAD_DIM) if hh % 2 == 0 else (lane >= HEAD_DIM)
        xq = p_qkv[:, pair * LANES:(pair + 1) * LANES]
        xk = p_qkv[:, FOX_WIDTH + pair * LANES:FOX_WIDTH + (pair + 1) * LANES]
        aq = aug[:, pair * LANES:(pair + 1) * LANES]
        ak = aug[:, (pairs + pair) * LANES:(pairs + pair + 1) * LANES]
        qf_ref[0, hh] = jnp.where(keep, xq * _EXP2_FOX, aq).astype(qf_ref.dtype)
        kf_ref[0, hh] = jnp.where(keep, xk, ak).astype(kf_ref.dtype)


def _prep_proj_weights(w_in, w_uq, w_ukv, b_forget):
    d = w_in.shape[0]
    sp = np.cumsum(IN_WIDTHS)[:-1]
    w_fq, w_fk, w_fv, w_f, w_cq, w_ckv, w_kr, w_gf, w_gm = jnp.split(w_in, [int(v) for v in sp], axis=1)
    zeros = lambda n: jnp.zeros((d, n), w_in.dtype)
    w_misc = jnp.concatenate([w_f, w_f, w_f, zeros(MLA_NOPE - 3 * HEADS), w_kr,
                              zeros(LANES - MLA_NOPE - MLA_ROPE)], axis=1).astype(BF16)
    w_qkv = w_in[:, :3 * FOX_WIDTH].astype(BF16)
    w_g = w_in[:, -2 * D_MODEL:].astype(BF16)

    r = w_uq.shape[0]
    uq = w_uq.reshape(r, HEADS, MLA_NOPE + MLA_ROPE)
    pad = jnp.zeros((r, HEADS, LANES - MLA_NOPE - MLA_ROPE), w_uq.dtype)
    w_q = jnp.concatenate([uq, pad], axis=2).reshape(r, HEADS * LANES).astype(BF16)

    rk = w_ukv.shape[0]
    ukv = w_ukv.reshape(rk, HEADS, MLA_NOPE + MLA_V)
    k_nope, v = ukv[:, :, :MLA_NOPE], ukv[:, :, MLA_NOPE:]
    w_kn = jnp.concatenate([k_nope, jnp.zeros((rk, HEADS, LANES - MLA_NOPE), w_ukv.dtype)], axis=2)
    w_kv = jnp.concatenate([w_kn.reshape(rk, HEADS * LANES), v.reshape(rk, HEADS * MLA_V)], axis=1).astype(BF16)

    bf3 = jnp.concatenate([b_forget, b_forget, b_forget,
                           jnp.zeros((LANES - 3 * HEADS,), b_forget.dtype)]).reshape(1, LANES)
    return (w_qkv, w_misc, w_cq.astype(BF16), w_ckv.astype(BF16), w_g, w_q, w_kv), bf3


def _decay_selector():
    pairs = HEADS // 2
    sel = np.zeros((LANES, 2 * pairs * LANES), np.float32)
    for hh in range(HEADS):
        base_q = (hh // 2) * LANES + (HEAD_DIM if hh % 2 == 0 else 0)
        base_k = (pairs + hh // 2) * LANES + (HEAD_DIM if hh % 2 == 0 else 0)
        for piece in range(3):
            sel[piece * HEADS + hh, base_q + piece] = 1.0
            sel[3 * HEADS, base_q + 3 + piece] = 1.0
            sel[3 * HEADS, base_k + piece] = 1.0
            sel[piece * HEADS + hh, base_k + 3 + piece] = -1.0
    return jnp.asarray(sel, BF16)


def _token_proj(x, mod, g_pre, cfull, ssign, weights, bf3, g_q, g_kv):
    bsz, seq, d = x.shape
    tm = TOKEN_TILE
    sel = _decay_selector()
    row = lambda b, i: (b, i, 0)
    head = lambda b, i: (b, 0, i, 0)
    head_shape = jax.ShapeDtypeStruct((bsz, HEADS, seq, LANES), BF16)
    head_spec = pl.BlockSpec((1, HEADS, tm, LANES), head)
    v_shape = jax.ShapeDtypeStruct((bsz, HEADS // 2, LANES, seq), BF16)
    v_spec = pl.BlockSpec((1, HEADS // 2, LANES, tm), lambda b, i: (b, 0, 0, i))
    g_shape = jax.ShapeDtypeStruct((bsz, seq, d), BF16)
    g_spec = pl.BlockSpec((1, tm, d), row)
    return pl.pallas_call(
        _token_proj_kernel,
        grid=(bsz, seq // tm),
        in_specs=[pl.BlockSpec((1, tm, d), row),
                  pl.BlockSpec((1, 6, d), lambda b, i: (b, 0, 0)),
                  _const_spec((1, d)), _const_spec((1, LANES)),
                  _const_spec((1, MLA_Q_LORA)), _const_spec((1, MLA_KV_LORA)),
                  pl.BlockSpec((1, tm, LANES), row), pl.BlockSpec((1, tm, LANES), row)]
                 + [_const_spec(w.shape) for w in weights] + [_const_spec(sel.shape)],
        out_specs=[head_spec, head_spec, v_spec, head_spec, head_spec, v_spec, g_spec, g_spec],
        out_shape=[head_shape, head_shape, v_shape, head_shape, head_shape, v_shape, g_shape, g_shape],
        scratch_shapes=[pltpu.VMEM((8, LANES), F32)],
        compiler_params=pltpu.CompilerParams(dimension_semantics=("arbitrary", "arbitrary"),
                                             vmem_limit_bytes=VMEM_LIMIT),
        name="token_proj",
    )(x, mod, g_pre.reshape(1, d), bf3, g_q.reshape(1, -1), g_kv.reshape(1, -1), cfull, ssign,
      *weights, sel)


def _attn_kernel(q_ref, k_ref, vt_ref, o_ref, sa_ref, sb_ref, m_ref, acc_ref, *, tile):
    seq = vt_ref.shape[3]
    nq = seq // tile
    half = tile // 2
    nt = (((1,), (1,)), ((), ()))
    vrow = lax.broadcasted_iota(jnp.int32, (LANES, tile), 0)
    sum_row = (HEAD_DIM, 0)
    one_rows = [jnp.where(vrow == sum_row[hh], 1.0, 0.0).astype(BF16) for hh in range(2)]
    keep = (vrow < HEAD_DIM, vrow >= HEAD_DIM)
    krow = lax.broadcasted_iota(jnp.int32, (LANES, LANES), 0)
    qcol = lax.broadcasted_iota(jnp.int32, (LANES, LANES), 1)
    diag_ok = krow <= qcol

    def values_t(vt, hh):
        return jnp.where(keep[hh], vt, one_rows[hh])

    def issue_scores(qi, j, s_out):
        for hh in range(2):
            q = q_ref[0, hh, qi * tile:(qi + 1) * tile, :]
            k = k_ref[0, hh, j * tile:(j + 1) * tile, :]
            if j == qi:
                s_out[hh, :half, :] = lax.dot_general(k[:half], q, nt, preferred_element_type=F32)
                s_out[hh, half:, half:] = lax.dot_general(k[half:], q[half:], nt, preferred_element_type=F32)
            else:
                s_out[hh] = lax.dot_general(k, q, nt, preferred_element_type=F32)

    def softmax_pv(s_in, hh, vth, masked):
        alphas, p_cols = [], []
        for c0 in range(0, tile, LANES):
            n_keys = c0 + LANES if masked else tile
            n_fill = tile if (c0 >= half or not masked) else half
            m_old = m_ref[hh, :, c0:c0 + LANES]

            def strip(r0):
                sb = s_in[hh, r0:r0 + LANES, c0:c0 + LANES]
                return jnp.where(diag_ok, sb, -jnp.inf) if (masked and r0 == c0) else sb

            mx = functools.reduce(jnp.maximum, [strip(r0) for r0 in range(0, n_keys, LANES)])
            m_new = jnp.maximum(m_old, jnp.max(mx, axis=0, keepdims=True))
            parts = [jnp.exp2(strip(r0) - m_new).astype(BF16) for r0 in range(0, n_keys, LANES)]
            parts += [jnp.zeros((LANES, LANES), BF16)] * ((n_fill - n_keys) // LANES)
            p_cols.append(jnp.concatenate(parts, axis=0))
            m_ref[hh, :, c0:c0 + LANES] = m_new
            alphas.append(jnp.exp2(m_old - m_new))
        alpha = jnp.concatenate(alphas, axis=1)
        n_left = half // LANES
        if masked:
            pv = jnp.concatenate([_dot(vth[:, :half], jnp.concatenate(p_cols[:n_left], axis=1)),
                                  _dot(vth, jnp.concatenate(p_cols[n_left:], axis=1))], axis=1)
        else:
            pv = _dot(vth, jnp.concatenate(p_cols, axis=1))
        acc_ref[hh] = alpha * acc_ref[hh] + pv

    def reset_state():
        m_ref[...] = jnp.full(m_ref.shape, -jnp.inf, F32)
        acc_ref[...] = jnp.zeros(acc_ref.shape, F32)

    blocks = [(qi, j) for qi in range(nq) for j in range(qi + 1)]
    bufs = (sa_ref, sb_ref)
    reset_state()
    issue_scores(0, 0, bufs[0])
    for t, (qi, j) in enumerate(blocks):
        s_in, s_out = bufs[t % 2], bufs[(t + 1) % 2]
        if t + 1 < len(blocks):
            issue_scores(*blocks[t + 1], s_out)
        vtj = vt_ref[0, 0, :, j * tile:(j + 1) * tile]
        for hh in range(2):
            softmax_pv(s_in, hh, values_t(vtj, hh), j == qi)
        if j == qi:
            outs = []
            for hh in range(2):
                acc = acc_ref[hh]
                outs.append(acc * (1.0 / acc[sum_row[hh]:sum_row[hh] + 1, :]))
            o_t = jnp.where(keep[0], outs[0], outs[1])
            o_ref[0, qi * tile:(qi + 1) * tile, :] = o_t.T.astype(o_ref.dtype)
            if qi + 1 < nq:
                reset_state()


def _causal_attn(q, k, vt):
    bsz, heads, seq, _ = q.shape
    qk_spec = pl.BlockSpec((1, 2, seq, LANES), lambda b, p: (b, p, 0, 0))
    vt_spec = pl.BlockSpec((1, 1, LANES, seq), lambda b, p: (b, p, 0, 0))
    o_spec = pl.BlockSpec((1, seq, LANES), lambda b, p: (b, 0, p))
    kern = functools.partial(_attn_kernel, tile=ATTN_TILE)
    return pl.pallas_call(
        kern,
        grid=(bsz, heads // 2),
        in_specs=[qk_spec, qk_spec, vt_spec],
        out_specs=o_spec,
        out_shape=jax.ShapeDtypeStruct((bsz, seq, heads * HEAD_DIM), BF16),
        scratch_shapes=[pltpu.VMEM((2, ATTN_TILE, ATTN_TILE), F32),
                        pltpu.VMEM((2, ATTN_TILE, ATTN_TILE), F32),
                        pltpu.VMEM((2, 1, ATTN_TILE), F32),
                        pltpu.VMEM((2, LANES, ATTN_TILE), F32)],
        compiler_params=pltpu.CompilerParams(dimension_semantics=("arbitrary", "arbitrary"),
                                             vmem_limit_bytes=VMEM_LIMIT),
        name="causal_attn",
    )(q, k, vt)


def _mix_ffn_kernel(x_ref, oa_ref, ob_ref, gf_ref, gm_ref, mod_ref, gpost_mix_ref, gpre_ffn_ref,
                    gpost_ffn_ref, wpf_ref, wpm_ref, wout_ref, win_ref, wdown_ref, o_ref):
    pa = _dot(oa_ref[0], wpf_ref[...])
    pb = _dot(ob_ref[0], wpm_ref[...])
    merged = gf_ref[0].astype(F32) * pa + gm_ref[0].astype(F32) * pb
    y = _dot(merged.astype(BF16), wout_ref[...])
    x = x_ref[0] + mod_ref[0, 2:3, :] * (_rms(y) * gpost_mix_ref[...])

    shift = mod_ref[0, 3:4, :]
    scale = mod_ref[0, 4:5, :]
    gate = mod_ref[0, 5:6, :]
    h = (_rms(x) * gpre_ffn_ref[...] * (1.0 + scale) + shift).astype(BF16)
    y = None
    for lo in range(0, D_FF, FFN_CHUNK):
        w = min(FFN_CHUNK, D_FF - lo)
        g = _dot(h, win_ref[:, lo:lo + w])
        u = _dot(h, win_ref[:, D_FF + lo:D_FF + lo + w])
        act = (g * jax.nn.sigmoid(g) * u).astype(BF16)
        part = _dot(act, wdown_ref[lo:lo + w, :])
        y = part if y is None else y + part
    o_ref[0] = x + gate * (_rms(y) * gpost_ffn_ref[...])


def _mix_ffn(x, o_a, o_b, gf, gm, mod, g_post_mix, g_pre_ffn, g_post_ffn, w_pf, w_pm, w_out, w_in, w_down):
    bsz, seq, d = x.shape
    tm = TOKEN_TILE
    row = lambda b, i: (b, i, 0)
    weights = (w_pf, w_pm, w_out, w_in, w_down)
    return pl.pallas_call(
        _mix_ffn_kernel,
        grid=(bsz, seq // tm),
        in_specs=[pl.BlockSpec((1, tm, d), row),
                  pl.BlockSpec((1, tm, FOX_WIDTH), row), pl.BlockSpec((1, tm, FOX_WIDTH), row),
                  pl.BlockSpec((1, tm, d), row), pl.BlockSpec((1, tm, d), row),
                  pl.BlockSpec((1, 6, d), lambda b, i: (b, 0, 0)),
                  _const_spec((1, d)), _const_spec((1, d)), _const_spec((1, d))]
                 + [_const_spec(w.shape) for w in weights],
        out_specs=pl.BlockSpec((1, tm, d), row),
        out_shape=jax.ShapeDtypeStruct(x.shape, F32),
        compiler_params=pltpu.CompilerParams(dimension_semantics=("arbitrary", "arbitrary"),
                                             vmem_limit_bytes=VMEM_LIMIT),
        name="mix_ffn",
    )(x, o_a, o_b, gf, gm, mod, g_post_mix.reshape(1, d), g_pre_ffn.reshape(1, d), g_post_ffn.reshape(1, d),
      *weights)


def kernel(x, c, positions, w_ada, b_ada, g_pre_mix, g_post_mix, g_pre_ffn, g_post_ffn, w_in, b_forget,
           g_q_lora, w_uq, g_kv_lora, w_ukv, w_proj_fox, w_proj_mla, w_out, w_ffn_in, w_ffn_out):
    bsz, seq, d = x.shape
    depth = w_ada.shape[0]
    cfull, ssign = _rope_tables(positions)
    for l in range(depth):
        mod = _adaln_mod(c, w_ada[l], b_ada[l]).reshape(bsz, 6, d)
        weights, bf3 = _prep_proj_weights(w_in[l], w_uq[l], w_ukv[l], b_forget[l])
        qf, kf, vf, qm, km, vm, gf, gm = _token_proj(
            x, mod, g_pre_mix[l], cfull, ssign, weights, bf3, g_q_lora[l], g_kv_lora[l])
        o_a = _causal_attn(qf, kf, vf)
        o_b = _causal_attn(qm, km, vm)
        x = _mix_ffn(x, o_a, o_b, gf, gm, mod, g_post_mix[l], g_pre_ffn[l], g_post_ffn[l],
                     w_proj_fox[l].astype(BF16), w_proj_mla[l].astype(BF16), w_out[l].astype(BF16),
                     w_ffn_in[l].astype(BF16), w_ffn_out[l].astype(BF16))
    return x
```

```python
import functools
import math

import jax
import jax.numpy as jnp
import numpy as np
from jax import lax
from jax.experimental import pallas as pl
from jax.experimental.pallas import tpu as pltpu

D_MODEL = 1024
HEADS = 8
HEAD_DIM = 64
FOX_WIDTH = HEADS * HEAD_DIM
MLA_NOPE = 64
MLA_ROPE = 32
MLA_V = 64
MLA_Q_LORA = 768
MLA_KV_LORA = 256
D_FF = 2816
ROPE_THETA = 10000.0
NORM_EPS = 1e-6
IN_WIDTHS = (FOX_WIDTH, FOX_WIDTH, FOX_WIDTH, HEADS, MLA_Q_LORA, MLA_KV_LORA, MLA_ROPE, D_MODEL, D_MODEL)

LANES = 128
TOKEN_TILE = 512
ATTN_TILE = 512
FFN_CHUNK = 512
VMEM_LIMIT = 56 * 1024 * 1024

_EXP2_FOX = math.log2(math.e) / math.sqrt(HEAD_DIM)
_EXP2_MLA = math.log2(math.e) / math.sqrt(MLA_NOPE + MLA_ROPE)

F32 = jnp.float32
BF16 = jnp.bfloat16


def _const_spec(shape):
    zeros = (0,) * len(shape)
    return pl.BlockSpec(shape, lambda *_: zeros, pipeline_mode=pl.Buffered(1))


def _rms(x):
    return x * lax.rsqrt(jnp.mean(x * x, axis=-1, keepdims=True) + NORM_EPS)


def _dot(a, b):
    return jnp.dot(a, b, preferred_element_type=F32)


def _adaln_kernel(c_ref, w_ref, b_ref, o_ref):
    c = c_ref[...]
    sc = c * jax.nn.sigmoid(c)
    o_ref[...] = jnp.dot(sc, w_ref[...], preferred_element_type=F32,
                         precision=lax.Precision.HIGHEST) + b_ref[...]


def _adaln_mod(c, w_ada, b_ada):
    bsz, d = c.shape
    n = w_ada.shape[1]
    tn = 1024
    return pl.pallas_call(
        _adaln_kernel,
        grid=(n // tn,),
        in_specs=[pl.BlockSpec((bsz, d), lambda j: (0, 0)),
                  pl.BlockSpec((d, tn), lambda j: (0, j)),
                  pl.BlockSpec((1, tn), lambda j: (0, j))],
        out_specs=pl.BlockSpec((bsz, tn), lambda j: (0, j)),
        out_shape=jax.ShapeDtypeStruct((bsz, n), F32),
        name="adaln_mod",
    )(c, w_ada, b_ada.reshape(1, n))


def _rope_kernel(pos_ref, invf_ref, cos_ref, sin_ref):
    half = MLA_ROPE // 2
    seq = pos_ref.shape[2]
    ang = invf_ref[:, 0:1] * pos_ref[0]
    c = jnp.cos(ang)
    s = jnp.sin(ang)
    ones = jnp.ones((MLA_NOPE, LANES), F32)
    zeros = jnp.zeros((MLA_NOPE, LANES), F32)
    pad = jnp.zeros((LANES - MLA_NOPE - MLA_ROPE, LANES), F32)
    for t0 in range(0, seq, LANES):
        ct = c[:, t0:t0 + LANES]
        st = s[:, t0:t0 + LANES]
        cos_ref[0, t0:t0 + LANES, :] = jnp.concatenate([ones, ct, ct, pad], axis=0).T
        sin_ref[0, t0:t0 + LANES, :] = jnp.concatenate([zeros, -st, st, pad], axis=0).T


def _rope_tables(positions):
    bsz, seq = positions.shape
    half = MLA_ROPE // 2
    inv_freq = 1.0 / (ROPE_THETA ** (np.arange(0, MLA_ROPE, 2, dtype=np.float32) / MLA_ROPE))
    invf = jnp.asarray(np.tile(inv_freq.astype(np.float32)[:, None], (1, LANES)))
    out_spec = pl.BlockSpec((1, seq, LANES), lambda b: (b, 0, 0))
    return pl.pallas_call(
        _rope_kernel,
        grid=(bsz,),
        in_specs=[pl.BlockSpec((1, 1, seq), lambda b: (b, 0, 0)),
                  pl.BlockSpec((half, LANES), lambda b: (0, 0))],
        out_specs=[out_spec, out_spec],
        out_shape=[jax.ShapeDtypeStruct((bsz, seq, LANES), F32)] * 2,
        name="rope_tables",
    )(positions.astype(F32).reshape(bsz, 1, seq), invf)


def _cumsum_rows(x):
    n = x.shape[0]
    row = lax.broadcasted_iota(jnp.int32, x.shape, 0)
    d = 1
    while d < n:
        x = x + jnp.where(row >= d, pltpu.roll(x, d, axis=0), 0.0)
        d *= 2
    return x


def _rot_half_lanes(x, lane):
    half = MLA_ROPE // 2
    return jnp.where(lane < MLA_NOPE + half, pltpu.roll(x, LANES - half, axis=1), pltpu.roll(x, half, axis=1))


def _store_values_transposed(vt_ref, v):
    for pair in range(HEADS // 2):
        vt_ref[0, pair] = v[:, pair * LANES:(pair + 1) * LANES].T.astype(vt_ref.dtype)


def _token_proj_kernel(x_ref, mod_ref, gpre_ref, bf_ref, gq_ref, gkv_ref, cos_ref, sin_ref,
                       wqkv_ref, wmisc_ref, wcq_ref, wckv_ref, wg_ref, wq_ref, wkv_ref, sel_ref,
                       qf_ref, kf_ref, vf_ref, qm_ref, km_ref, vm_ref, gf_ref, gm_ref,
                       carry_ref):
    tm = x_ref.shape[1]

    @pl.when(pl.program_id(1) == 0)
    def _():
        carry_ref[...] = jnp.zeros_like(carry_ref)

    x = x_ref[0]
    shift = mod_ref[0, 0:1, :]
    scale = mod_ref[0, 1:2, :]
    h = (_rms(x) * gpre_ref[...] * (1.0 + scale) + shift).astype(BF16)

    lane = lax.broadcasted_iota(jnp.int32, (tm, LANES), 1)
    cfull = cos_ref[0]
    ssign = sin_ref[0]

    misc = _dot(h, wmisc_ref[...])
    logit = misc + bf_ref[...]
    logf = jnp.minimum(logit, 0.0) - jnp.log(1.0 + jnp.exp(-jnp.abs(logit)))
    cum = _cumsum_rows(logf) + carry_ref[0:1, :]
    carry_ref[0:1, :] = cum[tm - 1:tm, :]

    for gi, ref in enumerate((gf_ref, gm_ref)):
        g = _dot(h, wg_ref[:, gi * D_MODEL:(gi + 1) * D_MODEL])
        ref[0] = jax.nn.sigmoid(g).astype(ref.dtype)

    cq = _dot(h, wcq_ref[...])
    nq = (_rms(cq) * gq_ref[...]).astype(BF16)
    qq = _dot(nq, wq_ref[...])
    for hh in range(HEADS):
        qa = qq[:, hh * LANES:(hh + 1) * LANES]
        qm_ref[0, hh] = ((qa * cfull + _rot_half_lanes(qa, lane) * ssign) * _EXP2_MLA).astype(qm_ref.dtype)

    ckv = _dot(h, wckv_ref[...])
    nkv = (_rms(ckv) * gkv_ref[...]).astype(BF16)
    kv = _dot(nkv, wkv_ref[...])
    _store_values_transposed(vm_ref, kv[:, HEADS * LANES:])
    in_rope = (lane >= MLA_NOPE) & (lane < MLA_NOPE + MLA_ROPE)
    kpe = jnp.where(in_rope, misc * cfull + _rot_half_lanes(misc, lane) * ssign, 0.0)
    for hh in range(HEADS):
        km_ref[0, hh] = (kv[:, hh * LANES:(hh + 1) * LANES] + kpe).astype(km_ref.dtype)

    p_qkv = _dot(h, wqkv_ref[...])
    _store_values_transposed(vf_ref, p_qkv[:, 2 * FOX_WIDTH:])

    a = cum * math.log2(math.e)
    a_hi = a.astype(BF16).astype(F32)
    r1 = a - a_hi
    a_mid = r1.astype(BF16).astype(F32)
    a_lo = (r1 - a_mid).astype(BF16).astype(F32)
    z = jnp.where(lane < 8, a_hi,
                  jnp.where(lane < 16, a_mid,
                            jnp.where(lane < 24, a_lo,
                                      jnp.where(lane == 24, 1.0, 0.0))))
    aug = _dot(z.astype(BF16), sel_ref[...])
    pairs = HEADS // 2
    for hh in range(HEADS):
        pair = hh // 2
        keep = (lane < HEAD_DIM) if hh % 2 == 0 else (lane >= HEAD_DIM)
        xq = p_qkv[:, pair * LANES:(pair + 1) * LANES]
        xk = p_qkv[:, FOX_WIDTH + pair * LANES:FOX_WIDTH + (pair + 1) * LANES]
        aq = aug[:, pair * LANES:(pair + 1) * LANES]
        ak = aug[:, (pairs + pair) * LANES:(pairs + pair + 1) * LANES]
        qf_ref[0, hh] = jnp.where(keep, xq * _EXP2_FOX, aq).astype(qf_ref.dtype)
        kf_ref[0, hh] = jnp.where(keep, xk, ak).astype(kf_ref.dtype)


def _prep_proj_weights(w_in, w_uq, w_ukv, b_forget):
    d = w_in.shape[0]
    sp = np.cumsum(IN_WIDTHS)[:-1]
    w_fq, w_fk, w_fv, w_f, w_cq, w_ckv, w_kr, w_gf, w_gm = jnp.split(w_in, [int(v) for v in sp], axis=1)
    zeros = lambda n: jnp.zeros((d, n), w_in.dtype)
    w_misc = jnp.concatenate([w_f, w_f, w_f, zeros(MLA_NOPE - 3 * HEADS), w_kr,
                              zeros(LANES - MLA_NOPE - MLA_ROPE)], axis=1).astype(BF16)
    w_qkv = w_in[:, :3 * FOX_WIDTH].astype(BF16)
    w_g = w_in[:, -2 * D_MODEL:].astype(BF16)

    r = w_uq.shape[0]
    uq = w_uq.reshape(r, HEADS, MLA_NOPE + MLA_ROPE)
    pad = jnp.zeros((r, HEADS, LANES - MLA_NOPE - MLA_ROPE), w_uq.dtype)
    w_q = jnp.concatenate([uq, pad], axis=2).reshape(r, HEADS * LANES).astype(BF16)

    rk = w_ukv.shape[0]
    ukv = w_ukv.reshape(rk, HEADS, MLA_NOPE + MLA_V)
    k_nope, v = ukv[:, :, :MLA_NOPE], ukv[:, :, MLA_NOPE:]
    w_kn = jnp.concatenate([k_nope, jnp.zeros((rk, HEADS, LANES - MLA_NOPE), w_ukv.dtype)], axis=2)
    w_kv = jnp.concatenate([w_kn.reshape(rk, HEADS * LANES), v.reshape(rk, HEADS * MLA_V)], axis=1).astype(BF16)

    bf3 = jnp.concatenate([b_forget, b_forget, b_forget,
                           jnp.zeros((LANES - 3 * HEADS,), b_forget.dtype)]).reshape(1, LANES)
    return (w_qkv, w_misc, w_cq.astype(BF16), w_ckv.astype(BF16), w_g, w_q, w_kv), bf3


def _decay_selector():
    pairs = HEADS // 2
    sel = np.zeros((LANES, 2 * pairs * LANES), np.float32)
    for hh in range(HEADS):
        base_q = (hh // 2) * LANES + (HEAD_DIM if hh % 2 == 0 else 0)
        base_k = (pairs + hh // 2) * LANES + (HEAD_DIM if hh % 2 == 0 else 0)
        for piece in range(3):
            sel[piece * HEADS + hh, base_q + piece] = 1.0
            sel[3 * HEADS, base_q + 3 + piece] = 1.0
            sel[3 * HEADS, base_k + piece] = 1.0
            sel[piece * HEADS + hh, base_k + 3 + piece] = -1.0
    return jnp.asarray(sel, BF16)


def _token_proj(x, mod, g_pre, cfull, ssign, weights, bf3, g_q, g_kv):
    bsz, seq, d = x.shape
    tm = TOKEN_TILE
    sel = _decay_selector()
    row = lambda b, i: (b, i, 0)
    head = lambda b, i: (b, 0, i, 0)
    head_shape = jax.ShapeDtypeStruct((bsz, HEADS, seq, LANES), BF16)
    head_spec = pl.BlockSpec((1, HEADS, tm, LANES), head)
    v_shape = jax.ShapeDtypeStruct((bsz, HEADS // 2, LANES, seq), BF16)
    v_spec = pl.BlockSpec((1, HEADS // 2, LANES, tm), lambda b, i: (b, 0, 0, i))
    g_shape = jax.ShapeDtypeStruct((bsz, seq, d), BF16)
    g_spec = pl.BlockSpec((1, tm, d), row)
    return pl.pallas_call(
        _token_proj_kernel,
        grid=(bsz, seq // tm),
        in_specs=[pl.BlockSpec((1, tm, d), row),
                  pl.BlockSpec((1, 6, d), lambda b, i: (b, 0, 0)),
                  _const_spec((1, d)), _const_spec((1, LANES)),
                  _const_spec((1, MLA_Q_LORA)), _const_spec((1, MLA_KV_LORA)),
                  pl.BlockSpec((1, tm, LANES), row), pl.BlockSpec((1, tm, LANES), row)]
                 + [_const_spec(w.shape) for w in weights] + [_const_spec(sel.shape)],
        out_specs=[head_spec, head_spec, v_spec, head_spec, head_spec, v_spec, g_spec, g_spec],
        out_shape=[head_shape, head_shape, v_shape, head_shape, head_shape, v_shape, g_shape, g_shape],
        scratch_shapes=[pltpu.VMEM((8, LANES), F32)],
        compiler_params=pltpu.CompilerParams(dimension_semantics=("arbitrary", "arbitrary"),
                                             vmem_limit_bytes=VMEM_LIMIT),
        name="token_proj",
    )(x, mod, g_pre.reshape(1, d), bf3, g_q.reshape(1, -1), g_kv.reshape(1, -1), cfull, ssign,
      *weights, sel)


def _attn_kernel(q_ref, k_ref, vt_ref, o_ref, sa_ref, sb_ref, m_ref, acc_ref, *, tile):
    seq = vt_ref.shape[3]
    nq = seq // tile
    half = tile // 2
    nt = (((1,), (1,)), ((), ()))
    vrow = lax.broadcasted_iota(jnp.int32, (LANES, tile), 0)
    sum_row = (HEAD_DIM, 0)
    one_rows = [jnp.where(vrow == sum_row[hh], 1.0, 0.0).astype(BF16) for hh in range(2)]
    keep = (vrow < HEAD_DIM, vrow >= HEAD_DIM)
    krow = lax.broadcasted_iota(jnp.int32, (LANES, LANES), 0)
    qcol = lax.broadcasted_iota(jnp.int32, (LANES, LANES), 1)
    diag_ok = krow <= qcol

    def values_t(vt, hh):
        return jnp.where(keep[hh], vt, one_rows[hh])

    def issue_scores(qi, j, s_out):
        for hh in range(2):
            q = q_ref[0, hh, qi * tile:(qi + 1) * tile, :]
            k = k_ref[0, hh, j * tile:(j + 1) * tile, :]
            if j == qi:
                s_out[hh, :half, :] = lax.dot_general(k[:half], q, nt, preferred_element_type=F32)
                s_out[hh, half:, half:] = lax.dot_general(k[half:], q[half:], nt, preferred_element_type=F32)
            else:
                s_out[hh] = lax.dot_general(k, q, nt, preferred_element_type=F32)

    def softmax_pv(s_in, hh, vth, masked):
        alphas, p_cols = [], []
        for c0 in range(0, tile, LANES):
            n_keys = c0 + LANES if masked else tile
            n_fill = tile if (c0 >= half or not masked) else half
            m_old = m_ref[hh, :, c0:c0 + LANES]

            def strip(r0):
                sb = s_in[hh, r0:r0 + LANES, c0:c0 + LANES]
                return jnp.where(diag_ok, sb, -jnp.inf) if (masked and r0 == c0) else sb

            mx = functools.reduce(jnp.maximum, [strip(r0) for r0 in range(0, n_keys, LANES)])
            m_new = jnp.maximum(m_old, jnp.max(mx, axis=0, keepdims=True))
            parts = [jnp.exp2(strip(r0) - m_new).astype(BF16) for r0 in range(0, n_keys, LANES)]
            parts += [jnp.zeros((LANES, LANES), BF16)] * ((n_fill - n_keys) // LANES)
            p_cols.append(jnp.concatenate(parts, axis=0))
            m_ref[hh, :, c0:c0 + LANES] = m_new
            alphas.append(jnp.exp2(m_old - m_new))
        alpha = jnp.concatenate(alphas, axis=1)
        n_left = half // LANES
        if masked:
            pv = jnp.concatenate([_dot(vth[:, :half], jnp.concatenate(p_cols[:n_left], axis=1)),
                                  _dot(vth, jnp.concatenate(p_cols[n_left:], axis=1))], axis=1)
        else:
            pv = _dot(vth, jnp.concatenate(p_cols, axis=1))
        acc_ref[hh] = alpha * acc_ref[hh] + pv

    def reset_state():
        m_ref[...] = jnp.full(m_ref.shape, -jnp.inf, F32)
        acc_ref[...] = jnp.zeros(acc_ref.shape, F32)

    blocks = [(qi, j) for qi in range(nq) for j in range(qi + 1)]
    bufs = (sa_ref, sb_ref)
    reset_state()
    issue_scores(0, 0, bufs[0])
    for t, (qi, j) in enumerate(blocks):
        s_in, s_out = bufs[t % 2], bufs[(t + 1) % 2]
        if t + 1 < len(blocks):
            issue_scores(*blocks[t + 1], s_out)
        vtj = vt_ref[0, 0, :, j * tile:(j + 1) * tile]
        for hh in range(2):
            softmax_pv(s_in, hh, values_t(vtj, hh), j == qi)
        if j == qi:
            outs = []
            for hh in range(2):
                acc = acc_ref[hh]
                outs.append(acc * (1.0 / acc[sum_row[hh]:sum_row[hh] + 1, :]))
            o_t = jnp.where(keep[0], outs[0], outs[1])
            o_ref[0, qi * tile:(qi + 1) * tile, :] = o_t.T.astype(o_ref.dtype)
            if qi + 1 < nq:
                reset_state()


def _causal_attn(q, k, vt):
    bsz, heads, seq, _ = q.shape
    qk_spec = pl.BlockSpec((1, 2, seq, LANES), lambda b, p: (b, p, 0, 0))
    vt_spec = pl.BlockSpec((1, 1, LANES, seq), lambda b, p: (b, p, 0, 0))
    o_spec = pl.BlockSpec((1, seq, LANES), lambda b, p: (b, 0, p))
    kern = functools.partial(_attn_kernel, tile=ATTN_TILE)
    return pl.pallas_call(
        kern,
        grid=(bsz, heads // 2),
        in_specs=[qk_spec, qk_spec, vt_spec],
        out_specs=o_spec,
        out_shape=jax.ShapeDtypeStruct((bsz, seq, heads * HEAD_DIM), BF16),
        scratch_shapes=[pltpu.VMEM((2, ATTN_TILE, ATTN_TILE), F32),
                        pltpu.VMEM((2, ATTN_TILE, ATTN_TILE), F32),
                        pltpu.VMEM((2, 1, ATTN_TILE), F32),
                        pltpu.VMEM((2, LANES, ATTN_TILE), F32)],
        compiler_params=pltpu.CompilerParams(dimension_semantics=("arbitrary", "arbitrary"),
                                             vmem_limit_bytes=VMEM_LIMIT),
        name="causal_attn",
    )(q, k, vt)


def _mix_ffn_kernel(x_ref, oa_ref, ob_ref, gf_ref, gm_ref, mod_ref, gpost_mix_ref, gpre_ffn_ref,
                    gpost_ffn_ref, wpf_ref, wpm_ref, wout_ref, win_ref, wdown_ref, o_ref):
    pa = _dot(oa_ref[0], wpf_ref[...])
    pb = _dot(ob_ref[0], wpm_ref[...])
    merged = gf_ref[0].astype(F32) * pa + gm_ref[0].astype(F32) * pb
    y = _dot(merged.astype(BF16), wout_ref[...])
    x = x_ref[0] + mod_ref[0, 2:3, :] * (_rms(y) * gpost_mix_ref[...])

    shift = mod_ref[0, 3:4, :]
    scale = mod_ref[0, 4:5, :]
    gate = mod_ref[0, 5:6, :]
    h = (_rms(x) * gpre_ffn_ref[...] * (1.0 + scale) + shift).astype(BF16)
    y = None
    for lo in range(0, D_FF, FFN_CHUNK):
        w = min(FFN_CHUNK, D_FF - lo)
        g = _dot(h, win_ref[:, lo:lo + w])
        u = _dot(h, win_ref[:, D_FF + lo:D_FF + lo + w])
        act = (g * jax.nn.sigmoid(g) * u).astype(BF16)
        part = _dot(act, wdown_ref[lo:lo + w, :])
        y = part if y is None else y + part
    o_ref[0] = x + gate * (_rms(y) * gpost_ffn_ref[...])


def _mix_ffn(x, o_a, o_b, gf, gm, mod, g_post_mix, g_pre_ffn, g_post_ffn, w_pf, w_pm, w_out, w_in, w_down):
    bsz, seq, d = x.shape
    tm = TOKEN_TILE
    row = lambda b, i: (b, i, 0)
    weights = (w_pf, w_pm, w_out, w_in, w_down)
    return pl.pallas_call(
        _mix_ffn_kernel,
        grid=(bsz, seq // tm),
        in_specs=[pl.BlockSpec((1, tm, d), row),
                  pl.BlockSpec((1, tm, FOX_WIDTH), row), pl.BlockSpec((1, tm, FOX_WIDTH), row),
                  pl.BlockSpec((1, tm, d), row), pl.BlockSpec((1, tm, d), row),
                  pl.BlockSpec((1, 6, d), lambda b, i: (b, 0, 0)),
                  _const_spec((1, d)), _const_spec((1, d)), _const_spec((1, d))]
                 + [_const_spec(w.shape) for w in weights],
        out_specs=pl.BlockSpec((1, tm, d), row),
        out_shape=jax.ShapeDtypeStruct(x.shape, F32),
        compiler_params=pltpu.CompilerParams(dimension_semantics=("arbitrary", "arbitrary"),
                                             vmem_limit_bytes=VMEM_LIMIT),
        name="mix_ffn",
    )(x, o_a, o_b, gf, gm, mod, g_post_mix.reshape(1, d), g_pre_ffn.reshape(1, d), g_post_ffn.reshape(1, d),
      *weights)


def kernel(x, c, positions, w_ada, b_ada, g_pre_mix, g_post_mix, g_pre_ffn, g_post_ffn, w_in, b_forget,
           g_q_lora, w_uq, g_kv_lora, w_ukv, w_proj_fox, w_proj_mla, w_out, w_ffn_in, w_ffn_out):
    bsz, seq, d = x.shape
    depth = w_ada.shape[0]
    cfull, ssign = _rope_tables(positions)
    for l in range(depth):
        mod = _adaln_mod(c, w_ada[l], b_ada[l]).reshape(bsz, 6, d)
        weights, bf3 = _prep_proj_weights(w_in[l], w_uq[l], w_ukv[l], b_forget[l])
        qf, kf, vf, qm, km, vm, gf, gm = _token_proj(
            x, mod, g_pre_mix[l], cfull, ssign, weights, bf3, g_q_lora[l], g_kv_lora[l])
        o_a = _causal_attn(qf, kf, vf)
        o_b = _causal_attn(qm, km, vm)
        x = _mix_ffn(x, o_a, o_b, gf, gm, mod, g_post_mix[l], g_pre_ffn[l], g_post_ffn[l],
                     w_proj_fox[l].astype(BF16), w_proj_mla[l].astype(BF16), w_out[l].astype(BF16),
                     w_ffn_in[l].astype(BF16), w_ffn_out[l].astype(BF16))
    return x
```

```python
import functools
import math

import jax
import jax.numpy as jnp
import numpy as np
from jax import lax
from jax.experimental import pallas as pl
from jax.experimental.pallas import tpu as pltpu

D_MODEL = 1024
HEADS = 8
HEAD_DIM = 64
FOX_WIDTH = HEADS * HEAD_DIM
MLA_NOPE = 64
MLA_ROPE = 32
MLA_V = 64
MLA_Q_LORA = 768
MLA_KV_LORA = 256
D_FF = 2816
ROPE_THETA = 10000.0
NORM_EPS = 1e-6
IN_WIDTHS = (FOX_WIDTH, FOX_WIDTH, FOX_WIDTH, HEADS, MLA_Q_LORA, MLA_KV_LORA, MLA_ROPE, D_MODEL, D_MODEL)

LANES = 128
TOKEN_TILE = 512
ATTN_TILE = 512
ATTN_PAIRS = 1
FFN_CHUNK = 512
VMEM_LIMIT = 56 * 1024 * 1024

_EXP2_FOX = math.log2(math.e) / math.sqrt(HEAD_DIM)
_EXP2_MLA = math.log2(math.e) / math.sqrt(MLA_NOPE + MLA_ROPE)

F32 = jnp.float32
BF16 = jnp.bfloat16


def _const_spec(shape):
    zeros = (0,) * len(shape)
    return pl.BlockSpec(shape, lambda *_: zeros, pipeline_mode=pl.Buffered(1))


def _rms(x):
    return x * lax.rsqrt(jnp.mean(x * x, axis=-1, keepdims=True) + NORM_EPS)


def _dot(a, b):
    return jnp.dot(a, b, preferred_element_type=F32)


def _adaln_kernel(c_ref, w_ref, b_ref, o_ref):
    c = c_ref[...]
    sc = c * jax.nn.sigmoid(c)
    o_ref[...] = _dot(sc.astype(BF16), w_ref[...].astype(BF16)) + b_ref[...]


def _adaln_mod(c, w_ada, b_ada):
    bsz, d = c.shape
    n = w_ada.shape[1]
    tn = 1024
    return pl.pallas_call(
        _adaln_kernel,
        grid=(n // tn,),
        in_specs=[pl.BlockSpec((bsz, d), lambda j: (0, 0)),
                  pl.BlockSpec((d, tn), lambda j: (0, j)),
                  pl.BlockSpec((1, tn), lambda j: (0, j))],
        out_specs=pl.BlockSpec((bsz, tn), lambda j: (0, j)),
        out_shape=jax.ShapeDtypeStruct((bsz, n), F32),
        name="adaln_mod",
    )(c, w_ada, b_ada.reshape(1, n))


def _rope_kernel(pos_ref, invf_ref, cos_ref, sin_ref):
    half = MLA_ROPE // 2
    seq = pos_ref.shape[2]
    ang = invf_ref[:, 0:1] * pos_ref[0]
    c = jnp.cos(ang)
    s = jnp.sin(ang)
    ones = jnp.ones((MLA_NOPE, LANES), F32)
    zeros = jnp.zeros((MLA_NOPE, LANES), F32)
    pad = jnp.zeros((LANES - MLA_NOPE - MLA_ROPE, LANES), F32)
    for t0 in range(0, seq, LANES):
        ct = c[:, t0:t0 + LANES]
        st = s[:, t0:t0 + LANES]
        cos_ref[0, t0:t0 + LANES, :] = jnp.concatenate([ones, ct, ct, pad], axis=0).T
        sin_ref[0, t0:t0 + LANES, :] = jnp.concatenate([zeros, -st, st, pad], axis=0).T


def _rope_tables(positions):
    bsz, seq = positions.shape
    half = MLA_ROPE // 2
    inv_freq = 1.0 / (ROPE_THETA ** (np.arange(0, MLA_ROPE, 2, dtype=np.float32) / MLA_ROPE))
    invf = jnp.asarray(np.tile(inv_freq.astype(np.float32)[:, None], (1, LANES)))
    out_spec = pl.BlockSpec((1, seq, LANES), lambda b: (b, 0, 0))
    return pl.pallas_call(
        _rope_kernel,
        grid=(bsz,),
        in_specs=[pl.BlockSpec((1, 1, seq), lambda b: (b, 0, 0)),
                  pl.BlockSpec((half, LANES), lambda b: (0, 0))],
        out_specs=[out_spec, out_spec],
        out_shape=[jax.ShapeDtypeStruct((bsz, seq, LANES), F32)] * 2,
        name="rope_tables",
    )(positions.astype(F32).reshape(bsz, 1, seq), invf)


def _cumsum_rows(x):
    n = x.shape[0]
    row = lax.broadcasted_iota(jnp.int32, x.shape, 0)
    d = 1
    while d < n:
        x = x + jnp.where(row >= d, pltpu.roll(x, d, axis=0), 0.0)
        d *= 2
    return x


def _rot_half_lanes(x, lane):
    half = MLA_ROPE // 2
    return jnp.where(lane < MLA_NOPE + half, pltpu.roll(x, LANES - half, axis=1), pltpu.roll(x, half, axis=1))


def _store_values_transposed(vt_ref, v):
    for pair in range(HEADS // 2):
        vt_ref[0, pair] = v[:, pair * LANES:(pair + 1) * LANES].T.astype(vt_ref.dtype)


def _token_proj_kernel(x_ref, mod_ref, gpre_ref, bf_ref, gq_ref, gkv_ref, cos_ref, sin_ref,
                       wqkv_ref, wmisc_ref, wcq_ref, wckv_ref, wg_ref, wq_ref, wkv_ref, sel_ref,
                       qf_ref, kf_ref, vf_ref, qm_ref, km_ref, vm_ref, gf_ref, gm_ref,
                       carry_ref):
    tm = x_ref.shape[1]

    @pl.when(pl.program_id(1) == 0)
    def _():
        carry_ref[...] = jnp.zeros_like(carry_ref)

    x = x_ref[0]
    shift = mod_ref[0, 0:1, :]
    scale = mod_ref[0, 1:2, :]
    h = (_rms(x) * gpre_ref[...] * (1.0 + scale) + shift).astype(BF16)

    lane = lax.broadcasted_iota(jnp.int32, (tm, LANES), 1)
    cfull = cos_ref[0]
    ssign = sin_ref[0]

    misc = _dot(h, wmisc_ref[...])
    logit = misc + bf_ref[...]
    logf = jnp.minimum(logit, 0.0) - jnp.log(1.0 + jnp.exp(-jnp.abs(logit)))
    cum = _cumsum_rows(logf) + carry_ref[0:1, :]
    carry_ref[0:1, :] = cum[tm - 1:tm, :]

    for gi, ref in enumerate((gf_ref, gm_ref)):
        g = _dot(h, wg_ref[:, gi * D_MODEL:(gi + 1) * D_MODEL])
        ref[0] = jax.nn.sigmoid(g).astype(ref.dtype)

    cq = _dot(h, wcq_ref[...])
    nq = (_rms(cq) * gq_ref[...]).astype(BF16)
    qq = _dot(nq, wq_ref[...])
    for hh in range(HEADS):
        qa = qq[:, hh * LANES:(hh + 1) * LANES]
        qm_ref[0, hh] = ((qa * cfull + _rot_half_lanes(qa, lane) * ssign) * _EXP2_MLA).astype(qm_ref.dtype)

    ckv = _dot(h, wckv_ref[...])
    nkv = (_rms(ckv) * gkv_ref[...]).astype(BF16)
    kv = _dot(nkv, wkv_ref[...])
    _store_values_transposed(vm_ref, kv[:, HEADS * LANES:])
    in_rope = (lane >= MLA_NOPE) & (lane < MLA_NOPE + MLA_ROPE)
    kpe = jnp.where(in_rope, misc * cfull + _rot_half_lanes(misc, lane) * ssign, 0.0)
    for hh in range(HEADS):
        km_ref[0, hh] = (kv[:, hh * LANES:(hh + 1) * LANES] + kpe).astype(km_ref.dtype)

    p_qkv = _dot(h, wqkv_ref[...])
    _store_values_transposed(vf_ref, p_qkv[:, 2 * FOX_WIDTH:])

    a = cum * math.log2(math.e)
    a_hi = a.astype(BF16).astype(F32)
    r1 = a - a_hi
    a_mid = r1.astype(BF16).astype(F32)
    a_lo = (r1 - a_mid).astype(BF16).astype(F32)
    z = jnp.where(lane < 8, a_hi,
                  jnp.where(lane < 16, a_mid,
                            jnp.where(lane < 24, a_lo,
                                      jnp.where(lane == 24, 1.0, 0.0))))
    aug = _dot(z.astype(BF16), sel_ref[...])
    pairs = HEADS // 2
    for hh in range(HEADS):
        pair = hh // 2
        keep = (lane < HEAD_DIM) if hh % 2 == 0 else (lane >= HEAD_DIM)
        xq = p_qkv[:, pair * LANES:(pair + 1) * LANES]
        xk = p_qkv[:, FOX_WIDTH + pair * LANES:FOX_WIDTH + (pair + 1) * LANES]
        aq = aug[:, pair * LANES:(pair + 1) * LANES]
        ak = aug[:, (pairs + pair) * LANES:(pairs + pair + 1) * LANES]
        qf_ref[0, hh] = jnp.where(keep, xq * _EXP2_FOX, aq).astype(qf_ref.dtype)
        kf_ref[0, hh] = jnp.where(keep, xk, ak).astype(kf_ref.dtype)


def _prep_proj_weights(w_in, w_uq, w_ukv, b_forget):
    d = w_in.shape[0]
    sp = np.cumsum(IN_WIDTHS)[:-1]
    w_fq, w_fk, w_fv, w_f, w_cq, w_ckv, w_kr, w_gf, w_gm = jnp.split(w_in, [int(v) for v in sp], axis=1)
    zeros = lambda n: jnp.zeros((d, n), w_in.dtype)
    w_misc = jnp.concatenate([w_f, w_f, w_f, zeros(MLA_NOPE - 3 * HEADS), w_kr,
                              zeros(LANES - MLA_NOPE - MLA_ROPE)], axis=1).astype(BF16)
    w_qkv = w_in[:, :3 * FOX_WIDTH].astype(BF16)
    w_g = w_in[:, -2 * D_MODEL:].astype(BF16)

    r = w_uq.shape[0]
    uq = w_uq.reshape(r, HEADS, MLA_NOPE + MLA_ROPE)
    pad = jnp.zeros((r, HEADS, LANES - MLA_NOPE - MLA_ROPE), w_uq.dtype)
    w_q = jnp.concatenate([uq, pad], axis=2).reshape(r, HEADS * LANES).astype(BF16)

    rk = w_ukv.shape[0]
    ukv = w_ukv.reshape(rk, HEADS, MLA_NOPE + MLA_V)
    k_nope, v = ukv[:, :, :MLA_NOPE], ukv[:, :, MLA_NOPE:]
    w_kn = jnp.concatenate([k_nope, jnp.zeros((rk, HEADS, LANES - MLA_NOPE), w_ukv.dtype)], axis=2)
    w_kv = jnp.concatenate([w_kn.reshape(rk, HEADS * LANES), v.reshape(rk, HEADS * MLA_V)], axis=1).astype(BF16)

    bf3 = jnp.concatenate([b_forget, b_forget, b_forget,
                           jnp.zeros((LANES - 3 * HEADS,), b_forget.dtype)]).reshape(1, LANES)
    return (w_qkv, w_misc, w_cq.astype(BF16), w_ckv.astype(BF16), w_g, w_q, w_kv), bf3


def _decay_selector():
    pairs = HEADS // 2
    sel = np.zeros((LANES, 2 * pairs * LANES), np.float32)
    for hh in range(HEADS):
        base_q = (hh // 2) * LANES + (HEAD_DIM if hh % 2 == 0 else 0)
        base_k = (pairs + hh // 2) * LANES + (HEAD_DIM if hh % 2 == 0 else 0)
        for piece in range(3):
            sel[piece * HEADS + hh, base_q + piece] = 1.0
            sel[3 * HEADS, base_q + 3 + piece] = 1.0
            sel[3 * HEADS, base_k + piece] = 1.0
            sel[piece * HEADS + hh, base_k + 3 + piece] = -1.0
    return jnp.asarray(sel, BF16)


def _token_proj(x, mod, g_pre, cfull, ssign, weights, bf3, g_q, g_kv):
    bsz, seq, d = x.shape
    tm = TOKEN_TILE
    sel = _decay_selector()
    row = lambda b, i: (b, i, 0)
    head = lambda b, i: (b, 0, i, 0)
    head_shape = jax.ShapeDtypeStruct((bsz, HEADS, seq, LANES), BF16)
    head_spec = pl.BlockSpec((1, HEADS, tm, LANES), head)
    v_shape = jax.ShapeDtypeStruct((bsz, HEADS // 2, LANES, seq), BF16)
    v_spec = pl.BlockSpec((1, HEADS // 2, LANES, tm), lambda b, i: (b, 0, 0, i))
    g_shape = jax.ShapeDtypeStruct((bsz, seq, d), BF16)
    g_spec = pl.BlockSpec((1, tm, d), row)
    return pl.pallas_call(
        _token_proj_kernel,
        grid=(bsz, seq // tm),
        in_specs=[pl.BlockSpec((1, tm, d), row),
                  pl.BlockSpec((1, 6, d), lambda b, i: (b, 0, 0)),
                  _const_spec((1, d)), _const_spec((1, LANES)),
                  _const_spec((1, MLA_Q_LORA)), _const_spec((1, MLA_KV_LORA)),
                  pl.BlockSpec((1, tm, LANES), row), pl.BlockSpec((1, tm, LANES), row)]
                 + [_const_spec(w.shape) for w in weights] + [_const_spec(sel.shape)],
        out_specs=[head_spec, head_spec, v_spec, head_spec, head_spec, v_spec, g_spec, g_spec],
        out_shape=[head_shape, head_shape, v_shape, head_shape, head_shape, v_shape, g_shape, g_shape],
        scratch_shapes=[pltpu.VMEM((8, LANES), F32)],
        compiler_params=pltpu.CompilerParams(dimension_semantics=("arbitrary", "arbitrary"),
                                             vmem_limit_bytes=VMEM_LIMIT),
        name="token_proj",
    )(x, mod, g_pre.reshape(1, d), bf3, g_q.reshape(1, -1), g_kv.reshape(1, -1), cfull, ssign,
      *weights, sel)


def _attn_kernel(zero_ref, q_ref, k_ref, vt_ref, o_ref, sa_ref, sb_ref, m_ref, acc_ref, *, tile):
    seq = vt_ref.shape[3]
    nq = seq // tile
    n_heads = q_ref.shape[1]
    z = zero_ref[0]
    half = tile // 2
    nt = (((1,), (1,)), ((), ()))
    vrow = lax.broadcasted_iota(jnp.int32, (LANES, tile), 0)
    sum_row = (HEAD_DIM, 0)
    one_rows = [jnp.where(vrow == sum_row[hh], 1.0, 0.0).astype(BF16) for hh in range(2)]
    keep = (vrow < HEAD_DIM, vrow >= HEAD_DIM)
    krow = lax.broadcasted_iota(jnp.int32, (LANES, LANES), 0)
    qcol = lax.broadcasted_iota(jnp.int32, (LANES, LANES), 1)
    diag_ok = krow <= qcol

    def values_t(vt, hh):
        return jnp.where(keep[hh % 2], vt, one_rows[hh % 2])

    def issue_scores(qi, j, s_out):
        for hh in range(n_heads):
            q = q_ref[0, hh, qi * tile:(qi + 1) * tile, :]
            k = k_ref[0, hh, j * tile:(j + 1) * tile, :]
            if j == qi:
                s_out[z + hh, :half, :] = lax.dot_general(k[:half], q, nt, preferred_element_type=F32)
                s_out[z + hh, half:, half:] = lax.dot_general(k[half:], q[half:], nt, preferred_element_type=F32)
            else:
                s_out[z + hh] = lax.dot_general(k, q, nt, preferred_element_type=F32)

    def softmax_pv(s_in, hh, vth, masked):
        alphas, p_cols = [], []
        for c0 in range(0, tile, LANES):
            n_keys = c0 + LANES if masked else tile
            n_fill = tile if (c0 >= half or not masked) else half
            m_old = m_ref[hh, :, c0:c0 + LANES]

            def strip(r0):
                sb = s_in[z + hh, r0:r0 + LANES, c0:c0 + LANES]
                return jnp.where(diag_ok, sb, -jnp.inf) if (masked and r0 == c0) else sb

            mx = functools.reduce(jnp.maximum, [strip(r0) for r0 in range(0, n_keys, LANES)])
            m_new = jnp.maximum(m_old, jnp.max(mx, axis=0, keepdims=True))
            parts = [jnp.exp2(strip(r0) - m_new).astype(BF16) for r0 in range(0, n_keys, LANES)]
            parts += [jnp.zeros((LANES, LANES), BF16)] * ((n_fill - n_keys) // LANES)
            p_cols.append(jnp.concatenate(parts, axis=0))
            m_ref[hh, :, c0:c0 + LANES] = m_new
            alphas.append(jnp.exp2(m_old - m_new))
        alpha = jnp.concatenate(alphas, axis=1)
        n_left = half // LANES
        if masked:
            pv = jnp.concatenate([_dot(vth[:, :half], jnp.concatenate(p_cols[:n_left], axis=1)),
                                  _dot(vth, jnp.concatenate(p_cols[n_left:], axis=1))], axis=1)
        else:
            pv = _dot(vth, jnp.concatenate(p_cols, axis=1))
        acc_ref[z + hh] = alpha * acc_ref[z + hh] + pv

    def reset_state():
        m_ref[...] = jnp.full(m_ref.shape, -jnp.inf, F32)
        acc_ref[...] = jnp.zeros(acc_ref.shape, F32)

    blocks = [(qi, j) for qi in range(nq) for j in range(qi + 1)]
    bufs = (sa_ref, sb_ref)
    reset_state()
    issue_scores(0, 0, bufs[0])
    for t, (qi, j) in enumerate(blocks):
        s_in, s_out = bufs[t % 2], bufs[(t + 1) % 2]
        if t + 1 < len(blocks):
            issue_scores(*blocks[t + 1], s_out)
        for hh in range(n_heads):
            vtj = vt_ref[0, hh // 2, :, j * tile:(j + 1) * tile]
            softmax_pv(s_in, hh, values_t(vtj, hh), j == qi)
        if j == qi:
            for pair in range(n_heads // 2):
                outs = []
                for hh in (2 * pair, 2 * pair + 1):
                    acc = acc_ref[z + hh]
                    outs.append(acc * (1.0 / acc[sum_row[hh % 2]:sum_row[hh % 2] + 1, :]))
                o_t = jnp.where(keep[0], outs[0], outs[1])
                o_ref[0, qi * tile:(qi + 1) * tile, pair * LANES:(pair + 1) * LANES] = o_t.T.astype(o_ref.dtype)
            if qi + 1 < nq:
                reset_state()


def _causal_attn(q, k, vt):
    bsz, heads, seq, _ = q.shape
    g = ATTN_PAIRS
    qk_spec = pl.BlockSpec((1, 2 * g, seq, LANES), lambda b, p: (b, p, 0, 0))
    vt_spec = pl.BlockSpec((1, g, LANES, seq), lambda b, p: (b, p, 0, 0))
    o_spec = pl.BlockSpec((1, seq, g * LANES), lambda b, p: (b, 0, p))
    kern = functools.partial(_attn_kernel, tile=ATTN_TILE)
    return pl.pallas_call(
        kern,
        grid=(bsz, heads // (2 * g)),
        in_specs=[pl.BlockSpec(memory_space=pltpu.SMEM), qk_spec, qk_spec, vt_spec],
        out_specs=o_spec,
        out_shape=jax.ShapeDtypeStruct((bsz, seq, heads * HEAD_DIM), BF16),
        scratch_shapes=[pltpu.VMEM((2 * g, ATTN_TILE, ATTN_TILE), F32),
                        pltpu.VMEM((2 * g, ATTN_TILE, ATTN_TILE), F32),
                        pltpu.VMEM((2 * g, 1, ATTN_TILE), F32),
                        pltpu.VMEM((2 * g, LANES, ATTN_TILE), F32)],
        compiler_params=pltpu.CompilerParams(dimension_semantics=("arbitrary", "arbitrary"),
                                             vmem_limit_bytes=VMEM_LIMIT),
        name="causal_attn",
    )(jnp.zeros((1,), jnp.int32), q, k, vt)


def _mix_ffn_kernel(x_ref, oa_ref, ob_ref, gf_ref, gm_ref, mod_ref, gpost_mix_ref, gpre_ffn_ref,
                    gpost_ffn_ref, wpf_ref, wpm_ref, wout_ref, win_ref, wdown_ref, o_ref):
    pa = _dot(oa_ref[0], wpf_ref[...])
    pb = _dot(ob_ref[0], wpm_ref[...])
    merged = gf_ref[0].astype(F32) * pa + gm_ref[0].astype(F32) * pb
    y = _dot(merged.astype(BF16), wout_ref[...])
    x = x_ref[0] + mod_ref[0, 2:3, :] * (_rms(y) * gpost_mix_ref[...])

    shift = mod_ref[0, 3:4, :]
    scale = mod_ref[0, 4:5, :]
    gate = mod_ref[0, 5:6, :]
    h = (_rms(x) * gpre_ffn_ref[...] * (1.0 + scale) + shift).astype(BF16)
    y = None
    for lo in range(0, D_FF, FFN_CHUNK):
        w = min(FFN_CHUNK, D_FF - lo)
        g = _dot(h, win_ref[:, lo:lo + w])
        u = _dot(h, win_ref[:, D_FF + lo:D_FF + lo + w])
        act = (g * jax.nn.sigmoid(g) * u).astype(BF16)
        part = _dot(act, wdown_ref[lo:lo + w, :])
        y = part if y is None else y + part
    o_ref[0] = x + gate * (_rms(y) * gpost_ffn_ref[...])


def _mix_ffn(x, o_a, o_b, gf, gm, mod, g_post_mix, g_pre_ffn, g_post_ffn, w_pf, w_pm, w_out, w_in, w_down):
    bsz, seq, d = x.shape
    tm = TOKEN_TILE
    row = lambda b, i: (b, i, 0)
    weights = (w_pf, w_pm, w_out, w_in, w_down)
    return pl.pallas_call(
        _mix_ffn_kernel,
        grid=(bsz, seq // tm),
        in_specs=[pl.BlockSpec((1, tm, d), row),
                  pl.BlockSpec((1, tm, FOX_WIDTH), row), pl.BlockSpec((1, tm, FOX_WIDTH), row),
                  pl.BlockSpec((1, tm, d), row), pl.BlockSpec((1, tm, d), row),
                  pl.BlockSpec((1, 6, d), lambda b, i: (b, 0, 0)),
                  _const_spec((1, d)), _const_spec((1, d)), _const_spec((1, d))]
                 + [_const_spec(w.shape) for w in weights],
        out_specs=pl.BlockSpec((1, tm, d), row),
        out_shape=jax.ShapeDtypeStruct(x.shape, F32),
        compiler_params=pltpu.CompilerParams(dimension_semantics=("arbitrary", "arbitrary"),
                                             vmem_limit_bytes=VMEM_LIMIT),
        name="mix_ffn",
    )(x, o_a, o_b, gf, gm, mod, g_post_mix.reshape(1, d), g_pre_ffn.reshape(1, d), g_post_ffn.reshape(1, d),
      *weights)


def kernel(x, c, positions, w_ada, b_ada, g_pre_mix, g_post_mix, g_pre_ffn, g_post_ffn, w_in, b_forget,
           g_q_lora, w_uq, g_kv_lora, w_ukv, w_proj_fox, w_proj_mla, w_out, w_ffn_in, w_ffn_out):
    bsz, seq, d = x.shape
    depth = w_ada.shape[0]
    cfull, ssign = _rope_tables(positions)
    for l in range(depth):
        mod = _adaln_mod(c, w_ada[l], b_ada[l]).reshape(bsz, 6, d)
        weights, bf3 = _prep_proj_weights(w_in[l], w_uq[l], w_ukv[l], b_forget[l])
        qf, kf, vf, qm, km, vm, gf, gm = _token_proj(
            x, mod, g_pre_mix[l], cfull, ssign, weights, bf3, g_q_lora[l], g_kv_lora[l])
        o_a = _causal_attn(qf, kf, vf)
        o_b = _causal_attn(qm, km, vm)
        x = _mix_ffn(x, o_a, o_b, gf, gm, mod, g_post_mix[l], g_pre_ffn[l], g_post_ffn[l],
                     w_proj_fox[l].astype(BF16), w_proj_mla[l].astype(BF16), w_out[l].astype(BF16),
                     w_ffn_in[l].astype(BF16), w_ffn_out[l].astype(BF16))
    return x
```

```python
import functools
import math

import jax
import jax.numpy as jnp
import numpy as np
from jax import lax
from jax.experimental import pallas as pl
from jax.experimental.pallas import tpu as pltpu

D_MODEL = 1024
HEADS = 8
HEAD_DIM = 64
FOX_WIDTH = HEADS * HEAD_DIM
MLA_NOPE = 64
MLA_ROPE = 32
MLA_V = 64
MLA_Q_LORA = 768
MLA_KV_LORA = 256
D_FF = 2816
ROPE_THETA = 10000.0
NORM_EPS = 1e-6
IN_WIDTHS = (FOX_WIDTH, FOX_WIDTH, FOX_WIDTH, HEADS, MLA_Q_LORA, MLA_KV_LORA, MLA_ROPE, D_MODEL, D_MODEL)

LANES = 128
TOKEN_TILE = 512
MIX_TILE = 512
ROW_PART = 256
ATTN_TILE = 512
ATTN_PAIRS = 1
FFN_CHUNK = 512
VMEM_LIMIT = 56 * 1024 * 1024

_EXP2_FOX = math.log2(math.e) / math.sqrt(HEAD_DIM)
_EXP2_MLA = math.log2(math.e) / math.sqrt(MLA_NOPE + MLA_ROPE)

F32 = jnp.float32
BF16 = jnp.bfloat16


def _const_spec(shape):
    zeros = (0,) * len(shape)
    return pl.BlockSpec(shape, lambda *_: zeros, pipeline_mode=pl.Buffered(1))


def _rms(x):
    return x * lax.rsqrt(jnp.mean(x * x, axis=-1, keepdims=True) + NORM_EPS)


def _dot(a, b):
    return jnp.dot(a, b, preferred_element_type=F32)


def _adaln_kernel(c_ref, w_ref, b_ref, o_ref):
    c = c_ref[...]
    sc = c * jax.nn.sigmoid(c)
    o_ref[...] = _dot(sc.astype(BF16), w_ref[...].astype(BF16)) + b_ref[...]


def _adaln_mod(c, w_ada, b_ada):
    bsz, d = c.shape
    n = w_ada.shape[1]
    tn = 1024
    return pl.pallas_call(
        _adaln_kernel,
        grid=(n // tn,),
        in_specs=[pl.BlockSpec((bsz, d), lambda j: (0, 0)),
                  pl.BlockSpec((d, tn), lambda j: (0, j)),
                  pl.BlockSpec((1, tn), lambda j: (0, j))],
        out_specs=pl.BlockSpec((bsz, tn), lambda j: (0, j)),
        out_shape=jax.ShapeDtypeStruct((bsz, n), F32),
        name="adaln_mod",
    )(c, w_ada, b_ada.reshape(1, n))


def _cumsum_rows(x):
    n = x.shape[0]
    row = lax.broadcasted_iota(jnp.int32, x.shape, 0)
    d = 1
    while d < n:
        x = x + jnp.where(row >= d, pltpu.roll(x, d, axis=0), 0.0)
        d *= 2
    return x


def _rope_tables(pos_row, invf_col):
    n = pos_row.shape[1]
    groups = LANES // MLA_ROPE
    ang = invf_col * pos_row
    c = jnp.cos(ang)
    s = jnp.sin(ang)
    cos_blocks, sin_blocks = [], []
    for t0 in range(0, n, LANES):
        ct = c[:, t0:t0 + LANES]
        st = s[:, t0:t0 + LANES]
        cos_blocks.append(jnp.concatenate([ct, ct] * groups, axis=0).T)
        sin_blocks.append(jnp.concatenate([-st, st] * groups, axis=0).T)
    return jnp.concatenate(cos_blocks, axis=0), jnp.concatenate(sin_blocks, axis=0)


def _rope(x, cos4, sin4, lane):
    half = MLA_ROPE // 2
    swapped = jnp.where((lane & (MLA_ROPE - 1)) < half, pltpu.roll(x, LANES - half, axis=1), pltpu.roll(x, half, axis=1))
    return x * cos4 + swapped * sin4


def _store_values_transposed(vt_ref, rows, v):
    n = v.shape[0]
    row = lax.broadcasted_iota(jnp.int32, (LANES, n), 0)
    for pair in range(HEADS // 2):
        vt = v[:, pair * LANES:(pair + 1) * LANES].T
        even = jnp.where(row < HEAD_DIM, vt, jnp.where(row == HEAD_DIM, 1.0, 0.0))
        odd = jnp.where(row >= HEAD_DIM, vt, jnp.where(row == 0, 1.0, 0.0))
        vt_ref[0, 2 * pair, :, rows] = even.astype(vt_ref.dtype)
        vt_ref[0, 2 * pair + 1, :, rows] = odd.astype(vt_ref.dtype)


def _token_proj_kernel(x_ref, mod_ref, gpre_ref, bf_ref, gq_ref, gkv_ref, pos_ref, invf_ref,
                       wqkv_ref, wmisc_ref, wcq_ref, wckv_ref, wg_ref, wq_ref, wkv_ref, sel_ref,
                       qf_ref, kf_ref, vf_ref, qm_ref, km_ref, vm_ref, gf_ref, gm_ref,
                       carry_ref, cos_ref, sin_ref):
    tm = x_ref.shape[1]

    @pl.when(pl.program_id(1) == 0)
    def _():
        carry_ref[...] = jnp.zeros_like(carry_ref)

    cos_ref[...], sin_ref[...] = _rope_tables(pos_ref[0], invf_ref[:, 0:1])

    shift = mod_ref[0, 0:1, :]
    scale = mod_ref[0, 1:2, :]

    def project(rows, n):
        h = (_rms(x_ref[0, rows, :]) * gpre_ref[...] * (1.0 + scale) + shift).astype(BF16)
        lane = lax.broadcasted_iota(jnp.int32, (n, LANES), 1)
        cos4 = cos_ref[rows, :]
        sin4 = sin_ref[rows, :]
        low = lane < HEAD_DIM

        misc = _dot(h, wmisc_ref[...])
        logit = misc + bf_ref[...]
        logf = jnp.minimum(logit, 0.0) - jnp.log(1.0 + jnp.exp(-jnp.abs(logit)))
        cum = _cumsum_rows(logf) + carry_ref[0:1, :]
        carry_ref[0:1, :] = cum[n - 1:n, :]

        cq = _dot(h, wcq_ref[...])
        ckv = _dot(h, wckv_ref[...])
        nq = (_rms(cq) * gq_ref[...]).astype(BF16)
        nkv = (_rms(ckv) * gkv_ref[...]).astype(BF16)
        for gi, ref in enumerate((gf_ref, gm_ref)):
            g = _dot(h, wg_ref[:, gi * D_MODEL:(gi + 1) * D_MODEL])
            ref[0, rows, :] = jax.nn.sigmoid(g).astype(ref.dtype)

        qq = _dot(nq, wq_ref[...]) * _EXP2_MLA
        nope_w = HEADS * MLA_NOPE
        q_rope = [_rope(qq[:, nope_w + g * LANES:nope_w + (g + 1) * LANES], cos4, sin4, lane)
                  for g in range(HEADS * MLA_ROPE // LANES)]
        per_group = LANES // MLA_ROPE
        for hh in range(HEADS):
            pair = hh // 2
            nope = qq[:, pair * LANES:(pair + 1) * LANES]
            src_lane = (hh % per_group) * MLA_ROPE
            dst_lane = HEAD_DIM if hh % 2 == 0 else 0
            rope = q_rope[hh // per_group]
            if src_lane != dst_lane:
                rope = pltpu.roll(rope, (dst_lane - src_lane) % LANES, axis=1)
            in_rope = (lane >= dst_lane) & (lane < dst_lane + MLA_ROPE)
            own = low if hh % 2 == 0 else jnp.logical_not(low)
            qm_ref[0, hh, rows, :] = jnp.where(own, nope, jnp.where(in_rope, rope, 0.0)).astype(qm_ref.dtype)
        kv = _dot(nkv, wkv_ref[...])
        _store_values_transposed(vm_ref, rows, kv[:, nope_w:])
        in_rope = (lane >= MLA_NOPE) & (lane < MLA_NOPE + MLA_ROPE)
        kpe_even = jnp.where(in_rope, _rope(misc, cos4, sin4, lane), 0.0)
        kpe_odd = pltpu.roll(kpe_even, LANES - HEAD_DIM, axis=1)
        for hh in range(HEADS):
            pair = hh // 2
            nope = kv[:, pair * LANES:(pair + 1) * LANES]
            k_full = jnp.where(low, nope, kpe_even) if hh % 2 == 0 else jnp.where(low, kpe_odd, nope)
            km_ref[0, hh, rows, :] = k_full.astype(km_ref.dtype)

        p_qkv = _dot(h, wqkv_ref[...])
        _store_values_transposed(vf_ref, rows, p_qkv[:, 2 * FOX_WIDTH:])

        a = cum * math.log2(math.e)
        a_hi = a.astype(BF16).astype(F32)
        r1 = a - a_hi
        a_mid = r1.astype(BF16).astype(F32)
        a_lo = (r1 - a_mid).astype(BF16).astype(F32)
        z = jnp.where(lane < 8, a_hi,
                      jnp.where(lane < 16, a_mid,
                                jnp.where(lane < 24, a_lo,
                                          jnp.where(lane == 24, 1.0, 0.0))))
        aug = _dot(z.astype(BF16), sel_ref[...])
        pairs = HEADS // 2
        for hh in range(HEADS):
            pair = hh // 2
            keep = (lane < HEAD_DIM) if hh % 2 == 0 else (lane >= HEAD_DIM)
            xq = p_qkv[:, pair * LANES:(pair + 1) * LANES]
            xk = p_qkv[:, FOX_WIDTH + pair * LANES:FOX_WIDTH + (pair + 1) * LANES]
            aq = aug[:, pair * LANES:(pair + 1) * LANES]
            ak = aug[:, (pairs + pair) * LANES:(pairs + pair + 1) * LANES]
            qf_ref[0, hh, rows, :] = jnp.where(keep, xq * _EXP2_FOX, aq).astype(qf_ref.dtype)
            kf_ref[0, hh, rows, :] = jnp.where(keep, xk, ak).astype(kf_ref.dtype)

    n = tm // 2
    for r0 in (0, n):
        project(pl.ds(r0, n), n)


def _prep_proj_weights(w_in, w_uq, w_ukv, b_forget):
    d = w_in.shape[0]
    sp = np.cumsum(IN_WIDTHS)[:-1]
    w_fq, w_fk, w_fv, w_f, w_cq, w_ckv, w_kr, w_gf, w_gm = jnp.split(w_in, [int(v) for v in sp], axis=1)
    zeros = lambda n: jnp.zeros((d, n), w_in.dtype)
    w_misc = jnp.concatenate([w_f, w_f, w_f, zeros(MLA_NOPE - 3 * HEADS), w_kr,
                              zeros(LANES - MLA_NOPE - MLA_ROPE)], axis=1).astype(BF16)
    w_qkv = w_in[:, :3 * FOX_WIDTH].astype(BF16)
    w_g = w_in[:, -2 * D_MODEL:].astype(BF16)

    r = w_uq.shape[0]
    uq = w_uq.reshape(r, HEADS, MLA_NOPE + MLA_ROPE)
    w_q = jnp.concatenate([uq[:, :, :MLA_NOPE].reshape(r, HEADS * MLA_NOPE),
                           uq[:, :, MLA_NOPE:].reshape(r, HEADS * MLA_ROPE)], axis=1).astype(BF16)

    rk = w_ukv.shape[0]
    ukv = w_ukv.reshape(rk, HEADS, MLA_NOPE + MLA_V)
    w_kv = jnp.concatenate([ukv[:, :, :MLA_NOPE].reshape(rk, HEADS * MLA_NOPE),
                            ukv[:, :, MLA_NOPE:].reshape(rk, HEADS * MLA_V)], axis=1).astype(BF16)

    bf3 = jnp.concatenate([b_forget, b_forget, b_forget,
                           jnp.zeros((LANES - 3 * HEADS,), b_forget.dtype)]).reshape(1, LANES)
    return (w_qkv, w_misc, w_cq.astype(BF16), w_ckv.astype(BF16), w_g, w_q, w_kv), bf3


def _decay_selector():
    pairs = HEADS // 2
    sel = np.zeros((LANES, 2 * pairs * LANES), np.float32)
    for hh in range(HEADS):
        base_q = (hh // 2) * LANES + (HEAD_DIM if hh % 2 == 0 else 0)
        base_k = (pairs + hh // 2) * LANES + (HEAD_DIM if hh % 2 == 0 else 0)
        for piece in range(3):
            sel[piece * HEADS + hh, base_q + piece] = 1.0
            sel[3 * HEADS, base_q + 3 + piece] = 1.0
            sel[3 * HEADS, base_k + piece] = 1.0
            sel[piece * HEADS + hh, base_k + 3 + piece] = -1.0
    return jnp.asarray(sel, BF16)


def _token_proj(x, mod, g_pre, positions, weights, bf3, g_q, g_kv):
    bsz, seq, d = x.shape
    tm = TOKEN_TILE
    sel = _decay_selector()
    half = MLA_ROPE // 2
    inv_freq = 1.0 / (ROPE_THETA ** (np.arange(0, MLA_ROPE, 2, dtype=np.float32) / MLA_ROPE))
    invf = jnp.asarray(np.tile(inv_freq.astype(np.float32)[:, None], (1, LANES)))
    pos = positions.astype(F32).reshape(bsz, 1, seq)
    row = lambda b, i: (b, i, 0)
    head = lambda b, i: (b, 0, i, 0)
    head_shape = jax.ShapeDtypeStruct((bsz, HEADS, seq, LANES), BF16)
    head_spec = pl.BlockSpec((1, HEADS, tm, LANES), head)
    v_shape = jax.ShapeDtypeStruct((bsz, HEADS, LANES, seq), BF16)
    v_spec = pl.BlockSpec((1, HEADS, LANES, tm), lambda b, i: (b, 0, 0, i))
    g_shape = jax.ShapeDtypeStruct((bsz, seq, d), BF16)
    g_spec = pl.BlockSpec((1, tm, d), row)
    return pl.pallas_call(
        _token_proj_kernel,
        grid=(bsz, seq // tm),
        in_specs=[pl.BlockSpec((1, tm, d), row),
                  pl.BlockSpec((1, 6, d), lambda b, i: (b, 0, 0)),
                  _const_spec((1, d)), _const_spec((1, LANES)),
                  _const_spec((1, MLA_Q_LORA)), _const_spec((1, MLA_KV_LORA)),
                  pl.BlockSpec((1, 1, tm), lambda b, i: (b, 0, i)), _const_spec((half, LANES))]
                 + [_const_spec(w.shape) for w in weights] + [_const_spec(sel.shape)],
        out_specs=[head_spec, head_spec, v_spec, head_spec, head_spec, v_spec, g_spec, g_spec],
        out_shape=[head_shape, head_shape, v_shape, head_shape, head_shape, v_shape, g_shape, g_shape],
        scratch_shapes=[pltpu.VMEM((8, LANES), F32),
                        pltpu.VMEM((tm, LANES), F32),
                        pltpu.VMEM((tm, LANES), F32)],
        compiler_params=pltpu.CompilerParams(dimension_semantics=("arbitrary", "arbitrary"),
                                             vmem_limit_bytes=VMEM_LIMIT),
        name="token_proj",
    )(x, mod, g_pre.reshape(1, d), bf3, g_q.reshape(1, -1), g_kv.reshape(1, -1), pos, invf,
      *weights, sel)


def _attn_kernel(zero_ref, q_ref, k_ref, vt_ref, o_ref, sa_ref, sb_ref, m_ref, acc_ref, *, tile):
    seq = vt_ref.shape[3]
    nq = seq // tile
    n_heads = q_ref.shape[1]
    z = zero_ref[0]
    half = tile // 2
    nt = (((1,), (1,)), ((), ()))
    vrow = lax.broadcasted_iota(jnp.int32, (LANES, tile), 0)
    sum_row = (HEAD_DIM, 0)
    keep_even = vrow < HEAD_DIM
    krow = lax.broadcasted_iota(jnp.int32, (LANES, LANES), 0)
    qcol = lax.broadcasted_iota(jnp.int32, (LANES, LANES), 1)
    diag_ok = krow <= qcol

    def issue_scores(qi, j, s_out):
        for hh in range(n_heads):
            q = q_ref[0, hh, qi * tile:(qi + 1) * tile, :]
            k = k_ref[0, hh, j * tile:(j + 1) * tile, :]
            if j == qi:
                s_out[z + hh, :half, :] = lax.dot_general(k[:half], q, nt, preferred_element_type=F32)
                s_out[z + hh, half:, half:] = lax.dot_general(k[half:], q[half:], nt, preferred_element_type=F32)
            else:
                s_out[z + hh] = lax.dot_general(k, q, nt, preferred_element_type=F32)

    def softmax_pv(s_in, hh, vth, masked):
        alphas, p_cols = [], []
        for c0 in range(0, tile, LANES):
            n_keys = c0 + LANES if masked else tile
            n_fill = tile if (c0 >= half or not masked) else half
            m_old = m_ref[hh, :, c0:c0 + LANES]

            def strip(r0):
                sb = s_in[z + hh, r0:r0 + LANES, c0:c0 + LANES]
                return jnp.where(diag_ok, sb, -jnp.inf) if (masked and r0 == c0) else sb

            mx = functools.reduce(jnp.maximum, [strip(r0) for r0 in range(0, n_keys, LANES)])
            m_new = jnp.maximum(m_old, jnp.max(mx, axis=0, keepdims=True))
            parts = [jnp.exp2(strip(r0) - m_new).astype(BF16) for r0 in range(0, n_keys, LANES)]
            parts += [jnp.zeros((LANES, LANES), BF16)] * ((n_fill - n_keys) // LANES)
            p_cols.append(jnp.concatenate(parts, axis=0))
            m_ref[hh, :, c0:c0 + LANES] = m_new
            alphas.append(jnp.exp2(m_old - m_new))
        alpha = jnp.concatenate(alphas, axis=1)
        n_left = half // LANES
        if masked:
            pv = jnp.concatenate([_dot(vth[:, :half], jnp.concatenate(p_cols[:n_left], axis=1)),
                                  _dot(vth, jnp.concatenate(p_cols[n_left:], axis=1))], axis=1)
        else:
            pv = _dot(vth, jnp.concatenate(p_cols, axis=1))
        acc_ref[z + hh] = alpha * acc_ref[z + hh] + pv

    def reset_state():
        m_ref[...] = jnp.full(m_ref.shape, -jnp.inf, F32)
        acc_ref[...] = jnp.zeros(acc_ref.shape, F32)

    blocks = [(qi, j) for qi in range(nq) for j in range(qi + 1)]
    bufs = (sa_ref, sb_ref)
    reset_state()
    issue_scores(0, 0, bufs[0])
    for t, (qi, j) in enumerate(blocks):
        s_in, s_out = bufs[t % 2], bufs[(t + 1) % 2]
        if t + 1 < len(blocks):
            issue_scores(*blocks[t + 1], s_out)
        for hh in range(n_heads):
            softmax_pv(s_in, hh, vt_ref[0, hh, :, j * tile:(j + 1) * tile], j == qi)
        if j == qi:
            for pair in range(n_heads // 2):
                outs = []
                for hh in (2 * pair, 2 * pair + 1):
                    acc = acc_ref[z + hh]
                    outs.append(acc * (1.0 / acc[sum_row[hh % 2]:sum_row[hh % 2] + 1, :]))
                o_t = jnp.where(keep_even, outs[0], outs[1])
                o_ref[0, qi * tile:(qi + 1) * tile, pair * LANES:(pair + 1) * LANES] = o_t.T.astype(o_ref.dtype)
            if qi + 1 < nq:
                reset_state()


def _causal_attn(q, k, vt):
    bsz, heads, seq, _ = q.shape
    g = ATTN_PAIRS
    qk_spec = pl.BlockSpec((1, 2 * g, seq, LANES), lambda b, p: (b, p, 0, 0))
    vt_spec = pl.BlockSpec((1, 2 * g, LANES, seq), lambda b, p: (b, p, 0, 0))
    o_spec = pl.BlockSpec((1, seq, g * LANES), lambda b, p: (b, 0, p))
    kern = functools.partial(_attn_kernel, tile=ATTN_TILE)
    return pl.pallas_call(
        kern,
        grid=(bsz, heads // (2 * g)),
        in_specs=[pl.BlockSpec(memory_space=pltpu.SMEM), qk_spec, qk_spec, vt_spec],
        out_specs=o_spec,
        out_shape=jax.ShapeDtypeStruct((bsz, seq, heads * HEAD_DIM), BF16),
        scratch_shapes=[pltpu.VMEM((2 * g, ATTN_TILE, ATTN_TILE), F32),
                        pltpu.VMEM((2 * g, ATTN_TILE, ATTN_TILE), F32),
                        pltpu.VMEM((2 * g, 1, ATTN_TILE), F32),
                        pltpu.VMEM((2 * g, LANES, ATTN_TILE), F32)],
        compiler_params=pltpu.CompilerParams(dimension_semantics=("arbitrary", "arbitrary"),
                                             vmem_limit_bytes=VMEM_LIMIT),
        name="causal_attn",
    )(jnp.zeros((1,), jnp.int32), q, k, vt)


def _mix_ffn_kernel(x_ref, oa_ref, ob_ref, gf_ref, gm_ref, mod_ref, gpost_mix_ref, gpre_ffn_ref,
                    gpost_ffn_ref, wpf_ref, wpm_ref, wout_ref, win_ref, wdown_ref, o_ref):
    tm = x_ref.shape[1]
    shift = mod_ref[0, 3:4, :]
    scale = mod_ref[0, 4:5, :]
    gate = mod_ref[0, 5:6, :]

    def mix_matmuls(rows):
        pa = _dot(oa_ref[0, rows, :], wpf_ref[...])
        pb = _dot(ob_ref[0, rows, :], wpm_ref[...])
        merged = gf_ref[0, rows, :].astype(F32) * pa + gm_ref[0, rows, :].astype(F32) * pb
        return _dot(merged.astype(BF16), wout_ref[...])

    def mix_residual(rows, y):
        x = x_ref[0, rows, :] + mod_ref[0, 2:3, :] * (_rms(y) * gpost_mix_ref[...])
        h = (_rms(x) * gpre_ffn_ref[...] * (1.0 + scale) + shift).astype(BF16)
        return x, h

    chunks = [(lo, min(FFN_CHUNK, D_FF - lo)) for lo in range(0, D_FF, FFN_CHUNK)]

    def up_matmuls(h, lo, w):
        return _dot(h, win_ref[:, lo:lo + w]), _dot(h, win_ref[:, D_FF + lo:D_FF + lo + w])

    def swiglu_matmuls(h):
        y = None
        gu = up_matmuls(h, *chunks[0])
        for c, (lo, w) in enumerate(chunks):
            g, u = gu
            if c + 1 < len(chunks):
                gu = up_matmuls(h, *chunks[c + 1])
            act = (g * jax.nn.sigmoid(g) * u).astype(BF16)
            part = _dot(act, wdown_ref[lo:lo + w, :])
            y = part if y is None else y + part
        return y

    parts = [pl.ds(r0, ROW_PART) for r0 in range(0, tm, ROW_PART)]
    y_mix, xs, y_ffn = {}, {}, {}
    for t in range(len(parts) + 2):
        if t < len(parts):
            y_mix[t] = mix_matmuls(parts[t])
        if 0 <= t - 1 < len(parts):
            xs[t - 1], h = mix_residual(parts[t - 1], y_mix.pop(t - 1))
            y_ffn[t - 1] = swiglu_matmuls(h)
        if 0 <= t - 2 < len(parts):
            o_ref[0, parts[t - 2], :] = xs.pop(t - 2) + gate * (_rms(y_ffn.pop(t - 2)) * gpost_ffn_ref[...])


def _mix_ffn(x, o_a, o_b, gf, gm, mod, g_post_mix, g_pre_ffn, g_post_ffn, w_pf, w_pm, w_out, w_in, w_down):
    bsz, seq, d = x.shape
    tm = MIX_TILE
    row = lambda b, i: (b, i, 0)
    weights = (w_pf, w_pm, w_out, w_in, w_down)
    return pl.pallas_call(
        _mix_ffn_kernel,
        grid=(bsz, seq // tm),
        in_specs=[pl.BlockSpec((1, tm, d), row),
                  pl.BlockSpec((1, tm, FOX_WIDTH), row), pl.BlockSpec((1, tm, FOX_WIDTH), row),
                  pl.BlockSpec((1, tm, d), row), pl.BlockSpec((1, tm, d), row),
                  pl.BlockSpec((1, 6, d), lambda b, i: (b, 0, 0)),
                  _const_spec((1, d)), _const_spec((1, d)), _const_spec((1, d))]
                 + [_const_spec(w.shape) for w in weights],
        out_specs=pl.BlockSpec((1, tm, d), row),
        out_shape=jax.ShapeDtypeStruct(x.shape, F32),
        compiler_params=pltpu.CompilerParams(dimension_semantics=("arbitrary", "arbitrary"),
                                             vmem_limit_bytes=VMEM_LIMIT),
        name="mix_ffn",
    )(x, o_a, o_b, gf, gm, mod, g_post_mix.reshape(1, d), g_pre_ffn.reshape(1, d), g_post_ffn.reshape(1, d),
      *weights)


def kernel(x, c, positions, w_ada, b_ada, g_pre_mix, g_post_mix, g_pre_ffn, g_post_ffn, w_in, b_forget,
           g_q_lora, w_uq, g_kv_lora, w_ukv, w_proj_fox, w_proj_mla, w_out, w_ffn_in, w_ffn_out):
    bsz, seq, d = x.shape
    depth = w_ada.shape[0]
    for l in range(depth):
        mod = _adaln_mod(c, w_ada[l], b_ada[l]).reshape(bsz, 6, d)
        weights, bf3 = _prep_proj_weights(w_in[l], w_uq[l], w_ukv[l], b_forget[l])
        qf, kf, vf, qm, km, vm, gf, gm = _token_proj(
            x, mod, g_pre_mix[l], positions, weights, bf3, g_q_lora[l], g_kv_lora[l])
        o_a = _causal_attn(qf, kf, vf)
        o_b = _causal_attn(qm, km, vm)
        x = _mix_ffn(x, o_a, o_b, gf, gm, mod, g_post_mix[l], g_pre_ffn[l], g_post_ffn[l],
                     w_proj_fox[l].astype(BF16), w_proj_mla[l].astype(BF16), w_out[l].astype(BF16),
                     w_ffn_in[l].astype(BF16), w_ffn_out[l].astype(BF16))
    return x
```

```python
import functools
import math

import jax
import jax.numpy as jnp
import numpy as np
from jax import lax
from jax.experimental import pallas as pl
from jax.experimental.pallas import tpu as pltpu

D_MODEL = 1024
HEADS = 8
HEAD_DIM = 64
FOX_WIDTH = HEADS * HEAD_DIM
MLA_NOPE = 64
MLA_ROPE = 32
MLA_V = 64
MLA_Q_LORA = 768
MLA_KV_LORA = 256
D_FF = 2816
ROPE_THETA = 10000.0
NORM_EPS = 1e-6
IN_WIDTHS = (FOX_WIDTH, FOX_WIDTH, FOX_WIDTH, HEADS, MLA_Q_LORA, MLA_KV_LORA, MLA_ROPE, D_MODEL, D_MODEL)

LANES = 128
TOKEN_TILE = 512
MIX_TILE = 512
ROW_PART = 256
ATTN_TILE = 512
ATTN_PAIRS = 1
FFN_CHUNK = 256
VMEM_LIMIT = 56 * 1024 * 1024

_EXP2_FOX = math.log2(math.e) / math.sqrt(HEAD_DIM)
_EXP2_MLA = math.log2(math.e) / math.sqrt(MLA_NOPE + MLA_ROPE)

F32 = jnp.float32
BF16 = jnp.bfloat16


def _const_spec(shape):
    zeros = (0,) * len(shape)
    return pl.BlockSpec(shape, lambda *_: zeros, pipeline_mode=pl.Buffered(1))


def _rms(x):
    return x * lax.rsqrt(jnp.mean(x * x, axis=-1, keepdims=True) + NORM_EPS)


def _dot(a, b):
    return jnp.dot(a, b, preferred_element_type=F32)


def _adaln_kernel(c_ref, w_ref, b_ref, o_ref):
    c = c_ref[...]
    sc = c * jax.nn.sigmoid(c)
    o_ref[...] = _dot(sc.astype(BF16), w_ref[...].astype(BF16)) + b_ref[...]


def _adaln_mod(c, w_ada, b_ada):
    bsz, d = c.shape
    n = w_ada.shape[1]
    tn = 2048
    return pl.pallas_call(
        _adaln_kernel,
        grid=(n // tn,),
        in_specs=[pl.BlockSpec((bsz, d), lambda j: (0, 0)),
                  pl.BlockSpec((d, tn), lambda j: (0, j)),
                  pl.BlockSpec((1, tn), lambda j: (0, j))],
        out_specs=pl.BlockSpec((bsz, tn), lambda j: (0, j)),
        out_shape=jax.ShapeDtypeStruct((bsz, n), F32),
        name="adaln_mod",
    )(c, w_ada, b_ada.reshape(1, n))


def _cumsum_rows(x):
    n = x.shape[0]
    row = lax.broadcasted_iota(jnp.int32, x.shape, 0)
    d = 1
    while d < n:
        x = x + jnp.where(row >= d, pltpu.roll(x, d, axis=0), 0.0)
        d *= 2
    return x


def _rope_tables(pos_row, invf_col):
    n = pos_row.shape[1]
    groups = LANES // MLA_ROPE
    ang = invf_col * pos_row
    c = jnp.cos(ang)
    s = jnp.sin(ang)
    cos_blocks, sin_blocks = [], []
    for t0 in range(0, n, LANES):
        ct = c[:, t0:t0 + LANES]
        st = s[:, t0:t0 + LANES]
        cos_blocks.append(jnp.concatenate([ct, ct] * groups, axis=0).T)
        sin_blocks.append(jnp.concatenate([-st, st] * groups, axis=0).T)
    return jnp.concatenate(cos_blocks, axis=0), jnp.concatenate(sin_blocks, axis=0)


def _rope(x, cos4, sin4, lane):
    half = MLA_ROPE // 2
    swapped = jnp.where((lane & (MLA_ROPE - 1)) < half, pltpu.roll(x, LANES - half, axis=1), pltpu.roll(x, half, axis=1))
    return x * cos4 + swapped * sin4


def _store_values_transposed(vt_ref, rows, v):
    n = v.shape[0]
    row = lax.broadcasted_iota(jnp.int32, (LANES, n), 0)
    for pair in range(HEADS // 2):
        vt = v[:, pair * LANES:(pair + 1) * LANES].T
        even = jnp.where(row < HEAD_DIM, vt, jnp.where(row == HEAD_DIM, 1.0, 0.0))
        odd = jnp.where(row >= HEAD_DIM, vt, jnp.where(row == 0, 1.0, 0.0))
        vt_ref[0, 2 * pair, :, rows] = even.astype(vt_ref.dtype)
        vt_ref[0, 2 * pair + 1, :, rows] = odd.astype(vt_ref.dtype)


def _token_proj_kernel(x_ref, mod_ref, gpre_ref, bf_ref, gq_ref, gkv_ref, pos_ref, invf_ref,
                       wqkv_ref, wmisc_ref, wcq_ref, wckv_ref, wg_ref, wq_ref, wkv_ref, sel_ref,
                       qf_ref, kf_ref, vf_ref, qm_ref, km_ref, vm_ref, gf_ref, gm_ref,
                       carry_ref, cos_ref, sin_ref):
    tm = x_ref.shape[1]

    @pl.when(pl.program_id(1) == 0)
    def _():
        carry_ref[...] = jnp.zeros_like(carry_ref)

    cos_ref[...], sin_ref[...] = _rope_tables(pos_ref[0], invf_ref[:, 0:1])

    shift = mod_ref[0, 0:1, :]
    scale = mod_ref[0, 1:2, :]

    def project(rows, n):
        h = (_rms(x_ref[0, rows, :]) * gpre_ref[...] * (1.0 + scale) + shift).astype(BF16)
        lane = lax.broadcasted_iota(jnp.int32, (n, LANES), 1)
        cos4 = cos_ref[rows, :]
        sin4 = sin_ref[rows, :]
        low = lane < HEAD_DIM

        misc = _dot(h, wmisc_ref[...])
        logit = misc + bf_ref[...]
        logf = jnp.minimum(logit, 0.0) - jnp.log(1.0 + jnp.exp(-jnp.abs(logit)))
        cum = _cumsum_rows(logf) + carry_ref[0:1, :]
        carry_ref[0:1, :] = cum[n - 1:n, :]

        cq = _dot(h, wcq_ref[...])
        ckv = _dot(h, wckv_ref[...])
        nq = (_rms(cq) * gq_ref[...]).astype(BF16)
        nkv = (_rms(ckv) * gkv_ref[...]).astype(BF16)
        for gi, ref in enumerate((gf_ref, gm_ref)):
            g = _dot(h, wg_ref[:, gi * D_MODEL:(gi + 1) * D_MODEL])
            ref[0, rows, :] = jax.nn.sigmoid(g).astype(ref.dtype)

        qq = _dot(nq, wq_ref[...]) * _EXP2_MLA
        nope_w = HEADS * MLA_NOPE
        q_rope = [_rope(qq[:, nope_w + g * LANES:nope_w + (g + 1) * LANES], cos4, sin4, lane)
                  for g in range(HEADS * MLA_ROPE // LANES)]
        per_group = LANES // MLA_ROPE
        for hh in range(HEADS):
            pair = hh // 2
            nope = qq[:, pair * LANES:(pair + 1) * LANES]
            src_lane = (hh % per_group) * MLA_ROPE
            dst_lane = HEAD_DIM if hh % 2 == 0 else 0
            rope = q_rope[hh // per_group]
            if src_lane != dst_lane:
                rope = pltpu.roll(rope, (dst_lane - src_lane) % LANES, axis=1)
            in_rope = (lane >= dst_lane) & (lane < dst_lane + MLA_ROPE)
            own = low if hh % 2 == 0 else jnp.logical_not(low)
            qm_ref[0, hh, rows, :] = jnp.where(own, nope, jnp.where(in_rope, rope, 0.0)).astype(qm_ref.dtype)
        kv = _dot(nkv, wkv_ref[...])
        _store_values_transposed(vm_ref, rows, kv[:, nope_w:])
        in_rope = (lane >= MLA_NOPE) & (lane < MLA_NOPE + MLA_ROPE)
        kpe_even = jnp.where(in_rope, _rope(misc, cos4, sin4, lane), 0.0)
        kpe_odd = pltpu.roll(kpe_even, LANES - HEAD_DIM, axis=1)
        for hh in range(HEADS):
            pair = hh // 2
            nope = kv[:, pair * LANES:(pair + 1) * LANES]
            k_full = jnp.where(low, nope, kpe_even) if hh % 2 == 0 else jnp.where(low, kpe_odd, nope)
            km_ref[0, hh, rows, :] = k_full.astype(km_ref.dtype)

        p_qkv = _dot(h, wqkv_ref[...])
        _store_values_transposed(vf_ref, rows, p_qkv[:, 2 * FOX_WIDTH:])

        a = cum * math.log2(math.e)
        a_hi = a.astype(BF16).astype(F32)
        r1 = a - a_hi
        a_mid = r1.astype(BF16).astype(F32)
        a_lo = (r1 - a_mid).astype(BF16).astype(F32)
        z = jnp.where(lane < 8, a_hi,
                      jnp.where(lane < 16, a_mid,
                                jnp.where(lane < 24, a_lo,
                                          jnp.where(lane == 24, 1.0, 0.0))))
        aug = _dot(z.astype(BF16), sel_ref[...])
        pairs = HEADS // 2
        for hh in range(HEADS):
            pair = hh // 2
            keep = (lane < HEAD_DIM) if hh % 2 == 0 else (lane >= HEAD_DIM)
            xq = p_qkv[:, pair * LANES:(pair + 1) * LANES]
            xk = p_qkv[:, FOX_WIDTH + pair * LANES:FOX_WIDTH + (pair + 1) * LANES]
            aq = aug[:, pair * LANES:(pair + 1) * LANES]
            ak = aug[:, (pairs + pair) * LANES:(pairs + pair + 1) * LANES]
            qf_ref[0, hh, rows, :] = jnp.where(keep, xq * _EXP2_FOX, aq).astype(qf_ref.dtype)
            kf_ref[0, hh, rows, :] = jnp.where(keep, xk, ak).astype(kf_ref.dtype)

    n = tm // 2
    for r0 in (0, n):
        project(pl.ds(r0, n), n)


def _prep_proj_weights(w_in, w_uq, w_ukv, b_forget):
    d = w_in.shape[0]
    sp = np.cumsum(IN_WIDTHS)[:-1]
    w_fq, w_fk, w_fv, w_f, w_cq, w_ckv, w_kr, w_gf, w_gm = jnp.split(w_in, [int(v) for v in sp], axis=1)
    zeros = lambda n: jnp.zeros((d, n), w_in.dtype)
    w_misc = jnp.concatenate([w_f, w_f, w_f, zeros(MLA_NOPE - 3 * HEADS), w_kr,
                              zeros(LANES - MLA_NOPE - MLA_ROPE)], axis=1).astype(BF16)
    w_qkv = w_in[:, :3 * FOX_WIDTH].astype(BF16)
    w_g = w_in[:, -2 * D_MODEL:].astype(BF16)

    r = w_uq.shape[0]
    uq = w_uq.reshape(r, HEADS, MLA_NOPE + MLA_ROPE)
    w_q = jnp.concatenate([uq[:, :, :MLA_NOPE].reshape(r, HEADS * MLA_NOPE),
                           uq[:, :, MLA_NOPE:].reshape(r, HEADS * MLA_ROPE)], axis=1).astype(BF16)

    rk = w_ukv.shape[0]
    ukv = w_ukv.reshape(rk, HEADS, MLA_NOPE + MLA_V)
    w_kv = jnp.concatenate([ukv[:, :, :MLA_NOPE].reshape(rk, HEADS * MLA_NOPE),
                            ukv[:, :, MLA_NOPE:].reshape(rk, HEADS * MLA_V)], axis=1).astype(BF16)

    bf3 = jnp.concatenate([b_forget, b_forget, b_forget,
                           jnp.zeros((LANES - 3 * HEADS,), b_forget.dtype)]).reshape(1, LANES)
    return (w_qkv, w_misc, w_cq.astype(BF16), w_ckv.astype(BF16), w_g, w_q, w_kv), bf3


def _decay_selector():
    pairs = HEADS // 2
    sel = np.zeros((LANES, 2 * pairs * LANES), np.float32)
    for hh in range(HEADS):
        base_q = (hh // 2) * LANES + (HEAD_DIM if hh % 2 == 0 else 0)
        base_k = (pairs + hh // 2) * LANES + (HEAD_DIM if hh % 2 == 0 else 0)
        for piece in range(3):
            sel[piece * HEADS + hh, base_q + piece] = 1.0
            sel[3 * HEADS, base_q + 3 + piece] = 1.0
            sel[3 * HEADS, base_k + piece] = 1.0
            sel[piece * HEADS + hh, base_k + 3 + piece] = -1.0
    return jnp.asarray(sel, BF16)


def _token_proj(x, mod, g_pre, positions, weights, bf3, g_q, g_kv):
    bsz, seq, d = x.shape
    tm = TOKEN_TILE
    sel = _decay_selector()
    half = MLA_ROPE // 2
    inv_freq = 1.0 / (ROPE_THETA ** (np.arange(0, MLA_ROPE, 2, dtype=np.float32) / MLA_ROPE))
    invf = jnp.asarray(np.tile(inv_freq.astype(np.float32)[:, None], (1, LANES)))
    pos = positions.astype(F32).reshape(bsz, 1, seq)
    row = lambda b, i: (b, i, 0)
    head = lambda b, i: (b, 0, i, 0)
    head_shape = jax.ShapeDtypeStruct((bsz, HEADS, seq, LANES), BF16)
    head_spec = pl.BlockSpec((1, HEADS, tm, LANES), head)
    v_shape = jax.ShapeDtypeStruct((bsz, HEADS, LANES, seq), BF16)
    v_spec = pl.BlockSpec((1, HEADS, LANES, tm), lambda b, i: (b, 0, 0, i))
    g_shape = jax.ShapeDtypeStruct((bsz, seq, d), BF16)
    g_spec = pl.BlockSpec((1, tm, d), row)
    return pl.pallas_call(
        _token_proj_kernel,
        grid=(bsz, seq // tm),
        in_specs=[pl.BlockSpec((1, tm, d), row),
                  pl.BlockSpec((1, 6, d), lambda b, i: (b, 0, 0)),
                  _const_spec((1, d)), _const_spec((1, LANES)),
                  _const_spec((1, MLA_Q_LORA)), _const_spec((1, MLA_KV_LORA)),
                  pl.BlockSpec((1, 1, tm), lambda b, i: (b, 0, i)), _const_spec((half, LANES))]
                 + [_const_spec(w.shape) for w in weights] + [_const_spec(sel.shape)],
        out_specs=[head_spec, head_spec, v_spec, head_spec, head_spec, v_spec, g_spec, g_spec],
        out_shape=[head_shape, head_shape, v_shape, head_shape, head_shape, v_shape, g_shape, g_shape],
        scratch_shapes=[pltpu.VMEM((8, LANES), F32),
                        pltpu.VMEM((tm, LANES), F32),
                        pltpu.VMEM((tm, LANES), F32)],
        compiler_params=pltpu.CompilerParams(dimension_semantics=("arbitrary", "arbitrary"),
                                             vmem_limit_bytes=VMEM_LIMIT),
        name="token_proj",
    )(x, mod, g_pre.reshape(1, d), bf3, g_q.reshape(1, -1), g_kv.reshape(1, -1), pos, invf,
      *weights, sel)


def _attn_kernel(zero_ref, q_ref, k_ref, vt_ref, o_ref, sa_ref, sb_ref, m_ref, acc_ref, *, tile):
    seq = vt_ref.shape[3]
    nq = seq // tile
    n_heads = q_ref.shape[1]
    z = zero_ref[0]
    half = tile // 2
    nt = (((1,), (1,)), ((), ()))
    vrow = lax.broadcasted_iota(jnp.int32, (LANES, tile), 0)
    sum_row = (HEAD_DIM, 0)
    keep_even = vrow < HEAD_DIM
    krow = lax.broadcasted_iota(jnp.int32, (LANES, LANES), 0)
    qcol = lax.broadcasted_iota(jnp.int32, (LANES, LANES), 1)
    diag_ok = krow <= qcol

    def issue_scores(qi, j, s_out, heads=None):
        for hh in (range(n_heads) if heads is None else heads):
            q = q_ref[0, hh, qi * tile:(qi + 1) * tile, :]
            k = k_ref[0, hh, j * tile:(j + 1) * tile, :]
            if j == qi:
                s_out[z + hh, :half, :] = lax.dot_general(k[:half], q, nt, preferred_element_type=F32)
                s_out[z + hh, half:, half:] = lax.dot_general(k[half:], q[half:], nt, preferred_element_type=F32)
            else:
                s_out[z + hh] = lax.dot_general(k, q, nt, preferred_element_type=F32)

    def softmax_pv(s_in, hh, vth, masked):
        alphas, p_cols = [], []
        for c0 in range(0, tile, LANES):
            n_keys = c0 + LANES if masked else tile
            n_fill = tile if (c0 >= half or not masked) else half
            m_old = m_ref[hh, :, c0:c0 + LANES]

            def strip(r0):
                sb = s_in[z + hh, r0:r0 + LANES, c0:c0 + LANES]
                return jnp.where(diag_ok, sb, -jnp.inf) if (masked and r0 == c0) else sb

            mx = functools.reduce(jnp.maximum, [strip(r0) for r0 in range(0, n_keys, LANES)])
            m_new = jnp.maximum(m_old, jnp.max(mx, axis=0, keepdims=True))
            parts = [jnp.exp2(strip(r0) - m_new).astype(BF16) for r0 in range(0, n_keys, LANES)]
            parts += [jnp.zeros((LANES, LANES), BF16)] * ((n_fill - n_keys) // LANES)
            p_cols.append(jnp.concatenate(parts, axis=0))
            m_ref[hh, :, c0:c0 + LANES] = m_new
            alphas.append(jnp.exp2(m_old - m_new))
        alpha = jnp.concatenate(alphas, axis=1)
        n_left = half // LANES
        if masked:
            pv = jnp.concatenate([_dot(vth[:, :half], jnp.concatenate(p_cols[:n_left], axis=1)),
                                  _dot(vth, jnp.concatenate(p_cols[n_left:], axis=1))], axis=1)
        else:
            pv = _dot(vth, jnp.concatenate(p_cols, axis=1))
        acc_ref[z + hh] = alpha * acc_ref[z + hh] + pv

    def reset_state():
        m_ref[...] = jnp.full(m_ref.shape, -jnp.inf, F32)
        acc_ref[...] = jnp.zeros(acc_ref.shape, F32)

    blocks = [(qi, j) for qi in range(nq) for j in range(qi + 1)]
    bufs = (sa_ref, sb_ref)
    reset_state()
    issue_scores(0, 0, bufs[0])
    for t, (qi, j) in enumerate(blocks):
        s_in, s_out = bufs[t % 2], bufs[(t + 1) % 2]
        for hh in range(n_heads):
            if t + 1 < len(blocks):
                issue_scores(*blocks[t + 1], s_out, heads=(hh,))
            softmax_pv(s_in, hh, vt_ref[0, hh, :, j * tile:(j + 1) * tile], j == qi)
        if j == qi:
            for pair in range(n_heads // 2):
                outs = []
                for hh in (2 * pair, 2 * pair + 1):
                    acc = acc_ref[z + hh]
                    outs.append(acc * (1.0 / acc[sum_row[hh % 2]:sum_row[hh % 2] + 1, :]))
                o_t = jnp.where(keep_even, outs[0], outs[1])
                o_ref[0, qi * tile:(qi + 1) * tile, pair * LANES:(pair + 1) * LANES] = o_t.T.astype(o_ref.dtype)
            if qi + 1 < nq:
                reset_state()


def _causal_attn(q, k, vt):
    bsz, heads, seq, _ = q.shape
    g = ATTN_PAIRS
    qk_spec = pl.BlockSpec((1, 2 * g, seq, LANES), lambda b, p: (b, p, 0, 0))
    vt_spec = pl.BlockSpec((1, 2 * g, LANES, seq), lambda b, p: (b, p, 0, 0))
    o_spec = pl.BlockSpec((1, seq, g * LANES), lambda b, p: (b, 0, p))
    kern = functools.partial(_attn_kernel, tile=ATTN_TILE)
    return pl.pallas_call(
        kern,
        grid=(bsz, heads // (2 * g)),
        in_specs=[pl.BlockSpec(memory_space=pltpu.SMEM), qk_spec, qk_spec, vt_spec],
        out_specs=o_spec,
        out_shape=jax.ShapeDtypeStruct((bsz, seq, heads * HEAD_DIM), BF16),
        scratch_shapes=[pltpu.VMEM((2 * g, ATTN_TILE, ATTN_TILE), F32),
                        pltpu.VMEM((2 * g, ATTN_TILE, ATTN_TILE), F32),
                        pltpu.VMEM((2 * g, 1, ATTN_TILE), F32),
                        pltpu.VMEM((2 * g, LANES, ATTN_TILE), F32)],
        compiler_params=pltpu.CompilerParams(dimension_semantics=("arbitrary", "arbitrary"),
                                             vmem_limit_bytes=VMEM_LIMIT),
        name="causal_attn",
    )(jnp.zeros((1,), jnp.int32), q, k, vt)


def _mix_ffn_kernel(x_ref, oa_ref, ob_ref, gf_ref, gm_ref, mod_ref, gpost_mix_ref, gpre_ffn_ref,
                    gpost_ffn_ref, wpf_ref, wpm_ref, wout_ref, win_ref, wdown_ref, o_ref):
    tm = x_ref.shape[1]
    shift = mod_ref[0, 3:4, :]
    scale = mod_ref[0, 4:5, :]
    gate = mod_ref[0, 5:6, :]

    def mix_matmuls(rows):
        pa = _dot(oa_ref[0, rows, :], wpf_ref[...])
        pb = _dot(ob_ref[0, rows, :], wpm_ref[...])
        merged = gf_ref[0, rows, :].astype(F32) * pa + gm_ref[0, rows, :].astype(F32) * pb
        return _dot(merged.astype(BF16), wout_ref[...])

    def mix_residual(rows, y):
        x = x_ref[0, rows, :] + mod_ref[0, 2:3, :] * (_rms(y) * gpost_mix_ref[...])
        h = (_rms(x) * gpre_ffn_ref[...] * (1.0 + scale) + shift).astype(BF16)
        return x, h

    chunks = [(lo, min(FFN_CHUNK, D_FF - lo)) for lo in range(0, D_FF, FFN_CHUNK)]

    def up_matmuls(h, lo, w):
        return _dot(h, win_ref[:, lo:lo + w]), _dot(h, win_ref[:, D_FF + lo:D_FF + lo + w])

    def swiglu_matmuls(h):
        y = None
        gu = up_matmuls(h, *chunks[0])
        for c, (lo, w) in enumerate(chunks):
            g, u = gu
            if c + 1 < len(chunks):
                gu = up_matmuls(h, *chunks[c + 1])
            act = (g * jax.nn.sigmoid(g) * u).astype(BF16)
            part = _dot(act, wdown_ref[lo:lo + w, :])
            y = part if y is None else y + part
        return y

    parts = [pl.ds(r0, ROW_PART) for r0 in range(0, tm, ROW_PART)]
    y_mix, xs, y_ffn = {}, {}, {}
    for t in range(len(parts) + 2):
        if t < len(parts):
            y_mix[t] = mix_matmuls(parts[t])
        if 0 <= t - 1 < len(parts):
            xs[t - 1], h = mix_residual(parts[t - 1], y_mix.pop(t - 1))
            y_ffn[t - 1] = swiglu_matmuls(h)
        if 0 <= t - 2 < len(parts):
            o_ref[0, parts[t - 2], :] = xs.pop(t - 2) + gate * (_rms(y_ffn.pop(t - 2)) * gpost_ffn_ref[...])


def _mix_ffn(x, o_a, o_b, gf, gm, mod, g_post_mix, g_pre_ffn, g_post_ffn, w_pf, w_pm, w_out, w_in, w_down):
    bsz, seq, d = x.shape
    tm = MIX_TILE
    row = lambda b, i: (b, i, 0)
    weights = (w_pf, w_pm, w_out, w_in, w_down)
    return pl.pallas_call(
        _mix_ffn_kernel,
        grid=(bsz, seq // tm),
        in_specs=[pl.BlockSpec((1, tm, d), row),
                  pl.BlockSpec((1, tm, FOX_WIDTH), row), pl.BlockSpec((1, tm, FOX_WIDTH), row),
                  pl.BlockSpec((1, tm, d), row), pl.BlockSpec((1, tm, d), row),
                  pl.BlockSpec((1, 6, d), lambda b, i: (b, 0, 0)),
                  _const_spec((1, d)), _const_spec((1, d)), _const_spec((1, d))]
                 + [_const_spec(w.shape) for w in weights],
        out_specs=pl.BlockSpec((1, tm, d), row),
        out_shape=jax.ShapeDtypeStruct(x.shape, F32),
        compiler_params=pltpu.CompilerParams(dimension_semantics=("arbitrary", "arbitrary"),
                                             vmem_limit_bytes=VMEM_LIMIT),
        name="mix_ffn",
    )(x, o_a, o_b, gf, gm, mod, g_post_mix.reshape(1, d), g_pre_ffn.reshape(1, d), g_post_ffn.reshape(1, d),
      *weights)


def kernel(x, c, positions, w_ada, b_ada, g_pre_mix, g_post_mix, g_pre_ffn, g_post_ffn, w_in, b_forget,
           g_q_lora, w_uq, g_kv_lora, w_ukv, w_proj_fox, w_proj_mla, w_out, w_ffn_in, w_ffn_out):
    bsz, seq, d = x.shape
    depth = w_ada.shape[0]
    for l in range(depth):
        mod = _adaln_mod(c, w_ada[l], b_ada[l]).reshape(bsz, 6, d)
        weights, bf3 = _prep_proj_weights(w_in[l], w_uq[l], w_ukv[l], b_forget[l])
        qf, kf, vf, qm, km, vm, gf, gm = _token_proj(
            x, mod, g_pre_mix[l], positions, weights, bf3, g_q_lora[l], g_kv_lora[l])
        o_a = _causal_attn(qf, kf, vf)
        o_b = _causal_attn(qm, km, vm)
        x = _mix_ffn(x, o_a, o_b, gf, gm, mod, g_post_mix[l], g_pre_ffn[l], g_post_ffn[l],
                     w_proj_fox[l].astype(BF16), w_proj_mla[l].astype(BF16), w_out[l].astype(BF16),
                     w_ffn_in[l].astype(BF16), w_ffn_out[l].astype(BF16))
    return x
```

```python
import functools
import math

import jax
import jax.numpy as jnp
import numpy as np
from jax import lax
from jax.experimental import pallas as pl
from jax.experimental.pallas import tpu as pltpu

D_MODEL = 1024
HEADS = 8
HEAD_DIM = 64
FOX_WIDTH = HEADS * HEAD_DIM
MLA_NOPE = 64
MLA_ROPE = 32
MLA_V = 64
MLA_Q_LORA = 768
MLA_KV_LORA = 256
D_FF = 2816
ROPE_THETA = 10000.0
NORM_EPS = 1e-6
IN_WIDTHS = (FOX_WIDTH, FOX_WIDTH, FOX_WIDTH, HEADS, MLA_Q_LORA, MLA_KV_LORA, MLA_ROPE, D_MODEL, D_MODEL)

LANES = 128
TOKEN_TILE = 512
MIX_TILE = 512
ROW_PART = 256
ATTN_TILE = 512
ATTN_PAIRS = 1
FFN_CHUNK = 256
VMEM_LIMIT = 56 * 1024 * 1024

_EXP2_FOX = math.log2(math.e) / math.sqrt(HEAD_DIM)
_EXP2_MLA = math.log2(math.e) / math.sqrt(MLA_NOPE + MLA_ROPE)

F32 = jnp.float32
BF16 = jnp.bfloat16


def _const_spec(shape):
    zeros = (0,) * len(shape)
    return pl.BlockSpec(shape, lambda *_: zeros, pipeline_mode=pl.Buffered(1))


def _rms(x):
    return x * lax.rsqrt(jnp.mean(x * x, axis=-1, keepdims=True) + NORM_EPS)


def _dot(a, b):
    return jnp.dot(a, b, preferred_element_type=F32)


def _adaln_kernel(c_ref, w_ref, b_ref, o_ref):
    c = c_ref[...]
    sc = c * jax.nn.sigmoid(c)
    o_ref[...] = _dot(sc.astype(BF16), w_ref[...].astype(BF16)) + b_ref[...]


def _adaln_mod(c, w_ada, b_ada):
    bsz, d = c.shape
    n = w_ada.shape[1]
    tn = 2048
    return pl.pallas_call(
        _adaln_kernel,
        grid=(n // tn,),
        in_specs=[pl.BlockSpec((bsz, d), lambda j: (0, 0)),
                  pl.BlockSpec((d, tn), lambda j: (0, j)),
                  pl.BlockSpec((1, tn), lambda j: (0, j))],
        out_specs=pl.BlockSpec((bsz, tn), lambda j: (0, j)),
        out_shape=jax.ShapeDtypeStruct((bsz, n), F32),
        name="adaln_mod",
    )(c, w_ada, b_ada.reshape(1, n))


def _cumsum_rows(x):
    n = x.shape[0]
    row = lax.broadcasted_iota(jnp.int32, x.shape, 0)
    d = 1
    while d < n:
        x = x + jnp.where(row >= d, pltpu.roll(x, d, axis=0), 0.0)
        d *= 2
    return x


def _rope_tables(pos_row, invf_col):
    n = pos_row.shape[1]
    groups = LANES // MLA_ROPE
    ang = invf_col * pos_row
    c = jnp.cos(ang)
    s = jnp.sin(ang)
    cos_blocks, sin_blocks = [], []
    for t0 in range(0, n, LANES):
        ct = c[:, t0:t0 + LANES]
        st = s[:, t0:t0 + LANES]
        cos_blocks.append(jnp.concatenate([ct, ct] * groups, axis=0).T)
        sin_blocks.append(jnp.concatenate([-st, st] * groups, axis=0).T)
    return jnp.concatenate(cos_blocks, axis=0), jnp.concatenate(sin_blocks, axis=0)


def _rope(x, cos4, sin4, lane):
    half = MLA_ROPE // 2
    swapped = jnp.where((lane & (MLA_ROPE - 1)) < half, pltpu.roll(x, LANES - half, axis=1), pltpu.roll(x, half, axis=1))
    return x * cos4 + swapped * sin4


def _store_values_transposed(vt_ref, rows, v):
    n = v.shape[0]
    row = lax.broadcasted_iota(jnp.int32, (LANES, n), 0)
    for pair in range(HEADS // 2):
        vt = v[:, pair * LANES:(pair + 1) * LANES].T
        even = jnp.where(row < HEAD_DIM, vt, jnp.where(row == HEAD_DIM, 1.0, 0.0))
        odd = jnp.where(row >= HEAD_DIM, vt, jnp.where(row == 0, 1.0, 0.0))
        vt_ref[0, 2 * pair, :, rows] = even.astype(vt_ref.dtype)
        vt_ref[0, 2 * pair + 1, :, rows] = odd.astype(vt_ref.dtype)


def _token_proj_kernel(x_ref, mod_ref, gpre_ref, bf_ref, gq_ref, gkv_ref, pos_ref, invf_ref,
                       wqkv_ref, wmisc_ref, wcq_ref, wckv_ref, wg_ref, wq_ref, wkv_ref, sel_ref,
                       qf_ref, kf_ref, vf_ref, qm_ref, km_ref, vm_ref, gf_ref, gm_ref,
                       carry_ref, cos_ref, sin_ref):
    tm = x_ref.shape[1]

    @pl.when(pl.program_id(1) == 0)
    def _():
        carry_ref[...] = jnp.zeros_like(carry_ref)

    cos_ref[...], sin_ref[...] = _rope_tables(pos_ref[0], invf_ref[:, 0:1])

    shift = mod_ref[0, 0:1, :]
    scale = mod_ref[0, 1:2, :]

    def project(rows, n):
        h = (_rms(x_ref[0, rows, :]) * gpre_ref[...] * (1.0 + scale) + shift).astype(BF16)
        lane = lax.broadcasted_iota(jnp.int32, (n, LANES), 1)
        cos4 = cos_ref[rows, :]
        sin4 = sin_ref[rows, :]
        low = lane < HEAD_DIM

        misc = _dot(h, wmisc_ref[...])
        logit = misc + bf_ref[...]
        logf = jnp.minimum(logit, 0.0) - jnp.log(1.0 + jnp.exp(-jnp.abs(logit)))
        cum = _cumsum_rows(logf) + carry_ref[0:1, :]
        carry_ref[0:1, :] = cum[n - 1:n, :]

        cq = _dot(h, wcq_ref[...])
        ckv = _dot(h, wckv_ref[...])
        nq = (_rms(cq) * gq_ref[...]).astype(BF16)
        nkv = (_rms(ckv) * gkv_ref[...]).astype(BF16)
        for gi, ref in enumerate((gf_ref, gm_ref)):
            g = _dot(h, wg_ref[:, gi * D_MODEL:(gi + 1) * D_MODEL])
            ref[0, rows, :] = jax.nn.sigmoid(g).astype(ref.dtype)

        qq = _dot(nq, wq_ref[...]) * _EXP2_MLA
        nope_w = HEADS * MLA_NOPE
        q_rope = [_rope(qq[:, nope_w + g * LANES:nope_w + (g + 1) * LANES], cos4, sin4, lane)
                  for g in range(HEADS * MLA_ROPE // LANES)]
        per_group = LANES // MLA_ROPE
        for hh in range(HEADS):
            pair = hh // 2
            nope = qq[:, pair * LANES:(pair + 1) * LANES]
            src_lane = (hh % per_group) * MLA_ROPE
            dst_lane = HEAD_DIM if hh % 2 == 0 else 0
            rope = q_rope[hh // per_group]
            if src_lane != dst_lane:
                rope = pltpu.roll(rope, (dst_lane - src_lane) % LANES, axis=1)
            in_rope = (lane >= dst_lane) & (lane < dst_lane + MLA_ROPE)
            own = low if hh % 2 == 0 else jnp.logical_not(low)
            qm_ref[0, hh, rows, :] = jnp.where(own, nope, jnp.where(in_rope, rope, 0.0)).astype(qm_ref.dtype)
        kv = _dot(nkv, wkv_ref[...])
        _store_values_transposed(vm_ref, rows, kv[:, nope_w:])
        in_rope = (lane >= MLA_NOPE) & (lane < MLA_NOPE + MLA_ROPE)
        kpe_even = jnp.where(in_rope, _rope(misc, cos4, sin4, lane), 0.0)
        kpe_odd = pltpu.roll(kpe_even, LANES - HEAD_DIM, axis=1)
        for hh in range(HEADS):
            pair = hh // 2
            nope = kv[:, pair * LANES:(pair + 1) * LANES]
            k_full = jnp.where(low, nope, kpe_even) if hh % 2 == 0 else jnp.where(low, kpe_odd, nope)
            km_ref[0, hh, rows, :] = k_full.astype(km_ref.dtype)

        p_qkv = _dot(h, wqkv_ref[...])
        _store_values_transposed(vf_ref, rows, p_qkv[:, 2 * FOX_WIDTH:])

        a = cum * math.log2(math.e)
        a_hi = a.astype(BF16).astype(F32)
        r1 = a - a_hi
        a_mid = r1.astype(BF16).astype(F32)
        a_lo = (r1 - a_mid).astype(BF16).astype(F32)
        z = jnp.where(lane < 8, a_hi,
                      jnp.where(lane < 16, a_mid,
                                jnp.where(lane < 24, a_lo,
                                          jnp.where(lane == 24, 1.0, 0.0))))
        aug = _dot(z.astype(BF16), sel_ref[...])
        pairs = HEADS // 2
        for hh in range(HEADS):
            pair = hh // 2
            keep = (lane < HEAD_DIM) if hh % 2 == 0 else (lane >= HEAD_DIM)
            xq = p_qkv[:, pair * LANES:(pair + 1) * LANES]
            xk = p_qkv[:, FOX_WIDTH + pair * LANES:FOX_WIDTH + (pair + 1) * LANES]
            aq = aug[:, pair * LANES:(pair + 1) * LANES]
            ak = aug[:, (pairs + pair) * LANES:(pairs + pair + 1) * LANES]
            qf_ref[0, hh, rows, :] = jnp.where(keep, xq * _EXP2_FOX, aq).astype(qf_ref.dtype)
            kf_ref[0, hh, rows, :] = jnp.where(keep, xk, ak).astype(kf_ref.dtype)

    n = tm // 2
    for r0 in (0, n):
        project(pl.ds(r0, n), n)


_IN_OFFSETS = tuple(int(v) for v in np.cumsum((0,) + IN_WIDTHS))


def _split_w_in_kernel(w_ref, qkv_ref, cq_ref, ckv_ref, g_ref):
    w = w_ref[...]
    o = _IN_OFFSETS
    qkv_ref[...] = w[:, o[0]:o[3]].astype(BF16)
    cq_ref[...] = w[:, o[4]:o[5]].astype(BF16)
    ckv_ref[...] = w[:, o[5]:o[6]].astype(BF16)
    g_ref[...] = w[:, o[7]:o[9]].astype(BF16)


def _split_w_in(w_in):
    d, n = w_in.shape
    rows = 256
    widths = (3 * FOX_WIDTH, MLA_Q_LORA, MLA_KV_LORA, 2 * D_MODEL)
    return pl.pallas_call(
        _split_w_in_kernel,
        grid=(d // rows,),
        in_specs=[pl.BlockSpec((rows, n), lambda i: (i, 0))],
        out_specs=[pl.BlockSpec((rows, w), lambda i: (i, 0)) for w in widths],
        out_shape=[jax.ShapeDtypeStruct((d, w), BF16) for w in widths],
        name="split_w_in",
    )(w_in)


def _prep_proj_weights(w_in, w_uq, w_ukv, b_forget):
    d = w_in.shape[0]
    o = _IN_OFFSETS
    w_f = w_in[:, o[3]:o[4]]
    w_kr = w_in[:, o[6]:o[7]]
    zeros = lambda n: jnp.zeros((d, n), w_in.dtype)
    w_misc = jnp.concatenate([w_f, w_f, w_f, zeros(MLA_NOPE - 3 * HEADS), w_kr,
                              zeros(LANES - MLA_NOPE - MLA_ROPE)], axis=1).astype(BF16)
    w_qkv, w_cq, w_ckv, w_g = _split_w_in(w_in)

    r = w_uq.shape[0]
    uq = w_uq.reshape(r, HEADS, MLA_NOPE + MLA_ROPE)
    w_q = jnp.concatenate([uq[:, :, :MLA_NOPE].reshape(r, HEADS * MLA_NOPE),
                           uq[:, :, MLA_NOPE:].reshape(r, HEADS * MLA_ROPE)], axis=1).astype(BF16)

    rk = w_ukv.shape[0]
    ukv = w_ukv.reshape(rk, HEADS, MLA_NOPE + MLA_V)
    w_kv = jnp.concatenate([ukv[:, :, :MLA_NOPE].reshape(rk, HEADS * MLA_NOPE),
                            ukv[:, :, MLA_NOPE:].reshape(rk, HEADS * MLA_V)], axis=1).astype(BF16)

    bf3 = jnp.concatenate([b_forget, b_forget, b_forget,
                           jnp.zeros((LANES - 3 * HEADS,), b_forget.dtype)]).reshape(1, LANES)
    return (w_qkv, w_misc, w_cq, w_ckv, w_g, w_q, w_kv), bf3


def _decay_selector():
    pairs = HEADS // 2
    sel = np.zeros((LANES, 2 * pairs * LANES), np.float32)
    for hh in range(HEADS):
        base_q = (hh // 2) * LANES + (HEAD_DIM if hh % 2 == 0 else 0)
        base_k = (pairs + hh // 2) * LANES + (HEAD_DIM if hh % 2 == 0 else 0)
        for piece in range(3):
            sel[piece * HEADS + hh, base_q + piece] = 1.0
            sel[3 * HEADS, base_q + 3 + piece] = 1.0
            sel[3 * HEADS, base_k + piece] = 1.0
            sel[piece * HEADS + hh, base_k + 3 + piece] = -1.0
    return jnp.asarray(sel, BF16)


def _token_proj(x, mod, g_pre, positions, weights, bf3, g_q, g_kv):
    bsz, seq, d = x.shape
    tm = TOKEN_TILE
    sel = _decay_selector()
    half = MLA_ROPE // 2
    inv_freq = 1.0 / (ROPE_THETA ** (np.arange(0, MLA_ROPE, 2, dtype=np.float32) / MLA_ROPE))
    invf = jnp.asarray(np.tile(inv_freq.astype(np.float32)[:, None], (1, LANES)))
    pos = positions.astype(F32).reshape(bsz, 1, seq)
    row = lambda b, i: (b, i, 0)
    head = lambda b, i: (b, 0, i, 0)
    head_shape = jax.ShapeDtypeStruct((bsz, HEADS, seq, LANES), BF16)
    head_spec = pl.BlockSpec((1, HEADS, tm, LANES), head)
    v_shape = jax.ShapeDtypeStruct((bsz, HEADS, LANES, seq), BF16)
    v_spec = pl.BlockSpec((1, HEADS, LANES, tm), lambda b, i: (b, 0, 0, i))
    g_shape = jax.ShapeDtypeStruct((bsz, seq, d), BF16)
    g_spec = pl.BlockSpec((1, tm, d), row)
    return pl.pallas_call(
        _token_proj_kernel,
        grid=(bsz, seq // tm),
        in_specs=[pl.BlockSpec((1, tm, d), row),
                  pl.BlockSpec((1, 6, d), lambda b, i: (b, 0, 0)),
                  _const_spec((1, d)), _const_spec((1, LANES)),
                  _const_spec((1, MLA_Q_LORA)), _const_spec((1, MLA_KV_LORA)),
                  pl.BlockSpec((1, 1, tm), lambda b, i: (b, 0, i)), _const_spec((half, LANES))]
                 + [_const_spec(w.shape) for w in weights] + [_const_spec(sel.shape)],
        out_specs=[head_spec, head_spec, v_spec, head_spec, head_spec, v_spec, g_spec, g_spec],
        out_shape=[head_shape, head_shape, v_shape, head_shape, head_shape, v_shape, g_shape, g_shape],
        scratch_shapes=[pltpu.VMEM((8, LANES), F32),
                        pltpu.VMEM((tm, LANES), F32),
                        pltpu.VMEM((tm, LANES), F32)],
        compiler_params=pltpu.CompilerParams(dimension_semantics=("arbitrary", "arbitrary"),
                                             vmem_limit_bytes=VMEM_LIMIT),
        name="token_proj",
    )(x, mod, g_pre.reshape(1, d), bf3, g_q.reshape(1, -1), g_kv.reshape(1, -1), pos, invf,
      *weights, sel)


def _attn_kernel(zero_ref, q_ref, k_ref, vt_ref, o_ref, sa_ref, sb_ref, m_ref, acc_ref, *, tile):
    seq = vt_ref.shape[3]
    nq = seq // tile
    n_heads = q_ref.shape[1]
    z = zero_ref[0]
    half = tile // 2
    nt = (((1,), (1,)), ((), ()))
    vrow = lax.broadcasted_iota(jnp.int32, (LANES, tile), 0)
    sum_row = (HEAD_DIM, 0)
    keep_even = vrow < HEAD_DIM
    krow = lax.broadcasted_iota(jnp.int32, (LANES, LANES), 0)
    qcol = lax.broadcasted_iota(jnp.int32, (LANES, LANES), 1)
    diag_ok = krow <= qcol

    def issue_scores(qi, j, s_out, heads=None):
        for hh in (range(n_heads) if heads is None else heads):
            q = q_ref[0, hh, qi * tile:(qi + 1) * tile, :]
            k = k_ref[0, hh, j * tile:(j + 1) * tile, :]
            if j == qi:
                s_out[z + hh, :half, :] = lax.dot_general(k[:half], q, nt, preferred_element_type=F32)
                s_out[z + hh, half:, half:] = lax.dot_general(k[half:], q[half:], nt, preferred_element_type=F32)
            else:
                s_out[z + hh] = lax.dot_general(k, q, nt, preferred_element_type=F32)

    def softmax_pv(s_in, hh, vth, masked):
        alphas, p_cols = [], []
        for c0 in range(0, tile, LANES):
            n_keys = c0 + LANES if masked else tile
            n_fill = tile if (c0 >= half or not masked) else half
            m_old = m_ref[hh, :, c0:c0 + LANES]

            def strip(r0):
                sb = s_in[z + hh, r0:r0 + LANES, c0:c0 + LANES]
                return jnp.where(diag_ok, sb, -jnp.inf) if (masked and r0 == c0) else sb

            mx = functools.reduce(jnp.maximum, [strip(r0) for r0 in range(0, n_keys, LANES)])
            m_new = jnp.maximum(m_old, jnp.max(mx, axis=0, keepdims=True))
            parts = [jnp.exp2(strip(r0) - m_new).astype(BF16) for r0 in range(0, n_keys, LANES)]
            parts += [jnp.zeros((LANES, LANES), BF16)] * ((n_fill - n_keys) // LANES)
            p_cols.append(jnp.concatenate(parts, axis=0))
            m_ref[hh, :, c0:c0 + LANES] = m_new
            alphas.append(jnp.exp2(m_old - m_new))
        alpha = jnp.concatenate(alphas, axis=1)
        n_left = half // LANES
        if masked:
            pv = jnp.concatenate([_dot(vth[:, :half], jnp.concatenate(p_cols[:n_left], axis=1)),
                                  _dot(vth, jnp.concatenate(p_cols[n_left:], axis=1))], axis=1)
        else:
            pv = _dot(vth, jnp.concatenate(p_cols, axis=1))
        acc_ref[z + hh] = alpha * acc_ref[z + hh] + pv

    def reset_state():
        m_ref[...] = jnp.full(m_ref.shape, -jnp.inf, F32)
        acc_ref[...] = jnp.zeros(acc_ref.shape, F32)

    blocks = [(qi, j) for qi in range(nq) for j in range(qi + 1)]
    bufs = (sa_ref, sb_ref)
    reset_state()
    issue_scores(0, 0, bufs[0])
    for t, (qi, j) in enumerate(blocks):
        s_in, s_out = bufs[t % 2], bufs[(t + 1) % 2]
        for hh in range(n_heads):
            if t + 1 < len(blocks):
                issue_scores(*blocks[t + 1], s_out, heads=(hh,))
            softmax_pv(s_in, hh, vt_ref[0, hh, :, j * tile:(j + 1) * tile], j == qi)
        if j == qi:
            for pair in range(n_heads // 2):
                outs = []
                for hh in (2 * pair, 2 * pair + 1):
                    acc = acc_ref[z + hh]
                    outs.append(acc * (1.0 / acc[sum_row[hh % 2]:sum_row[hh % 2] + 1, :]))
                o_t = jnp.where(keep_even, outs[0], outs[1])
                o_ref[0, qi * tile:(qi + 1) * tile, pair * LANES:(pair + 1) * LANES] = o_t.T.astype(o_ref.dtype)
            if qi + 1 < nq:
                reset_state()


def _causal_attn(q, k, vt):
    bsz, heads, seq, _ = q.shape
    g = ATTN_PAIRS
    qk_spec = pl.BlockSpec((1, 2 * g, seq, LANES), lambda b, p: (b, p, 0, 0))
    vt_spec = pl.BlockSpec((1, 2 * g, LANES, seq), lambda b, p: (b, p, 0, 0))
    o_spec = pl.BlockSpec((1, seq, g * LANES), lambda b, p: (b, 0, p))
    kern = functools.partial(_attn_kernel, tile=ATTN_TILE)
    return pl.pallas_call(
        kern,
        grid=(bsz, heads // (2 * g)),
        in_specs=[pl.BlockSpec(memory_space=pltpu.SMEM), qk_spec, qk_spec, vt_spec],
        out_specs=o_spec,
        out_shape=jax.ShapeDtypeStruct((bsz, seq, heads * HEAD_DIM), BF16),
        scratch_shapes=[pltpu.VMEM((2 * g, ATTN_TILE, ATTN_TILE), F32),
                        pltpu.VMEM((2 * g, ATTN_TILE, ATTN_TILE), F32),
                        pltpu.VMEM((2 * g, 1, ATTN_TILE), F32),
                        pltpu.VMEM((2 * g, LANES, ATTN_TILE), F32)],
        compiler_params=pltpu.CompilerParams(dimension_semantics=("arbitrary", "arbitrary"),
                                             vmem_limit_bytes=VMEM_LIMIT),
        name="causal_attn",
    )(jnp.zeros((1,), jnp.int32), q, k, vt)


def _mix_ffn_kernel(x_ref, oa_ref, ob_ref, gf_ref, gm_ref, mod_ref, gpost_mix_ref, gpre_ffn_ref,
                    gpost_ffn_ref, wpf_ref, wpm_ref, wout_ref, win_ref, wdown_ref, o_ref):
    tm = x_ref.shape[1]
    shift = mod_ref[0, 3:4, :]
    scale = mod_ref[0, 4:5, :]
    gate = mod_ref[0, 5:6, :]

    def mix_matmuls(rows):
        pa = _dot(oa_ref[0, rows, :], wpf_ref[...])
        pb = _dot(ob_ref[0, rows, :], wpm_ref[...])
        merged = gf_ref[0, rows, :].astype(F32) * pa + gm_ref[0, rows, :].astype(F32) * pb
        return _dot(merged.astype(BF16), wout_ref[...])

    def mix_residual(rows, y):
        x = x_ref[0, rows, :] + mod_ref[0, 2:3, :] * (_rms(y) * gpost_mix_ref[...])
        h = (_rms(x) * gpre_ffn_ref[...] * (1.0 + scale) + shift).astype(BF16)
        return x, h

    chunks = [(lo, min(FFN_CHUNK, D_FF - lo)) for lo in range(0, D_FF, FFN_CHUNK)]

    def up_matmuls(h, lo, w):
        return _dot(h, win_ref[:, lo:lo + w]), _dot(h, win_ref[:, D_FF + lo:D_FF + lo + w])

    def swiglu_matmuls(h):
        y = None
        gu = up_matmuls(h, *chunks[0])
        for c, (lo, w) in enumerate(chunks):
            g, u = gu
            if c + 1 < len(chunks):
                gu = up_matmuls(h, *chunks[c + 1])
            act = (g * jax.nn.sigmoid(g) * u).astype(BF16)
            part = _dot(act, wdown_ref[lo:lo + w, :])
            y = part if y is None else y + part
        return y

    parts = [pl.ds(r0, ROW_PART) for r0 in range(0, tm, ROW_PART)]
    y_mix, xs, y_ffn = {}, {}, {}
    for t in range(len(parts) + 2):
        if t < len(parts):
            y_mix[t] = mix_matmuls(parts[t])
        if 0 <= t - 1 < len(parts):
            xs[t - 1], h = mix_residual(parts[t - 1], y_mix.pop(t - 1))
            y_ffn[t - 1] = swiglu_matmuls(h)
        if 0 <= t - 2 < len(parts):
            o_ref[0, parts[t - 2], :] = xs.pop(t - 2) + gate * (_rms(y_ffn.pop(t - 2)) * gpost_ffn_ref[...])


def _mix_ffn(x, o_a, o_b, gf, gm, mod, g_post_mix, g_pre_ffn, g_post_ffn, w_pf, w_pm, w_out, w_in, w_down):
    bsz, seq, d = x.shape
    tm = MIX_TILE
    row = lambda b, i: (b, i, 0)
    weights = (w_pf, w_pm, w_out, w_in, w_down)
    return pl.pallas_call(
        _mix_ffn_kernel,
        grid=(bsz, seq // tm),
        in_specs=[pl.BlockSpec((1, tm, d), row),
                  pl.BlockSpec((1, tm, FOX_WIDTH), row), pl.BlockSpec((1, tm, FOX_WIDTH), row),
                  pl.BlockSpec((1, tm, d), row), pl.BlockSpec((1, tm, d), row),
                  pl.BlockSpec((1, 6, d), lambda b, i: (b, 0, 0)),
                  _const_spec((1, d)), _const_spec((1, d)), _const_spec((1, d))]
                 + [_const_spec(w.shape) for w in weights],
        out_specs=pl.BlockSpec((1, tm, d), row),
        out_shape=jax.ShapeDtypeStruct(x.shape, F32),
        compiler_params=pltpu.CompilerParams(dimension_semantics=("arbitrary", "arbitrary"),
                                             vmem_limit_bytes=VMEM_LIMIT),
        name="mix_ffn",
    )(x, o_a, o_b, gf, gm, mod, g_post_mix.reshape(1, d), g_pre_ffn.reshape(1, d), g_post_ffn.reshape(1, d),
      *weights)


def kernel(x, c, positions, w_ada, b_ada, g_pre_mix, g_post_mix, g_pre_ffn, g_post_ffn, w_in, b_forget,
           g_q_lora, w_uq, g_kv_lora, w_ukv, w_proj_fox, w_proj_mla, w_out, w_ffn_in, w_ffn_out):
    bsz, seq, d = x.shape
    depth = w_ada.shape[0]
    for l in range(depth):
        mod = _adaln_mod(c, w_ada[l], b_ada[l]).reshape(bsz, 6, d)
        weights, bf3 = _prep_proj_weights(w_in[l], w_uq[l], w_ukv[l], b_forget[l])
        qf, kf, vf, qm, km, vm, gf, gm = _token_proj(
            x, mod, g_pre_mix[l], positions, weights, bf3, g_q_lora[l], g_kv_lora[l])
        o_a = _causal_attn(qf, kf, vf)
        o_b = _causal_attn(qm, km, vm)
        x = _mix_ffn(x, o_a, o_b, gf, gm, mod, g_post_mix[l], g_pre_ffn[l], g_post_ffn[l],
                     w_proj_fox[l].astype(BF16), w_proj_mla[l].astype(BF16), w_out[l].astype(BF16),
                     w_ffn_in[l].astype(BF16), w_ffn_out[l].astype(BF16))
    return x
```

```python
import functools
import math

import jax
import jax.numpy as jnp
import numpy as np
from jax import lax
from jax.experimental import pallas as pl
from jax.experimental.pallas import tpu as pltpu

D_MODEL = 1024
HEADS = 8
HEAD_DIM = 64
FOX_WIDTH = HEADS * HEAD_DIM
MLA_NOPE = 64
MLA_ROPE = 32
MLA_V = 64
MLA_Q_LORA = 768
MLA_KV_LORA = 256
D_FF = 2816
ROPE_THETA = 10000.0
NORM_EPS = 1e-6
IN_WIDTHS = (FOX_WIDTH, FOX_WIDTH, FOX_WIDTH, HEADS, MLA_Q_LORA, MLA_KV_LORA, MLA_ROPE, D_MODEL, D_MODEL)

LANES = 128
TOKEN_TILE = 512
MIX_TILE = 512
ROW_PART = 256
ATTN_TILE = 512
ATTN_PAIRS = 1
FFN_CHUNK = 256
VMEM_LIMIT = 56 * 1024 * 1024

_EXP2_FOX = math.log2(math.e) / math.sqrt(HEAD_DIM)
_EXP2_MLA = math.log2(math.e) / math.sqrt(MLA_NOPE + MLA_ROPE)

F32 = jnp.float32
BF16 = jnp.bfloat16


def _const_spec(shape):
    zeros = (0,) * len(shape)
    return pl.BlockSpec(shape, lambda *_: zeros, pipeline_mode=pl.Buffered(1))


def _rms(x):
    return x * lax.rsqrt(jnp.mean(x * x, axis=-1, keepdims=True) + NORM_EPS)


def _dot(a, b):
    return jnp.dot(a, b, preferred_element_type=F32)


def _adaln_kernel(c_ref, w_ref, b_ref, o_ref):
    c = c_ref[...]
    sc = c * jax.nn.sigmoid(c)
    o_ref[...] = _dot(sc.astype(BF16), w_ref[...].astype(BF16)) + b_ref[...]


def _adaln_mod(c, w_ada, b_ada):
    bsz, d = c.shape
    n = w_ada.shape[1]
    tn = 2048
    return pl.pallas_call(
        _adaln_kernel,
        grid=(n // tn,),
        in_specs=[pl.BlockSpec((bsz, d), lambda j: (0, 0)),
                  pl.BlockSpec((d, tn), lambda j: (0, j)),
                  pl.BlockSpec((1, tn), lambda j: (0, j))],
        out_specs=pl.BlockSpec((bsz, tn), lambda j: (0, j)),
        out_shape=jax.ShapeDtypeStruct((bsz, n), F32),
        name="adaln_mod",
    )(c, w_ada, b_ada.reshape(1, n))


def _cumsum_rows(x):
    n = x.shape[0]
    row = lax.broadcasted_iota(jnp.int32, x.shape, 0)
    d = 1
    while d < n:
        x = x + jnp.where(row >= d, pltpu.roll(x, d, axis=0), 0.0)
        d *= 2
    return x


def _rope_tables(pos_row, invf_col):
    n = pos_row.shape[1]
    groups = LANES // MLA_ROPE
    ang = invf_col * pos_row
    c = jnp.cos(ang)
    s = jnp.sin(ang)
    cos_blocks, sin_blocks = [], []
    for t0 in range(0, n, LANES):
        ct = c[:, t0:t0 + LANES]
        st = s[:, t0:t0 + LANES]
        cos_blocks.append(jnp.concatenate([ct, ct] * groups, axis=0).T)
        sin_blocks.append(jnp.concatenate([-st, st] * groups, axis=0).T)
    return jnp.concatenate(cos_blocks, axis=0), jnp.concatenate(sin_blocks, axis=0)


def _rope(x, cos4, sin4, lane):
    half = MLA_ROPE // 2
    swapped = jnp.where((lane & (MLA_ROPE - 1)) < half, pltpu.roll(x, LANES - half, axis=1), pltpu.roll(x, half, axis=1))
    return x * cos4 + swapped * sin4


def _store_values_transposed(vt_ref, rows, v):
    n = v.shape[0]
    row = lax.broadcasted_iota(jnp.int32, (LANES, n), 0)
    for pair in range(HEADS // 2):
        vt = v[:, pair * LANES:(pair + 1) * LANES].T
        even = jnp.where(row < HEAD_DIM, vt, jnp.where(row == HEAD_DIM, 1.0, 0.0))
        odd = jnp.where(row >= HEAD_DIM, vt, jnp.where(row == 0, 1.0, 0.0))
        vt_ref[0, 2 * pair, :, rows] = even.astype(vt_ref.dtype)
        vt_ref[0, 2 * pair + 1, :, rows] = odd.astype(vt_ref.dtype)


def _token_proj_kernel(x_ref, mod_ref, gpre_ref, bf_ref, gq_ref, gkv_ref, pos_ref, invf_ref,
                       wqkv_ref, wmisc_ref, wcq_ref, wckv_ref, wg_ref, wq_ref, wkv_ref, sel_ref,
                       qf_ref, kf_ref, vf_ref, qm_ref, km_ref, vm_ref, gf_ref, gm_ref,
                       carry_ref, cos_ref, sin_ref):
    tm = x_ref.shape[1]

    @pl.when(pl.program_id(1) == 0)
    def _():
        carry_ref[...] = jnp.zeros_like(carry_ref)

    cos_ref[...], sin_ref[...] = _rope_tables(pos_ref[0], invf_ref[:, 0:1])

    shift = mod_ref[0, 0:1, :]
    scale = mod_ref[0, 1:2, :]

    def project(rows, n):
        h = (_rms(x_ref[0, rows, :]) * gpre_ref[...] * (1.0 + scale) + shift).astype(BF16)
        lane = lax.broadcasted_iota(jnp.int32, (n, LANES), 1)
        cos4 = cos_ref[rows, :]
        sin4 = sin_ref[rows, :]
        low = lane < HEAD_DIM

        misc = _dot(h, wmisc_ref[...])
        logit = misc + bf_ref[...]
        logf = jnp.minimum(logit, 0.0) - jnp.log(1.0 + jnp.exp(-jnp.abs(logit)))
        cum = _cumsum_rows(logf) + carry_ref[0:1, :]
        carry_ref[0:1, :] = cum[n - 1:n, :]

        cq = _dot(h, wcq_ref[...])
        ckv = _dot(h, wckv_ref[...])
        nq = (_rms(cq) * gq_ref[...]).astype(BF16)
        nkv = (_rms(ckv) * gkv_ref[...]).astype(BF16)
        for gi, ref in enumerate((gf_ref, gm_ref)):
            g = _dot(h, wg_ref[:, gi * D_MODEL:(gi + 1) * D_MODEL])
            ref[0, rows, :] = jax.nn.sigmoid(g).astype(ref.dtype)

        qq = _dot(nq, wq_ref[...]) * _EXP2_MLA
        nope_w = HEADS * MLA_NOPE
        q_rope = [_rope(qq[:, nope_w + g * LANES:nope_w + (g + 1) * LANES], cos4, sin4, lane)
                  for g in range(HEADS * MLA_ROPE // LANES)]
        per_group = LANES // MLA_ROPE
        for hh in range(HEADS):
            pair = hh // 2
            nope = qq[:, pair * LANES:(pair + 1) * LANES]
            src_lane = (hh % per_group) * MLA_ROPE
            dst_lane = HEAD_DIM if hh % 2 == 0 else 0
            rope = q_rope[hh // per_group]
            if src_lane != dst_lane:
                rope = pltpu.roll(rope, (dst_lane - src_lane) % LANES, axis=1)
            in_rope = (lane >= dst_lane) & (lane < dst_lane + MLA_ROPE)
            own = low if hh % 2 == 0 else jnp.logical_not(low)
            qm_ref[0, hh, rows, :] = jnp.where(own, nope, jnp.where(in_rope, rope, 0.0)).astype(qm_ref.dtype)
        kv = _dot(nkv, wkv_ref[...])
        _store_values_transposed(vm_ref, rows, kv[:, nope_w:])
        in_rope = (lane >= MLA_NOPE) & (lane < MLA_NOPE + MLA_ROPE)
        kpe_even = jnp.where(in_rope, _rope(misc, cos4, sin4, lane), 0.0)
        kpe_odd = pltpu.roll(kpe_even, LANES - HEAD_DIM, axis=1)
        for hh in range(HEADS):
            pair = hh // 2
            nope = kv[:, pair * LANES:(pair + 1) * LANES]
            k_full = jnp.where(low, nope, kpe_even) if hh % 2 == 0 else jnp.where(low, kpe_odd, nope)
            km_ref[0, hh, rows, :] = k_full.astype(km_ref.dtype)

        p_qkv = _dot(h, wqkv_ref[...])
        _store_values_transposed(vf_ref, rows, p_qkv[:, 2 * FOX_WIDTH:])

        a = cum * math.log2(math.e)
        a_hi = a.astype(BF16).astype(F32)
        r1 = a - a_hi
        a_mid = r1.astype(BF16).astype(F32)
        a_lo = (r1 - a_mid).astype(BF16).astype(F32)
        z = jnp.where(lane < 8, a_hi,
                      jnp.where(lane < 16, a_mid,
                                jnp.where(lane < 24, a_lo,
                                          jnp.where(lane == 24, 1.0, 0.0))))
        aug = _dot(z.astype(BF16), sel_ref[...])
        pairs = HEADS // 2
        for hh in range(HEADS):
            pair = hh // 2
            keep = (lane < HEAD_DIM) if hh % 2 == 0 else (lane >= HEAD_DIM)
            xq = p_qkv[:, pair * LANES:(pair + 1) * LANES]
            xk = p_qkv[:, FOX_WIDTH + pair * LANES:FOX_WIDTH + (pair + 1) * LANES]
            aq = aug[:, pair * LANES:(pair + 1) * LANES]
            ak = aug[:, (pairs + pair) * LANES:(pairs + pair + 1) * LANES]
            qf_ref[0, hh, rows, :] = jnp.where(keep, xq * _EXP2_FOX, aq).astype(qf_ref.dtype)
            kf_ref[0, hh, rows, :] = jnp.where(keep, xk, ak).astype(kf_ref.dtype)

    n = tm // 2
    for r0 in (0, n):
        project(pl.ds(r0, n), n)


_IN_OFFSETS = tuple(int(v) for v in np.cumsum((0,) + IN_WIDTHS))


def _split_w_in_kernel(wt_ref, qkv_ref, misc_ref, cq_ref, ckv_ref, g_ref):
    wt = wt_ref[0]
    cols = wt.shape[1]
    o = _IN_OFFSETS
    qkv_ref[...] = wt[o[0]:o[3], :].T.astype(BF16)
    cq_ref[...] = wt[o[4]:o[5], :].T.astype(BF16)
    ckv_ref[...] = wt[o[5]:o[6], :].T.astype(BF16)
    g_ref[...] = wt[o[7]:o[9], :].T.astype(BF16)
    f = wt[o[3]:o[4], :]
    kr = wt[o[6]:o[7], :]
    misc_t = jnp.concatenate([f, f, f, jnp.zeros((MLA_NOPE - 3 * HEADS, cols), F32), kr,
                              jnp.zeros((LANES - MLA_NOPE - MLA_ROPE, cols), F32)], axis=0)
    misc_ref[...] = misc_t.T.astype(BF16)


def _split_w_in(w_in_all, layer):
    _, d, n = w_in_all.shape
    cols = 256
    widths = (3 * FOX_WIDTH, LANES, MLA_Q_LORA, MLA_KV_LORA, 2 * D_MODEL)
    return pl.pallas_call(
        _split_w_in_kernel,
        grid=(d // cols,),
        in_specs=[pl.BlockSpec((1, n, cols), lambda i: (layer, 0, i))],
        out_specs=[pl.BlockSpec((cols, w), lambda i: (i, 0)) for w in widths],
        out_shape=[jax.ShapeDtypeStruct((d, w), BF16) for w in widths],
        name="split_w_in",
    )(jnp.transpose(w_in_all, (0, 2, 1)))


def _prep_proj_weights(w_in_all, layer, w_uq, w_ukv, b_forget):
    w_qkv, w_misc, w_cq, w_ckv, w_g = _split_w_in(w_in_all, layer)

    r = w_uq.shape[0]
    uq = w_uq.reshape(r, HEADS, MLA_NOPE + MLA_ROPE)
    w_q = jnp.concatenate([uq[:, :, :MLA_NOPE].reshape(r, HEADS * MLA_NOPE),
                           uq[:, :, MLA_NOPE:].reshape(r, HEADS * MLA_ROPE)], axis=1).astype(BF16)

    rk = w_ukv.shape[0]
    ukv = w_ukv.reshape(rk, HEADS, MLA_NOPE + MLA_V)
    w_kv = jnp.concatenate([ukv[:, :, :MLA_NOPE].reshape(rk, HEADS * MLA_NOPE),
                            ukv[:, :, MLA_NOPE:].reshape(rk, HEADS * MLA_V)], axis=1).astype(BF16)

    bf3 = jnp.concatenate([b_forget, b_forget, b_forget,
                           jnp.zeros((LANES - 3 * HEADS,), b_forget.dtype)]).reshape(1, LANES)
    return (w_qkv, w_misc, w_cq, w_ckv, w_g, w_q, w_kv), bf3


def _decay_selector():
    pairs = HEADS // 2
    sel = np.zeros((LANES, 2 * pairs * LANES), np.float32)
    for hh in range(HEADS):
        base_q = (hh // 2) * LANES + (HEAD_DIM if hh % 2 == 0 else 0)
        base_k = (pairs + hh // 2) * LANES + (HEAD_DIM if hh % 2 == 0 else 0)
        for piece in range(3):
            sel[piece * HEADS + hh, base_q + piece] = 1.0
            sel[3 * HEADS, base_q + 3 + piece] = 1.0
            sel[3 * HEADS, base_k + piece] = 1.0
            sel[piece * HEADS + hh, base_k + 3 + piece] = -1.0
    return jnp.asarray(sel, BF16)


def _token_proj(x, mod, g_pre, positions, weights, bf3, g_q, g_kv):
    bsz, seq, d = x.shape
    tm = TOKEN_TILE
    sel = _decay_selector()
    half = MLA_ROPE // 2
    inv_freq = 1.0 / (ROPE_THETA ** (np.arange(0, MLA_ROPE, 2, dtype=np.float32) / MLA_ROPE))
    invf = jnp.asarray(np.tile(inv_freq.astype(np.float32)[:, None], (1, LANES)))
    pos = positions.astype(F32).reshape(bsz, 1, seq)
    row = lambda b, i: (b, i, 0)
    head = lambda b, i: (b, 0, i, 0)
    head_shape = jax.ShapeDtypeStruct((bsz, HEADS, seq, LANES), BF16)
    head_spec = pl.BlockSpec((1, HEADS, tm, LANES), head)
    v_shape = jax.ShapeDtypeStruct((bsz, HEADS, LANES, seq), BF16)
    v_spec = pl.BlockSpec((1, HEADS, LANES, tm), lambda b, i: (b, 0, 0, i))
    g_shape = jax.ShapeDtypeStruct((bsz, seq, d), BF16)
    g_spec = pl.BlockSpec((1, tm, d), row)
    return pl.pallas_call(
        _token_proj_kernel,
        grid=(bsz, seq // tm),
        in_specs=[pl.BlockSpec((1, tm, d), row),
                  pl.BlockSpec((1, 6, d), lambda b, i: (b, 0, 0)),
                  _const_spec((1, d)), _const_spec((1, LANES)),
                  _const_spec((1, MLA_Q_LORA)), _const_spec((1, MLA_KV_LORA)),
                  pl.BlockSpec((1, 1, tm), lambda b, i: (b, 0, i)), _const_spec((half, LANES))]
                 + [_const_spec(w.shape) for w in weights] + [_const_spec(sel.shape)],
        out_specs=[head_spec, head_spec, v_spec, head_spec, head_spec, v_spec, g_spec, g_spec],
        out_shape=[head_shape, head_shape, v_shape, head_shape, head_shape, v_shape, g_shape, g_shape],
        scratch_shapes=[pltpu.VMEM((8, LANES), F32),
                        pltpu.VMEM((tm, LANES), F32),
                        pltpu.VMEM((tm, LANES), F32)],
        compiler_params=pltpu.CompilerParams(dimension_semantics=("arbitrary", "arbitrary"),
                                             vmem_limit_bytes=VMEM_LIMIT),
        name="token_proj",
    )(x, mod, g_pre.reshape(1, d), bf3, g_q.reshape(1, -1), g_kv.reshape(1, -1), pos, invf,
      *weights, sel)


def _attn_kernel(zero_ref, q_ref, k_ref, vt_ref, o_ref, sa_ref, sb_ref, m_ref, acc_ref, *, tile):
    seq = vt_ref.shape[3]
    nq = seq // tile
    n_heads = q_ref.shape[1]
    z = zero_ref[0]
    half = tile // 2
    nt = (((1,), (1,)), ((), ()))
    vrow = lax.broadcasted_iota(jnp.int32, (LANES, tile), 0)
    sum_row = (HEAD_DIM, 0)
    keep_even = vrow < HEAD_DIM
    krow = lax.broadcasted_iota(jnp.int32, (LANES, LANES), 0)
    qcol = lax.broadcasted_iota(jnp.int32, (LANES, LANES), 1)
    diag_ok = krow <= qcol

    def issue_scores(qi, j, s_out, heads=None):
        for hh in (range(n_heads) if heads is None else heads):
            q = q_ref[0, hh, qi * tile:(qi + 1) * tile, :]
            k = k_ref[0, hh, j * tile:(j + 1) * tile, :]
            if j == qi:
                s_out[z + hh, :half, :] = lax.dot_general(k[:half], q, nt, preferred_element_type=F32)
                s_out[z + hh, half:, half:] = lax.dot_general(k[half:], q[half:], nt, preferred_element_type=F32)
            else:
                s_out[z + hh] = lax.dot_general(k, q, nt, preferred_element_type=F32)

    def softmax_pv(s_in, hh, vth, masked):
        alphas, p_cols = [], []
        for c0 in range(0, tile, LANES):
            n_keys = c0 + LANES if masked else tile
            n_fill = tile if (c0 >= half or not masked) else half
            m_old = m_ref[hh, :, c0:c0 + LANES]

            def strip(r0):
                sb = s_in[z + hh, r0:r0 + LANES, c0:c0 + LANES]
                return jnp.where(diag_ok, sb, -jnp.inf) if (masked and r0 == c0) else sb

            mx = functools.reduce(jnp.maximum, [strip(r0) for r0 in range(0, n_keys, LANES)])
            m_new = jnp.maximum(m_old, jnp.max(mx, axis=0, keepdims=True))
            parts = [jnp.exp2(strip(r0) - m_new).astype(BF16) for r0 in range(0, n_keys, LANES)]
            parts += [jnp.zeros((LANES, LANES), BF16)] * ((n_fill - n_keys) // LANES)
            p_cols.append(jnp.concatenate(parts, axis=0))
            m_ref[hh, :, c0:c0 + LANES] = m_new
            alphas.append(jnp.exp2(m_old - m_new))
        alpha = jnp.concatenate(alphas, axis=1)
        n_left = half // LANES
        if masked:
            pv = jnp.concatenate([_dot(vth[:, :half], jnp.concatenate(p_cols[:n_left], axis=1)),
                                  _dot(vth, jnp.concatenate(p_cols[n_left:], axis=1))], axis=1)
        else:
            pv = _dot(vth, jnp.concatenate(p_cols, axis=1))
        acc_ref[z + hh] = alpha * acc_ref[z + hh] + pv

    def reset_state():
        m_ref[...] = jnp.full(m_ref.shape, -jnp.inf, F32)
        acc_ref[...] = jnp.zeros(acc_ref.shape, F32)

    blocks = [(qi, j) for qi in range(nq) for j in range(qi + 1)]
    bufs = (sa_ref, sb_ref)
    reset_state()
    issue_scores(0, 0, bufs[0])
    for t, (qi, j) in enumerate(blocks):
        s_in, s_out = bufs[t % 2], bufs[(t + 1) % 2]
        for hh in range(n_heads):
            if t + 1 < len(blocks):
                issue_scores(*blocks[t + 1], s_out, heads=(hh,))
            softmax_pv(s_in, hh, vt_ref[0, hh, :, j * tile:(j + 1) * tile], j == qi)
        if j == qi:
            for pair in range(n_heads // 2):
                outs = []
                for hh in (2 * pair, 2 * pair + 1):
                    acc = acc_ref[z + hh]
                    outs.append(acc * (1.0 / acc[sum_row[hh % 2]:sum_row[hh % 2] + 1, :]))
                o_t = jnp.where(keep_even, outs[0], outs[1])
                o_ref[0, qi * tile:(qi + 1) * tile, pair * LANES:(pair + 1) * LANES] = o_t.T.astype(o_ref.dtype)
            if qi + 1 < nq:
                reset_state()


def _causal_attn(q, k, vt):
    bsz, heads, seq, _ = q.shape
    g = ATTN_PAIRS
    qk_spec = pl.BlockSpec((1, 2 * g, seq, LANES), lambda b, p: (b, p, 0, 0))
    vt_spec = pl.BlockSpec((1, 2 * g, LANES, seq), lambda b, p: (b, p, 0, 0))
    o_spec = pl.BlockSpec((1, seq, g * LANES), lambda b, p: (b, 0, p))
    kern = functools.partial(_attn_kernel, tile=ATTN_TILE)
    return pl.pallas_call(
        kern,
        grid=(bsz, heads // (2 * g)),
        in_specs=[pl.BlockSpec(memory_space=pltpu.SMEM), qk_spec, qk_spec, vt_spec],
        out_specs=o_spec,
        out_shape=jax.ShapeDtypeStruct((bsz, seq, heads * HEAD_DIM), BF16),
        scratch_shapes=[pltpu.VMEM((2 * g, ATTN_TILE, ATTN_TILE), F32),
                        pltpu.VMEM((2 * g, ATTN_TILE, ATTN_TILE), F32),
                        pltpu.VMEM((2 * g, 1, ATTN_TILE), F32),
                        pltpu.VMEM((2 * g, LANES, ATTN_TILE), F32)],
        compiler_params=pltpu.CompilerParams(dimension_semantics=("arbitrary", "arbitrary"),
                                             vmem_limit_bytes=VMEM_LIMIT),
        name="causal_attn",
    )(jnp.zeros((1,), jnp.int32), q, k, vt)


def _mix_ffn_kernel(x_ref, oa_ref, ob_ref, gf_ref, gm_ref, mod_ref, gpost_mix_ref, gpre_ffn_ref,
                    gpost_ffn_ref, wpf_ref, wpm_ref, wout_ref, win_ref, wdown_ref, o_ref):
    tm = x_ref.shape[1]
    shift = mod_ref[0, 3:4, :]
    scale = mod_ref[0, 4:5, :]
    gate = mod_ref[0, 5:6, :]

    def mix_matmuls(rows):
        pa = _dot(oa_ref[0, rows, :], wpf_ref[...])
        pb = _dot(ob_ref[0, rows, :], wpm_ref[...])
        merged = gf_ref[0, rows, :].astype(F32) * pa + gm_ref[0, rows, :].astype(F32) * pb
        return _dot(merged.astype(BF16), wout_ref[...])

    def mix_residual(rows, y):
        x = x_ref[0, rows, :] + mod_ref[0, 2:3, :] * (_rms(y) * gpost_mix_ref[...])
        h = (_rms(x) * gpre_ffn_ref[...] * (1.0 + scale) + shift).astype(BF16)
        return x, h

    chunks = [(lo, min(FFN_CHUNK, D_FF - lo)) for lo in range(0, D_FF, FFN_CHUNK)]

    def up_matmuls(h, lo, w):
        return _dot(h, win_ref[:, lo:lo + w]), _dot(h, win_ref[:, D_FF + lo:D_FF + lo + w])

    def swiglu_matmuls(h):
        y = None
        gu = up_matmuls(h, *chunks[0])
        for c, (lo, w) in enumerate(chunks):
            g, u = gu
            if c + 1 < len(chunks):
                gu = up_matmuls(h, *chunks[c + 1])
            act = (g * jax.nn.sigmoid(g) * u).astype(BF16)
            part = _dot(act, wdown_ref[lo:lo + w, :])
            y = part if y is None else y + part
        return y

    parts = [pl.ds(r0, ROW_PART) for r0 in range(0, tm, ROW_PART)]
    y_mix, xs, y_ffn = {}, {}, {}
    for t in range(len(parts) + 2):
        if t < len(parts):
            y_mix[t] = mix_matmuls(parts[t])
        if 0 <= t - 1 < len(parts):
            xs[t - 1], h = mix_residual(parts[t - 1], y_mix.pop(t - 1))
            y_ffn[t - 1] = swiglu_matmuls(h)
        if 0 <= t - 2 < len(parts):
            o_ref[0, parts[t - 2], :] = xs.pop(t - 2) + gate * (_rms(y_ffn.pop(t - 2)) * gpost_ffn_ref[...])


def _mix_ffn(x, o_a, o_b, gf, gm, mod, g_post_mix, g_pre_ffn, g_post_ffn, w_pf, w_pm, w_out, w_in, w_down):
    bsz, seq, d = x.shape
    tm = MIX_TILE
    row = lambda b, i: (b, i, 0)
    weights = (w_pf, w_pm, w_out, w_in, w_down)
    return pl.pallas_call(
        _mix_ffn_kernel,
        grid=(bsz, seq // tm),
        in_specs=[pl.BlockSpec((1, tm, d), row),
                  pl.BlockSpec((1, tm, FOX_WIDTH), row), pl.BlockSpec((1, tm, FOX_WIDTH), row),
                  pl.BlockSpec((1, tm, d), row), pl.BlockSpec((1, tm, d), row),
                  pl.BlockSpec((1, 6, d), lambda b, i: (b, 0, 0)),
                  _const_spec((1, d)), _const_spec((1, d)), _const_spec((1, d))]
                 + [_const_spec(w.shape) for w in weights],
        out_specs=pl.BlockSpec((1, tm, d), row),
        out_shape=jax.ShapeDtypeStruct(x.shape, F32),
        compiler_params=pltpu.CompilerParams(dimension_semantics=("arbitrary", "arbitrary"),
                                             vmem_limit_bytes=VMEM_LIMIT),
        name="mix_ffn",
    )(x, o_a, o_b, gf, gm, mod, g_post_mix.reshape(1, d), g_pre_ffn.reshape(1, d), g_post_ffn.reshape(1, d),
      *weights)


def kernel(x, c, positions, w_ada, b_ada, g_pre_mix, g_post_mix, g_pre_ffn, g_post_ffn, w_in, b_forget,
           g_q_lora, w_uq, g_kv_lora, w_ukv, w_proj_fox, w_proj_mla, w_out, w_ffn_in, w_ffn_out):
    bsz, seq, d = x.shape
    depth = w_ada.shape[0]
    for l in range(depth):
        mod = _adaln_mod(c, w_ada[l], b_ada[l]).reshape(bsz, 6, d)
        weights, bf3 = _prep_proj_weights(w_in, l, w_uq[l], w_ukv[l], b_forget[l])
        qf, kf, vf, qm, km, vm, gf, gm = _token_proj(
            x, mod, g_pre_mix[l], positions, weights, bf3, g_q_lora[l], g_kv_lora[l])
        o_a = _causal_attn(qf, kf, vf)
        o_b = _causal_attn(qm, km, vm)
        x = _mix_ffn(x, o_a, o_b, gf, gm, mod, g_post_mix[l], g_pre_ffn[l], g_post_ffn[l],
                     w_proj_fox[l].astype(BF16), w_proj_mla[l].astype(BF16), w_out[l].astype(BF16),
                     w_ffn_in[l].astype(BF16), w_ffn_out[l].astype(BF16))
    return x
```

```python
import functools
import math

import jax
import jax.numpy as jnp
import numpy as np
from jax import lax
from jax.experimental import pallas as pl
from jax.experimental.pallas import tpu as pltpu

D_MODEL = 1024
HEADS = 8
HEAD_DIM = 64
FOX_WIDTH = HEADS * HEAD_DIM
MLA_NOPE = 64
MLA_ROPE = 32
MLA_V = 64
MLA_Q_LORA = 768
MLA_KV_LORA = 256
D_FF = 2816
ROPE_THETA = 10000.0
NORM_EPS = 1e-6
IN_WIDTHS = (FOX_WIDTH, FOX_WIDTH, FOX_WIDTH, HEADS, MLA_Q_LORA, MLA_KV_LORA, MLA_ROPE, D_MODEL, D_MODEL)

LANES = 128
TOKEN_TILE = 512
MIX_TILE = 512
ROW_PART = 256
ATTN_TILE = 512
ATTN_PAIRS = 2
FFN_CHUNK = 256
VMEM_LIMIT = 56 * 1024 * 1024

_EXP2_FOX = math.log2(math.e) / math.sqrt(HEAD_DIM)
_EXP2_MLA = math.log2(math.e) / math.sqrt(MLA_NOPE + MLA_ROPE)

F32 = jnp.float32
BF16 = jnp.bfloat16


def _const_spec(shape):
    zeros = (0,) * len(shape)
    return pl.BlockSpec(shape, lambda *_: zeros, pipeline_mode=pl.Buffered(1))


def _rms(x):
    return x * lax.rsqrt(jnp.mean(x * x, axis=-1, keepdims=True) + NORM_EPS)


def _dot(a, b):
    return jnp.dot(a, b, preferred_element_type=F32)


def _adaln_kernel(c_ref, w_ref, b_ref, o_ref):
    c = c_ref[...]
    sc = c * jax.nn.sigmoid(c)
    o_ref[...] = _dot(sc.astype(BF16), w_ref[...].astype(BF16)) + b_ref[...]


def _adaln_mod(c, w_ada, b_ada):
    bsz, d = c.shape
    n = w_ada.shape[1]
    tn = 2048
    return pl.pallas_call(
        _adaln_kernel,
        grid=(n // tn,),
        in_specs=[pl.BlockSpec((bsz, d), lambda j: (0, 0)),
                  pl.BlockSpec((d, tn), lambda j: (0, j)),
                  pl.BlockSpec((1, tn), lambda j: (0, j))],
        out_specs=pl.BlockSpec((bsz, tn), lambda j: (0, j)),
        out_shape=jax.ShapeDtypeStruct((bsz, n), F32),
        name="adaln_mod",
    )(c, w_ada, b_ada.reshape(1, n))


def _cumsum_rows(x):
    n = x.shape[0]
    row = lax.broadcasted_iota(jnp.int32, x.shape, 0)
    d = 1
    while d < n:
        x = x + jnp.where(row >= d, pltpu.roll(x, d, axis=0), 0.0)
        d *= 2
    return x


def _rope_tables(pos_row, invf_col):
    n = pos_row.shape[1]
    groups = LANES // MLA_ROPE
    ang = invf_col * pos_row
    c = jnp.cos(ang)
    s = jnp.sin(ang)
    cos_blocks, sin_blocks = [], []
    for t0 in range(0, n, LANES):
        ct = c[:, t0:t0 + LANES]
        st = s[:, t0:t0 + LANES]
        cos_blocks.append(jnp.concatenate([ct, ct] * groups, axis=0).T)
        sin_blocks.append(jnp.concatenate([-st, st] * groups, axis=0).T)
    return jnp.concatenate(cos_blocks, axis=0), jnp.concatenate(sin_blocks, axis=0)


def _rope(x, cos4, sin4, lane):
    half = MLA_ROPE // 2
    swapped = jnp.where((lane & (MLA_ROPE - 1)) < half, pltpu.roll(x, LANES - half, axis=1), pltpu.roll(x, half, axis=1))
    return x * cos4 + swapped * sin4


def _store_values_transposed(vt_ref, rows, v):
    n = v.shape[0]
    row = lax.broadcasted_iota(jnp.int32, (LANES, n), 0)
    for pair in range(HEADS // 2):
        vt = v[:, pair * LANES:(pair + 1) * LANES].T
        even = jnp.where(row < HEAD_DIM, vt, jnp.where(row == HEAD_DIM, 1.0, 0.0))
        odd = jnp.where(row >= HEAD_DIM, vt, jnp.where(row == 0, 1.0, 0.0))
        vt_ref[0, 2 * pair, :, rows] = even.astype(vt_ref.dtype)
        vt_ref[0, 2 * pair + 1, :, rows] = odd.astype(vt_ref.dtype)


def _token_proj_kernel(x_ref, mod_ref, gpre_ref, bf_ref, gq_ref, gkv_ref, pos_ref, invf_ref,
                       wqkv_ref, wmisc_ref, wcq_ref, wckv_ref, wg_ref, wq_ref, wkv_ref, sel_ref,
                       qf_ref, kf_ref, vf_ref, qm_ref, km_ref, vm_ref, gf_ref, gm_ref,
                       carry_ref, cos_ref, sin_ref):
    tm = x_ref.shape[1]

    @pl.when(pl.program_id(1) == 0)
    def _():
        carry_ref[...] = jnp.zeros_like(carry_ref)

    cos_ref[...], sin_ref[...] = _rope_tables(pos_ref[0], invf_ref[:, 0:1])

    shift = mod_ref[0, 0:1, :]
    scale = mod_ref[0, 1:2, :]

    def project(rows, n):
        h = (_rms(x_ref[0, rows, :]) * gpre_ref[...] * (1.0 + scale) + shift).astype(BF16)
        lane = lax.broadcasted_iota(jnp.int32, (n, LANES), 1)
        cos4 = cos_ref[rows, :]
        sin4 = sin_ref[rows, :]
        low = lane < HEAD_DIM

        misc = _dot(h, wmisc_ref[...])
        logit = misc + bf_ref[...]
        logf = jnp.minimum(logit, 0.0) - jnp.log(1.0 + jnp.exp(-jnp.abs(logit)))
        cum = _cumsum_rows(logf) + carry_ref[0:1, :]
        carry_ref[0:1, :] = cum[n - 1:n, :]

        cq = _dot(h, wcq_ref[...])
        ckv = _dot(h, wckv_ref[...])
        nq = (_rms(cq) * gq_ref[...]).astype(BF16)
        nkv = (_rms(ckv) * gkv_ref[...]).astype(BF16)
        for gi, ref in enumerate((gf_ref, gm_ref)):
            g = _dot(h, wg_ref[:, gi * D_MODEL:(gi + 1) * D_MODEL])
            ref[0, rows, :] = jax.nn.sigmoid(g).astype(ref.dtype)

        qq = _dot(nq, wq_ref[...]) * _EXP2_MLA
        nope_w = HEADS * MLA_NOPE
        q_rope = [_rope(qq[:, nope_w + g * LANES:nope_w + (g + 1) * LANES], cos4, sin4, lane)
                  for g in range(HEADS * MLA_ROPE // LANES)]
        per_group = LANES // MLA_ROPE
        for hh in range(HEADS):
            pair = hh // 2
            nope = qq[:, pair * LANES:(pair + 1) * LANES]
            src_lane = (hh % per_group) * MLA_ROPE
            dst_lane = HEAD_DIM if hh % 2 == 0 else 0
            rope = q_rope[hh // per_group]
            if src_lane != dst_lane:
                rope = pltpu.roll(rope, (dst_lane - src_lane) % LANES, axis=1)
            in_rope = (lane >= dst_lane) & (lane < dst_lane + MLA_ROPE)
            own = low if hh % 2 == 0 else jnp.logical_not(low)
            qm_ref[0, hh, rows, :] = jnp.where(own, nope, jnp.where(in_rope, rope, 0.0)).astype(qm_ref.dtype)
        kv = _dot(nkv, wkv_ref[...])
        _store_values_transposed(vm_ref, rows, kv[:, nope_w:])
        in_rope = (lane >= MLA_NOPE) & (lane < MLA_NOPE + MLA_ROPE)
        kpe_even = jnp.where(in_rope, _rope(misc, cos4, sin4, lane), 0.0)
        kpe_odd = pltpu.roll(kpe_even, LANES - HEAD_DIM, axis=1)
        for hh in range(HEADS):
            pair = hh // 2
            nope = kv[:, pair * LANES:(pair + 1) * LANES]
            k_full = jnp.where(low, nope, kpe_even) if hh % 2 == 0 else jnp.where(low, kpe_odd, nope)
            km_ref[0, hh, rows, :] = k_full.astype(km_ref.dtype)

        p_qkv = _dot(h, wqkv_ref[...])
        _store_values_transposed(vf_ref, rows, p_qkv[:, 2 * FOX_WIDTH:])

        a = cum * math.log2(math.e)
        a_hi = a.astype(BF16).astype(F32)
        r1 = a - a_hi
        a_mid = r1.astype(BF16).astype(F32)
        a_lo = (r1 - a_mid).astype(BF16).astype(F32)
        z = jnp.where(lane < 8, a_hi,
                      jnp.where(lane < 16, a_mid,
                                jnp.where(lane < 24, a_lo,
                                          jnp.where(lane == 24, 1.0, 0.0))))
        aug = _dot(z.astype(BF16), sel_ref[...])
        pairs = HEADS // 2
        for hh in range(HEADS):
            pair = hh // 2
            keep = (lane < HEAD_DIM) if hh % 2 == 0 else (lane >= HEAD_DIM)
            xq = p_qkv[:, pair * LANES:(pair + 1) * LANES]
            xk = p_qkv[:, FOX_WIDTH + pair * LANES:FOX_WIDTH + (pair + 1) * LANES]
            aq = aug[:, pair * LANES:(pair + 1) * LANES]
            ak = aug[:, (pairs + pair) * LANES:(pairs + pair + 1) * LANES]
            qf_ref[0, hh, rows, :] = jnp.where(keep, xq * _EXP2_FOX, aq).astype(qf_ref.dtype)
            kf_ref[0, hh, rows, :] = jnp.where(keep, xk, ak).astype(kf_ref.dtype)

    n = tm // 2
    for r0 in (0, n):
        project(pl.ds(r0, n), n)


_IN_OFFSETS = tuple(int(v) for v in np.cumsum((0,) + IN_WIDTHS))


def _split_w_in_kernel(wt_ref, qkv_ref, misc_ref, cq_ref, ckv_ref, g_ref):
    wt = wt_ref[0]
    cols = wt.shape[1]
    o = _IN_OFFSETS
    qkv_ref[...] = wt[o[0]:o[3], :].T.astype(BF16)
    cq_ref[...] = wt[o[4]:o[5], :].T.astype(BF16)
    ckv_ref[...] = wt[o[5]:o[6], :].T.astype(BF16)
    g_ref[...] = wt[o[7]:o[9], :].T.astype(BF16)
    f = wt[o[3]:o[4], :]
    kr = wt[o[6]:o[7], :]
    misc_t = jnp.concatenate([f, f, f, jnp.zeros((MLA_NOPE - 3 * HEADS, cols), F32), kr,
                              jnp.zeros((LANES - MLA_NOPE - MLA_ROPE, cols), F32)], axis=0)
    misc_ref[...] = misc_t.T.astype(BF16)


def _split_w_in(w_in_all, layer):
    _, d, n = w_in_all.shape
    cols = 256
    widths = (3 * FOX_WIDTH, LANES, MLA_Q_LORA, MLA_KV_LORA, 2 * D_MODEL)
    return pl.pallas_call(
        _split_w_in_kernel,
        grid=(d // cols,),
        in_specs=[pl.BlockSpec((1, n, cols), lambda i: (layer, 0, i))],
        out_specs=[pl.BlockSpec((cols, w), lambda i: (i, 0)) for w in widths],
        out_shape=[jax.ShapeDtypeStruct((d, w), BF16) for w in widths],
        name="split_w_in",
    )(jnp.transpose(w_in_all, (0, 2, 1)))


def _prep_proj_weights(w_in_all, layer, w_uq, w_ukv, b_forget):
    w_qkv, w_misc, w_cq, w_ckv, w_g = _split_w_in(w_in_all, layer)

    r = w_uq.shape[0]
    uq = w_uq.reshape(r, HEADS, MLA_NOPE + MLA_ROPE)
    w_q = jnp.concatenate([uq[:, :, :MLA_NOPE].reshape(r, HEADS * MLA_NOPE),
                           uq[:, :, MLA_NOPE:].reshape(r, HEADS * MLA_ROPE)], axis=1).astype(BF16)

    rk = w_ukv.shape[0]
    ukv = w_ukv.reshape(rk, HEADS, MLA_NOPE + MLA_V)
    w_kv = jnp.concatenate([ukv[:, :, :MLA_NOPE].reshape(rk, HEADS * MLA_NOPE),
                            ukv[:, :, MLA_NOPE:].reshape(rk, HEADS * MLA_V)], axis=1).astype(BF16)

    bf3 = jnp.concatenate([b_forget, b_forget, b_forget,
                           jnp.zeros((LANES - 3 * HEADS,), b_forget.dtype)]).reshape(1, LANES)
    return (w_qkv, w_misc, w_cq, w_ckv, w_g, w_q, w_kv), bf3


def _decay_selector():
    pairs = HEADS // 2
    sel = np.zeros((LANES, 2 * pairs * LANES), np.float32)
    for hh in range(HEADS):
        base_q = (hh // 2) * LANES + (HEAD_DIM if hh % 2 == 0 else 0)
        base_k = (pairs + hh // 2) * LANES + (HEAD_DIM if hh % 2 == 0 else 0)
        for piece in range(3):
            sel[piece * HEADS + hh, base_q + piece] = 1.0
            sel[3 * HEADS, base_q + 3 + piece] = 1.0
            sel[3 * HEADS, base_k + piece] = 1.0
            sel[piece * HEADS + hh, base_k + 3 + piece] = -1.0
    return jnp.asarray(sel, BF16)


def _token_proj(x, mod, g_pre, positions, weights, bf3, g_q, g_kv):
    bsz, seq, d = x.shape
    tm = TOKEN_TILE
    sel = _decay_selector()
    half = MLA_ROPE // 2
    inv_freq = 1.0 / (ROPE_THETA ** (np.arange(0, MLA_ROPE, 2, dtype=np.float32) / MLA_ROPE))
    invf = jnp.asarray(np.tile(inv_freq.astype(np.float32)[:, None], (1, LANES)))
    pos = positions.astype(F32).reshape(bsz, 1, seq)
    row = lambda b, i: (b, i, 0)
    head = lambda b, i: (b, 0, i, 0)
    head_shape = jax.ShapeDtypeStruct((bsz, HEADS, seq, LANES), BF16)
    head_spec = pl.BlockSpec((1, HEADS, tm, LANES), head)
    v_shape = jax.ShapeDtypeStruct((bsz, HEADS, LANES, seq), BF16)
    v_spec = pl.BlockSpec((1, HEADS, LANES, tm), lambda b, i: (b, 0, 0, i))
    g_shape = jax.ShapeDtypeStruct((bsz, seq, d), BF16)
    g_spec = pl.BlockSpec((1, tm, d), row)
    return pl.pallas_call(
        _token_proj_kernel,
        grid=(bsz, seq // tm),
        in_specs=[pl.BlockSpec((1, tm, d), row),
                  pl.BlockSpec((1, 6, d), lambda b, i: (b, 0, 0)),
                  _const_spec((1, d)), _const_spec((1, LANES)),
                  _const_spec((1, MLA_Q_LORA)), _const_spec((1, MLA_KV_LORA)),
                  pl.BlockSpec((1, 1, tm), lambda b, i: (b, 0, i)), _const_spec((half, LANES))]
                 + [_const_spec(w.shape) for w in weights] + [_const_spec(sel.shape)],
        out_specs=[head_spec, head_spec, v_spec, head_spec, head_spec, v_spec, g_spec, g_spec],
        out_shape=[head_shape, head_shape, v_shape, head_shape, head_shape, v_shape, g_shape, g_shape],
        scratch_shapes=[pltpu.VMEM((8, LANES), F32),
                        pltpu.VMEM((tm, LANES), F32),
                        pltpu.VMEM((tm, LANES), F32)],
        compiler_params=pltpu.CompilerParams(dimension_semantics=("arbitrary", "arbitrary"),
                                             vmem_limit_bytes=VMEM_LIMIT),
        name="token_proj",
    )(x, mod, g_pre.reshape(1, d), bf3, g_q.reshape(1, -1), g_kv.reshape(1, -1), pos, invf,
      *weights, sel)


def _attn_kernel(zero_ref, q_ref, k_ref, vt_ref, o_ref, sa_ref, sb_ref, m_ref, acc_ref, *, tile):
    seq = vt_ref.shape[3]
    nq = seq // tile
    n_heads = q_ref.shape[1]
    z = zero_ref[0]
    half = tile // 2
    nt = (((1,), (1,)), ((), ()))
    vrow = lax.broadcasted_iota(jnp.int32, (LANES, tile), 0)
    sum_row = (HEAD_DIM, 0)
    keep_even = vrow < HEAD_DIM
    krow = lax.broadcasted_iota(jnp.int32, (LANES, LANES), 0)
    qcol = lax.broadcasted_iota(jnp.int32, (LANES, LANES), 1)
    diag_ok = krow <= qcol

    def issue_scores(qi, j, s_out, heads=None):
        for hh in (range(n_heads) if heads is None else heads):
            q = q_ref[0, hh, qi * tile:(qi + 1) * tile, :]
            k = k_ref[0, hh, j * tile:(j + 1) * tile, :]
            if j == qi:
                s_out[z + hh, :half, :] = lax.dot_general(k[:half], q, nt, preferred_element_type=F32)
                s_out[z + hh, half:, half:] = lax.dot_general(k[half:], q[half:], nt, preferred_element_type=F32)
            else:
                s_out[z + hh] = lax.dot_general(k, q, nt, preferred_element_type=F32)

    def softmax_pv(s_in, hh, vth, masked):
        alphas, p_cols = [], []
        for c0 in range(0, tile, LANES):
            n_keys = c0 + LANES if masked else tile
            n_fill = tile if (c0 >= half or not masked) else half
            m_old = m_ref[hh, :, c0:c0 + LANES]

            def strip(r0):
                sb = s_in[z + hh, r0:r0 + LANES, c0:c0 + LANES]
                return jnp.where(diag_ok, sb, -jnp.inf) if (masked and r0 == c0) else sb

            mx = functools.reduce(jnp.maximum, [strip(r0) for r0 in range(0, n_keys, LANES)])
            m_new = jnp.maximum(m_old, jnp.max(mx, axis=0, keepdims=True))
            parts = [jnp.exp2(strip(r0) - m_new).astype(BF16) for r0 in range(0, n_keys, LANES)]
            parts += [jnp.zeros((LANES, LANES), BF16)] * ((n_fill - n_keys) // LANES)
            p_cols.append(jnp.concatenate(parts, axis=0))
            m_ref[hh, :, c0:c0 + LANES] = m_new
            alphas.append(jnp.exp2(m_old - m_new))
        alpha = jnp.concatenate(alphas, axis=1)
        n_left = half // LANES
        if masked:
            pv = jnp.concatenate([_dot(vth[:, :half], jnp.concatenate(p_cols[:n_left], axis=1)),
                                  _dot(vth, jnp.concatenate(p_cols[n_left:], axis=1))], axis=1)
        else:
            pv = _dot(vth, jnp.concatenate(p_cols, axis=1))
        acc_ref[z + hh] = alpha * acc_ref[z + hh] + pv

    def reset_state(heads):
        for hh in heads:
            m_ref[hh] = jnp.full(m_ref.shape[1:], -jnp.inf, F32)
            acc_ref[hh] = jnp.zeros(acc_ref.shape[1:], F32)

    blocks = [(qi, j) for qi in range(nq) for j in range(qi + 1)]
    bufs = (sa_ref, sb_ref)
    for pair in range(n_heads // 2):
        heads = (2 * pair, 2 * pair + 1)
        reset_state(heads)
        issue_scores(0, 0, bufs[0], heads)
        for t, (qi, j) in enumerate(blocks):
            s_in, s_out = bufs[t % 2], bufs[(t + 1) % 2]
            for hh in heads:
                if t + 1 < len(blocks):
                    issue_scores(*blocks[t + 1], s_out, heads=(hh,))
                softmax_pv(s_in, hh, vt_ref[0, hh, :, j * tile:(j + 1) * tile], j == qi)
            if j == qi:
                outs = []
                for hh in heads:
                    acc = acc_ref[z + hh]
                    outs.append(acc * (1.0 / acc[sum_row[hh % 2]:sum_row[hh % 2] + 1, :]))
                o_t = jnp.where(keep_even, outs[0], outs[1])
                o_ref[0, qi * tile:(qi + 1) * tile, pair * LANES:(pair + 1) * LANES] = o_t.T.astype(o_ref.dtype)
                if qi + 1 < nq:
                    reset_state(heads)


def _causal_attn(q, k, vt):
    bsz, heads, seq, _ = q.shape
    g = ATTN_PAIRS
    qk_spec = pl.BlockSpec((1, 2 * g, seq, LANES), lambda b, p: (b, p, 0, 0))
    vt_spec = pl.BlockSpec((1, 2 * g, LANES, seq), lambda b, p: (b, p, 0, 0))
    o_spec = pl.BlockSpec((1, seq, g * LANES), lambda b, p: (b, 0, p))
    kern = functools.partial(_attn_kernel, tile=ATTN_TILE)
    return pl.pallas_call(
        kern,
        grid=(bsz, heads // (2 * g)),
        in_specs=[pl.BlockSpec(memory_space=pltpu.SMEM), qk_spec, qk_spec, vt_spec],
        out_specs=o_spec,
        out_shape=jax.ShapeDtypeStruct((bsz, seq, heads * HEAD_DIM), BF16),
        scratch_shapes=[pltpu.VMEM((2 * g, ATTN_TILE, ATTN_TILE), F32),
                        pltpu.VMEM((2 * g, ATTN_TILE, ATTN_TILE), F32),
                        pltpu.VMEM((2 * g, 1, ATTN_TILE), F32),
                        pltpu.VMEM((2 * g, LANES, ATTN_TILE), F32)],
        compiler_params=pltpu.CompilerParams(dimension_semantics=("arbitrary", "arbitrary"),
                                             vmem_limit_bytes=VMEM_LIMIT),
        name="causal_attn",
    )(jnp.zeros((1,), jnp.int32), q, k, vt)


def _mix_ffn_kernel(x_ref, oa_ref, ob_ref, gf_ref, gm_ref, mod_ref, gpost_mix_ref, gpre_ffn_ref,
                    gpost_ffn_ref, wpf_ref, wpm_ref, wout_ref, win_ref, wdown_ref, o_ref):
    tm = x_ref.shape[1]
    shift = mod_ref[0, 3:4, :]
    scale = mod_ref[0, 4:5, :]
    gate = mod_ref[0, 5:6, :]

    def mix_matmuls(rows):
        pa = _dot(oa_ref[0, rows, :], wpf_ref[...])
        pb = _dot(ob_ref[0, rows, :], wpm_ref[...])
        merged = gf_ref[0, rows, :].astype(F32) * pa + gm_ref[0, rows, :].astype(F32) * pb
        return _dot(merged.astype(BF16), wout_ref[...])

    def mix_residual(rows, y):
        x = x_ref[0, rows, :] + mod_ref[0, 2:3, :] * (_rms(y) * gpost_mix_ref[...])
        h = (_rms(x) * gpre_ffn_ref[...] * (1.0 + scale) + shift).astype(BF16)
        return x, h

    chunks = [(lo, min(FFN_CHUNK, D_FF - lo)) for lo in range(0, D_FF, FFN_CHUNK)]

    def up_matmuls(h, lo, w):
        return _dot(h, win_ref[:, lo:lo + w]), _dot(h, win_ref[:, D_FF + lo:D_FF + lo + w])

    def swiglu_matmuls(h):
        y = None
        gu = up_matmuls(h, *chunks[0])
        for c, (lo, w) in enumerate(chunks):
            g, u = gu
            if c + 1 < len(chunks):
                gu = up_matmuls(h, *chunks[c + 1])
            act = (g * jax.nn.sigmoid(g) * u).astype(BF16)
            part = _dot(act, wdown_ref[lo:lo + w, :])
            y = part if y is None else y + part
        return y

    parts = [pl.ds(r0, ROW_PART) for r0 in range(0, tm, ROW_PART)]
    y_mix, xs, y_ffn = {}, {}, {}
    for t in range(len(parts) + 2):
        if t < len(parts):
            y_mix[t] = mix_matmuls(parts[t])
        if 0 <= t - 1 < len(parts):
            xs[t - 1], h = mix_residual(parts[t - 1], y_mix.pop(t - 1))
            y_ffn[t - 1] = swiglu_matmuls(h)
        if 0 <= t - 2 < len(parts):
            o_ref[0, parts[t - 2], :] = xs.pop(t - 2) + gate * (_rms(y_ffn.pop(t - 2)) * gpost_ffn_ref[...])


def _mix_ffn(x, o_a, o_b, gf, gm, mod, g_post_mix, g_pre_ffn, g_post_ffn, w_pf, w_pm, w_out, w_in, w_down):
    bsz, seq, d = x.shape
    tm = MIX_TILE
    row = lambda b, i: (b, i, 0)
    weights = (w_pf, w_pm, w_out, w_in, w_down)
    return pl.pallas_call(
        _mix_ffn_kernel,
        grid=(bsz, seq // tm),
        in_specs=[pl.BlockSpec((1, tm, d), row),
                  pl.BlockSpec((1, tm, FOX_WIDTH), row), pl.BlockSpec((1, tm, FOX_WIDTH), row),
                  pl.BlockSpec((1, tm, d), row), pl.BlockSpec((1, tm, d), row),
                  pl.BlockSpec((1, 6, d), lambda b, i: (b, 0, 0)),
                  _const_spec((1, d)), _const_spec((1, d)), _const_spec((1, d))]
                 + [_const_spec(w.shape) for w in weights],
        out_specs=pl.BlockSpec((1, tm, d), row),
        out_shape=jax.ShapeDtypeStruct(x.shape, F32),
        compiler_params=pltpu.CompilerParams(dimension_semantics=("arbitrary", "arbitrary"),
                                             vmem_limit_bytes=VMEM_LIMIT),
        name="mix_ffn",
    )(x, o_a, o_b, gf, gm, mod, g_post_mix.reshape(1, d), g_pre_ffn.reshape(1, d), g_post_ffn.reshape(1, d),
      *weights)


def kernel(x, c, positions, w_ada, b_ada, g_pre_mix, g_post_mix, g_pre_ffn, g_post_ffn, w_in, b_forget,
           g_q_lora, w_uq, g_kv_lora, w_ukv, w_proj_fox, w_proj_mla, w_out, w_ffn_in, w_ffn_out):
    bsz, seq, d = x.shape
    depth = w_ada.shape[0]
    for l in range(depth):
        mod = _adaln_mod(c, w_ada[l], b_ada[l]).reshape(bsz, 6, d)
        weights, bf3 = _prep_proj_weights(w_in, l, w_uq[l], w_ukv[l], b_forget[l])
        qf, kf, vf, qm, km, vm, gf, gm = _token_proj(
            x, mod, g_pre_mix[l], positions, weights, bf3, g_q_lora[l], g_kv_lora[l])
        o_a = _causal_attn(qf, kf, vf)
        o_b = _causal_attn(qm, km, vm)
        x = _mix_ffn(x, o_a, o_b, gf, gm, mod, g_post_mix[l], g_pre_ffn[l], g_post_ffn[l],
                     w_proj_fox[l].astype(BF16), w_proj_mla[l].astype(BF16), w_out[l].astype(BF16),
                     w_ffn_in[l].astype(BF16), w_ffn_out[l].astype(BF16))
    return x
```

```python
import functools
import math

import jax
import jax.numpy as jnp
import numpy as np
from jax import lax
from jax.experimental import pallas as pl
from jax.experimental.pallas import tpu as pltpu

D_MODEL = 1024
HEADS = 8
HEAD_DIM = 64
FOX_WIDTH = HEADS * HEAD_DIM
MLA_NOPE = 64
MLA_ROPE = 32
MLA_V = 64
MLA_Q_LORA = 768
MLA_KV_LORA = 256
D_FF = 2816
ROPE_THETA = 10000.0
NORM_EPS = 1e-6
IN_WIDTHS = (FOX_WIDTH, FOX_WIDTH, FOX_WIDTH, HEADS, MLA_Q_LORA, MLA_KV_LORA, MLA_ROPE, D_MODEL, D_MODEL)

LANES = 128
BF16_ROWS = 16
TOKEN_TILE = 512
MIX_TILE = 512
ROW_PART = 256
ATTN_TILE = 512
ATTN_PAIRS = 2
FFN_CHUNK = 256
VMEM_LIMIT = 56 * 1024 * 1024

_EXP2_FOX = math.log2(math.e) / math.sqrt(HEAD_DIM)
_EXP2_MLA = math.log2(math.e) / math.sqrt(MLA_NOPE + MLA_ROPE)

F32 = jnp.float32
BF16 = jnp.bfloat16


def _const_spec(shape):
    zeros = (0,) * len(shape)
    return pl.BlockSpec(shape, lambda *_: zeros, pipeline_mode=pl.Buffered(1))


def _rms(x):
    return x * lax.rsqrt(jnp.mean(x * x, axis=-1, keepdims=True) + NORM_EPS)


def _dot(a, b):
    return jnp.dot(a, b, preferred_element_type=F32)


def _adaln_kernel(c_ref, w_ref, b_ref, o_ref):
    c = c_ref[...]
    sc = c * jax.nn.sigmoid(c)
    o_ref[...] = _dot(sc.astype(BF16), w_ref[...].astype(BF16)) + b_ref[...]


def _adaln_mod(c, w_ada, b_ada):
    bsz, d = c.shape
    n = w_ada.shape[1]
    tn = 2048
    return pl.pallas_call(
        _adaln_kernel,
        grid=(n // tn,),
        in_specs=[pl.BlockSpec((bsz, d), lambda j: (0, 0)),
                  pl.BlockSpec((d, tn), lambda j: (0, j)),
                  pl.BlockSpec((1, tn), lambda j: (0, j))],
        out_specs=pl.BlockSpec((bsz, tn), lambda j: (0, j)),
        out_shape=jax.ShapeDtypeStruct((bsz, n), F32),
        name="adaln_mod",
    )(c, w_ada, b_ada.reshape(1, n))


def _cumsum_rows(x):
    n = x.shape[0]
    row = lax.broadcasted_iota(jnp.int32, x.shape, 0)
    d = 1
    while d < n:
        x = x + jnp.where(row >= d, pltpu.roll(x, d, axis=0), 0.0)
        d *= 2
    return x


def _rope_tables(pos_row, invf_col):
    n = pos_row.shape[1]
    groups = LANES // MLA_ROPE
    ang = invf_col * pos_row
    c = jnp.cos(ang)
    s = jnp.sin(ang)
    cos_blocks, sin_blocks = [], []
    for t0 in range(0, n, LANES):
        ct = c[:, t0:t0 + LANES]
        st = s[:, t0:t0 + LANES]
        cos_blocks.append(jnp.concatenate([ct, ct] * groups, axis=0).T)
        sin_blocks.append(jnp.concatenate([-st, st] * groups, axis=0).T)
    return jnp.concatenate(cos_blocks, axis=0), jnp.concatenate(sin_blocks, axis=0)


def _rope(x, cos4, sin4, lane):
    half = MLA_ROPE // 2
    swapped = jnp.where((lane & (MLA_ROPE - 1)) < half, pltpu.roll(x, LANES - half, axis=1), pltpu.roll(x, half, axis=1))
    return x * cos4 + swapped * sin4


def _store_values_transposed(vt_ref, rows, v):
    n = v.shape[0]
    row = lax.broadcasted_iota(jnp.int32, (LANES, n), 0)
    for pair in range(HEADS // 2):
        vt = v[:, pair * LANES:(pair + 1) * LANES].T
        even = jnp.where(row < HEAD_DIM, vt, jnp.where(row == HEAD_DIM, 1.0, 0.0))
        odd = jnp.where(row >= HEAD_DIM, vt, jnp.where(row == 0, 1.0, 0.0))
        vt_ref[0, 2 * pair, :, rows] = even.astype(vt_ref.dtype)
        vt_ref[0, 2 * pair + 1, :, rows] = odd.astype(vt_ref.dtype)


def _token_proj_kernel(x_ref, mod_ref, gpre_ref, bf_ref, gq_ref, gkv_ref, pos_ref, invf_ref,
                       wqkv_ref, wmisc_ref, wcq_ref, wckv_ref, wg_ref, wq_ref, wkv_ref, sel_ref,
                       late0_ref, late1_ref, late2_ref, late3_ref, late4_ref,
                       qf_ref, kf_ref, vf_ref, qm_ref, km_ref, vm_ref, gf_ref, gm_ref,
                       cast0_ref, cast1_ref, cast2_ref, cast3_ref, cast4_ref,
                       carry_ref, cos_ref, sin_ref):
    tm = x_ref.shape[1]

    @pl.when(pl.program_id(1) == 0)
    def _():
        carry_ref[...] = jnp.zeros_like(carry_ref)

    cos_ref[...], sin_ref[...] = _rope_tables(pos_ref[0], invf_ref[:, 0:1])

    shift = mod_ref[0, 0:1, :]
    scale = mod_ref[0, 1:2, :]

    def project(rows, n):
        h = (_rms(x_ref[0, rows, :]) * gpre_ref[...] * (1.0 + scale) + shift).astype(BF16)
        lane = lax.broadcasted_iota(jnp.int32, (n, LANES), 1)
        cos4 = cos_ref[rows, :]
        sin4 = sin_ref[rows, :]
        low = lane < HEAD_DIM

        misc = _dot(h, wmisc_ref[...])
        logit = misc + bf_ref[...]
        logf = jnp.minimum(logit, 0.0) - jnp.log(1.0 + jnp.exp(-jnp.abs(logit)))
        cum = _cumsum_rows(logf) + carry_ref[0:1, :]
        carry_ref[0:1, :] = cum[n - 1:n, :]

        cq = _dot(h, wcq_ref[...])
        ckv = _dot(h, wckv_ref[...])
        nq = (_rms(cq) * gq_ref[...]).astype(BF16)
        nkv = (_rms(ckv) * gkv_ref[...]).astype(BF16)
        for gi, ref in enumerate((gf_ref, gm_ref)):
            g = _dot(h, wg_ref[:, gi * D_MODEL:(gi + 1) * D_MODEL])
            ref[0, rows, :] = jax.nn.sigmoid(g).astype(ref.dtype)

        qq = _dot(nq, wq_ref[...]) * _EXP2_MLA
        nope_w = HEADS * MLA_NOPE
        q_rope = [_rope(qq[:, nope_w + g * LANES:nope_w + (g + 1) * LANES], cos4, sin4, lane)
                  for g in range(HEADS * MLA_ROPE // LANES)]
        per_group = LANES // MLA_ROPE
        for hh in range(HEADS):
            pair = hh // 2
            nope = qq[:, pair * LANES:(pair + 1) * LANES]
            src_lane = (hh % per_group) * MLA_ROPE
            dst_lane = HEAD_DIM if hh % 2 == 0 else 0
            rope = q_rope[hh // per_group]
            if src_lane != dst_lane:
                rope = pltpu.roll(rope, (dst_lane - src_lane) % LANES, axis=1)
            in_rope = (lane >= dst_lane) & (lane < dst_lane + MLA_ROPE)
            own = low if hh % 2 == 0 else jnp.logical_not(low)
            qm_ref[0, hh, rows, :] = jnp.where(own, nope, jnp.where(in_rope, rope, 0.0)).astype(qm_ref.dtype)
        kv = _dot(nkv, wkv_ref[...])
        _store_values_transposed(vm_ref, rows, kv[:, nope_w:])
        in_rope = (lane >= MLA_NOPE) & (lane < MLA_NOPE + MLA_ROPE)
        kpe_even = jnp.where(in_rope, _rope(misc, cos4, sin4, lane), 0.0)
        kpe_odd = pltpu.roll(kpe_even, LANES - HEAD_DIM, axis=1)
        for hh in range(HEADS):
            pair = hh // 2
            nope = kv[:, pair * LANES:(pair + 1) * LANES]
            k_full = jnp.where(low, nope, kpe_even) if hh % 2 == 0 else jnp.where(low, kpe_odd, nope)
            km_ref[0, hh, rows, :] = k_full.astype(km_ref.dtype)

        p_qkv = _dot(h, wqkv_ref[...])
        _store_values_transposed(vf_ref, rows, p_qkv[:, 2 * FOX_WIDTH:])

        a = cum * math.log2(math.e)
        a_hi = a.astype(BF16).astype(F32)
        r1 = a - a_hi
        a_mid = r1.astype(BF16).astype(F32)
        a_lo = (r1 - a_mid).astype(BF16).astype(F32)
        z = jnp.where(lane < 8, a_hi,
                      jnp.where(lane < 16, a_mid,
                                jnp.where(lane < 24, a_lo,
                                          jnp.where(lane == 24, 1.0, 0.0))))
        aug = _dot(z.astype(BF16), sel_ref[...])
        pairs = HEADS // 2
        for hh in range(HEADS):
            pair = hh // 2
            keep = (lane < HEAD_DIM) if hh % 2 == 0 else (lane >= HEAD_DIM)
            xq = p_qkv[:, pair * LANES:(pair + 1) * LANES]
            xk = p_qkv[:, FOX_WIDTH + pair * LANES:FOX_WIDTH + (pair + 1) * LANES]
            aq = aug[:, pair * LANES:(pair + 1) * LANES]
            ak = aug[:, (pairs + pair) * LANES:(pairs + pair + 1) * LANES]
            qf_ref[0, hh, rows, :] = jnp.where(keep, xq * _EXP2_FOX, aq).astype(qf_ref.dtype)
            kf_ref[0, hh, rows, :] = jnp.where(keep, xk, ak).astype(kf_ref.dtype)

    n = tm // 2
    for r0 in (0, n):
        project(pl.ds(r0, n), n)

    for src, dst in ((late0_ref, cast0_ref), (late1_ref, cast1_ref), (late2_ref, cast2_ref),
                     (late3_ref, cast3_ref), (late4_ref, cast4_ref)):
        dst[...] = src[0].astype(dst.dtype)


_IN_OFFSETS = tuple(int(v) for v in np.cumsum((0,) + IN_WIDTHS))


def _split_w_in_kernel(wt_ref, qkv_ref, misc_ref, cq_ref, ckv_ref, g_ref):
    wt = wt_ref[0]
    cols = wt.shape[1]
    o = _IN_OFFSETS
    qkv_ref[...] = wt[o[0]:o[3], :].T.astype(BF16)
    cq_ref[...] = wt[o[4]:o[5], :].T.astype(BF16)
    ckv_ref[...] = wt[o[5]:o[6], :].T.astype(BF16)
    g_ref[...] = wt[o[7]:o[9], :].T.astype(BF16)
    f = wt[o[3]:o[4], :]
    kr = wt[o[6]:o[7], :]
    misc_t = jnp.concatenate([f, f, f, jnp.zeros((MLA_NOPE - 3 * HEADS, cols), F32), kr,
                              jnp.zeros((LANES - MLA_NOPE - MLA_ROPE, cols), F32)], axis=0)
    misc_ref[...] = misc_t.T.astype(BF16)


def _split_w_in(w_in_all, layer):
    _, d, n = w_in_all.shape
    cols = 256
    widths = (3 * FOX_WIDTH, LANES, MLA_Q_LORA, MLA_KV_LORA, 2 * D_MODEL)
    return pl.pallas_call(
        _split_w_in_kernel,
        grid=(d // cols,),
        in_specs=[pl.BlockSpec((1, n, cols), lambda i: (layer, 0, i))],
        out_specs=[pl.BlockSpec((cols, w), lambda i: (i, 0)) for w in widths],
        out_shape=[jax.ShapeDtypeStruct((d, w), BF16) for w in widths],
        name="split_w_in",
    )(jnp.transpose(w_in_all, (0, 2, 1)))


def _prep_proj_weights(w_in_all, layer, w_uq, w_ukv, b_forget):
    w_qkv, w_misc, w_cq, w_ckv, w_g = _split_w_in(w_in_all, layer)

    r = w_uq.shape[0]
    uq = w_uq.reshape(r, HEADS, MLA_NOPE + MLA_ROPE)
    w_q = jnp.concatenate([uq[:, :, :MLA_NOPE].reshape(r, HEADS * MLA_NOPE),
                           uq[:, :, MLA_NOPE:].reshape(r, HEADS * MLA_ROPE)], axis=1).astype(BF16)

    rk = w_ukv.shape[0]
    ukv = w_ukv.reshape(rk, HEADS, MLA_NOPE + MLA_V)
    w_kv = jnp.concatenate([ukv[:, :, :MLA_NOPE].reshape(rk, HEADS * MLA_NOPE),
                            ukv[:, :, MLA_NOPE:].reshape(rk, HEADS * MLA_V)], axis=1).astype(BF16)

    bf3 = jnp.concatenate([b_forget, b_forget, b_forget,
                           jnp.zeros((LANES - 3 * HEADS,), b_forget.dtype)]).reshape(1, LANES)
    return (w_qkv, w_misc, w_cq, w_ckv, w_g, w_q, w_kv), bf3


def _decay_selector():
    pairs = HEADS // 2
    sel = np.zeros((LANES, 2 * pairs * LANES), np.float32)
    for hh in range(HEADS):
        base_q = (hh // 2) * LANES + (HEAD_DIM if hh % 2 == 0 else 0)
        base_k = (pairs + hh // 2) * LANES + (HEAD_DIM if hh % 2 == 0 else 0)
        for piece in range(3):
            sel[piece * HEADS + hh, base_q + piece] = 1.0
            sel[3 * HEADS, base_q + 3 + piece] = 1.0
            sel[3 * HEADS, base_k + piece] = 1.0
            sel[piece * HEADS + hh, base_k + 3 + piece] = -1.0
    return jnp.asarray(sel, BF16)


def _cast_block_rows(n_rows, n_steps):
    rb = -(-n_rows // n_steps)
    rb += -rb % BF16_ROWS
    while n_rows % rb:
        rb += BF16_ROWS
    return rb


def _token_proj(x, mod, g_pre, positions, weights, bf3, g_q, g_kv, late_weights, layer):
    bsz, seq, d = x.shape
    tm = TOKEN_TILE
    steps_per_batch = seq // tm
    n_steps = bsz * steps_per_batch
    late_in, late_out, late_shapes = [], [], []
    for w in late_weights:
        _, rows, cols = w.shape
        rb = _cast_block_rows(rows, n_steps)
        last = rows // rb - 1
        late_in.append(pl.BlockSpec(
            (1, rb, cols), lambda b, i, last=last: (layer, jnp.minimum(b * steps_per_batch + i, last), 0)))
        late_out.append(pl.BlockSpec(
            (rb, cols), lambda b, i, last=last: (jnp.minimum(b * steps_per_batch + i, last), 0)))
        late_shapes.append(jax.ShapeDtypeStruct((rows, cols), BF16))
    sel = _decay_selector()
    half = MLA_ROPE // 2
    inv_freq = 1.0 / (ROPE_THETA ** (np.arange(0, MLA_ROPE, 2, dtype=np.float32) / MLA_ROPE))
    invf = jnp.asarray(np.tile(inv_freq.astype(np.float32)[:, None], (1, LANES)))
    pos = positions.astype(F32).reshape(bsz, 1, seq)
    row = lambda b, i: (b, i, 0)
    head = lambda b, i: (b, 0, i, 0)
    head_shape = jax.ShapeDtypeStruct((bsz, HEADS, seq, LANES), BF16)
    head_spec = pl.BlockSpec((1, HEADS, tm, LANES), head)
    v_shape = jax.ShapeDtypeStruct((bsz, HEADS, LANES, seq), BF16)
    v_spec = pl.BlockSpec((1, HEADS, LANES, tm), lambda b, i: (b, 0, 0, i))
    g_shape = jax.ShapeDtypeStruct((bsz, seq, d), BF16)
    g_spec = pl.BlockSpec((1, tm, d), row)
    return pl.pallas_call(
        _token_proj_kernel,
        grid=(bsz, seq // tm),
        in_specs=[pl.BlockSpec((1, tm, d), row),
                  pl.BlockSpec((1, 6, d), lambda b, i: (b, 0, 0)),
                  _const_spec((1, d)), _const_spec((1, LANES)),
                  _const_spec((1, MLA_Q_LORA)), _const_spec((1, MLA_KV_LORA)),
                  pl.BlockSpec((1, 1, tm), lambda b, i: (b, 0, i)), _const_spec((half, LANES))]
                 + [_const_spec(w.shape) for w in weights] + [_const_spec(sel.shape)] + late_in,
        out_specs=[head_spec, head_spec, v_spec, head_spec, head_spec, v_spec, g_spec, g_spec] + late_out,
        out_shape=[head_shape, head_shape, v_shape, head_shape, head_shape, v_shape, g_shape, g_shape]
                  + late_shapes,
        scratch_shapes=[pltpu.VMEM((8, LANES), F32),
                        pltpu.VMEM((tm, LANES), F32),
                        pltpu.VMEM((tm, LANES), F32)],
        compiler_params=pltpu.CompilerParams(dimension_semantics=("arbitrary", "arbitrary"),
                                             vmem_limit_bytes=VMEM_LIMIT),
        name="token_proj",
    )(x, mod, g_pre.reshape(1, d), bf3, g_q.reshape(1, -1), g_kv.reshape(1, -1), pos, invf,
      *weights, sel, *late_weights)


def _attn_kernel(zero_ref, q_ref, k_ref, vt_ref, o_ref, sa_ref, sb_ref, m_ref, acc_ref, *, tile):
    seq = vt_ref.shape[3]
    nq = seq // tile
    n_heads = q_ref.shape[1]
    z = zero_ref[0]
    half = tile // 2
    nt = (((1,), (1,)), ((), ()))
    vrow = lax.broadcasted_iota(jnp.int32, (LANES, tile), 0)
    sum_row = (HEAD_DIM, 0)
    keep_even = vrow < HEAD_DIM
    krow = lax.broadcasted_iota(jnp.int32, (LANES, LANES), 0)
    qcol = lax.broadcasted_iota(jnp.int32, (LANES, LANES), 1)
    diag_ok = krow <= qcol

    def issue_scores(qi, j, s_out, heads=None):
        for hh in (range(n_heads) if heads is None else heads):
            q = q_ref[0, hh, qi * tile:(qi + 1) * tile, :]
            k = k_ref[0, hh, j * tile:(j + 1) * tile, :]
            if j == qi:
                s_out[z + hh, :half, :] = lax.dot_general(k[:half], q, nt, preferred_element_type=F32)
                s_out[z + hh, half:, half:] = lax.dot_general(k[half:], q[half:], nt, preferred_element_type=F32)
            else:
                s_out[z + hh] = lax.dot_general(k, q, nt, preferred_element_type=F32)

    def softmax_pv(s_in, hh, vth, masked):
        alphas, p_cols = [], []
        for c0 in range(0, tile, LANES):
            n_keys = c0 + LANES if masked else tile
            n_fill = tile if (c0 >= half or not masked) else half
            m_old = m_ref[hh, :, c0:c0 + LANES]

            def strip(r0):
                sb = s_in[z + hh, r0:r0 + LANES, c0:c0 + LANES]
                return jnp.where(diag_ok, sb, -jnp.inf) if (masked and r0 == c0) else sb

            mx = functools.reduce(jnp.maximum, [strip(r0) for r0 in range(0, n_keys, LANES)])
            m_new = jnp.maximum(m_old, jnp.max(mx, axis=0, keepdims=True))
            parts = [jnp.exp2(strip(r0) - m_new).astype(BF16) for r0 in range(0, n_keys, LANES)]
            parts += [jnp.zeros((LANES, LANES), BF16)] * ((n_fill - n_keys) // LANES)
            p_cols.append(jnp.concatenate(parts, axis=0))
            m_ref[hh, :, c0:c0 + LANES] = m_new
            alphas.append(jnp.exp2(m_old - m_new))
        alpha = jnp.concatenate(alphas, axis=1)
        n_left = half // LANES
        if masked:
            pv = jnp.concatenate([_dot(vth[:, :half], jnp.concatenate(p_cols[:n_left], axis=1)),
                                  _dot(vth, jnp.concatenate(p_cols[n_left:], axis=1))], axis=1)
        else:
            pv = _dot(vth, jnp.concatenate(p_cols, axis=1))
        acc_ref[z + hh] = alpha * acc_ref[z + hh] + pv

    def reset_state(heads):
        for hh in heads:
            m_ref[hh] = jnp.full(m_ref.shape[1:], -jnp.inf, F32)
            acc_ref[hh] = jnp.zeros(acc_ref.shape[1:], F32)

    blocks = [(qi, j) for qi in range(nq) for j in range(qi + 1)]
    bufs = (sa_ref, sb_ref)
    for pair in range(n_heads // 2):
        heads = (2 * pair, 2 * pair + 1)
        reset_state(heads)
        issue_scores(0, 0, bufs[0], heads)
        for t, (qi, j) in enumerate(blocks):
            s_in, s_out = bufs[t % 2], bufs[(t + 1) % 2]
            for hh in heads:
                if t + 1 < len(blocks):
                    issue_scores(*blocks[t + 1], s_out, heads=(hh,))
                softmax_pv(s_in, hh, vt_ref[0, hh, :, j * tile:(j + 1) * tile], j == qi)
            if j == qi:
                outs = []
                for hh in heads:
                    acc = acc_ref[z + hh]
                    outs.append(acc * (1.0 / acc[sum_row[hh % 2]:sum_row[hh % 2] + 1, :]))
                o_t = jnp.where(keep_even, outs[0], outs[1])
                o_ref[0, qi * tile:(qi + 1) * tile, pair * LANES:(pair + 1) * LANES] = o_t.T.astype(o_ref.dtype)
                if qi + 1 < nq:
                    reset_state(heads)


def _causal_attn(q, k, vt):
    bsz, heads, seq, _ = q.shape
    g = ATTN_PAIRS
    qk_spec = pl.BlockSpec((1, 2 * g, seq, LANES), lambda b, p: (b, p, 0, 0))
    vt_spec = pl.BlockSpec((1, 2 * g, LANES, seq), lambda b, p: (b, p, 0, 0))
    o_spec = pl.BlockSpec((1, seq, g * LANES), lambda b, p: (b, 0, p))
    kern = functools.partial(_attn_kernel, tile=ATTN_TILE)
    return pl.pallas_call(
        kern,
        grid=(bsz, heads // (2 * g)),
        in_specs=[pl.BlockSpec(memory_space=pltpu.SMEM), qk_spec, qk_spec, vt_spec],
        out_specs=o_spec,
        out_shape=jax.ShapeDtypeStruct((bsz, seq, heads * HEAD_DIM), BF16),
        scratch_shapes=[pltpu.VMEM((2 * g, ATTN_TILE, ATTN_TILE), F32),
                        pltpu.VMEM((2 * g, ATTN_TILE, ATTN_TILE), F32),
                        pltpu.VMEM((2 * g, 1, ATTN_TILE), F32),
                        pltpu.VMEM((2 * g, LANES, ATTN_TILE), F32)],
        compiler_params=pltpu.CompilerParams(dimension_semantics=("arbitrary", "arbitrary"),
                                             vmem_limit_bytes=VMEM_LIMIT),
        name="causal_attn",
    )(jnp.zeros((1,), jnp.int32), q, k, vt)


def _mix_ffn_kernel(x_ref, oa_ref, ob_ref, gf_ref, gm_ref, mod_ref, gpost_mix_ref, gpre_ffn_ref,
                    gpost_ffn_ref, wpf_ref, wpm_ref, wout_ref, win_ref, wdown_ref, o_ref):
    tm = x_ref.shape[1]
    shift = mod_ref[0, 3:4, :]
    scale = mod_ref[0, 4:5, :]
    gate = mod_ref[0, 5:6, :]

    def mix_matmuls(rows):
        pa = _dot(oa_ref[0, rows, :], wpf_ref[...])
        pb = _dot(ob_ref[0, rows, :], wpm_ref[...])
        merged = gf_ref[0, rows, :].astype(F32) * pa + gm_ref[0, rows, :].astype(F32) * pb
        return _dot(merged.astype(BF16), wout_ref[...])

    def mix_residual(rows, y):
        x = x_ref[0, rows, :] + mod_ref[0, 2:3, :] * (_rms(y) * gpost_mix_ref[...])
        h = (_rms(x) * gpre_ffn_ref[...] * (1.0 + scale) + shift).astype(BF16)
        return x, h

    chunks = [(lo, min(FFN_CHUNK, D_FF - lo)) for lo in range(0, D_FF, FFN_CHUNK)]

    def up_matmuls(h, lo, w):
        return _dot(h, win_ref[:, lo:lo + w]), _dot(h, win_ref[:, D_FF + lo:D_FF + lo + w])

    def swiglu_matmuls(h):
        y = None
        gu = up_matmuls(h, *chunks[0])
        for c, (lo, w) in enumerate(chunks):
            g, u = gu
            if c + 1 < len(chunks):
                gu = up_matmuls(h, *chunks[c + 1])
            act = (g * jax.nn.sigmoid(g) * u).astype(BF16)
            part = _dot(act, wdown_ref[lo:lo + w, :])
            y = part if y is None else y + part
        return y

    parts = [pl.ds(r0, ROW_PART) for r0 in range(0, tm, ROW_PART)]
    y_mix, xs, y_ffn = {}, {}, {}
    for t in range(len(parts) + 2):
        if t < len(parts):
            y_mix[t] = mix_matmuls(parts[t])
        if 0 <= t - 1 < len(parts):
            xs[t - 1], h = mix_residual(parts[t - 1], y_mix.pop(t - 1))
            y_ffn[t - 1] = swiglu_matmuls(h)
        if 0 <= t - 2 < len(parts):
            o_ref[0, parts[t - 2], :] = xs.pop(t - 2) + gate * (_rms(y_ffn.pop(t - 2)) * gpost_ffn_ref[...])


def _mix_ffn(x, o_a, o_b, gf, gm, mod, g_post_mix, g_pre_ffn, g_post_ffn, w_pf, w_pm, w_out, w_in, w_down):
    bsz, seq, d = x.shape
    tm = MIX_TILE
    row = lambda b, i: (b, i, 0)
    weights = (w_pf, w_pm, w_out, w_in, w_down)
    return pl.pallas_call(
        _mix_ffn_kernel,
        grid=(bsz, seq // tm),
        in_specs=[pl.BlockSpec((1, tm, d), row),
                  pl.BlockSpec((1, tm, FOX_WIDTH), row), pl.BlockSpec((1, tm, FOX_WIDTH), row),
                  pl.BlockSpec((1, tm, d), row), pl.BlockSpec((1, tm, d), row),
                  pl.BlockSpec((1, 6, d), lambda b, i: (b, 0, 0)),
                  _const_spec((1, d)), _const_spec((1, d)), _const_spec((1, d))]
                 + [_const_spec(w.shape) for w in weights],
        out_specs=pl.BlockSpec((1, tm, d), row),
        out_shape=jax.ShapeDtypeStruct(x.shape, F32),
        compiler_params=pltpu.CompilerParams(dimension_semantics=("arbitrary", "arbitrary"),
                                             vmem_limit_bytes=VMEM_LIMIT),
        name="mix_ffn",
    )(x, o_a, o_b, gf, gm, mod, g_post_mix.reshape(1, d), g_pre_ffn.reshape(1, d), g_post_ffn.reshape(1, d),
      *weights)


def kernel(x, c, positions, w_ada, b_ada, g_pre_mix, g_post_mix, g_pre_ffn, g_post_ffn, w_in, b_forget,
           g_q_lora, w_uq, g_kv_lora, w_ukv, w_proj_fox, w_proj_mla, w_out, w_ffn_in, w_ffn_out):
    bsz, seq, d = x.shape
    depth = w_ada.shape[0]
    for l in range(depth):
        mod = _adaln_mod(c, w_ada[l], b_ada[l]).reshape(bsz, 6, d)
        weights, bf3 = _prep_proj_weights(w_in, l, w_uq[l], w_ukv[l], b_forget[l])
        late = (w_proj_fox, w_proj_mla, w_out, w_ffn_in, w_ffn_out)
        qf, kf, vf, qm, km, vm, gf, gm, *late_bf16 = _token_proj(
            x, mod, g_pre_mix[l], positions, weights, bf3, g_q_lora[l], g_kv_lora[l], late, l)
        o_a = _causal_attn(qf, kf, vf)
        o_b = _causal_attn(qm, km, vm)
        x = _mix_ffn(x, o_a, o_b, gf, gm, mod, g_post_mix[l], g_pre_ffn[l], g_post_ffn[l], *late_bf16)
    return x
```

```python
import functools
import math

import jax
import jax.numpy as jnp
import numpy as np
from jax import lax
from jax.experimental import pallas as pl
from jax.experimental.pallas import tpu as pltpu

D_MODEL = 1024
HEADS = 8
HEAD_DIM = 64
FOX_WIDTH = HEADS * HEAD_DIM
MLA_NOPE = 64
MLA_ROPE = 32
MLA_V = 64
MLA_Q_LORA = 768
MLA_KV_LORA = 256
D_FF = 2816
ROPE_THETA = 10000.0
NORM_EPS = 1e-6
IN_WIDTHS = (FOX_WIDTH, FOX_WIDTH, FOX_WIDTH, HEADS, MLA_Q_LORA, MLA_KV_LORA, MLA_ROPE, D_MODEL, D_MODEL)

LANES = 128
BF16_ROWS = 16
TOKEN_TILE = 512
MIX_TILE = 512
ROW_PART = 256
ATTN_TILE = 512
ATTN_PAIRS = 2
FFN_CHUNK = 256
VMEM_LIMIT = 56 * 1024 * 1024

_EXP2_FOX = math.log2(math.e) / math.sqrt(HEAD_DIM)
_EXP2_MLA = math.log2(math.e) / math.sqrt(MLA_NOPE + MLA_ROPE)

F32 = jnp.float32
BF16 = jnp.bfloat16


def _const_spec(shape):
    zeros = (0,) * len(shape)
    return pl.BlockSpec(shape, lambda *_: zeros, pipeline_mode=pl.Buffered(1))


def _rms(x):
    return x * lax.rsqrt(jnp.mean(x * x, axis=-1, keepdims=True) + NORM_EPS)


def _dot(a, b):
    return jnp.dot(a, b, preferred_element_type=F32)


def _cumsum_rows(x):
    n = x.shape[0]
    row = lax.broadcasted_iota(jnp.int32, x.shape, 0)
    d = 1
    while d < n:
        x = x + jnp.where(row >= d, pltpu.roll(x, d, axis=0), 0.0)
        d *= 2
    return x


def _rope_tables(pos_row, invf_col):
    n = pos_row.shape[1]
    groups = LANES // MLA_ROPE
    ang = invf_col * pos_row
    c = jnp.cos(ang)
    s = jnp.sin(ang)
    cos_blocks, sin_blocks = [], []
    for t0 in range(0, n, LANES):
        ct = c[:, t0:t0 + LANES]
        st = s[:, t0:t0 + LANES]
        cos_blocks.append(jnp.concatenate([ct, ct] * groups, axis=0).T)
        sin_blocks.append(jnp.concatenate([-st, st] * groups, axis=0).T)
    return jnp.concatenate(cos_blocks, axis=0), jnp.concatenate(sin_blocks, axis=0)


def _rope(x, cos4, sin4, lane):
    half = MLA_ROPE // 2
    swapped = jnp.where((lane & (MLA_ROPE - 1)) < half, pltpu.roll(x, LANES - half, axis=1), pltpu.roll(x, half, axis=1))
    return x * cos4 + swapped * sin4


def _store_values_transposed(vt_ref, rows, v):
    n = v.shape[0]
    row = lax.broadcasted_iota(jnp.int32, (LANES, n), 0)
    for pair in range(HEADS // 2):
        vt = v[:, pair * LANES:(pair + 1) * LANES].T
        even = jnp.where(row < HEAD_DIM, vt, jnp.where(row == HEAD_DIM, 1.0, 0.0))
        odd = jnp.where(row >= HEAD_DIM, vt, jnp.where(row == 0, 1.0, 0.0))
        vt_ref[0, 2 * pair, :, rows] = even.astype(vt_ref.dtype)
        vt_ref[0, 2 * pair + 1, :, rows] = odd.astype(vt_ref.dtype)


def _token_proj_kernel(x_ref, mod_ref, gpre_ref, bf_ref, gq_ref, gkv_ref, pos_ref, invf_ref,
                       wqkv_ref, wmisc_ref, wcq_ref, wckv_ref, wg_ref, wq_ref, wkv_ref, sel_ref,
                       late0_ref, late1_ref, late2_ref, late3_ref, late4_ref,
                       qf_ref, kf_ref, vf_ref, qm_ref, km_ref, vm_ref, gf_ref, gm_ref,
                       cast0_ref, cast1_ref, cast2_ref, cast3_ref, cast4_ref,
                       carry_ref, cos_ref, sin_ref):
    tm = x_ref.shape[1]

    @pl.when(pl.program_id(1) == 0)
    def _():
        carry_ref[...] = jnp.zeros_like(carry_ref)

    cos_ref[...], sin_ref[...] = _rope_tables(pos_ref[0], invf_ref[:, 0:1])

    mod = lambda k: mod_ref[pl.ds(pl.program_id(0), 1), k * D_MODEL:(k + 1) * D_MODEL]
    shift = mod(0)
    scale = mod(1)

    def project(rows, n):
        h = (_rms(x_ref[0, rows, :]) * gpre_ref[...] * (1.0 + scale) + shift).astype(BF16)
        lane = lax.broadcasted_iota(jnp.int32, (n, LANES), 1)
        cos4 = cos_ref[rows, :]
        sin4 = sin_ref[rows, :]
        low = lane < HEAD_DIM

        misc = _dot(h, wmisc_ref[...])
        logit = misc + bf_ref[...]
        logf = jnp.minimum(logit, 0.0) - jnp.log(1.0 + jnp.exp(-jnp.abs(logit)))
        cum = _cumsum_rows(logf) + carry_ref[0:1, :]
        carry_ref[0:1, :] = cum[n - 1:n, :]

        cq = _dot(h, wcq_ref[...])
        ckv = _dot(h, wckv_ref[...])
        nq = (_rms(cq) * gq_ref[...]).astype(BF16)
        nkv = (_rms(ckv) * gkv_ref[...]).astype(BF16)
        for gi, ref in enumerate((gf_ref, gm_ref)):
            g = _dot(h, wg_ref[:, gi * D_MODEL:(gi + 1) * D_MODEL])
            ref[0, rows, :] = jax.nn.sigmoid(g).astype(ref.dtype)

        qq = _dot(nq, wq_ref[...]) * _EXP2_MLA
        nope_w = HEADS * MLA_NOPE
        q_rope = [_rope(qq[:, nope_w + g * LANES:nope_w + (g + 1) * LANES], cos4, sin4, lane)
                  for g in range(HEADS * MLA_ROPE // LANES)]
        per_group = LANES // MLA_ROPE
        for hh in range(HEADS):
            pair = hh // 2
            nope = qq[:, pair * LANES:(pair + 1) * LANES]
            src_lane = (hh % per_group) * MLA_ROPE
            dst_lane = HEAD_DIM if hh % 2 == 0 else 0
            rope = q_rope[hh // per_group]
            if src_lane != dst_lane:
                rope = pltpu.roll(rope, (dst_lane - src_lane) % LANES, axis=1)
            in_rope = (lane >= dst_lane) & (lane < dst_lane + MLA_ROPE)
            own = low if hh % 2 == 0 else jnp.logical_not(low)
            qm_ref[0, hh, rows, :] = jnp.where(own, nope, jnp.where(in_rope, rope, 0.0)).astype(qm_ref.dtype)
        kv = _dot(nkv, wkv_ref[...])
        _store_values_transposed(vm_ref, rows, kv[:, nope_w:])
        in_rope = (lane >= MLA_NOPE) & (lane < MLA_NOPE + MLA_ROPE)
        kpe_even = jnp.where(in_rope, _rope(misc, cos4, sin4, lane), 0.0)
        kpe_odd = pltpu.roll(kpe_even, LANES - HEAD_DIM, axis=1)
        for hh in range(HEADS):
            pair = hh // 2
            nope = kv[:, pair * LANES:(pair + 1) * LANES]
            k_full = jnp.where(low, nope, kpe_even) if hh % 2 == 0 else jnp.where(low, kpe_odd, nope)
            km_ref[0, hh, rows, :] = k_full.astype(km_ref.dtype)

        p_qkv = _dot(h, wqkv_ref[...])
        _store_values_transposed(vf_ref, rows, p_qkv[:, 2 * FOX_WIDTH:])

        a = cum * math.log2(math.e)
        a_hi = a.astype(BF16).astype(F32)
        r1 = a - a_hi
        a_mid = r1.astype(BF16).astype(F32)
        a_lo = (r1 - a_mid).astype(BF16).astype(F32)
        z = jnp.where(lane < 8, a_hi,
                      jnp.where(lane < 16, a_mid,
                                jnp.where(lane < 24, a_lo,
                                          jnp.where(lane == 24, 1.0, 0.0))))
        aug = _dot(z.astype(BF16), sel_ref[...])
        pairs = HEADS // 2
        for hh in range(HEADS):
            pair = hh // 2
            keep = (lane < HEAD_DIM) if hh % 2 == 0 else (lane >= HEAD_DIM)
            xq = p_qkv[:, pair * LANES:(pair + 1) * LANES]
            xk = p_qkv[:, FOX_WIDTH + pair * LANES:FOX_WIDTH + (pair + 1) * LANES]
            aq = aug[:, pair * LANES:(pair + 1) * LANES]
            ak = aug[:, (pairs + pair) * LANES:(pairs + pair + 1) * LANES]
            qf_ref[0, hh, rows, :] = jnp.where(keep, xq * _EXP2_FOX, aq).astype(qf_ref.dtype)
            kf_ref[0, hh, rows, :] = jnp.where(keep, xk, ak).astype(kf_ref.dtype)

    n = tm // 2
    for r0 in (0, n):
        project(pl.ds(r0, n), n)

    for src, dst in ((late0_ref, cast0_ref), (late1_ref, cast1_ref), (late2_ref, cast2_ref),
                     (late3_ref, cast3_ref), (late4_ref, cast4_ref)):
        dst[...] = src[0].astype(dst.dtype)


_IN_OFFSETS = tuple(int(v) for v in np.cumsum((0,) + IN_WIDTHS))


def _prep_kernel(c_ref, wada_ref, bada_ref, wt_ref, mod_ref, qkv_ref, misc_ref, cq_ref, ckv_ref, g_ref):
    c = c_ref[...]
    sc = c * jax.nn.sigmoid(c)
    mod_ref[...] = _dot(sc.astype(BF16), wada_ref[0].astype(BF16)) + bada_ref[0]

    wt = wt_ref[0]
    cols = wt.shape[1]
    o = _IN_OFFSETS
    qkv_ref[...] = wt[o[0]:o[3], :].T.astype(BF16)
    cq_ref[...] = wt[o[4]:o[5], :].T.astype(BF16)
    ckv_ref[...] = wt[o[5]:o[6], :].T.astype(BF16)
    g_ref[...] = wt[o[7]:o[9], :].T.astype(BF16)
    f = wt[o[3]:o[4], :]
    kr = wt[o[6]:o[7], :]
    misc_t = jnp.concatenate([f, f, f, jnp.zeros((MLA_NOPE - 3 * HEADS, cols), F32), kr,
                              jnp.zeros((LANES - MLA_NOPE - MLA_ROPE, cols), F32)], axis=0)
    misc_ref[...] = misc_t.T.astype(BF16)


def _prep(c, w_ada_all, b_ada_all, w_in_all, layer):
    bsz, d = c.shape
    depth, _, n_mod = w_ada_all.shape
    n_in = w_in_all.shape[2]
    cols = 256
    steps = d // cols
    tn = n_mod // steps
    assert d % cols == 0 and n_mod % steps == 0 and tn % LANES == 0
    widths = (3 * FOX_WIDTH, LANES, MLA_Q_LORA, MLA_KV_LORA, 2 * D_MODEL)
    return pl.pallas_call(
        _prep_kernel,
        grid=(steps,),
        in_specs=[pl.BlockSpec((bsz, d), lambda j: (0, 0)),
                  pl.BlockSpec((1, d, tn), lambda j: (layer, 0, j)),
                  pl.BlockSpec((1, 1, tn), lambda j: (layer, 0, j)),
                  pl.BlockSpec((1, n_in, cols), lambda j: (layer, 0, j))],
        out_specs=[pl.BlockSpec((bsz, tn), lambda j: (0, j))]
                  + [pl.BlockSpec((cols, w), lambda j: (j, 0)) for w in widths],
        out_shape=[jax.ShapeDtypeStruct((bsz, n_mod), F32)]
                  + [jax.ShapeDtypeStruct((d, w), BF16) for w in widths],
        name="prep",
    )(c, w_ada_all, b_ada_all.reshape(depth, 1, n_mod), jnp.transpose(w_in_all, (0, 2, 1)))


def _prep_proj_weights(w_in_parts, w_uq, w_ukv, b_forget):
    w_qkv, w_misc, w_cq, w_ckv, w_g = w_in_parts

    r = w_uq.shape[0]
    uq = w_uq.reshape(r, HEADS, MLA_NOPE + MLA_ROPE)
    w_q = jnp.concatenate([uq[:, :, :MLA_NOPE].reshape(r, HEADS * MLA_NOPE),
                           uq[:, :, MLA_NOPE:].reshape(r, HEADS * MLA_ROPE)], axis=1).astype(BF16)

    rk = w_ukv.shape[0]
    ukv = w_ukv.reshape(rk, HEADS, MLA_NOPE + MLA_V)
    w_kv = jnp.concatenate([ukv[:, :, :MLA_NOPE].reshape(rk, HEADS * MLA_NOPE),
                            ukv[:, :, MLA_NOPE:].reshape(rk, HEADS * MLA_V)], axis=1).astype(BF16)

    bf3 = jnp.concatenate([b_forget, b_forget, b_forget,
                           jnp.zeros((LANES - 3 * HEADS,), b_forget.dtype)]).reshape(1, LANES)
    return (w_qkv, w_misc, w_cq, w_ckv, w_g, w_q, w_kv), bf3


def _decay_selector():
    pairs = HEADS // 2
    sel = np.zeros((LANES, 2 * pairs * LANES), np.float32)
    for hh in range(HEADS):
        base_q = (hh // 2) * LANES + (HEAD_DIM if hh % 2 == 0 else 0)
        base_k = (pairs + hh // 2) * LANES + (HEAD_DIM if hh % 2 == 0 else 0)
        for piece in range(3):
            sel[piece * HEADS + hh, base_q + piece] = 1.0
            sel[3 * HEADS, base_q + 3 + piece] = 1.0
            sel[3 * HEADS, base_k + piece] = 1.0
            sel[piece * HEADS + hh, base_k + 3 + piece] = -1.0
    return jnp.asarray(sel, BF16)


def _cast_block_rows(n_rows, n_steps):
    rb = -(-n_rows // n_steps)
    rb += -rb % BF16_ROWS
    while n_rows % rb:
        rb += BF16_ROWS
    return rb


def _token_proj(x, mod, g_pre, positions, weights, bf3, g_q, g_kv, late_weights, layer):
    bsz, seq, d = x.shape
    tm = TOKEN_TILE
    steps_per_batch = seq // tm
    n_steps = bsz * steps_per_batch
    late_in, late_out, late_shapes = [], [], []
    for w in late_weights:
        _, rows, cols = w.shape
        rb = _cast_block_rows(rows, n_steps)
        last = rows // rb - 1
        late_in.append(pl.BlockSpec(
            (1, rb, cols), lambda b, i, last=last: (layer, jnp.minimum(b * steps_per_batch + i, last), 0)))
        late_out.append(pl.BlockSpec(
            (rb, cols), lambda b, i, last=last: (jnp.minimum(b * steps_per_batch + i, last), 0)))
        late_shapes.append(jax.ShapeDtypeStruct((rows, cols), BF16))
    sel = _decay_selector()
    half = MLA_ROPE // 2
    inv_freq = 1.0 / (ROPE_THETA ** (np.arange(0, MLA_ROPE, 2, dtype=np.float32) / MLA_ROPE))
    invf = jnp.asarray(np.tile(inv_freq.astype(np.float32)[:, None], (1, LANES)))
    pos = positions.astype(F32).reshape(bsz, 1, seq)
    row = lambda b, i: (b, i, 0)
    head = lambda b, i: (b, 0, i, 0)
    head_shape = jax.ShapeDtypeStruct((bsz, HEADS, seq, LANES), BF16)
    head_spec = pl.BlockSpec((1, HEADS, tm, LANES), head)
    v_shape = jax.ShapeDtypeStruct((bsz, HEADS, LANES, seq), BF16)
    v_spec = pl.BlockSpec((1, HEADS, LANES, tm), lambda b, i: (b, 0, 0, i))
    g_shape = jax.ShapeDtypeStruct((bsz, seq, d), BF16)
    g_spec = pl.BlockSpec((1, tm, d), row)
    return pl.pallas_call(
        _token_proj_kernel,
        grid=(bsz, seq // tm),
        in_specs=[pl.BlockSpec((1, tm, d), row),
                  _const_spec(mod.shape),
                  _const_spec((1, d)), _const_spec((1, LANES)),
                  _const_spec((1, MLA_Q_LORA)), _const_spec((1, MLA_KV_LORA)),
                  pl.BlockSpec((1, 1, tm), lambda b, i: (b, 0, i)), _const_spec((half, LANES))]
                 + [_const_spec(w.shape) for w in weights] + [_const_spec(sel.shape)] + late_in,
        out_specs=[head_spec, head_spec, v_spec, head_spec, head_spec, v_spec, g_spec, g_spec] + late_out,
        out_shape=[head_shape, head_shape, v_shape, head_shape, head_shape, v_shape, g_shape, g_shape]
                  + late_shapes,
        scratch_shapes=[pltpu.VMEM((8, LANES), F32),
                        pltpu.VMEM((tm, LANES), F32),
                        pltpu.VMEM((tm, LANES), F32)],
        compiler_params=pltpu.CompilerParams(dimension_semantics=("arbitrary", "arbitrary"),
                                             vmem_limit_bytes=VMEM_LIMIT),
        name="token_proj",
    )(x, mod, g_pre.reshape(1, d), bf3, g_q.reshape(1, -1), g_kv.reshape(1, -1), pos, invf,
      *weights, sel, *late_weights)


def _attn_kernel(zero_ref, q_ref, k_ref, vt_ref, o_ref, sa_ref, sb_ref, m_ref, acc_ref, *, tile):
    seq = vt_ref.shape[3]
    nq = seq // tile
    n_heads = q_ref.shape[1]
    z = zero_ref[0]
    half = tile // 2
    nt = (((1,), (1,)), ((), ()))
    vrow = lax.broadcasted_iota(jnp.int32, (LANES, tile), 0)
    sum_row = (HEAD_DIM, 0)
    keep_even = vrow < HEAD_DIM
    krow = lax.broadcasted_iota(jnp.int32, (LANES, LANES), 0)
    qcol = lax.broadcasted_iota(jnp.int32, (LANES, LANES), 1)
    diag_ok = krow <= qcol

    def issue_scores(qi, j, s_out, heads=None):
        for hh in (range(n_heads) if heads is None else heads):
            q = q_ref[0, hh, qi * tile:(qi + 1) * tile, :]
            k = k_ref[0, hh, j * tile:(j + 1) * tile, :]
            if j == qi:
                s_out[z + hh, :half, :] = lax.dot_general(k[:half], q, nt, preferred_element_type=F32)
                s_out[z + hh, half:, half:] = lax.dot_general(k[half:], q[half:], nt, preferred_element_type=F32)
            else:
                s_out[z + hh] = lax.dot_general(k, q, nt, preferred_element_type=F32)

    def softmax_pv(s_in, hh, vth, masked):
        alphas, p_cols = [], []
        for c0 in range(0, tile, LANES):
            n_keys = c0 + LANES if masked else tile
            n_fill = tile if (c0 >= half or not masked) else half
            m_old = m_ref[hh, :, c0:c0 + LANES]

            def strip(r0):
                sb = s_in[z + hh, r0:r0 + LANES, c0:c0 + LANES]
                return jnp.where(diag_ok, sb, -jnp.inf) if (masked and r0 == c0) else sb

            mx = functools.reduce(jnp.maximum, [strip(r0) for r0 in range(0, n_keys, LANES)])
            m_new = jnp.maximum(m_old, jnp.max(mx, axis=0, keepdims=True))
            parts = [jnp.exp2(strip(r0) - m_new).astype(BF16) for r0 in range(0, n_keys, LANES)]
            parts += [jnp.zeros((LANES, LANES), BF16)] * ((n_fill - n_keys) // LANES)
            p_cols.append(jnp.concatenate(parts, axis=0))
            m_ref[hh, :, c0:c0 + LANES] = m_new
            alphas.append(jnp.exp2(m_old - m_new))
        alpha = jnp.concatenate(alphas, axis=1)
        n_left = half // LANES
        if masked:
            pv = jnp.concatenate([_dot(vth[:, :half], jnp.concatenate(p_cols[:n_left], axis=1)),
                                  _dot(vth, jnp.concatenate(p_cols[n_left:], axis=1))], axis=1)
        else:
            pv = _dot(vth, jnp.concatenate(p_cols, axis=1))
        acc_ref[z + hh] = alpha * acc_ref[z + hh] + pv

    def reset_state(heads):
        for hh in heads:
            m_ref[hh] = jnp.full(m_ref.shape[1:], -jnp.inf, F32)
            acc_ref[hh] = jnp.zeros(acc_ref.shape[1:], F32)

    blocks = [(qi, j) for qi in range(nq) for j in range(qi + 1)]
    bufs = (sa_ref, sb_ref)
    for pair in range(n_heads // 2):
        heads = (2 * pair, 2 * pair + 1)
        reset_state(heads)
        issue_scores(0, 0, bufs[0], heads)
        for t, (qi, j) in enumerate(blocks):
            s_in, s_out = bufs[t % 2], bufs[(t + 1) % 2]
            for hh in heads:
                if t + 1 < len(blocks):
                    issue_scores(*blocks[t + 1], s_out, heads=(hh,))
                softmax_pv(s_in, hh, vt_ref[0, hh, :, j * tile:(j + 1) * tile], j == qi)
            if j == qi:
                outs = []
                for hh in heads:
                    acc = acc_ref[z + hh]
                    outs.append(acc * (1.0 / acc[sum_row[hh % 2]:sum_row[hh % 2] + 1, :]))
                o_t = jnp.where(keep_even, outs[0], outs[1])
                o_ref[0, qi * tile:(qi + 1) * tile, pair * LANES:(pair + 1) * LANES] = o_t.T.astype(o_ref.dtype)
                if qi + 1 < nq:
                    reset_state(heads)


def _causal_attn(q, k, vt):
    bsz, heads, seq, _ = q.shape
    g = ATTN_PAIRS
    qk_spec = pl.BlockSpec((1, 2 * g, seq, LANES), lambda b, p: (b, p, 0, 0))
    vt_spec = pl.BlockSpec((1, 2 * g, LANES, seq), lambda b, p: (b, p, 0, 0))
    o_spec = pl.BlockSpec((1, seq, g * LANES), lambda b, p: (b, 0, p))
    kern = functools.partial(_attn_kernel, tile=ATTN_TILE)
    return pl.pallas_call(
        kern,
        grid=(bsz, heads // (2 * g)),
        in_specs=[pl.BlockSpec(memory_space=pltpu.SMEM), qk_spec, qk_spec, vt_spec],
        out_specs=o_spec,
        out_shape=jax.ShapeDtypeStruct((bsz, seq, heads * HEAD_DIM), BF16),
        scratch_shapes=[pltpu.VMEM((2 * g, ATTN_TILE, ATTN_TILE), F32),
                        pltpu.VMEM((2 * g, ATTN_TILE, ATTN_TILE), F32),
                        pltpu.VMEM((2 * g, 1, ATTN_TILE), F32),
                        pltpu.VMEM((2 * g, LANES, ATTN_TILE), F32)],
        compiler_params=pltpu.CompilerParams(dimension_semantics=("arbitrary", "arbitrary"),
                                             vmem_limit_bytes=VMEM_LIMIT),
        name="causal_attn",
    )(jnp.zeros((1,), jnp.int32), q, k, vt)


def _mix_ffn_kernel(x_ref, oa_ref, ob_ref, gf_ref, gm_ref, mod_ref, gpost_mix_ref, gpre_ffn_ref,
                    gpost_ffn_ref, wpf_ref, wpm_ref, wout_ref, win_ref, wdown_ref, o_ref):
    tm = x_ref.shape[1]
    mod = lambda k: mod_ref[pl.ds(pl.program_id(0), 1), k * D_MODEL:(k + 1) * D_MODEL]
    gate_mix = mod(2)
    shift = mod(3)
    scale = mod(4)
    gate = mod(5)

    def mix_matmuls(rows):
        pa = _dot(oa_ref[0, rows, :], wpf_ref[...])
        pb = _dot(ob_ref[0, rows, :], wpm_ref[...])
        merged = gf_ref[0, rows, :].astype(F32) * pa + gm_ref[0, rows, :].astype(F32) * pb
        return _dot(merged.astype(BF16), wout_ref[...])

    def mix_residual(rows, y):
        x = x_ref[0, rows, :] + gate_mix * (_rms(y) * gpost_mix_ref[...])
        h = (_rms(x) * gpre_ffn_ref[...] * (1.0 + scale) + shift).astype(BF16)
        return x, h

    chunks = [(lo, min(FFN_CHUNK, D_FF - lo)) for lo in range(0, D_FF, FFN_CHUNK)]

    def up_matmuls(h, lo, w):
        return _dot(h, win_ref[:, lo:lo + w]), _dot(h, win_ref[:, D_FF + lo:D_FF + lo + w])

    def swiglu_matmuls(h):
        y = None
        gu = up_matmuls(h, *chunks[0])
        for c, (lo, w) in enumerate(chunks):
            g, u = gu
            if c + 1 < len(chunks):
                gu = up_matmuls(h, *chunks[c + 1])
            act = (g * jax.nn.sigmoid(g) * u).astype(BF16)
            part = _dot(act, wdown_ref[lo:lo + w, :])
            y = part if y is None else y + part
        return y

    parts = [pl.ds(r0, ROW_PART) for r0 in range(0, tm, ROW_PART)]
    y_mix, xs, y_ffn = {}, {}, {}
    for t in range(len(parts) + 2):
        if t < len(parts):
            y_mix[t] = mix_matmuls(parts[t])
        if 0 <= t - 1 < len(parts):
            xs[t - 1], h = mix_residual(parts[t - 1], y_mix.pop(t - 1))
            y_ffn[t - 1] = swiglu_matmuls(h)
        if 0 <= t - 2 < len(parts):
            o_ref[0, parts[t - 2], :] = xs.pop(t - 2) + gate * (_rms(y_ffn.pop(t - 2)) * gpost_ffn_ref[...])


def _mix_ffn(x, o_a, o_b, gf, gm, mod, g_post_mix, g_pre_ffn, g_post_ffn, w_pf, w_pm, w_out, w_in, w_down):
    bsz, seq, d = x.shape
    tm = MIX_TILE
    row = lambda b, i: (b, i, 0)
    weights = (w_pf, w_pm, w_out, w_in, w_down)
    return pl.pallas_call(
        _mix_ffn_kernel,
        grid=(bsz, seq // tm),
        in_specs=[pl.BlockSpec((1, tm, d), row),
                  pl.BlockSpec((1, tm, FOX_WIDTH), row), pl.BlockSpec((1, tm, FOX_WIDTH), row),
                  pl.BlockSpec((1, tm, d), row), pl.BlockSpec((1, tm, d), row),
                  _const_spec(mod.shape),
                  _const_spec((1, d)), _const_spec((1, d)), _const_spec((1, d))]
                 + [_const_spec(w.shape) for w in weights],
        out_specs=pl.BlockSpec((1, tm, d), row),
        out_shape=jax.ShapeDtypeStruct(x.shape, F32),
        compiler_params=pltpu.CompilerParams(dimension_semantics=("arbitrary", "arbitrary"),
                                             vmem_limit_bytes=VMEM_LIMIT),
        name="mix_ffn",
    )(x, o_a, o_b, gf, gm, mod, g_post_mix.reshape(1, d), g_pre_ffn.reshape(1, d), g_post_ffn.reshape(1, d),
      *weights)


def kernel(x, c, positions, w_ada, b_ada, g_pre_mix, g_post_mix, g_pre_ffn, g_post_ffn, w_in, b_forget,
           g_q_lora, w_uq, g_kv_lora, w_ukv, w_proj_fox, w_proj_mla, w_out, w_ffn_in, w_ffn_out):
    bsz, seq, d = x.shape
    depth = w_ada.shape[0]
    for l in range(depth):
        mod, *w_in_parts = _prep(c, w_ada, b_ada, w_in, l)
        weights, bf3 = _prep_proj_weights(w_in_parts, w_uq[l], w_ukv[l], b_forget[l])
        late = (w_proj_fox, w_proj_mla, w_out, w_ffn_in, w_ffn_out)
        qf, kf, vf, qm, km, vm, gf, gm, *late_bf16 = _token_proj(
            x, mod, g_pre_mix[l], positions, weights, bf3, g_q_lora[l], g_kv_lora[l], late, l)
        o_a = _causal_attn(qf, kf, vf)
        o_b = _causal_attn(qm, km, vm)
        x = _mix_ffn(x, o_a, o_b, gf, gm, mod, g_post_mix[l], g_pre_ffn[l], g_post_ffn[l], *late_bf16)
    return x
```

```python
import functools
import math

import jax
import jax.numpy as jnp
import numpy as np
from jax import lax
from jax.experimental import pallas as pl
from jax.experimental.pallas import tpu as pltpu

D_MODEL = 1024
HEADS = 8
HEAD_DIM = 64
FOX_WIDTH = HEADS * HEAD_DIM
MLA_NOPE = 64
MLA_ROPE = 32
MLA_V = 64
MLA_Q_LORA = 768
MLA_KV_LORA = 256
D_FF = 2816
ROPE_THETA = 10000.0
NORM_EPS = 1e-6
IN_WIDTHS = (FOX_WIDTH, FOX_WIDTH, FOX_WIDTH, HEADS, MLA_Q_LORA, MLA_KV_LORA, MLA_ROPE, D_MODEL, D_MODEL)

LANES = 128
BF16_ROWS = 16
TOKEN_TILE = 512
MIX_TILE = 512
ROW_PART = 256
ATTN_TILE = 512
ATTN_PAIRS = 2
FFN_CHUNK = 256
VMEM_LIMIT = 56 * 1024 * 1024

_EXP2_FOX = math.log2(math.e) / math.sqrt(HEAD_DIM)
_EXP2_MLA = math.log2(math.e) / math.sqrt(MLA_NOPE + MLA_ROPE)

F32 = jnp.float32
BF16 = jnp.bfloat16


def _const_spec(shape):
    zeros = (0,) * len(shape)
    return pl.BlockSpec(shape, lambda *_: zeros, pipeline_mode=pl.Buffered(1))


def _rms(x):
    return x * lax.rsqrt(jnp.mean(x * x, axis=-1, keepdims=True) + NORM_EPS)


def _dot(a, b):
    return jnp.dot(a, b, preferred_element_type=F32)


def _cumsum_rows(x):
    n = x.shape[0]
    row = lax.broadcasted_iota(jnp.int32, x.shape, 0)
    d = 1
    while d < n:
        x = x + jnp.where(row >= d, pltpu.roll(x, d, axis=0), 0.0)
        d *= 2
    return x


def _rope_tables(pos_row, invf_col):
    n = pos_row.shape[1]
    groups = LANES // MLA_ROPE
    ang = invf_col * pos_row
    c = jnp.cos(ang)
    s = jnp.sin(ang)
    cos_blocks, sin_blocks = [], []
    for t0 in range(0, n, LANES):
        ct = c[:, t0:t0 + LANES]
        st = s[:, t0:t0 + LANES]
        cos_blocks.append(jnp.concatenate([ct, ct] * groups, axis=0).T)
        sin_blocks.append(jnp.concatenate([-st, st] * groups, axis=0).T)
    return jnp.concatenate(cos_blocks, axis=0), jnp.concatenate(sin_blocks, axis=0)


def _rope(x, cos4, sin4, lane):
    half = MLA_ROPE // 2
    swapped = jnp.where((lane & (MLA_ROPE - 1)) < half, pltpu.roll(x, LANES - half, axis=1), pltpu.roll(x, half, axis=1))
    return x * cos4 + swapped * sin4


def _store_values_transposed(vt_ref, rows, v):
    n = v.shape[0]
    row = lax.broadcasted_iota(jnp.int32, (LANES, n), 0)
    for pair in range(HEADS // 2):
        vt = v[:, pair * LANES:(pair + 1) * LANES].T
        even = jnp.where(row < HEAD_DIM, vt, jnp.where(row == HEAD_DIM, 1.0, 0.0))
        odd = jnp.where(row >= HEAD_DIM, vt, jnp.where(row == 0, 1.0, 0.0))
        vt_ref[0, 2 * pair, :, rows] = even.astype(vt_ref.dtype)
        vt_ref[0, 2 * pair + 1, :, rows] = odd.astype(vt_ref.dtype)


def _token_proj_kernel(x_ref, mod_ref, gpre_ref, bf_ref, gq_ref, gkv_ref, pos_ref, invf_ref,
                       wqkv_ref, wmisc_ref, wcq_ref, wckv_ref, wg_ref, wq_ref, wkv_ref, sel_ref,
                       late0_ref, late1_ref, late2_ref, late3_ref, late4_ref,
                       qf_ref, kf_ref, vf_ref, qm_ref, km_ref, vm_ref, gf_ref, gm_ref,
                       cast0_ref, cast1_ref, cast2_ref, cast3_ref, cast4_ref,
                       carry_ref, cos_ref, sin_ref):
    tm = x_ref.shape[1]

    @pl.when(pl.program_id(1) == 0)
    def _():
        carry_ref[...] = jnp.zeros_like(carry_ref)

    cos_ref[...], sin_ref[...] = _rope_tables(pos_ref[0], invf_ref[:, 0:1])

    mod = lambda k: mod_ref[pl.ds(pl.program_id(0), 1), k * D_MODEL:(k + 1) * D_MODEL]
    shift = mod(0)
    scale = mod(1)

    def project(rows, n):
        h = (_rms(x_ref[0, rows, :]) * gpre_ref[...] * (1.0 + scale) + shift).astype(BF16)
        lane = lax.broadcasted_iota(jnp.int32, (n, LANES), 1)
        cos4 = cos_ref[rows, :]
        sin4 = sin_ref[rows, :]
        low = lane < HEAD_DIM

        misc = _dot(h, wmisc_ref[...])
        logit = misc + bf_ref[...]
        logf = jnp.minimum(logit, 0.0) - jnp.log(1.0 + jnp.exp(-jnp.abs(logit)))
        cum = _cumsum_rows(logf) + carry_ref[0:1, :]
        carry_ref[0:1, :] = cum[n - 1:n, :]

        cq = _dot(h, wcq_ref[...])
        ckv = _dot(h, wckv_ref[...])
        nq = (_rms(cq) * gq_ref[...]).astype(BF16)
        nkv = (_rms(ckv) * gkv_ref[...]).astype(BF16)
        p_qkv = _dot(h, wqkv_ref[...])
        _store_values_transposed(vf_ref, rows, p_qkv[:, 2 * FOX_WIDTH:])

        qq = _dot(nq, wq_ref[...]) * _EXP2_MLA
        nope_w = HEADS * MLA_NOPE
        q_rope = [_rope(qq[:, nope_w + g * LANES:nope_w + (g + 1) * LANES], cos4, sin4, lane)
                  for g in range(HEADS * MLA_ROPE // LANES)]
        per_group = LANES // MLA_ROPE
        for hh in range(HEADS):
            pair = hh // 2
            nope = qq[:, pair * LANES:(pair + 1) * LANES]
            src_lane = (hh % per_group) * MLA_ROPE
            dst_lane = HEAD_DIM if hh % 2 == 0 else 0
            rope = q_rope[hh // per_group]
            if src_lane != dst_lane:
                rope = pltpu.roll(rope, (dst_lane - src_lane) % LANES, axis=1)
            in_rope = (lane >= dst_lane) & (lane < dst_lane + MLA_ROPE)
            own = low if hh % 2 == 0 else jnp.logical_not(low)
            qm_ref[0, hh, rows, :] = jnp.where(own, nope, jnp.where(in_rope, rope, 0.0)).astype(qm_ref.dtype)
        kv = _dot(nkv, wkv_ref[...])
        _store_values_transposed(vm_ref, rows, kv[:, nope_w:])
        in_rope = (lane >= MLA_NOPE) & (lane < MLA_NOPE + MLA_ROPE)
        kpe_even = jnp.where(in_rope, _rope(misc, cos4, sin4, lane), 0.0)
        kpe_odd = pltpu.roll(kpe_even, LANES - HEAD_DIM, axis=1)
        for hh in range(HEADS):
            pair = hh // 2
            nope = kv[:, pair * LANES:(pair + 1) * LANES]
            k_full = jnp.where(low, nope, kpe_even) if hh % 2 == 0 else jnp.where(low, kpe_odd, nope)
            km_ref[0, hh, rows, :] = k_full.astype(km_ref.dtype)

        a = cum * math.log2(math.e)
        a_hi = a.astype(BF16).astype(F32)
        r1 = a - a_hi
        a_mid = r1.astype(BF16).astype(F32)
        a_lo = (r1 - a_mid).astype(BF16).astype(F32)
        z = jnp.where(lane < 8, a_hi,
                      jnp.where(lane < 16, a_mid,
                                jnp.where(lane < 24, a_lo,
                                          jnp.where(lane == 24, 1.0, 0.0))))
        aug = _dot(z.astype(BF16), sel_ref[...])
        pairs = HEADS // 2
        for hh in range(HEADS):
            pair = hh // 2
            keep = (lane < HEAD_DIM) if hh % 2 == 0 else (lane >= HEAD_DIM)
            xq = p_qkv[:, pair * LANES:(pair + 1) * LANES]
            xk = p_qkv[:, FOX_WIDTH + pair * LANES:FOX_WIDTH + (pair + 1) * LANES]
            aq = aug[:, pair * LANES:(pair + 1) * LANES]
            ak = aug[:, (pairs + pair) * LANES:(pairs + pair + 1) * LANES]
            qf_ref[0, hh, rows, :] = jnp.where(keep, xq * _EXP2_FOX, aq).astype(qf_ref.dtype)
            kf_ref[0, hh, rows, :] = jnp.where(keep, xk, ak).astype(kf_ref.dtype)

        for gi, ref in enumerate((gf_ref, gm_ref)):
            g = _dot(h, wg_ref[:, gi * D_MODEL:(gi + 1) * D_MODEL])
            ref[0, rows, :] = jax.nn.sigmoid(g).astype(ref.dtype)

    n = tm // 2
    for r0 in (0, n):
        project(pl.ds(r0, n), n)

    for src, dst in ((late0_ref, cast0_ref), (late1_ref, cast1_ref), (late2_ref, cast2_ref),
                     (late3_ref, cast3_ref), (late4_ref, cast4_ref)):
        dst[...] = src[0].astype(dst.dtype)


_IN_OFFSETS = tuple(int(v) for v in np.cumsum((0,) + IN_WIDTHS))


def _prep_kernel(c_ref, wada_ref, bada_ref, wt_ref, mod_ref, qkv_ref, misc_ref, cq_ref, ckv_ref, g_ref):
    c = c_ref[...]
    sc = c * jax.nn.sigmoid(c)
    mod_ref[...] = _dot(sc.astype(BF16), wada_ref[0].astype(BF16)) + bada_ref[0]

    wt = wt_ref[0]
    cols = wt.shape[1]
    o = _IN_OFFSETS
    qkv_ref[...] = wt[o[0]:o[3], :].T.astype(BF16)
    cq_ref[...] = wt[o[4]:o[5], :].T.astype(BF16)
    ckv_ref[...] = wt[o[5]:o[6], :].T.astype(BF16)
    g_ref[...] = wt[o[7]:o[9], :].T.astype(BF16)
    f = wt[o[3]:o[4], :]
    kr = wt[o[6]:o[7], :]
    misc_t = jnp.concatenate([f, f, f, jnp.zeros((MLA_NOPE - 3 * HEADS, cols), F32), kr,
                              jnp.zeros((LANES - MLA_NOPE - MLA_ROPE, cols), F32)], axis=0)
    misc_ref[...] = misc_t.T.astype(BF16)


def _prep(c, w_ada_all, b_ada_all, w_in_all, layer):
    bsz, d = c.shape
    depth, _, n_mod = w_ada_all.shape
    n_in = w_in_all.shape[2]
    cols = 256
    steps = d // cols
    tn = n_mod // steps
    assert d % cols == 0 and n_mod % steps == 0 and tn % LANES == 0
    widths = (3 * FOX_WIDTH, LANES, MLA_Q_LORA, MLA_KV_LORA, 2 * D_MODEL)
    return pl.pallas_call(
        _prep_kernel,
        grid=(steps,),
        in_specs=[pl.BlockSpec((bsz, d), lambda j: (0, 0)),
                  pl.BlockSpec((1, d, tn), lambda j: (layer, 0, j)),
                  pl.BlockSpec((1, 1, tn), lambda j: (layer, 0, j)),
                  pl.BlockSpec((1, n_in, cols), lambda j: (layer, 0, j))],
        out_specs=[pl.BlockSpec((bsz, tn), lambda j: (0, j))]
                  + [pl.BlockSpec((cols, w), lambda j: (j, 0)) for w in widths],
        out_shape=[jax.ShapeDtypeStruct((bsz, n_mod), F32)]
                  + [jax.ShapeDtypeStruct((d, w), BF16) for w in widths],
        name="prep",
    )(c, w_ada_all, b_ada_all.reshape(depth, 1, n_mod), jnp.transpose(w_in_all, (0, 2, 1)))


def _prep_proj_weights(w_in_parts, w_uq, w_ukv, b_forget):
    w_qkv, w_misc, w_cq, w_ckv, w_g = w_in_parts

    r = w_uq.shape[0]
    uq = w_uq.reshape(r, HEADS, MLA_NOPE + MLA_ROPE)
    w_q = jnp.concatenate([uq[:, :, :MLA_NOPE].reshape(r, HEADS * MLA_NOPE),
                           uq[:, :, MLA_NOPE:].reshape(r, HEADS * MLA_ROPE)], axis=1).astype(BF16)

    rk = w_ukv.shape[0]
    ukv = w_ukv.reshape(rk, HEADS, MLA_NOPE + MLA_V)
    w_kv = jnp.concatenate([ukv[:, :, :MLA_NOPE].reshape(rk, HEADS * MLA_NOPE),
                            ukv[:, :, MLA_NOPE:].reshape(rk, HEADS * MLA_V)], axis=1).astype(BF16)

    bf3 = jnp.concatenate([b_forget, b_forget, b_forget,
                           jnp.zeros((LANES - 3 * HEADS,), b_forget.dtype)]).reshape(1, LANES)
    return (w_qkv, w_misc, w_cq, w_ckv, w_g, w_q, w_kv), bf3


def _decay_selector():
    pairs = HEADS // 2
    sel = np.zeros((LANES, 2 * pairs * LANES), np.float32)
    for hh in range(HEADS):
        base_q = (hh // 2) * LANES + (HEAD_DIM if hh % 2 == 0 else 0)
        base_k = (pairs + hh // 2) * LANES + (HEAD_DIM if hh % 2 == 0 else 0)
        for piece in range(3):
            sel[piece * HEADS + hh, base_q + piece] = 1.0
            sel[3 * HEADS, base_q + 3 + piece] = 1.0
            sel[3 * HEADS, base_k + piece] = 1.0
            sel[piece * HEADS + hh, base_k + 3 + piece] = -1.0
    return jnp.asarray(sel, BF16)


def _cast_block_rows(n_rows, n_steps):
    rb = -(-n_rows // n_steps)
    rb += -rb % BF16_ROWS
    while n_rows % rb:
        rb += BF16_ROWS
    return rb


def _token_proj(x, mod, g_pre, positions, weights, bf3, g_q, g_kv, late_weights, layer):
    bsz, seq, d = x.shape
    tm = TOKEN_TILE
    steps_per_batch = seq // tm
    n_steps = bsz * steps_per_batch
    late_in, late_out, late_shapes = [], [], []
    for w in late_weights:
        _, rows, cols = w.shape
        rb = _cast_block_rows(rows, n_steps)
        last = rows // rb - 1
        late_in.append(pl.BlockSpec(
            (1, rb, cols), lambda b, i, last=last: (layer, jnp.minimum(b * steps_per_batch + i, last), 0)))
        late_out.append(pl.BlockSpec(
            (rb, cols), lambda b, i, last=last: (jnp.minimum(b * steps_per_batch + i, last), 0)))
        late_shapes.append(jax.ShapeDtypeStruct((rows, cols), BF16))
    sel = _decay_selector()
    half = MLA_ROPE // 2
    inv_freq = 1.0 / (ROPE_THETA ** (np.arange(0, MLA_ROPE, 2, dtype=np.float32) / MLA_ROPE))
    invf = jnp.asarray(np.tile(inv_freq.astype(np.float32)[:, None], (1, LANES)))
    pos = positions.astype(F32).reshape(bsz, 1, seq)
    row = lambda b, i: (b, i, 0)
    head = lambda b, i: (b, 0, i, 0)
    head_shape = jax.ShapeDtypeStruct((bsz, HEADS, seq, LANES), BF16)
    head_spec = pl.BlockSpec((1, HEADS, tm, LANES), head)
    v_shape = jax.ShapeDtypeStruct((bsz, HEADS, LANES, seq), BF16)
    v_spec = pl.BlockSpec((1, HEADS, LANES, tm), lambda b, i: (b, 0, 0, i))
    g_shape = jax.ShapeDtypeStruct((bsz, seq, d), BF16)
    g_spec = pl.BlockSpec((1, tm, d), row)
    return pl.pallas_call(
        _token_proj_kernel,
        grid=(bsz, seq // tm),
        in_specs=[pl.BlockSpec((1, tm, d), row),
                  _const_spec(mod.shape),
                  _const_spec((1, d)), _const_spec((1, LANES)),
                  _const_spec((1, MLA_Q_LORA)), _const_spec((1, MLA_KV_LORA)),
                  pl.BlockSpec((1, 1, tm), lambda b, i: (b, 0, i)), _const_spec((half, LANES))]
                 + [_const_spec(w.shape) for w in weights] + [_const_spec(sel.shape)] + late_in,
        out_specs=[head_spec, head_spec, v_spec, head_spec, head_spec, v_spec, g_spec, g_spec] + late_out,
        out_shape=[head_shape, head_shape, v_shape, head_shape, head_shape, v_shape, g_shape, g_shape]
                  + late_shapes,
        scratch_shapes=[pltpu.VMEM((8, LANES), F32),
                        pltpu.VMEM((tm, LANES), F32),
                        pltpu.VMEM((tm, LANES), F32)],
        compiler_params=pltpu.CompilerParams(dimension_semantics=("arbitrary", "arbitrary"),
                                             vmem_limit_bytes=VMEM_LIMIT),
        name="token_proj",
    )(x, mod, g_pre.reshape(1, d), bf3, g_q.reshape(1, -1), g_kv.reshape(1, -1), pos, invf,
      *weights, sel, *late_weights)


def _attn_kernel(zero_ref, q_ref, k_ref, vt_ref, o_ref, sa_ref, sb_ref, m_ref, acc_ref, *, tile):
    seq = vt_ref.shape[3]
    nq = seq // tile
    n_heads = q_ref.shape[1]
    z = zero_ref[0]
    half = tile // 2
    nt = (((1,), (1,)), ((), ()))
    vrow = lax.broadcasted_iota(jnp.int32, (LANES, tile), 0)
    sum_row = (HEAD_DIM, 0)
    keep_even = vrow < HEAD_DIM
    krow = lax.broadcasted_iota(jnp.int32, (LANES, LANES), 0)
    qcol = lax.broadcasted_iota(jnp.int32, (LANES, LANES), 1)
    diag_ok = krow <= qcol

    def issue_scores(qi, j, s_out, heads=None):
        for hh in (range(n_heads) if heads is None else heads):
            q = q_ref[0, hh, qi * tile:(qi + 1) * tile, :]
            k = k_ref[0, hh, j * tile:(j + 1) * tile, :]
            if j == qi:
                s_out[z + hh, :half, :] = lax.dot_general(k[:half], q, nt, preferred_element_type=F32)
                s_out[z + hh, half:, half:] = lax.dot_general(k[half:], q[half:], nt, preferred_element_type=F32)
            else:
                s_out[z + hh] = lax.dot_general(k, q, nt, preferred_element_type=F32)

    def softmax_pv(s_in, hh, vth, masked):
        alphas, p_cols = [], []
        for c0 in range(0, tile, LANES):
            n_keys = c0 + LANES if masked else tile
            n_fill = tile if (c0 >= half or not masked) else half
            m_old = m_ref[hh, :, c0:c0 + LANES]

            def strip(r0):
                sb = s_in[z + hh, r0:r0 + LANES, c0:c0 + LANES]
                return jnp.where(diag_ok, sb, -jnp.inf) if (masked and r0 == c0) else sb

            mx = functools.reduce(jnp.maximum, [strip(r0) for r0 in range(0, n_keys, LANES)])
            m_new = jnp.maximum(m_old, jnp.max(mx, axis=0, keepdims=True))
            parts = [jnp.exp2(strip(r0) - m_new).astype(BF16) for r0 in range(0, n_keys, LANES)]
            parts += [jnp.zeros((LANES, LANES), BF16)] * ((n_fill - n_keys) // LANES)
            p_cols.append(jnp.concatenate(parts, axis=0))
            m_ref[hh, :, c0:c0 + LANES] = m_new
            alphas.append(jnp.exp2(m_old - m_new))
        alpha = jnp.concatenate(alphas, axis=1)
        n_left = half // LANES
        if masked:
            pv = jnp.concatenate([_dot(vth[:, :half], jnp.concatenate(p_cols[:n_left], axis=1)),
                                  _dot(vth, jnp.concatenate(p_cols[n_left:], axis=1))], axis=1)
        else:
            pv = _dot(vth, jnp.concatenate(p_cols, axis=1))
        acc_ref[z + hh] = alpha * acc_ref[z + hh] + pv

    def reset_state(heads):
        for hh in heads:
            m_ref[hh] = jnp.full(m_ref.shape[1:], -jnp.inf, F32)
            acc_ref[hh] = jnp.zeros(acc_ref.shape[1:], F32)

    blocks = [(qi, j) for qi in range(nq) for j in range(qi + 1)]
    bufs = (sa_ref, sb_ref)
    for pair in range(n_heads // 2):
        heads = (2 * pair, 2 * pair + 1)
        reset_state(heads)
        issue_scores(0, 0, bufs[0], heads)
        for t, (qi, j) in enumerate(blocks):
            s_in, s_out = bufs[t % 2], bufs[(t + 1) % 2]
            for hh in heads:
                if t + 1 < len(blocks):
                    issue_scores(*blocks[t + 1], s_out, heads=(hh,))
                softmax_pv(s_in, hh, vt_ref[0, hh, :, j * tile:(j + 1) * tile], j == qi)
            if j == qi:
                outs = []
                for hh in heads:
                    acc = acc_ref[z + hh]
                    outs.append(acc * (1.0 / acc[sum_row[hh % 2]:sum_row[hh % 2] + 1, :]))
                o_t = jnp.where(keep_even, outs[0], outs[1])
                o_ref[0, qi * tile:(qi + 1) * tile, pair * LANES:(pair + 1) * LANES] = o_t.T.astype(o_ref.dtype)
                if qi + 1 < nq:
                    reset_state(heads)


def _causal_attn(q, k, vt):
    bsz, heads, seq, _ = q.shape
    g = ATTN_PAIRS
    qk_spec = pl.BlockSpec((1, 2 * g, seq, LANES), lambda b, p: (b, p, 0, 0))
    vt_spec = pl.BlockSpec((1, 2 * g, LANES, seq), lambda b, p: (b, p, 0, 0))
    o_spec = pl.BlockSpec((1, seq, g * LANES), lambda b, p: (b, 0, p))
    kern = functools.partial(_attn_kernel, tile=ATTN_TILE)
    return pl.pallas_call(
        kern,
        grid=(bsz, heads // (2 * g)),
        in_specs=[pl.BlockSpec(memory_space=pltpu.SMEM), qk_spec, qk_spec, vt_spec],
        out_specs=o_spec,
        out_shape=jax.ShapeDtypeStruct((bsz, seq, heads * HEAD_DIM), BF16),
        scratch_shapes=[pltpu.VMEM((2 * g, ATTN_TILE, ATTN_TILE), F32),
                        pltpu.VMEM((2 * g, ATTN_TILE, ATTN_TILE), F32),
                        pltpu.VMEM((2 * g, 1, ATTN_TILE), F32),
                        pltpu.VMEM((2 * g, LANES, ATTN_TILE), F32)],
        compiler_params=pltpu.CompilerParams(dimension_semantics=("arbitrary", "arbitrary"),
                                             vmem_limit_bytes=VMEM_LIMIT),
        name="causal_attn",
    )(jnp.zeros((1,), jnp.int32), q, k, vt)


def _mix_ffn_kernel(x_ref, oa_ref, ob_ref, gf_ref, gm_ref, mod_ref, gpost_mix_ref, gpre_ffn_ref,
                    gpost_ffn_ref, wpf_ref, wpm_ref, wout_ref, win_ref, wdown_ref, o_ref):
    tm = x_ref.shape[1]
    mod = lambda k: mod_ref[pl.ds(pl.program_id(0), 1), k * D_MODEL:(k + 1) * D_MODEL]
    gate_mix = mod(2)
    shift = mod(3)
    scale = mod(4)
    gate = mod(5)

    def mix_matmuls(rows):
        pa = _dot(oa_ref[0, rows, :], wpf_ref[...])
        pb = _dot(ob_ref[0, rows, :], wpm_ref[...])
        merged = gf_ref[0, rows, :].astype(F32) * pa + gm_ref[0, rows, :].astype(F32) * pb
        return _dot(merged.astype(BF16), wout_ref[...])

    def mix_residual(rows, y):
        x = x_ref[0, rows, :] + gate_mix * (_rms(y) * gpost_mix_ref[...])
        h = (_rms(x) * gpre_ffn_ref[...] * (1.0 + scale) + shift).astype(BF16)
        return x, h

    chunks = [(lo, min(FFN_CHUNK, D_FF - lo)) for lo in range(0, D_FF, FFN_CHUNK)]

    def up_matmuls(h, lo, w):
        return _dot(h, win_ref[:, lo:lo + w]), _dot(h, win_ref[:, D_FF + lo:D_FF + lo + w])

    def swiglu_matmuls(h):
        y = None
        gu = up_matmuls(h, *chunks[0])
        for c, (lo, w) in enumerate(chunks):
            g, u = gu
            if c + 1 < len(chunks):
                gu = up_matmuls(h, *chunks[c + 1])
            act = (g * jax.nn.sigmoid(g) * u).astype(BF16)
            part = _dot(act, wdown_ref[lo:lo + w, :])
            y = part if y is None else y + part
        return y

    parts = [pl.ds(r0, ROW_PART) for r0 in range(0, tm, ROW_PART)]
    y_mix, xs, y_ffn = {}, {}, {}
    for t in range(len(parts) + 2):
        if t < len(parts):
            y_mix[t] = mix_matmuls(parts[t])
        if 0 <= t - 1 < len(parts):
            xs[t - 1], h = mix_residual(parts[t - 1], y_mix.pop(t - 1))
            y_ffn[t - 1] = swiglu_matmuls(h)
        if 0 <= t - 2 < len(parts):
            o_ref[0, parts[t - 2], :] = xs.pop(t - 2) + gate * (_rms(y_ffn.pop(t - 2)) * gpost_ffn_ref[...])


def _mix_ffn(x, o_a, o_b, gf, gm, mod, g_post_mix, g_pre_ffn, g_post_ffn, w_pf, w_pm, w_out, w_in, w_down):
    bsz, seq, d = x.shape
    tm = MIX_TILE
    row = lambda b, i: (b, i, 0)
    weights = (w_pf, w_pm, w_out, w_in, w_down)
    return pl.pallas_call(
        _mix_ffn_kernel,
        grid=(bsz, seq // tm),
        in_specs=[pl.BlockSpec((1, tm, d), row),
                  pl.BlockSpec((1, tm, FOX_WIDTH), row), pl.BlockSpec((1, tm, FOX_WIDTH), row),
                  pl.BlockSpec((1, tm, d), row), pl.BlockSpec((1, tm, d), row),
                  _const_spec(mod.shape),
                  _const_spec((1, d)), _const_spec((1, d)), _const_spec((1, d))]
                 + [_const_spec(w.shape) for w in weights],
        out_specs=pl.BlockSpec((1, tm, d), row),
        out_shape=jax.ShapeDtypeStruct(x.shape, F32),
        compiler_params=pltpu.CompilerParams(dimension_semantics=("arbitrary", "arbitrary"),
                                             vmem_limit_bytes=VMEM_LIMIT),
        name="mix_ffn",
    )(x, o_a, o_b, gf, gm, mod, g_post_mix.reshape(1, d), g_pre_ffn.reshape(1, d), g_post_ffn.reshape(1, d),
      *weights)


def kernel(x, c, positions, w_ada, b_ada, g_pre_mix, g_post_mix, g_pre_ffn, g_post_ffn, w_in, b_forget,
           g_q_lora, w_uq, g_kv_lora, w_ukv, w_proj_fox, w_proj_mla, w_out, w_ffn_in, w_ffn_out):
    bsz, seq, d = x.shape
    depth = w_ada.shape[0]
    for l in range(depth):
        mod, *w_in_parts = _prep(c, w_ada, b_ada, w_in, l)
        weights, bf3 = _prep_proj_weights(w_in_parts, w_uq[l], w_ukv[l], b_forget[l])
        late = (w_proj_fox, w_proj_mla, w_out, w_ffn_in, w_ffn_out)
        qf, kf, vf, qm, km, vm, gf, gm, *late_bf16 = _token_proj(
            x, mod, g_pre_mix[l], positions, weights, bf3, g_q_lora[l], g_kv_lora[l], late, l)
        o_a = _causal_attn(qf, kf, vf)
        o_b = _causal_attn(qm, km, vm)
        x = _mix_ffn(x, o_a, o_b, gf, gm, mod, g_post_mix[l], g_pre_ffn[l], g_post_ffn[l], *late_bf16)
    return x
```

```python
import functools
import math

import jax
import jax.numpy as jnp
import numpy as np
from jax import lax
from jax.experimental import pallas as pl
from jax.experimental.pallas import tpu as pltpu

D_MODEL = 1024
HEADS = 8
HEAD_DIM = 64
FOX_WIDTH = HEADS * HEAD_DIM
MLA_NOPE = 64
MLA_ROPE = 32
MLA_V = 64
MLA_Q_LORA = 768
MLA_KV_LORA = 256
D_FF = 2816
ROPE_THETA = 10000.0
NORM_EPS = 1e-6
IN_WIDTHS = (FOX_WIDTH, FOX_WIDTH, FOX_WIDTH, HEADS, MLA_Q_LORA, MLA_KV_LORA, MLA_ROPE, D_MODEL, D_MODEL)

LANES = 128
SUBLANES = 8
BF16_ROWS = 16
TOKEN_TILE = 512
MIX_TILE = 512
ROW_PART = 256
ATTN_TILE = 512
ATTN_PAIRS = 4
FFN_CHUNK = 256
VMEM_LIMIT = 56 * 1024 * 1024

_EXP2_FOX = math.log2(math.e) / math.sqrt(HEAD_DIM)
_EXP2_MLA = math.log2(math.e) / math.sqrt(MLA_NOPE + MLA_ROPE)

F32 = jnp.float32
BF16 = jnp.bfloat16


def _const_spec(shape):
    zeros = (0,) * len(shape)
    return pl.BlockSpec(shape, lambda *_: zeros, pipeline_mode=pl.Buffered(1))


def _rms(x):
    return x * lax.rsqrt(jnp.mean(x * x, axis=-1, keepdims=True) + NORM_EPS)


def _dot(a, b):
    return jnp.dot(a, b, preferred_element_type=F32)


def _cumsum_rows(x):
    n = x.shape[0]
    row = lax.broadcasted_iota(jnp.int32, x.shape, 0)
    d = 1
    while d < n:
        x = x + jnp.where(row >= d, pltpu.roll(x, d, axis=0), 0.0)
        d *= 2
    return x


def _rope_tables(pos_row, invf_col):
    n = pos_row.shape[1]
    groups = LANES // MLA_ROPE
    ang = invf_col * pos_row
    c = jnp.cos(ang)
    s = jnp.sin(ang)
    cos_blocks, sin_blocks = [], []
    for t0 in range(0, n, LANES):
        ct = c[:, t0:t0 + LANES]
        st = s[:, t0:t0 + LANES]
        cos_blocks.append(jnp.concatenate([ct, ct] * groups, axis=0).T)
        sin_blocks.append(jnp.concatenate([-st, st] * groups, axis=0).T)
    return jnp.concatenate(cos_blocks, axis=0), jnp.concatenate(sin_blocks, axis=0)


def _rope(x, cos4, sin4, lane):
    half = MLA_ROPE // 2
    swapped = jnp.where((lane & (MLA_ROPE - 1)) < half, pltpu.roll(x, LANES - half, axis=1), pltpu.roll(x, half, axis=1))
    return x * cos4 + swapped * sin4


def _store_values_transposed(vt_ref, rows, v):
    n = v.shape[0]
    row = lax.broadcasted_iota(jnp.int32, (LANES, n), 0)
    for pair in range(HEADS // 2):
        vt = v[:, pair * LANES:(pair + 1) * LANES].T
        even = jnp.where(row < HEAD_DIM, vt, jnp.where(row == HEAD_DIM, 1.0, 0.0))
        odd = jnp.where(row >= HEAD_DIM, vt, jnp.where(row == 0, 1.0, 0.0))
        vt_ref[0, 2 * pair, :, rows] = even.astype(vt_ref.dtype)
        vt_ref[0, 2 * pair + 1, :, rows] = odd.astype(vt_ref.dtype)


def _token_proj_kernel(x_ref, mod_ref, gpre_ref, bf_ref, gq_ref, gkv_ref, pos_ref, invf_ref,
                       wqkv_ref, wmisc_ref, wcq_ref, wckv_ref, wg_ref, wq_ref, wkv_ref, sel_ref,
                       late0_ref, late1_ref, late2_ref, late3_ref, late4_ref,
                       qf_ref, kf_ref, vf_ref, qm_ref, km_ref, vm_ref, gf_ref, gm_ref,
                       cast0_ref, cast1_ref, cast2_ref, cast3_ref, cast4_ref,
                       carry_ref, cos_ref, sin_ref):
    tm = x_ref.shape[1]

    @pl.when(pl.program_id(1) == 0)
    def _():
        carry_ref[...] = jnp.zeros_like(carry_ref)

    cos_ref[...], sin_ref[...] = _rope_tables(pos_ref[0], invf_ref[:, 0:1])

    mod = lambda k: mod_ref[pl.ds(pl.program_id(0), 1), k * D_MODEL:(k + 1) * D_MODEL]
    shift = mod(0)
    scale = mod(1)

    def project(rows, n):
        h = (_rms(x_ref[0, rows, :]) * gpre_ref[...] * (1.0 + scale) + shift).astype(BF16)
        lane = lax.broadcasted_iota(jnp.int32, (n, LANES), 1)
        cos4 = cos_ref[rows, :]
        sin4 = sin_ref[rows, :]
        low = lane < HEAD_DIM

        misc = _dot(h, wmisc_ref[...])
        logit = misc + bf_ref[...]
        logf = jnp.minimum(logit, 0.0) - jnp.log(1.0 + jnp.exp(-jnp.abs(logit)))
        cum = _cumsum_rows(logf) + carry_ref[0:1, :]
        carry_ref[0:1, :] = cum[n - 1:n, :]

        cq = _dot(h, wcq_ref[...])
        ckv = _dot(h, wckv_ref[...])
        nq = (_rms(cq) * gq_ref[...]).astype(BF16)
        nkv = (_rms(ckv) * gkv_ref[...]).astype(BF16)
        p_qkv = _dot(h, wqkv_ref[...])
        _store_values_transposed(vf_ref, rows, p_qkv[:, 2 * FOX_WIDTH:])

        qq = _dot(nq, wq_ref[...]) * _EXP2_MLA
        nope_w = HEADS * MLA_NOPE
        q_rope = [_rope(qq[:, nope_w + g * LANES:nope_w + (g + 1) * LANES], cos4, sin4, lane)
                  for g in range(HEADS * MLA_ROPE // LANES)]
        per_group = LANES // MLA_ROPE
        for hh in range(HEADS):
            pair = hh // 2
            nope = qq[:, pair * LANES:(pair + 1) * LANES]
            src_lane = (hh % per_group) * MLA_ROPE
            dst_lane = HEAD_DIM if hh % 2 == 0 else 0
            rope = q_rope[hh // per_group]
            if src_lane != dst_lane:
                rope = pltpu.roll(rope, (dst_lane - src_lane) % LANES, axis=1)
            in_rope = (lane >= dst_lane) & (lane < dst_lane + MLA_ROPE)
            own = low if hh % 2 == 0 else jnp.logical_not(low)
            qm_ref[0, hh, rows, :] = jnp.where(own, nope, jnp.where(in_rope, rope, 0.0)).astype(qm_ref.dtype)
        kv = _dot(nkv, wkv_ref[...])
        _store_values_transposed(vm_ref, rows, kv[:, nope_w:])
        in_rope = (lane >= MLA_NOPE) & (lane < MLA_NOPE + MLA_ROPE)
        kpe_even = jnp.where(in_rope, _rope(misc, cos4, sin4, lane), 0.0)
        kpe_odd = pltpu.roll(kpe_even, LANES - HEAD_DIM, axis=1)
        for hh in range(HEADS):
            pair = hh // 2
            nope = kv[:, pair * LANES:(pair + 1) * LANES]
            k_full = jnp.where(low, nope, kpe_even) if hh % 2 == 0 else jnp.where(low, kpe_odd, nope)
            km_ref[0, hh, rows, :] = k_full.astype(km_ref.dtype)

        a = cum * math.log2(math.e)
        a_hi = a.astype(BF16).astype(F32)
        r1 = a - a_hi
        a_mid = r1.astype(BF16).astype(F32)
        a_lo = (r1 - a_mid).astype(BF16).astype(F32)
        z = jnp.where(lane < HEADS, a_hi,
                      jnp.where(lane < 2 * HEADS, a_mid,
                                jnp.where(lane < 3 * HEADS, a_lo,
                                          jnp.where(lane == 3 * HEADS, 1.0, 0.0))))
        aug = _dot(z.astype(BF16), sel_ref[...])
        pairs = HEADS // 2
        for hh in range(HEADS):
            pair = hh // 2
            keep = (lane < HEAD_DIM) if hh % 2 == 0 else (lane >= HEAD_DIM)
            xq = p_qkv[:, pair * LANES:(pair + 1) * LANES]
            xk = p_qkv[:, FOX_WIDTH + pair * LANES:FOX_WIDTH + (pair + 1) * LANES]
            aq = aug[:, pair * LANES:(pair + 1) * LANES]
            ak = aug[:, (pairs + pair) * LANES:(pairs + pair + 1) * LANES]
            qf_ref[0, hh, rows, :] = jnp.where(keep, xq * _EXP2_FOX, aq).astype(qf_ref.dtype)
            kf_ref[0, hh, rows, :] = jnp.where(keep, xk, ak).astype(kf_ref.dtype)

        for gi, ref in enumerate((gf_ref, gm_ref)):
            g = _dot(h, wg_ref[:, gi * D_MODEL:(gi + 1) * D_MODEL])
            ref[0, rows, :] = jax.nn.sigmoid(g).astype(ref.dtype)

    n = tm // 2
    for r0 in (0, n):
        project(pl.ds(r0, n), n)

    for src, dst in ((late0_ref, cast0_ref), (late1_ref, cast1_ref), (late2_ref, cast2_ref),
                     (late3_ref, cast3_ref), (late4_ref, cast4_ref)):
        dst[...] = src[0].astype(dst.dtype)


_IN_OFFSETS = tuple(int(v) for v in np.cumsum((0,) + IN_WIDTHS))


def _prep_kernel(c_ref, wada_ref, bada_ref, wt_ref, mod_ref, qkv_ref, misc_ref, cq_ref, ckv_ref, g_ref):
    c = c_ref[...]
    sc = c * jax.nn.sigmoid(c)
    mod_ref[...] = _dot(sc.astype(BF16), wada_ref[0].astype(BF16)) + bada_ref[0]

    wt = wt_ref[0]
    cols = wt.shape[1]
    o = _IN_OFFSETS
    qkv_ref[...] = wt[o[0]:o[3], :].T.astype(BF16)
    cq_ref[...] = wt[o[4]:o[5], :].T.astype(BF16)
    ckv_ref[...] = wt[o[5]:o[6], :].T.astype(BF16)
    g_ref[...] = wt[o[7]:o[9], :].T.astype(BF16)
    f = wt[o[3]:o[4], :]
    kr = wt[o[6]:o[7], :]
    misc_t = jnp.concatenate([f, f, f, jnp.zeros((MLA_NOPE - 3 * HEADS, cols), F32), kr,
                              jnp.zeros((LANES - MLA_NOPE - MLA_ROPE, cols), F32)], axis=0)
    misc_ref[...] = misc_t.T.astype(BF16)


def _prep(c, w_ada_all, b_ada_all, w_in_all, layer):
    bsz, d = c.shape
    depth, _, n_mod = w_ada_all.shape
    n_in = w_in_all.shape[2]
    cols = 2 * LANES
    steps = d // cols
    tn = n_mod // steps
    assert d % cols == 0 and n_mod % steps == 0 and tn % LANES == 0
    widths = (3 * FOX_WIDTH, LANES, MLA_Q_LORA, MLA_KV_LORA, 2 * D_MODEL)
    return pl.pallas_call(
        _prep_kernel,
        grid=(steps,),
        in_specs=[pl.BlockSpec((bsz, d), lambda j: (0, 0)),
                  pl.BlockSpec((1, d, tn), lambda j: (layer, 0, j)),
                  pl.BlockSpec((1, 1, tn), lambda j: (layer, 0, j)),
                  pl.BlockSpec((1, n_in, cols), lambda j: (layer, 0, j))],
        out_specs=[pl.BlockSpec((bsz, tn), lambda j: (0, j))]
                  + [pl.BlockSpec((cols, w), lambda j: (j, 0)) for w in widths],
        out_shape=[jax.ShapeDtypeStruct((bsz, n_mod), F32)]
                  + [jax.ShapeDtypeStruct((d, w), BF16) for w in widths],
        name="prep",
    )(c, w_ada_all, b_ada_all.reshape(depth, 1, n_mod), jnp.transpose(w_in_all, (0, 2, 1)))


def _prep_proj_weights(w_in_parts, w_uq, w_ukv, b_forget):
    w_qkv, w_misc, w_cq, w_ckv, w_g = w_in_parts

    r = w_uq.shape[0]
    uq = w_uq.reshape(r, HEADS, MLA_NOPE + MLA_ROPE)
    w_q = jnp.concatenate([uq[:, :, :MLA_NOPE].reshape(r, HEADS * MLA_NOPE),
                           uq[:, :, MLA_NOPE:].reshape(r, HEADS * MLA_ROPE)], axis=1).astype(BF16)

    rk = w_ukv.shape[0]
    ukv = w_ukv.reshape(rk, HEADS, MLA_NOPE + MLA_V)
    w_kv = jnp.concatenate([ukv[:, :, :MLA_NOPE].reshape(rk, HEADS * MLA_NOPE),
                            ukv[:, :, MLA_NOPE:].reshape(rk, HEADS * MLA_V)], axis=1).astype(BF16)

    bf3 = jnp.concatenate([b_forget, b_forget, b_forget,
                           jnp.zeros((LANES - 3 * HEADS,), b_forget.dtype)]).reshape(1, LANES)
    return (w_qkv, w_misc, w_cq, w_ckv, w_g, w_q, w_kv), bf3


def _decay_selector():
    pairs = HEADS // 2
    sel = np.zeros((LANES, 2 * pairs * LANES), np.float32)
    for hh in range(HEADS):
        base_q = (hh // 2) * LANES + (HEAD_DIM if hh % 2 == 0 else 0)
        base_k = (pairs + hh // 2) * LANES + (HEAD_DIM if hh % 2 == 0 else 0)
        for piece in range(3):
            sel[piece * HEADS + hh, base_q + piece] = 1.0
            sel[3 * HEADS, base_q + 3 + piece] = 1.0
            sel[3 * HEADS, base_k + piece] = 1.0
            sel[piece * HEADS + hh, base_k + 3 + piece] = -1.0
    return jnp.asarray(sel, BF16)


def _cast_block_rows(n_rows, n_steps):
    rb = -(-n_rows // n_steps)
    rb += -rb % BF16_ROWS
    while n_rows % rb:
        rb += BF16_ROWS
    return rb


def _token_proj(x, mod, g_pre, positions, weights, bf3, g_q, g_kv, late_weights, layer):
    bsz, seq, d = x.shape
    tm = TOKEN_TILE
    steps_per_batch = seq // tm
    n_steps = bsz * steps_per_batch
    late_in, late_out, late_shapes = [], [], []
    for w in late_weights:
        _, rows, cols = w.shape
        rb = _cast_block_rows(rows, n_steps)
        last = rows // rb - 1
        late_in.append(pl.BlockSpec(
            (1, rb, cols), lambda b, i, last=last: (layer, jnp.minimum(b * steps_per_batch + i, last), 0)))
        late_out.append(pl.BlockSpec(
            (rb, cols), lambda b, i, last=last: (jnp.minimum(b * steps_per_batch + i, last), 0)))
        late_shapes.append(jax.ShapeDtypeStruct((rows, cols), BF16))
    sel = _decay_selector()
    half = MLA_ROPE // 2
    inv_freq = 1.0 / (ROPE_THETA ** (np.arange(0, MLA_ROPE, 2, dtype=np.float32) / MLA_ROPE))
    invf = jnp.asarray(np.tile(inv_freq.astype(np.float32)[:, None], (1, LANES)))
    pos = positions.astype(F32).reshape(bsz, 1, seq)
    row = lambda b, i: (b, i, 0)
    head = lambda b, i: (b, 0, i, 0)
    head_shape = jax.ShapeDtypeStruct((bsz, HEADS, seq, LANES), BF16)
    head_spec = pl.BlockSpec((1, HEADS, tm, LANES), head)
    v_shape = jax.ShapeDtypeStruct((bsz, HEADS, LANES, seq), BF16)
    v_spec = pl.BlockSpec((1, HEADS, LANES, tm), lambda b, i: (b, 0, 0, i))
    g_shape = jax.ShapeDtypeStruct((bsz, seq, d), BF16)
    g_spec = pl.BlockSpec((1, tm, d), row)
    return pl.pallas_call(
        _token_proj_kernel,
        grid=(bsz, seq // tm),
        in_specs=[pl.BlockSpec((1, tm, d), row),
                  _const_spec(mod.shape),
                  _const_spec((1, d)), _const_spec((1, LANES)),
                  _const_spec((1, MLA_Q_LORA)), _const_spec((1, MLA_KV_LORA)),
                  pl.BlockSpec((1, 1, tm), lambda b, i: (b, 0, i)), _const_spec((half, LANES))]
                 + [_const_spec(w.shape) for w in weights] + [_const_spec(sel.shape)] + late_in,
        out_specs=[head_spec, head_spec, v_spec, head_spec, head_spec, v_spec, g_spec, g_spec] + late_out,
        out_shape=[head_shape, head_shape, v_shape, head_shape, head_shape, v_shape, g_shape, g_shape]
                  + late_shapes,
        scratch_shapes=[pltpu.VMEM((SUBLANES, LANES), F32),
                        pltpu.VMEM((tm, LANES), F32),
                        pltpu.VMEM((tm, LANES), F32)],
        compiler_params=pltpu.CompilerParams(dimension_semantics=("arbitrary", "arbitrary"),
                                             vmem_limit_bytes=VMEM_LIMIT),
        name="token_proj",
    )(x, mod, g_pre.reshape(1, d), bf3, g_q.reshape(1, -1), g_kv.reshape(1, -1), pos, invf,
      *weights, sel, *late_weights)


def _attn_kernel(zero_ref, q_ref, k_ref, vt_ref, o_ref, sa_ref, sb_ref, m_ref, acc_ref, *, tile):
    seq = vt_ref.shape[3]
    nq = seq // tile
    n_heads = q_ref.shape[1]
    z = zero_ref[0]
    half = tile // 2
    nt = (((1,), (1,)), ((), ()))
    vrow = lax.broadcasted_iota(jnp.int32, (LANES, tile), 0)
    sum_row = (HEAD_DIM, 0)
    keep_even = vrow < HEAD_DIM
    krow = lax.broadcasted_iota(jnp.int32, (LANES, LANES), 0)
    qcol = lax.broadcasted_iota(jnp.int32, (LANES, LANES), 1)
    diag_ok = krow <= qcol

    def issue_scores(qi, j, s_out, heads=None):
        for hh in (range(n_heads) if heads is None else heads):
            q = q_ref[0, hh, qi * tile:(qi + 1) * tile, :]
            k = k_ref[0, hh, j * tile:(j + 1) * tile, :]
            if j == qi:
                s_out[z + hh, :half, :] = lax.dot_general(k[:half], q, nt, preferred_element_type=F32)
                s_out[z + hh, half:, half:] = lax.dot_general(k[half:], q[half:], nt, preferred_element_type=F32)
            else:
                s_out[z + hh] = lax.dot_general(k, q, nt, preferred_element_type=F32)

    def softmax_pv(s_in, hh, vth, masked):
        alphas, p_cols = [], []
        for c0 in range(0, tile, LANES):
            n_keys = c0 + LANES if masked else tile
            n_fill = tile if (c0 >= half or not masked) else half
            m_old = m_ref[hh, :, c0:c0 + LANES]

            def strip(r0):
                sb = s_in[z + hh, r0:r0 + LANES, c0:c0 + LANES]
                return jnp.where(diag_ok, sb, -jnp.inf) if (masked and r0 == c0) else sb

            mx = functools.reduce(jnp.maximum, [strip(r0) for r0 in range(0, n_keys, LANES)])
            m_new = jnp.maximum(m_old, jnp.max(mx, axis=0, keepdims=True))
            parts = [jnp.exp2(strip(r0) - m_new).astype(BF16) for r0 in range(0, n_keys, LANES)]
            parts += [jnp.zeros((LANES, LANES), BF16)] * ((n_fill - n_keys) // LANES)
            p_cols.append(jnp.concatenate(parts, axis=0))
            m_ref[hh, :, c0:c0 + LANES] = m_new
            alphas.append(jnp.exp2(m_old - m_new))
        alpha = jnp.concatenate(alphas, axis=1)
        n_left = half // LANES
        if masked:
            pv = jnp.concatenate([_dot(vth[:, :half], jnp.concatenate(p_cols[:n_left], axis=1)),
                                  _dot(vth, jnp.concatenate(p_cols[n_left:], axis=1))], axis=1)
        else:
            pv = _dot(vth, jnp.concatenate(p_cols, axis=1))
        acc_ref[z + hh] = alpha * acc_ref[z + hh] + pv

    def reset_state(heads):
        for hh in heads:
            m_ref[hh] = jnp.full(m_ref.shape[1:], -jnp.inf, F32)
            acc_ref[hh] = jnp.zeros(acc_ref.shape[1:], F32)

    blocks = [(qi, j) for qi in range(nq) for j in range(qi + 1)]
    bufs = (sa_ref, sb_ref)
    for pair in range(n_heads // 2):
        heads = (2 * pair, 2 * pair + 1)
        reset_state(heads)
        issue_scores(0, 0, bufs[0], heads)
        for t, (qi, j) in enumerate(blocks):
            s_in, s_out = bufs[t % 2], bufs[(t + 1) % 2]
            for hh in heads:
                if t + 1 < len(blocks):
                    issue_scores(*blocks[t + 1], s_out, heads=(hh,))
                softmax_pv(s_in, hh, vt_ref[0, hh, :, j * tile:(j + 1) * tile], j == qi)
            if j == qi:
                outs = []
                for hh in heads:
                    acc = acc_ref[z + hh]
                    outs.append(acc * (1.0 / acc[sum_row[hh % 2]:sum_row[hh % 2] + 1, :]))
                o_t = jnp.where(keep_even, outs[0], outs[1])
                o_ref[0, qi * tile:(qi + 1) * tile, pair * LANES:(pair + 1) * LANES] = o_t.T.astype(o_ref.dtype)
                if qi + 1 < nq:
                    reset_state(heads)


def _causal_attn(q, k, vt):
    bsz, heads, seq, _ = q.shape
    g = ATTN_PAIRS
    qk_spec = pl.BlockSpec((1, 2 * g, seq, LANES), lambda b, p: (b, p, 0, 0))
    vt_spec = pl.BlockSpec((1, 2 * g, LANES, seq), lambda b, p: (b, p, 0, 0))
    o_spec = pl.BlockSpec((1, seq, g * LANES), lambda b, p: (b, 0, p))
    kern = functools.partial(_attn_kernel, tile=ATTN_TILE)
    return pl.pallas_call(
        kern,
        grid=(bsz, heads // (2 * g)),
        in_specs=[pl.BlockSpec(memory_space=pltpu.SMEM), qk_spec, qk_spec, vt_spec],
        out_specs=o_spec,
        out_shape=jax.ShapeDtypeStruct((bsz, seq, heads * HEAD_DIM), BF16),
        scratch_shapes=[pltpu.VMEM((2 * g, ATTN_TILE, ATTN_TILE), F32),
                        pltpu.VMEM((2 * g, ATTN_TILE, ATTN_TILE), F32),
                        pltpu.VMEM((2 * g, 1, ATTN_TILE), F32),
                        pltpu.VMEM((2 * g, LANES, ATTN_TILE), F32)],
        compiler_params=pltpu.CompilerParams(dimension_semantics=("arbitrary", "arbitrary"),
                                             vmem_limit_bytes=VMEM_LIMIT),
        name="causal_attn",
    )(jnp.zeros((1,), jnp.int32), q, k, vt)


def _mix_ffn_kernel(x_ref, oa_ref, ob_ref, gf_ref, gm_ref, mod_ref, gpost_mix_ref, gpre_ffn_ref,
                    gpost_ffn_ref, wpf_ref, wpm_ref, wout_ref, win_ref, wdown_ref, o_ref):
    tm = x_ref.shape[1]
    mod = lambda k: mod_ref[pl.ds(pl.program_id(0), 1), k * D_MODEL:(k + 1) * D_MODEL]
    gate_mix = mod(2)
    shift = mod(3)
    scale = mod(4)
    gate = mod(5)

    def mix_matmuls(rows):
        pa = _dot(oa_ref[0, rows, :], wpf_ref[...])
        pb = _dot(ob_ref[0, rows, :], wpm_ref[...])
        merged = gf_ref[0, rows, :].astype(F32) * pa + gm_ref[0, rows, :].astype(F32) * pb
        return _dot(merged.astype(BF16), wout_ref[...])

    def mix_residual(rows, y):
        x = x_ref[0, rows, :] + gate_mix * (_rms(y) * gpost_mix_ref[...])
        h = (_rms(x) * gpre_ffn_ref[...] * (1.0 + scale) + shift).astype(BF16)
        return x, h

    chunks = [(lo, min(FFN_CHUNK, D_FF - lo)) for lo in range(0, D_FF, FFN_CHUNK)]

    def up_matmuls(h, lo, w):
        return _dot(h, win_ref[:, lo:lo + w]), _dot(h, win_ref[:, D_FF + lo:D_FF + lo + w])

    def swiglu_matmuls(h):
        y = None
        gu = up_matmuls(h, *chunks[0])
        for c, (lo, w) in enumerate(chunks):
            g, u = gu
            if c + 1 < len(chunks):
                gu = up_matmuls(h, *chunks[c + 1])
            act = (g * jax.nn.sigmoid(g) * u).astype(BF16)
            part = _dot(act, wdown_ref[lo:lo + w, :])
            y = part if y is None else y + part
        return y

    parts = [pl.ds(r0, ROW_PART) for r0 in range(0, tm, ROW_PART)]
    y_mix, xs, y_ffn = {}, {}, {}
    for t in range(len(parts) + 2):
        if t < len(parts):
            y_mix[t] = mix_matmuls(parts[t])
        if 0 <= t - 1 < len(parts):
            xs[t - 1], h = mix_residual(parts[t - 1], y_mix.pop(t - 1))
            y_ffn[t - 1] = swiglu_matmuls(h)
        if 0 <= t - 2 < len(parts):
            o_ref[0, parts[t - 2], :] = xs.pop(t - 2) + gate * (_rms(y_ffn.pop(t - 2)) * gpost_ffn_ref[...])


def _mix_ffn(x, o_a, o_b, gf, gm, mod, g_post_mix, g_pre_ffn, g_post_ffn, w_pf, w_pm, w_out, w_in, w_down):
    bsz, seq, d = x.shape
    tm = MIX_TILE
    row = lambda b, i: (b, i, 0)
    weights = (w_pf, w_pm, w_out, w_in, w_down)
    return pl.pallas_call(
        _mix_ffn_kernel,
        grid=(bsz, seq // tm),
        in_specs=[pl.BlockSpec((1, tm, d), row),
                  pl.BlockSpec((1, tm, FOX_WIDTH), row), pl.BlockSpec((1, tm, FOX_WIDTH), row),
                  pl.BlockSpec((1, tm, d), row), pl.BlockSpec((1, tm, d), row),
                  _const_spec(mod.shape),
                  _const_spec((1, d)), _const_spec((1, d)), _const_spec((1, d))]
                 + [_const_spec(w.shape) for w in weights],
        out_specs=pl.BlockSpec((1, tm, d), row),
        out_shape=jax.ShapeDtypeStruct(x.shape, F32),
        compiler_params=pltpu.CompilerParams(dimension_semantics=("arbitrary", "arbitrary"),
                                             vmem_limit_bytes=VMEM_LIMIT),
        name="mix_ffn",
    )(x, o_a, o_b, gf, gm, mod, g_post_mix.reshape(1, d), g_pre_ffn.reshape(1, d), g_post_ffn.reshape(1, d),
      *weights)


def kernel(x, c, positions, w_ada, b_ada, g_pre_mix, g_post_mix, g_pre_ffn, g_post_ffn, w_in, b_forget,
           g_q_lora, w_uq, g_kv_lora, w_ukv, w_proj_fox, w_proj_mla, w_out, w_ffn_in, w_ffn_out):
    bsz, seq, d = x.shape
    depth = w_ada.shape[0]
    for l in range(depth):
        mod, *w_in_parts = _prep(c, w_ada, b_ada, w_in, l)
        weights, bf3 = _prep_proj_weights(w_in_parts, w_uq[l], w_ukv[l], b_forget[l])
        late = (w_proj_fox, w_proj_mla, w_out, w_ffn_in, w_ffn_out)
        qf, kf, vf, qm, km, vm, gf, gm, *late_bf16 = _token_proj(
            x, mod, g_pre_mix[l], positions, weights, bf3, g_q_lora[l], g_kv_lora[l], late, l)
        o_a = _causal_attn(qf, kf, vf)
        o_b = _causal_attn(qm, km, vm)
        x = _mix_ffn(x, o_a, o_b, gf, gm, mod, g_post_mix[l], g_pre_ffn[l], g_post_ffn[l], *late_bf16)
    return x
```

```python
import functools
import math

import jax
import jax.numpy as jnp
import numpy as np
from jax import lax
from jax.experimental import pallas as pl
from jax.experimental.pallas import tpu as pltpu

D_MODEL = 1024
HEADS = 8
HEAD_DIM = 64
FOX_WIDTH = HEADS * HEAD_DIM
MLA_NOPE = 64
MLA_ROPE = 32
MLA_V = 64
MLA_Q_LORA = 768
MLA_KV_LORA = 256
D_FF = 2816
ROPE_THETA = 10000.0
NORM_EPS = 1e-6
IN_WIDTHS = (FOX_WIDTH, FOX_WIDTH, FOX_WIDTH, HEADS, MLA_Q_LORA, MLA_KV_LORA, MLA_ROPE, D_MODEL, D_MODEL)

LANES = 128
SUBLANES = 8
BF16_ROWS = 16
TOKEN_TILE = 512
MIX_TILE = 512
ROW_PART = 256
ATTN_TILE = 512
ATTN_PAIRS = 2
FFN_CHUNK = 256
VMEM_LIMIT = 56 * 1024 * 1024

_EXP2_FOX = math.log2(math.e) / math.sqrt(HEAD_DIM)
_EXP2_MLA = math.log2(math.e) / math.sqrt(MLA_NOPE + MLA_ROPE)

F32 = jnp.float32
BF16 = jnp.bfloat16


def _const_spec(shape):
    zeros = (0,) * len(shape)
    return pl.BlockSpec(shape, lambda *_: zeros, pipeline_mode=pl.Buffered(1))


def _rms(x):
    return x * lax.rsqrt(jnp.mean(x * x, axis=-1, keepdims=True) + NORM_EPS)


def _dot(a, b):
    return jnp.dot(a, b, preferred_element_type=F32)


def _cumsum_rows(x):
    n = x.shape[0]
    row = lax.broadcasted_iota(jnp.int32, x.shape, 0)
    d = 1
    while d < n:
        x = x + jnp.where(row >= d, pltpu.roll(x, d, axis=0), 0.0)
        d *= 2
    return x


def _rope_tables(pos_row, invf_col):
    n = pos_row.shape[1]
    groups = LANES // MLA_ROPE
    ang = invf_col * pos_row
    c = jnp.cos(ang)
    s = jnp.sin(ang)
    cos_blocks, sin_blocks = [], []
    for t0 in range(0, n, LANES):
        ct = c[:, t0:t0 + LANES]
        st = s[:, t0:t0 + LANES]
        cos_blocks.append(jnp.concatenate([ct, ct] * groups, axis=0).T)
        sin_blocks.append(jnp.concatenate([-st, st] * groups, axis=0).T)
    return jnp.concatenate(cos_blocks, axis=0), jnp.concatenate(sin_blocks, axis=0)


def _rope(x, cos4, sin4, lane):
    half = MLA_ROPE // 2
    swapped = jnp.where((lane & (MLA_ROPE - 1)) < half, pltpu.roll(x, LANES - half, axis=1), pltpu.roll(x, half, axis=1))
    return x * cos4 + swapped * sin4


def _store_values_transposed(vt_ref, rows, v):
    n = v.shape[0]
    row = lax.broadcasted_iota(jnp.int32, (LANES, n), 0)
    for pair in range(HEADS // 2):
        vt = v[:, pair * LANES:(pair + 1) * LANES].T
        even = jnp.where(row < HEAD_DIM, vt, jnp.where(row == HEAD_DIM, 1.0, 0.0))
        odd = jnp.where(row >= HEAD_DIM, vt, jnp.where(row == 0, 1.0, 0.0))
        vt_ref[0, 2 * pair, :, rows] = even.astype(vt_ref.dtype)
        vt_ref[0, 2 * pair + 1, :, rows] = odd.astype(vt_ref.dtype)


def _token_proj_kernel(x_ref, mod_ref, gpre_ref, bf_ref, gq_ref, gkv_ref, pos_ref, invf_ref,
                       wqkv_ref, wmisc_ref, wcq_ref, wckv_ref, wg_ref, wq_ref, wkv_ref, sel_ref,
                       late0_ref, late1_ref, late2_ref, late3_ref, late4_ref,
                       qf_ref, kf_ref, vf_ref, qm_ref, km_ref, vm_ref, gf_ref, gm_ref,
                       cast0_ref, cast1_ref, cast2_ref, cast3_ref, cast4_ref,
                       carry_ref, cos_ref, sin_ref):
    tm = x_ref.shape[1]

    @pl.when(pl.program_id(1) == 0)
    def _():
        carry_ref[...] = jnp.zeros_like(carry_ref)

    cos_ref[...], sin_ref[...] = _rope_tables(pos_ref[0], invf_ref[:, 0:1])

    mod = lambda k: mod_ref[pl.ds(pl.program_id(0), 1), k * D_MODEL:(k + 1) * D_MODEL]
    shift = mod(0)
    scale = mod(1)

    def project(rows, n):
        h = (_rms(x_ref[0, rows, :]) * gpre_ref[...] * (1.0 + scale) + shift).astype(BF16)
        lane = lax.broadcasted_iota(jnp.int32, (n, LANES), 1)
        cos4 = cos_ref[rows, :]
        sin4 = sin_ref[rows, :]
        low = lane < HEAD_DIM

        misc = _dot(h, wmisc_ref[...])
        logit = misc + bf_ref[...]
        logf = jnp.minimum(logit, 0.0) - jnp.log(1.0 + jnp.exp(-jnp.abs(logit)))
        cum = _cumsum_rows(logf) + carry_ref[0:1, :]
        carry_ref[0:1, :] = cum[n - 1:n, :]

        cq = _dot(h, wcq_ref[...])
        ckv = _dot(h, wckv_ref[...])
        nq = (_rms(cq) * gq_ref[...]).astype(BF16)
        nkv = (_rms(ckv) * gkv_ref[...]).astype(BF16)
        p_qkv = _dot(h, wqkv_ref[...])
        _store_values_transposed(vf_ref, rows, p_qkv[:, 2 * FOX_WIDTH:])

        qq = _dot(nq, wq_ref[...]) * _EXP2_MLA
        nope_w = HEADS * MLA_NOPE
        q_rope = [_rope(qq[:, nope_w + g * LANES:nope_w + (g + 1) * LANES], cos4, sin4, lane)
                  for g in range(HEADS * MLA_ROPE // LANES)]
        per_group = LANES // MLA_ROPE
        for hh in range(HEADS):
            pair = hh // 2
            nope = qq[:, pair * LANES:(pair + 1) * LANES]
            src_lane = (hh % per_group) * MLA_ROPE
            dst_lane = HEAD_DIM if hh % 2 == 0 else 0
            rope = q_rope[hh // per_group]
            if src_lane != dst_lane:
                rope = pltpu.roll(rope, (dst_lane - src_lane) % LANES, axis=1)
            in_rope = (lane >= dst_lane) & (lane < dst_lane + MLA_ROPE)
            own = low if hh % 2 == 0 else jnp.logical_not(low)
            qm_ref[0, hh, rows, :] = jnp.where(own, nope, jnp.where(in_rope, rope, 0.0)).astype(qm_ref.dtype)
        kv = _dot(nkv, wkv_ref[...])
        _store_values_transposed(vm_ref, rows, kv[:, nope_w:])
        in_rope = (lane >= MLA_NOPE) & (lane < MLA_NOPE + MLA_ROPE)
        kpe_even = jnp.where(in_rope, _rope(misc, cos4, sin4, lane), 0.0)
        kpe_odd = pltpu.roll(kpe_even, LANES - HEAD_DIM, axis=1)
        for hh in range(HEADS):
            pair = hh // 2
            nope = kv[:, pair * LANES:(pair + 1) * LANES]
            k_full = jnp.where(low, nope, kpe_even) if hh % 2 == 0 else jnp.where(low, kpe_odd, nope)
            km_ref[0, hh, rows, :] = k_full.astype(km_ref.dtype)

        a = cum * math.log2(math.e)
        a_hi = a.astype(BF16).astype(F32)
        r1 = a - a_hi
        a_mid = r1.astype(BF16).astype(F32)
        a_lo = (r1 - a_mid).astype(BF16).astype(F32)
        z = jnp.where(lane < HEADS, a_hi,
                      jnp.where(lane < 2 * HEADS, a_mid,
                                jnp.where(lane < 3 * HEADS, a_lo,
                                          jnp.where(lane == 3 * HEADS, 1.0, 0.0))))
        aug = _dot(z.astype(BF16), sel_ref[...])
        pairs = HEADS // 2
        for hh in range(HEADS):
            pair = hh // 2
            keep = (lane < HEAD_DIM) if hh % 2 == 0 else (lane >= HEAD_DIM)
            xq = p_qkv[:, pair * LANES:(pair + 1) * LANES]
            xk = p_qkv[:, FOX_WIDTH + pair * LANES:FOX_WIDTH + (pair + 1) * LANES]
            aq = aug[:, pair * LANES:(pair + 1) * LANES]
            ak = aug[:, (pairs + pair) * LANES:(pairs + pair + 1) * LANES]
            qf_ref[0, hh, rows, :] = jnp.where(keep, xq * _EXP2_FOX, aq).astype(qf_ref.dtype)
            kf_ref[0, hh, rows, :] = jnp.where(keep, xk, ak).astype(kf_ref.dtype)

        for gi, ref in enumerate((gf_ref, gm_ref)):
            g = _dot(h, wg_ref[:, gi * D_MODEL:(gi + 1) * D_MODEL])
            ref[0, rows, :] = jax.nn.sigmoid(g).astype(ref.dtype)

    n = tm // 2
    for r0 in (0, n):
        project(pl.ds(r0, n), n)

    for src, dst in ((late0_ref, cast0_ref), (late1_ref, cast1_ref), (late2_ref, cast2_ref),
                     (late3_ref, cast3_ref), (late4_ref, cast4_ref)):
        dst[...] = src[0].astype(dst.dtype)


_IN_OFFSETS = tuple(int(v) for v in np.cumsum((0,) + IN_WIDTHS))


def _prep_kernel(c_ref, wada_ref, bada_ref, wt_ref, mod_ref, qkv_ref, misc_ref, cq_ref, ckv_ref, g_ref):
    c = c_ref[...]
    sc = c * jax.nn.sigmoid(c)
    mod_ref[...] = _dot(sc.astype(BF16), wada_ref[0].astype(BF16)) + bada_ref[0]

    wt = wt_ref[0]
    cols = wt.shape[1]
    o = _IN_OFFSETS
    qkv_ref[...] = wt[o[0]:o[3], :].T.astype(BF16)
    cq_ref[...] = wt[o[4]:o[5], :].T.astype(BF16)
    ckv_ref[...] = wt[o[5]:o[6], :].T.astype(BF16)
    g_ref[...] = wt[o[7]:o[9], :].T.astype(BF16)
    f = wt[o[3]:o[4], :]
    kr = wt[o[6]:o[7], :]
    misc_t = jnp.concatenate([f, f, f, jnp.zeros((MLA_NOPE - 3 * HEADS, cols), F32), kr,
                              jnp.zeros((LANES - MLA_NOPE - MLA_ROPE, cols), F32)], axis=0)
    misc_ref[...] = misc_t.T.astype(BF16)


def _prep(c, w_ada_all, b_ada_all, w_in_all, layer):
    bsz, d = c.shape
    depth, _, n_mod = w_ada_all.shape
    n_in = w_in_all.shape[2]
    cols = 2 * LANES
    steps = d // cols
    tn = n_mod // steps
    assert d % cols == 0 and n_mod % steps == 0 and tn % LANES == 0
    widths = (3 * FOX_WIDTH, LANES, MLA_Q_LORA, MLA_KV_LORA, 2 * D_MODEL)
    return pl.pallas_call(
        _prep_kernel,
        grid=(steps,),
        in_specs=[pl.BlockSpec((bsz, d), lambda j: (0, 0)),
                  pl.BlockSpec((1, d, tn), lambda j: (layer, 0, j)),
                  pl.BlockSpec((1, 1, tn), lambda j: (layer, 0, j)),
                  pl.BlockSpec((1, n_in, cols), lambda j: (layer, 0, j))],
        out_specs=[pl.BlockSpec((bsz, tn), lambda j: (0, j))]
                  + [pl.BlockSpec((cols, w), lambda j: (j, 0)) for w in widths],
        out_shape=[jax.ShapeDtypeStruct((bsz, n_mod), F32)]
                  + [jax.ShapeDtypeStruct((d, w), BF16) for w in widths],
        name="prep",
    )(c, w_ada_all, b_ada_all.reshape(depth, 1, n_mod), jnp.transpose(w_in_all, (0, 2, 1)))


def _prep_proj_weights(w_in_parts, w_uq, w_ukv, b_forget):
    w_qkv, w_misc, w_cq, w_ckv, w_g = w_in_parts

    r = w_uq.shape[0]
    uq = w_uq.reshape(r, HEADS, MLA_NOPE + MLA_ROPE)
    w_q = jnp.concatenate([uq[:, :, :MLA_NOPE].reshape(r, HEADS * MLA_NOPE),
                           uq[:, :, MLA_NOPE:].reshape(r, HEADS * MLA_ROPE)], axis=1).astype(BF16)

    rk = w_ukv.shape[0]
    ukv = w_ukv.reshape(rk, HEADS, MLA_NOPE + MLA_V)
    w_kv = jnp.concatenate([ukv[:, :, :MLA_NOPE].reshape(rk, HEADS * MLA_NOPE),
                            ukv[:, :, MLA_NOPE:].reshape(rk, HEADS * MLA_V)], axis=1).astype(BF16)

    bf3 = jnp.concatenate([b_forget, b_forget, b_forget,
                           jnp.zeros((LANES - 3 * HEADS,), b_forget.dtype)]).reshape(1, LANES)
    return (w_qkv, w_misc, w_cq, w_ckv, w_g, w_q, w_kv), bf3


def _decay_selector():
    pairs = HEADS // 2
    sel = np.zeros((LANES, 2 * pairs * LANES), np.float32)
    for hh in range(HEADS):
        base_q = (hh // 2) * LANES + (HEAD_DIM if hh % 2 == 0 else 0)
        base_k = (pairs + hh // 2) * LANES + (HEAD_DIM if hh % 2 == 0 else 0)
        for piece in range(3):
            sel[piece * HEADS + hh, base_q + piece] = 1.0
            sel[3 * HEADS, base_q + 3 + piece] = 1.0
            sel[3 * HEADS, base_k + piece] = 1.0
            sel[piece * HEADS + hh, base_k + 3 + piece] = -1.0
    return jnp.asarray(sel, BF16)


def _cast_block_rows(n_rows, n_steps):
    rb = -(-n_rows // n_steps)
    rb += -rb % BF16_ROWS
    while n_rows % rb:
        rb += BF16_ROWS
    return rb


def _token_proj(x, mod, g_pre, positions, weights, bf3, g_q, g_kv, late_weights, layer):
    bsz, seq, d = x.shape
    tm = TOKEN_TILE
    steps_per_batch = seq // tm
    n_steps = bsz * steps_per_batch
    late_in, late_out, late_shapes = [], [], []
    for w in late_weights:
        _, rows, cols = w.shape
        rb = _cast_block_rows(rows, n_steps)
        last = rows // rb - 1
        late_in.append(pl.BlockSpec(
            (1, rb, cols), lambda b, i, last=last: (layer, jnp.minimum(b * steps_per_batch + i, last), 0)))
        late_out.append(pl.BlockSpec(
            (rb, cols), lambda b, i, last=last: (jnp.minimum(b * steps_per_batch + i, last), 0)))
        late_shapes.append(jax.ShapeDtypeStruct((rows, cols), BF16))
    sel = _decay_selector()
    half = MLA_ROPE // 2
    inv_freq = 1.0 / (ROPE_THETA ** (np.arange(0, MLA_ROPE, 2, dtype=np.float32) / MLA_ROPE))
    invf = jnp.asarray(np.tile(inv_freq.astype(np.float32)[:, None], (1, LANES)))
    pos = positions.astype(F32).reshape(bsz, 1, seq)
    row = lambda b, i: (b, i, 0)
    head = lambda b, i: (b, 0, i, 0)
    head_shape = jax.ShapeDtypeStruct((bsz, HEADS, seq, LANES), BF16)
    head_spec = pl.BlockSpec((1, HEADS, tm, LANES), head)
    v_shape = jax.ShapeDtypeStruct((bsz, HEADS, LANES, seq), BF16)
    v_spec = pl.BlockSpec((1, HEADS, LANES, tm), lambda b, i: (b, 0, 0, i))
    g_shape = jax.ShapeDtypeStruct((bsz, seq, d), BF16)
    g_spec = pl.BlockSpec((1, tm, d), row)
    return pl.pallas_call(
        _token_proj_kernel,
        grid=(bsz, seq // tm),
        in_specs=[pl.BlockSpec((1, tm, d), row),
                  _const_spec(mod.shape),
                  _const_spec((1, d)), _const_spec((1, LANES)),
                  _const_spec((1, MLA_Q_LORA)), _const_spec((1, MLA_KV_LORA)),
                  pl.BlockSpec((1, 1, tm), lambda b, i: (b, 0, i)), _const_spec((half, LANES))]
                 + [_const_spec(w.shape) for w in weights] + [_const_spec(sel.shape)] + late_in,
        out_specs=[head_spec, head_spec, v_spec, head_spec, head_spec, v_spec, g_spec, g_spec] + late_out,
        out_shape=[head_shape, head_shape, v_shape, head_shape, head_shape, v_shape, g_shape, g_shape]
                  + late_shapes,
        scratch_shapes=[pltpu.VMEM((SUBLANES, LANES), F32),
                        pltpu.VMEM((tm, LANES), F32),
                        pltpu.VMEM((tm, LANES), F32)],
        compiler_params=pltpu.CompilerParams(dimension_semantics=("arbitrary", "arbitrary"),
                                             vmem_limit_bytes=VMEM_LIMIT),
        name="token_proj",
    )(x, mod, g_pre.reshape(1, d), bf3, g_q.reshape(1, -1), g_kv.reshape(1, -1), pos, invf,
      *weights, sel, *late_weights)


def _attn_kernel(zero_ref, q_ref, k_ref, vt_ref, o_ref, sa_ref, sb_ref, m_ref, acc_ref, *, tile):
    seq = vt_ref.shape[3]
    nq = seq // tile
    n_heads = q_ref.shape[1]
    z = zero_ref[0]
    half = tile // 2
    nt = (((1,), (1,)), ((), ()))
    vrow = lax.broadcasted_iota(jnp.int32, (LANES, tile), 0)
    sum_row = (HEAD_DIM, 0)
    keep_even = vrow < HEAD_DIM
    krow = lax.broadcasted_iota(jnp.int32, (LANES, LANES), 0)
    qcol = lax.broadcasted_iota(jnp.int32, (LANES, LANES), 1)
    diag_ok = krow <= qcol

    def issue_scores(qi, j, s_out, heads=None):
        for hh in (range(n_heads) if heads is None else heads):
            q = q_ref[0, hh, qi * tile:(qi + 1) * tile, :]
            k = k_ref[0, hh, j * tile:(j + 1) * tile, :]
            if j == qi:
                s_out[z + hh, :half, :] = lax.dot_general(k[:half], q, nt, preferred_element_type=F32)
                s_out[z + hh, half:, half:] = lax.dot_general(k[half:], q[half:], nt, preferred_element_type=F32)
            else:
                s_out[z + hh] = lax.dot_general(k, q, nt, preferred_element_type=F32)

    def softmax_pv(s_in, hh, vth, masked):
        alphas, p_cols = [], []
        for c0 in range(0, tile, LANES):
            n_keys = c0 + LANES if masked else tile
            n_fill = tile if (c0 >= half or not masked) else half
            m_old = m_ref[hh, :, c0:c0 + LANES]

            def strip(r0):
                sb = s_in[z + hh, r0:r0 + LANES, c0:c0 + LANES]
                return jnp.where(diag_ok, sb, -jnp.inf) if (masked and r0 == c0) else sb

            mx = functools.reduce(jnp.maximum, [strip(r0) for r0 in range(0, n_keys, LANES)])
            m_new = jnp.maximum(m_old, jnp.max(mx, axis=0, keepdims=True))
            parts = [jnp.exp2(strip(r0) - m_new).astype(BF16) for r0 in range(0, n_keys, LANES)]
            parts += [jnp.zeros((LANES, LANES), BF16)] * ((n_fill - n_keys) // LANES)
            p_cols.append(jnp.concatenate(parts, axis=0))
            m_ref[hh, :, c0:c0 + LANES] = m_new
            alphas.append(jnp.exp2(m_old - m_new))
        alpha = jnp.concatenate(alphas, axis=1)
        n_left = half // LANES
        if masked:
            pv = jnp.concatenate([_dot(vth[:, :half], jnp.concatenate(p_cols[:n_left], axis=1)),
                                  _dot(vth, jnp.concatenate(p_cols[n_left:], axis=1))], axis=1)
        else:
            pv = _dot(vth, jnp.concatenate(p_cols, axis=1))
        acc_ref[z + hh] = alpha * acc_ref[z + hh] + pv

    def reset_state(heads):
        for hh in heads:
            m_ref[hh] = jnp.full(m_ref.shape[1:], -jnp.inf, F32)
            acc_ref[hh] = jnp.zeros(acc_ref.shape[1:], F32)

    blocks = [(qi, j) for qi in range(nq) for j in range(qi + 1)]
    bufs = (sa_ref, sb_ref)
    for pair in range(n_heads // 2):
        heads = (2 * pair, 2 * pair + 1)
        reset_state(heads)
        issue_scores(0, 0, bufs[0], heads)
        for t, (qi, j) in enumerate(blocks):
            s_in, s_out = bufs[t % 2], bufs[(t + 1) % 2]
            for hh in heads:
                if t + 1 < len(blocks):
                    issue_scores(*blocks[t + 1], s_out, heads=(hh,))
                softmax_pv(s_in, hh, vt_ref[0, hh, :, j * tile:(j + 1) * tile], j == qi)
            if j == qi:
                outs = []
                for hh in heads:
                    acc = acc_ref[z + hh]
                    outs.append(acc * (1.0 / acc[sum_row[hh % 2]:sum_row[hh % 2] + 1, :]))
                o_t = jnp.where(keep_even, outs[0], outs[1])
                o_ref[0, qi * tile:(qi + 1) * tile, pair * LANES:(pair + 1) * LANES] = o_t.T.astype(o_ref.dtype)
                if qi + 1 < nq:
                    reset_state(heads)


def _causal_attn(q, k, vt):
    bsz, heads, seq, _ = q.shape
    g = ATTN_PAIRS
    qk_spec = pl.BlockSpec((1, 2 * g, seq, LANES), lambda b, p: (b, p, 0, 0))
    vt_spec = pl.BlockSpec((1, 2 * g, LANES, seq), lambda b, p: (b, p, 0, 0))
    o_spec = pl.BlockSpec((1, seq, g * LANES), lambda b, p: (b, 0, p))
    kern = functools.partial(_attn_kernel, tile=ATTN_TILE)
    return pl.pallas_call(
        kern,
        grid=(bsz, heads // (2 * g)),
        in_specs=[pl.BlockSpec(memory_space=pltpu.SMEM), qk_spec, qk_spec, vt_spec],
        out_specs=o_spec,
        out_shape=jax.ShapeDtypeStruct((bsz, seq, heads * HEAD_DIM), BF16),
        scratch_shapes=[pltpu.VMEM((2 * g, ATTN_TILE, ATTN_TILE), F32),
                        pltpu.VMEM((2 * g, ATTN_TILE, ATTN_TILE), F32),
                        pltpu.VMEM((2 * g, 1, ATTN_TILE), F32),
                        pltpu.VMEM((2 * g, LANES, ATTN_TILE), F32)],
        compiler_params=pltpu.CompilerParams(dimension_semantics=("arbitrary", "arbitrary"),
                                             vmem_limit_bytes=VMEM_LIMIT),
        name="causal_attn",
    )(jnp.zeros((1,), jnp.int32), q, k, vt)


def _mix_ffn_kernel(x_ref, oa_ref, ob_ref, gf_ref, gm_ref, mod_ref, gpost_mix_ref, gpre_ffn_ref,
                    gpost_ffn_ref, wpf_ref, wpm_ref, wout_ref, win_ref, wdown_ref, o_ref):
    tm = x_ref.shape[1]
    mod = lambda k: mod_ref[pl.ds(pl.program_id(0), 1), k * D_MODEL:(k + 1) * D_MODEL]
    gate_mix = mod(2)
    shift = mod(3)
    scale = mod(4)
    gate = mod(5)

    def mix_matmuls(rows):
        pa = _dot(oa_ref[0, rows, :], wpf_ref[...])
        pb = _dot(ob_ref[0, rows, :], wpm_ref[...])
        merged = gf_ref[0, rows, :].astype(F32) * pa + gm_ref[0, rows, :].astype(F32) * pb
        return _dot(merged.astype(BF16), wout_ref[...])

    def mix_residual(rows, y):
        x = x_ref[0, rows, :] + gate_mix * (_rms(y) * gpost_mix_ref[...])
        h = (_rms(x) * gpre_ffn_ref[...] * (1.0 + scale) + shift).astype(BF16)
        return x, h

    chunks = [(lo, min(FFN_CHUNK, D_FF - lo)) for lo in range(0, D_FF, FFN_CHUNK)]

    def up_matmuls(h, lo, w):
        return _dot(h, win_ref[:, lo:lo + w]), _dot(h, win_ref[:, D_FF + lo:D_FF + lo + w])

    def swiglu_matmuls(h):
        y = None
        gu = up_matmuls(h, *chunks[0])
        for c, (lo, w) in enumerate(chunks):
            g, u = gu
            if c + 1 < len(chunks):
                gu = up_matmuls(h, *chunks[c + 1])
            act = (g * jax.nn.sigmoid(g) * u).astype(BF16)
            part = _dot(act, wdown_ref[lo:lo + w, :])
            y = part if y is None else y + part
        return y

    parts = [pl.ds(r0, ROW_PART) for r0 in range(0, tm, ROW_PART)]
    y_mix, xs, y_ffn = {}, {}, {}
    for t in range(len(parts) + 2):
        if t < len(parts):
            y_mix[t] = mix_matmuls(parts[t])
        if 0 <= t - 1 < len(parts):
            xs[t - 1], h = mix_residual(parts[t - 1], y_mix.pop(t - 1))
            y_ffn[t - 1] = swiglu_matmuls(h)
        if 0 <= t - 2 < len(parts):
            o_ref[0, parts[t - 2], :] = xs.pop(t - 2) + gate * (_rms(y_ffn.pop(t - 2)) * gpost_ffn_ref[...])


def _mix_ffn(x, o_a, o_b, gf, gm, mod, g_post_mix, g_pre_ffn, g_post_ffn, w_pf, w_pm, w_out, w_in, w_down):
    bsz, seq, d = x.shape
    tm = MIX_TILE
    row = lambda b, i: (b, i, 0)
    weights = (w_pf, w_pm, w_out, w_in, w_down)
    return pl.pallas_call(
        _mix_ffn_kernel,
        grid=(bsz, seq // tm),
        in_specs=[pl.BlockSpec((1, tm, d), row),
                  pl.BlockSpec((1, tm, FOX_WIDTH), row), pl.BlockSpec((1, tm, FOX_WIDTH), row),
                  pl.BlockSpec((1, tm, d), row), pl.BlockSpec((1, tm, d), row),
                  _const_spec(mod.shape),
                  _const_spec((1, d)), _const_spec((1, d)), _const_spec((1, d))]
                 + [_const_spec(w.shape) for w in weights],
        out_specs=pl.BlockSpec((1, tm, d), row),
        out_shape=jax.ShapeDtypeStruct(x.shape, F32),
        compiler_params=pltpu.CompilerParams(dimension_semantics=("arbitrary", "arbitrary"),
                                             vmem_limit_bytes=VMEM_LIMIT),
        name="mix_ffn",
    )(x, o_a, o_b, gf, gm, mod, g_post_mix.reshape(1, d), g_pre_ffn.reshape(1, d), g_post_ffn.reshape(1, d),
      *weights)


def kernel(x, c, positions, w_ada, b_ada, g_pre_mix, g_post_mix, g_pre_ffn, g_post_ffn, w_in, b_forget,
           g_q_lora, w_uq, g_kv_lora, w_ukv, w_proj_fox, w_proj_mla, w_out, w_ffn_in, w_ffn_out):
    bsz, seq, d = x.shape
    depth = w_ada.shape[0]
    for l in range(depth):
        mod, *w_in_parts = _prep(c, w_ada, b_ada, w_in, l)
        weights, bf3 = _prep_proj_weights(w_in_parts, w_uq[l], w_ukv[l], b_forget[l])
        late = (w_proj_fox, w_proj_mla, w_out, w_ffn_in, w_ffn_out)
        qf, kf, vf, qm, km, vm, gf, gm, *late_bf16 = _token_proj(
            x, mod, g_pre_mix[l], positions, weights, bf3, g_q_lora[l], g_kv_lora[l], late, l)
        o_a = _causal_attn(qf, kf, vf)
        o_b = _causal_attn(qm, km, vm)
        x = _mix_ffn(x, o_a, o_b, gf, gm, mod, g_post_mix[l], g_pre_ffn[l], g_post_ffn[l], *late_bf16)
    return x
```

```python
import functools
import math

import jax
import jax.numpy as jnp
import numpy as np
from jax import lax
from jax.experimental import pallas as pl
from jax.experimental.pallas import tpu as pltpu

D_MODEL = 1024
HEADS = 8
HEAD_DIM = 64
FOX_WIDTH = HEADS * HEAD_DIM
MLA_NOPE = 64
MLA_ROPE = 32
MLA_V = 64
MLA_Q_LORA = 768
MLA_KV_LORA = 256
D_FF = 2816
ROPE_THETA = 10000.0
NORM_EPS = 1e-6
IN_WIDTHS = (FOX_WIDTH, FOX_WIDTH, FOX_WIDTH, HEADS, MLA_Q_LORA, MLA_KV_LORA, MLA_ROPE, D_MODEL, D_MODEL)

LANES = 128
SUBLANES = 8
BF16_ROWS = 16
TOKEN_TILE = 512
MIX_TILE = 512
ROW_PART = 256
ATTN_TILE = 512
ATTN_PAIRS = 2
FFN_CHUNK = 256
VMEM_LIMIT = 56 * 1024 * 1024

_EXP2_FOX = math.log2(math.e) / math.sqrt(HEAD_DIM)
_EXP2_MLA = math.log2(math.e) / math.sqrt(MLA_NOPE + MLA_ROPE)

F32 = jnp.float32
BF16 = jnp.bfloat16


def _const_spec(shape):
    zeros = (0,) * len(shape)
    return pl.BlockSpec(shape, lambda *_: zeros, pipeline_mode=pl.Buffered(1))


def _rms(x):
    return x * lax.rsqrt(jnp.mean(x * x, axis=-1, keepdims=True) + NORM_EPS)


def _dot(a, b):
    return jnp.dot(a, b, preferred_element_type=F32)


def _cumsum_rows(x):
    n = x.shape[0]
    row = lax.broadcasted_iota(jnp.int32, x.shape, 0)
    d = 1
    while d < n:
        x = x + jnp.where(row >= d, pltpu.roll(x, d, axis=0), 0.0)
        d *= 2
    return x


def _rope_tables(pos_row, invf_col):
    n = pos_row.shape[1]
    groups = LANES // MLA_ROPE
    ang = invf_col * pos_row
    c = jnp.cos(ang)
    s = jnp.sin(ang)
    cos_blocks, sin_blocks = [], []
    for t0 in range(0, n, LANES):
        ct = c[:, t0:t0 + LANES]
        st = s[:, t0:t0 + LANES]
        cos_blocks.append(jnp.concatenate([ct, ct] * groups, axis=0).T)
        sin_blocks.append(jnp.concatenate([-st, st] * groups, axis=0).T)
    return jnp.concatenate(cos_blocks, axis=0), jnp.concatenate(sin_blocks, axis=0)


def _rope(x, cos4, sin4, lane):
    half = MLA_ROPE // 2
    swapped = jnp.where((lane & (MLA_ROPE - 1)) < half, pltpu.roll(x, LANES - half, axis=1), pltpu.roll(x, half, axis=1))
    return x * cos4 + swapped * sin4


def _store_values_transposed(vt_ref, rows, v):
    n = v.shape[0]
    row = lax.broadcasted_iota(jnp.int32, (LANES, n), 0)
    for pair in range(HEADS // 2):
        vt = v[:, pair * LANES:(pair + 1) * LANES].T
        even = jnp.where(row < HEAD_DIM, vt, jnp.where(row == HEAD_DIM, 1.0, 0.0))
        odd = jnp.where(row >= HEAD_DIM, vt, jnp.where(row == 0, 1.0, 0.0))
        vt_ref[0, 2 * pair, :, rows] = even.astype(vt_ref.dtype)
        vt_ref[0, 2 * pair + 1, :, rows] = odd.astype(vt_ref.dtype)


def _token_proj_kernel(x_ref, mod_ref, gpre_ref, bf_ref, gq_ref, gkv_ref, pos_ref, invf_ref,
                       wqkv_ref, wmisc_ref, wcq_ref, wckv_ref, wg_ref, wq_ref, wkv_ref, sel_ref,
                       late0_ref, late1_ref, late2_ref, late3_ref, late4_ref,
                       qf_ref, kf_ref, vf_ref, qm_ref, km_ref, vm_ref, gf_ref, gm_ref,
                       cast0_ref, cast1_ref, cast2_ref, cast3_ref, cast4_ref,
                       carry_ref, cos_ref, sin_ref):
    tm = x_ref.shape[1]

    @pl.when(pl.program_id(1) == 0)
    def _():
        carry_ref[...] = jnp.zeros_like(carry_ref)

    cos_ref[...], sin_ref[...] = _rope_tables(pos_ref[0], invf_ref[:, 0:1])

    mod = lambda k: mod_ref[pl.ds(pl.program_id(0), 1), k * D_MODEL:(k + 1) * D_MODEL]
    shift = mod(0)
    scale = mod(1)

    def project(rows, n):
        h = (_rms(x_ref[0, rows, :]) * gpre_ref[...] * (1.0 + scale) + shift).astype(BF16)
        lane = lax.broadcasted_iota(jnp.int32, (n, LANES), 1)
        cos4 = cos_ref[rows, :]
        sin4 = sin_ref[rows, :]
        low = lane < HEAD_DIM

        misc = _dot(h, wmisc_ref[...])
        logit = misc + bf_ref[...]
        logf = jnp.minimum(logit, 0.0) - jnp.log(1.0 + jnp.exp(-jnp.abs(logit)))
        cum = _cumsum_rows(logf) + carry_ref[0:1, :]
        carry_ref[0:1, :] = cum[n - 1:n, :]

        cq = _dot(h, wcq_ref[...])
        ckv = _dot(h, wckv_ref[...])
        nq = (_rms(cq) * gq_ref[...]).astype(BF16)
        nkv = (_rms(ckv) * gkv_ref[...]).astype(BF16)
        p_qkv = _dot(h, wqkv_ref[...])
        _store_values_transposed(vf_ref, rows, p_qkv[:, 2 * FOX_WIDTH:])

        qq = _dot(nq, wq_ref[...]) * _EXP2_MLA
        nope_w = HEADS * MLA_NOPE
        q_rope = [_rope(qq[:, nope_w + g * LANES:nope_w + (g + 1) * LANES], cos4, sin4, lane)
                  for g in range(HEADS * MLA_ROPE // LANES)]
        per_group = LANES // MLA_ROPE
        for hh in range(HEADS):
            pair = hh // 2
            nope = qq[:, pair * LANES:(pair + 1) * LANES]
            src_lane = (hh % per_group) * MLA_ROPE
            dst_lane = HEAD_DIM if hh % 2 == 0 else 0
            rope = q_rope[hh // per_group]
            if src_lane != dst_lane:
                rope = pltpu.roll(rope, (dst_lane - src_lane) % LANES, axis=1)
            in_rope = (lane >= dst_lane) & (lane < dst_lane + MLA_ROPE)
            own = low if hh % 2 == 0 else jnp.logical_not(low)
            qm_ref[0, hh, rows, :] = jnp.where(own, nope, jnp.where(in_rope, rope, 0.0)).astype(qm_ref.dtype)
        kv = _dot(nkv, wkv_ref[...])
        _store_values_transposed(vm_ref, rows, kv[:, nope_w:])
        in_rope = (lane >= MLA_NOPE) & (lane < MLA_NOPE + MLA_ROPE)
        kpe_even = jnp.where(in_rope, _rope(misc, cos4, sin4, lane), 0.0)
        kpe_odd = pltpu.roll(kpe_even, LANES - HEAD_DIM, axis=1)
        for hh in range(HEADS):
            pair = hh // 2
            nope = kv[:, pair * LANES:(pair + 1) * LANES]
            k_full = jnp.where(low, nope, kpe_even) if hh % 2 == 0 else jnp.where(low, kpe_odd, nope)
            km_ref[0, hh, rows, :] = k_full.astype(km_ref.dtype)

        a = cum * math.log2(math.e)
        a_hi = a.astype(BF16).astype(F32)
        r1 = a - a_hi
        a_mid = r1.astype(BF16).astype(F32)
        a_lo = (r1 - a_mid).astype(BF16).astype(F32)
        z = jnp.where(lane < HEADS, a_hi,
                      jnp.where(lane < 2 * HEADS, a_mid,
                                jnp.where(lane < 3 * HEADS, a_lo,
                                          jnp.where(lane == 3 * HEADS, 1.0, 0.0))))
        aug = _dot(z.astype(BF16), sel_ref[...])
        pairs = HEADS // 2
        for hh in range(HEADS):
            pair = hh // 2
            keep = (lane < HEAD_DIM) if hh % 2 == 0 else (lane >= HEAD_DIM)
            xq = p_qkv[:, pair * LANES:(pair + 1) * LANES]
            xk = p_qkv[:, FOX_WIDTH + pair * LANES:FOX_WIDTH + (pair + 1) * LANES]
            aq = aug[:, pair * LANES:(pair + 1) * LANES]
            ak = aug[:, (pairs + pair) * LANES:(pairs + pair + 1) * LANES]
            qf_ref[0, hh, rows, :] = jnp.where(keep, xq * _EXP2_FOX, aq).astype(qf_ref.dtype)
            kf_ref[0, hh, rows, :] = jnp.where(keep, xk, ak).astype(kf_ref.dtype)

        for gi, ref in enumerate((gf_ref, gm_ref)):
            g = _dot(h, wg_ref[:, gi * D_MODEL:(gi + 1) * D_MODEL])
            ref[0, rows, :] = jax.nn.sigmoid(g).astype(ref.dtype)

    n = tm // 2
    for r0 in (0, n):
        project(pl.ds(r0, n), n)

    for src, dst in ((late0_ref, cast0_ref), (late1_ref, cast1_ref), (late2_ref, cast2_ref),
                     (late3_ref, cast3_ref), (late4_ref, cast4_ref)):
        dst[...] = src[0].astype(dst.dtype)


_IN_OFFSETS = tuple(int(v) for v in np.cumsum((0,) + IN_WIDTHS))


def _prep_kernel(c_ref, wada_ref, bada_ref, wt_ref, mod_ref, qkv_ref, misc_ref, cq_ref, ckv_ref, g_ref):
    c = c_ref[...]
    sc = c * jax.nn.sigmoid(c)
    mod_ref[...] = _dot(sc.astype(BF16), wada_ref[0].astype(BF16)) + bada_ref[0]

    wt = wt_ref[0]
    cols = wt.shape[1]
    o = _IN_OFFSETS
    qkv_ref[...] = wt[o[0]:o[3], :].T.astype(BF16)
    cq_ref[...] = wt[o[4]:o[5], :].T.astype(BF16)
    ckv_ref[...] = wt[o[5]:o[6], :].T.astype(BF16)
    g_ref[...] = wt[o[7]:o[9], :].T.astype(BF16)
    f = wt[o[3]:o[4], :]
    kr = wt[o[6]:o[7], :]
    misc_t = jnp.concatenate([f, f, f, jnp.zeros((MLA_NOPE - 3 * HEADS, cols), F32), kr,
                              jnp.zeros((LANES - MLA_NOPE - MLA_ROPE, cols), F32)], axis=0)
    misc_ref[...] = misc_t.T.astype(BF16)


def _prep(c, w_ada_all, b_ada_all, w_in_all, layer):
    bsz, d = c.shape
    depth, _, n_mod = w_ada_all.shape
    n_in = w_in_all.shape[2]
    cols = 2 * LANES
    steps = d // cols
    tn = n_mod // steps
    assert d % cols == 0 and n_mod % steps == 0 and tn % LANES == 0
    widths = (3 * FOX_WIDTH, LANES, MLA_Q_LORA, MLA_KV_LORA, 2 * D_MODEL)
    return pl.pallas_call(
        _prep_kernel,
        grid=(steps,),
        in_specs=[pl.BlockSpec((bsz, d), lambda j: (0, 0)),
                  pl.BlockSpec((1, d, tn), lambda j: (layer, 0, j)),
                  pl.BlockSpec((1, 1, tn), lambda j: (layer, 0, j)),
                  pl.BlockSpec((1, n_in, cols), lambda j: (layer, 0, j))],
        out_specs=[pl.BlockSpec((bsz, tn), lambda j: (0, j))]
                  + [pl.BlockSpec((cols, w), lambda j: (j, 0)) for w in widths],
        out_shape=[jax.ShapeDtypeStruct((bsz, n_mod), F32)]
                  + [jax.ShapeDtypeStruct((d, w), BF16) for w in widths],
        name="prep",
    )(c, w_ada_all, b_ada_all.reshape(depth, 1, n_mod), jnp.transpose(w_in_all, (0, 2, 1)))


def _prep_proj_weights(w_in_parts, w_uq, w_ukv, b_forget):
    w_qkv, w_misc, w_cq, w_ckv, w_g = w_in_parts

    r = w_uq.shape[0]
    uq = w_uq.reshape(r, HEADS, MLA_NOPE + MLA_ROPE)
    w_q = jnp.concatenate([uq[:, :, :MLA_NOPE].reshape(r, HEADS * MLA_NOPE),
                           uq[:, :, MLA_NOPE:].reshape(r, HEADS * MLA_ROPE)], axis=1).astype(BF16)

    rk = w_ukv.shape[0]
    ukv = w_ukv.reshape(rk, HEADS, MLA_NOPE + MLA_V)
    w_kv = jnp.concatenate([ukv[:, :, :MLA_NOPE].reshape(rk, HEADS * MLA_NOPE),
                            ukv[:, :, MLA_NOPE:].reshape(rk, HEADS * MLA_V)], axis=1).astype(BF16)

    bf3 = jnp.concatenate([b_forget, b_forget, b_forget,
                           jnp.zeros((LANES - 3 * HEADS,), b_forget.dtype)]).reshape(1, LANES)
    return (w_qkv, w_misc, w_cq, w_ckv, w_g, w_q, w_kv), bf3


def _decay_selector():
    pairs = HEADS // 2
    sel = np.zeros((LANES, 2 * pairs * LANES), np.float32)
    for hh in range(HEADS):
        base_q = (hh // 2) * LANES + (HEAD_DIM if hh % 2 == 0 else 0)
        base_k = (pairs + hh // 2) * LANES + (HEAD_DIM if hh % 2 == 0 else 0)
        for piece in range(3):
            sel[piece * HEADS + hh, base_q + piece] = 1.0
            sel[3 * HEADS, base_q + 3 + piece] = 1.0
            sel[3 * HEADS, base_k + piece] = 1.0
            sel[piece * HEADS + hh, base_k + 3 + piece] = -1.0
    return jnp.asarray(sel, BF16)


def _cast_block_rows(n_rows, n_steps):
    rb = -(-n_rows // n_steps)
    rb += -rb % BF16_ROWS
    while n_rows % rb:
        rb += BF16_ROWS
    return rb


def _token_proj(x, mod, g_pre, positions, weights, bf3, g_q, g_kv, late_weights, layer):
    bsz, seq, d = x.shape
    tm = TOKEN_TILE
    steps_per_batch = seq // tm
    n_steps = bsz * steps_per_batch
    late_in, late_out, late_shapes = [], [], []
    for w in late_weights:
        _, rows, cols = w.shape
        rb = _cast_block_rows(rows, n_steps)
        last = rows // rb - 1
        late_in.append(pl.BlockSpec(
            (1, rb, cols), lambda b, i, last=last: (layer, jnp.minimum(b * steps_per_batch + i, last), 0)))
        late_out.append(pl.BlockSpec(
            (rb, cols), lambda b, i, last=last: (jnp.minimum(b * steps_per_batch + i, last), 0)))
        late_shapes.append(jax.ShapeDtypeStruct((rows, cols), BF16))
    sel = _decay_selector()
    half = MLA_ROPE // 2
    inv_freq = 1.0 / (ROPE_THETA ** (np.arange(0, MLA_ROPE, 2, dtype=np.float32) / MLA_ROPE))
    invf = jnp.asarray(np.tile(inv_freq.astype(np.float32)[:, None], (1, LANES)))
    pos = positions.astype(F32).reshape(bsz, 1, seq)
    row = lambda b, i: (b, i, 0)
    head = lambda b, i: (b, 0, i, 0)
    head_shape = jax.ShapeDtypeStruct((bsz, HEADS, seq, LANES), BF16)
    head_spec = pl.BlockSpec((1, HEADS, tm, LANES), head)
    v_shape = jax.ShapeDtypeStruct((bsz, HEADS, LANES, seq), BF16)
    v_spec = pl.BlockSpec((1, HEADS, LANES, tm), lambda b, i: (b, 0, 0, i))
    g_shape = jax.ShapeDtypeStruct((bsz, seq, d), BF16)
    g_spec = pl.BlockSpec((1, tm, d), row)
    return pl.pallas_call(
        _token_proj_kernel,
        grid=(bsz, seq // tm),
        in_specs=[pl.BlockSpec((1, tm, d), row),
                  _const_spec(mod.shape),
                  _const_spec((1, d)), _const_spec((1, LANES)),
                  _const_spec((1, MLA_Q_LORA)), _const_spec((1, MLA_KV_LORA)),
                  pl.BlockSpec((1, 1, tm), lambda b, i: (b, 0, i)), _const_spec((half, LANES))]
                 + [_const_spec(w.shape) for w in weights] + [_const_spec(sel.shape)] + late_in,
        out_specs=[head_spec, head_spec, v_spec, head_spec, head_spec, v_spec, g_spec, g_spec] + late_out,
        out_shape=[head_shape, head_shape, v_shape, head_shape, head_shape, v_shape, g_shape, g_shape]
                  + late_shapes,
        scratch_shapes=[pltpu.VMEM((SUBLANES, LANES), F32),
                        pltpu.VMEM((tm, LANES), F32),
                        pltpu.VMEM((tm, LANES), F32)],
        compiler_params=pltpu.CompilerParams(dimension_semantics=("arbitrary", "arbitrary"),
                                             vmem_limit_bytes=VMEM_LIMIT),
        name="token_proj",
    )(x, mod, g_pre.reshape(1, d), bf3, g_q.reshape(1, -1), g_kv.reshape(1, -1), pos, invf,
      *weights, sel, *late_weights)


def _attn_kernel(zero_ref, q_ref, k_ref, vt_ref, o_ref, sa_ref, sb_ref, smax_ref, m_ref, acc_ref, *, tile):
    seq = vt_ref.shape[3]
    nq = seq // tile
    n_heads = q_ref.shape[1]
    z = zero_ref[0]
    half = tile // 2
    nt = (((1,), (1,)), ((), ()))
    vrow = lax.broadcasted_iota(jnp.int32, (LANES, tile), 0)
    sum_row = (HEAD_DIM, 0)
    keep_even = vrow < HEAD_DIM
    krow = lax.broadcasted_iota(jnp.int32, (LANES, LANES), 0)
    qcol = lax.broadcasted_iota(jnp.int32, (LANES, LANES), 1)
    diag_ok = krow <= qcol

    def issue_scores(qi, j, s_out, slot, heads=None):
        for hh in (range(n_heads) if heads is None else heads):
            q = q_ref[0, hh, qi * tile:(qi + 1) * tile, :]
            k = k_ref[0, hh, j * tile:(j + 1) * tile, :]
            if j == qi:
                s_out[z + hh, :half, :] = lax.dot_general(k[:half], q, nt, preferred_element_type=F32)
                s_out[z + hh, half:, half:] = lax.dot_general(k[half:], q[half:], nt, preferred_element_type=F32)
            else:
                s = lax.dot_general(k, q, nt, preferred_element_type=F32)
                s_out[z + hh] = s
                smax_ref[slot, hh] = jnp.max(s, axis=0, keepdims=True)

    def softmax_pv(s_in, slot, hh, vth, masked):
        alphas, p_cols = [], []
        for c0 in range(0, tile, LANES):
            n_keys = c0 + LANES if masked else tile
            n_fill = tile if (c0 >= half or not masked) else half
            m_old = m_ref[hh, :, c0:c0 + LANES]

            def strip(r0):
                sb = s_in[z + hh, r0:r0 + LANES, c0:c0 + LANES]
                return jnp.where(diag_ok, sb, -jnp.inf) if (masked and r0 == c0) else sb

            if masked:
                mx = functools.reduce(jnp.maximum, [strip(r0) for r0 in range(0, n_keys, LANES)])
                m_new = jnp.maximum(m_old, jnp.max(mx, axis=0, keepdims=True))
            else:
                m_new = jnp.maximum(m_old, smax_ref[slot, hh, :, c0:c0 + LANES])
            parts = [jnp.exp2(strip(r0) - m_new).astype(BF16) for r0 in range(0, n_keys, LANES)]
            parts += [jnp.zeros((LANES, LANES), BF16)] * ((n_fill - n_keys) // LANES)
            p_cols.append(jnp.concatenate(parts, axis=0))
            m_ref[hh, :, c0:c0 + LANES] = m_new
            alphas.append(jnp.exp2(m_old - m_new))
        alpha = jnp.concatenate(alphas, axis=1)
        n_left = half // LANES
        if masked:
            pv = jnp.concatenate([_dot(vth[:, :half], jnp.concatenate(p_cols[:n_left], axis=1)),
                                  _dot(vth, jnp.concatenate(p_cols[n_left:], axis=1))], axis=1)
        else:
            pv = _dot(vth, jnp.concatenate(p_cols, axis=1))
        acc_ref[z + hh] = alpha * acc_ref[z + hh] + pv

    def reset_state(heads):
        for hh in heads:
            m_ref[hh] = jnp.full(m_ref.shape[1:], -jnp.inf, F32)
            acc_ref[hh] = jnp.zeros(acc_ref.shape[1:], F32)

    blocks = [(qi, j) for qi in range(nq) for j in range(qi + 1)]
    bufs = (sa_ref, sb_ref)
    for pair in range(n_heads // 2):
        heads = (2 * pair, 2 * pair + 1)
        reset_state(heads)
        issue_scores(0, 0, bufs[0], 0, heads)
        for t, (qi, j) in enumerate(blocks):
            s_in, s_out = bufs[t % 2], bufs[(t + 1) % 2]
            for hh in heads:
                if t + 1 < len(blocks):
                    issue_scores(*blocks[t + 1], s_out, (t + 1) % 2, heads=(hh,))
                softmax_pv(s_in, t % 2, hh, vt_ref[0, hh, :, j * tile:(j + 1) * tile], j == qi)
            if j == qi:
                outs = []
                for hh in heads:
                    acc = acc_ref[z + hh]
                    outs.append(acc * (1.0 / acc[sum_row[hh % 2]:sum_row[hh % 2] + 1, :]))
                o_t = jnp.where(keep_even, outs[0], outs[1])
                o_ref[0, qi * tile:(qi + 1) * tile, pair * LANES:(pair + 1) * LANES] = o_t.T.astype(o_ref.dtype)
                if qi + 1 < nq:
                    reset_state(heads)


def _causal_attn(q, k, vt):
    bsz, heads, seq, _ = q.shape
    g = ATTN_PAIRS
    qk_spec = pl.BlockSpec((1, 2 * g, seq, LANES), lambda b, p: (b, p, 0, 0))
    vt_spec = pl.BlockSpec((1, 2 * g, LANES, seq), lambda b, p: (b, p, 0, 0))
    o_spec = pl.BlockSpec((1, seq, g * LANES), lambda b, p: (b, 0, p))
    kern = functools.partial(_attn_kernel, tile=ATTN_TILE)
    return pl.pallas_call(
        kern,
        grid=(bsz, heads // (2 * g)),
        in_specs=[pl.BlockSpec(memory_space=pltpu.SMEM), qk_spec, qk_spec, vt_spec],
        out_specs=o_spec,
        out_shape=jax.ShapeDtypeStruct((bsz, seq, heads * HEAD_DIM), BF16),
        scratch_shapes=[pltpu.VMEM((2 * g, ATTN_TILE, ATTN_TILE), F32),
                        pltpu.VMEM((2 * g, ATTN_TILE, ATTN_TILE), F32),
                        pltpu.VMEM((2, 2 * g, 1, ATTN_TILE), F32),
                        pltpu.VMEM((2 * g, 1, ATTN_TILE), F32),
                        pltpu.VMEM((2 * g, LANES, ATTN_TILE), F32)],
        compiler_params=pltpu.CompilerParams(dimension_semantics=("arbitrary", "arbitrary"),
                                             vmem_limit_bytes=VMEM_LIMIT),
        name="causal_attn",
    )(jnp.zeros((1,), jnp.int32), q, k, vt)


def _mix_ffn_kernel(x_ref, oa_ref, ob_ref, gf_ref, gm_ref, mod_ref, gpost_mix_ref, gpre_ffn_ref,
                    gpost_ffn_ref, wpf_ref, wpm_ref, wout_ref, win_ref, wdown_ref, o_ref):
    tm = x_ref.shape[1]
    mod = lambda k: mod_ref[pl.ds(pl.program_id(0), 1), k * D_MODEL:(k + 1) * D_MODEL]
    gate_mix = mod(2)
    shift = mod(3)
    scale = mod(4)
    gate = mod(5)

    def mix_matmuls(rows):
        pa = _dot(oa_ref[0, rows, :], wpf_ref[...])
        pb = _dot(ob_ref[0, rows, :], wpm_ref[...])
        merged = gf_ref[0, rows, :].astype(F32) * pa + gm_ref[0, rows, :].astype(F32) * pb
        return _dot(merged.astype(BF16), wout_ref[...])

    def mix_residual(rows, y):
        x = x_ref[0, rows, :] + gate_mix * (_rms(y) * gpost_mix_ref[...])
        h = (_rms(x) * gpre_ffn_ref[...] * (1.0 + scale) + shift).astype(BF16)
        return x, h

    chunks = [(lo, min(FFN_CHUNK, D_FF - lo)) for lo in range(0, D_FF, FFN_CHUNK)]

    def up_matmuls(h, lo, w):
        return _dot(h, win_ref[:, lo:lo + w]), _dot(h, win_ref[:, D_FF + lo:D_FF + lo + w])

    def swiglu_matmuls(h):
        y = None
        gu = up_matmuls(h, *chunks[0])
        for c, (lo, w) in enumerate(chunks):
            g, u = gu
            if c + 1 < len(chunks):
                gu = up_matmuls(h, *chunks[c + 1])
            act = (g * jax.nn.sigmoid(g) * u).astype(BF16)
            part = _dot(act, wdown_ref[lo:lo + w, :])
            y = part if y is None else y + part
        return y

    parts = [pl.ds(r0, ROW_PART) for r0 in range(0, tm, ROW_PART)]
    y_mix, xs, y_ffn = {}, {}, {}
    for t in range(len(parts) + 2):
        if t < len(parts):
            y_mix[t] = mix_matmuls(parts[t])
        if 0 <= t - 1 < len(parts):
            xs[t - 1], h = mix_residual(parts[t - 1], y_mix.pop(t - 1))
            y_ffn[t - 1] = swiglu_matmuls(h)
        if 0 <= t - 2 < len(parts):
            o_ref[0, parts[t - 2], :] = xs.pop(t - 2) + gate * (_rms(y_ffn.pop(t - 2)) * gpost_ffn_ref[...])


def _mix_ffn(x, o_a, o_b, gf, gm, mod, g_post_mix, g_pre_ffn, g_post_ffn, w_pf, w_pm, w_out, w_in, w_down):
    bsz, seq, d = x.shape
    tm = MIX_TILE
    row = lambda b, i: (b, i, 0)
    weights = (w_pf, w_pm, w_out, w_in, w_down)
    return pl.pallas_call(
        _mix_ffn_kernel,
        grid=(bsz, seq // tm),
        in_specs=[pl.BlockSpec((1, tm, d), row),
                  pl.BlockSpec((1, tm, FOX_WIDTH), row), pl.BlockSpec((1, tm, FOX_WIDTH), row),
                  pl.BlockSpec((1, tm, d), row), pl.BlockSpec((1, tm, d), row),
                  _const_spec(mod.shape),
                  _const_spec((1, d)), _const_spec((1, d)), _const_spec((1, d))]
                 + [_const_spec(w.shape) for w in weights],
        out_specs=pl.BlockSpec((1, tm, d), row),
        out_shape=jax.ShapeDtypeStruct(x.shape, F32),
        compiler_params=pltpu.CompilerParams(dimension_semantics=("arbitrary", "arbitrary"),
                                             vmem_limit_bytes=VMEM_LIMIT),
        name="mix_ffn",
    )(x, o_a, o_b, gf, gm, mod, g_post_mix.reshape(1, d), g_pre_ffn.reshape(1, d), g_post_ffn.reshape(1, d),
      *weights)


def kernel(x, c, positions, w_ada, b_ada, g_pre_mix, g_post_mix, g_pre_ffn, g_post_ffn, w_in, b_forget,
           g_q_lora, w_uq, g_kv_lora, w_ukv, w_proj_fox, w_proj_mla, w_out, w_ffn_in, w_ffn_out):
    bsz, seq, d = x.shape
    depth = w_ada.shape[0]
    for l in range(depth):
        mod, *w_in_parts = _prep(c, w_ada, b_ada, w_in, l)
        weights, bf3 = _prep_proj_weights(w_in_parts, w_uq[l], w_ukv[l], b_forget[l])
        late = (w_proj_fox, w_proj_mla, w_out, w_ffn_in, w_ffn_out)
        qf, kf, vf, qm, km, vm, gf, gm, *late_bf16 = _token_proj(
            x, mod, g_pre_mix[l], positions, weights, bf3, g_q_lora[l], g_kv_lora[l], late, l)
        o_a = _causal_attn(qf, kf, vf)
        o_b = _causal_attn(qm, km, vm)
        x = _mix_ffn(x, o_a, o_b, gf, gm, mod, g_post_mix[l], g_pre_ffn[l], g_post_ffn[l], *late_bf16)
    return x
```

```python
import functools
import math

import jax
import jax.numpy as jnp
import numpy as np
from jax import lax
from jax.experimental import pallas as pl
from jax.experimental.pallas import tpu as pltpu

D_MODEL = 1024
HEADS = 8
HEAD_DIM = 64
FOX_WIDTH = HEADS * HEAD_DIM
MLA_NOPE = 64
MLA_ROPE = 32
MLA_V = 64
MLA_Q_LORA = 768
MLA_KV_LORA = 256
D_FF = 2816
ROPE_THETA = 10000.0
NORM_EPS = 1e-6
IN_WIDTHS = (FOX_WIDTH, FOX_WIDTH, FOX_WIDTH, HEADS, MLA_Q_LORA, MLA_KV_LORA, MLA_ROPE, D_MODEL, D_MODEL)

LANES = 128
SUBLANES = 8
BF16_ROWS = 16
TOKEN_TILE = 512
MIX_TILE = 512
ROW_PART = 256
ATTN_TILE = 512
ATTN_PAIRS = 2
FFN_CHUNK = 256
VMEM_LIMIT = 56 * 1024 * 1024

_EXP2_FOX = math.log2(math.e) / math.sqrt(HEAD_DIM)
_EXP2_MLA = math.log2(math.e) / math.sqrt(MLA_NOPE + MLA_ROPE)

F32 = jnp.float32
BF16 = jnp.bfloat16


def _const_spec(shape):
    zeros = (0,) * len(shape)
    return pl.BlockSpec(shape, lambda *_: zeros, pipeline_mode=pl.Buffered(1))


def _rms(x):
    return x * lax.rsqrt(jnp.mean(x * x, axis=-1, keepdims=True) + NORM_EPS)


def _dot(a, b):
    return jnp.dot(a, b, preferred_element_type=F32)


def _cumsum_rows(x):
    n = x.shape[0]
    row = lax.broadcasted_iota(jnp.int32, x.shape, 0)
    d = 1
    while d < n:
        x = x + jnp.where(row >= d, pltpu.roll(x, d, axis=0), 0.0)
        d *= 2
    return x


def _rope_tables(pos_row, invf_col):
    n = pos_row.shape[1]
    groups = LANES // MLA_ROPE
    ang = invf_col * pos_row
    c = jnp.cos(ang)
    s = jnp.sin(ang)
    cos_blocks, sin_blocks = [], []
    for t0 in range(0, n, LANES):
        ct = c[:, t0:t0 + LANES]
        st = s[:, t0:t0 + LANES]
        cos_blocks.append(jnp.concatenate([ct, ct] * groups, axis=0).T)
        sin_blocks.append(jnp.concatenate([-st, st] * groups, axis=0).T)
    return jnp.concatenate(cos_blocks, axis=0), jnp.concatenate(sin_blocks, axis=0)


def _rope(x, cos4, sin4, lane):
    half = MLA_ROPE // 2
    swapped = jnp.where((lane & (MLA_ROPE - 1)) < half, pltpu.roll(x, LANES - half, axis=1), pltpu.roll(x, half, axis=1))
    return x * cos4 + swapped * sin4


def _store_values_transposed(vt_ref, rows, v):
    n = v.shape[0]
    row = lax.broadcasted_iota(jnp.int32, (LANES, n), 0)
    for pair in range(HEADS // 2):
        vt = v[:, pair * LANES:(pair + 1) * LANES].T
        even = jnp.where(row < HEAD_DIM, vt, jnp.where(row == HEAD_DIM, 1.0, 0.0))
        odd = jnp.where(row >= HEAD_DIM, vt, jnp.where(row == 0, 1.0, 0.0))
        vt_ref[0, 2 * pair, :, rows] = even.astype(vt_ref.dtype)
        vt_ref[0, 2 * pair + 1, :, rows] = odd.astype(vt_ref.dtype)


def _token_proj_kernel(x_ref, mod_ref, gpre_ref, bf_ref, gq_ref, gkv_ref, pos_ref, invf_ref,
                       wqkv_ref, wmisc_ref, wcq_ref, wckv_ref, wg_ref, wq_ref, wkv_ref, sel_ref,
                       late0_ref, late1_ref, late2_ref, late3_ref, late4_ref,
                       qf_ref, kf_ref, vf_ref, qm_ref, km_ref, vm_ref, gf_ref, gm_ref,
                       cast0_ref, cast1_ref, cast2_ref, cast3_ref, cast4_ref,
                       carry_ref, cos_ref, sin_ref):
    tm = x_ref.shape[1]

    @pl.when(pl.program_id(1) == 0)
    def _():
        carry_ref[...] = jnp.zeros_like(carry_ref)

    cos_ref[...], sin_ref[...] = _rope_tables(pos_ref[0], invf_ref[:, 0:1])

    mod = lambda k: mod_ref[pl.ds(pl.program_id(0), 1), k * D_MODEL:(k + 1) * D_MODEL]
    shift = mod(0)
    scale = mod(1)

    def project(rows, n):
        h = (_rms(x_ref[0, rows, :]) * gpre_ref[...] * (1.0 + scale) + shift).astype(BF16)
        lane = lax.broadcasted_iota(jnp.int32, (n, LANES), 1)
        cos4 = cos_ref[rows, :]
        sin4 = sin_ref[rows, :]
        low = lane < HEAD_DIM

        misc = _dot(h, wmisc_ref[...])
        logit = misc + bf_ref[...]
        logf = jnp.minimum(logit, 0.0) - jnp.log(1.0 + jnp.exp(-jnp.abs(logit)))
        cum = _cumsum_rows(logf) + carry_ref[0:1, :]
        carry_ref[0:1, :] = cum[n - 1:n, :]

        cq = _dot(h, wcq_ref[...])
        ckv = _dot(h, wckv_ref[...])
        nq = (_rms(cq) * gq_ref[...]).astype(BF16)
        nkv = (_rms(ckv) * gkv_ref[...]).astype(BF16)
        p_qkv = _dot(h, wqkv_ref[...])
        _store_values_transposed(vf_ref, rows, p_qkv[:, 2 * FOX_WIDTH:])

        qq = _dot(nq, wq_ref[...]) * _EXP2_MLA
        nope_w = HEADS * MLA_NOPE
        q_rope = [_rope(qq[:, nope_w + g * LANES:nope_w + (g + 1) * LANES], cos4, sin4, lane)
                  for g in range(HEADS * MLA_ROPE // LANES)]
        per_group = LANES // MLA_ROPE
        for hh in range(HEADS):
            pair = hh // 2
            nope = qq[:, pair * LANES:(pair + 1) * LANES]
            src_lane = (hh % per_group) * MLA_ROPE
            dst_lane = HEAD_DIM if hh % 2 == 0 else 0
            rope = q_rope[hh // per_group]
            if src_lane != dst_lane:
                rope = pltpu.roll(rope, (dst_lane - src_lane) % LANES, axis=1)
            in_rope = (lane >= dst_lane) & (lane < dst_lane + MLA_ROPE)
            own = low if hh % 2 == 0 else jnp.logical_not(low)
            qm_ref[0, hh, rows, :] = jnp.where(own, nope, jnp.where(in_rope, rope, 0.0)).astype(qm_ref.dtype)
        kv = _dot(nkv, wkv_ref[...])
        _store_values_transposed(vm_ref, rows, kv[:, nope_w:])
        in_rope = (lane >= MLA_NOPE) & (lane < MLA_NOPE + MLA_ROPE)
        kpe_even = jnp.where(in_rope, _rope(misc, cos4, sin4, lane), 0.0)
        kpe_odd = pltpu.roll(kpe_even, LANES - HEAD_DIM, axis=1)
        for hh in range(HEADS):
            pair = hh // 2
            nope = kv[:, pair * LANES:(pair + 1) * LANES]
            k_full = jnp.where(low, nope, kpe_even) if hh % 2 == 0 else jnp.where(low, kpe_odd, nope)
            km_ref[0, hh, rows, :] = k_full.astype(km_ref.dtype)

        a = cum * math.log2(math.e)
        a_hi = a.astype(BF16).astype(F32)
        r1 = a - a_hi
        a_mid = r1.astype(BF16).astype(F32)
        a_lo = (r1 - a_mid).astype(BF16).astype(F32)
        z = jnp.where(lane < HEADS, a_hi,
                      jnp.where(lane < 2 * HEADS, a_mid,
                                jnp.where(lane < 3 * HEADS, a_lo,
                                          jnp.where(lane == 3 * HEADS, 1.0, 0.0))))
        aug = _dot(z.astype(BF16), sel_ref[...])
        pairs = HEADS // 2
        for hh in range(HEADS):
            pair = hh // 2
            keep = (lane < HEAD_DIM) if hh % 2 == 0 else (lane >= HEAD_DIM)
            xq = p_qkv[:, pair * LANES:(pair + 1) * LANES]
            xk = p_qkv[:, FOX_WIDTH + pair * LANES:FOX_WIDTH + (pair + 1) * LANES]
            aq = aug[:, pair * LANES:(pair + 1) * LANES]
            ak = aug[:, (pairs + pair) * LANES:(pairs + pair + 1) * LANES]
            qf_ref[0, hh, rows, :] = jnp.where(keep, xq * _EXP2_FOX, aq).astype(qf_ref.dtype)
            kf_ref[0, hh, rows, :] = jnp.where(keep, xk, ak).astype(kf_ref.dtype)

        for gi, ref in enumerate((gf_ref, gm_ref)):
            g = _dot(h, wg_ref[:, gi * D_MODEL:(gi + 1) * D_MODEL])
            ref[0, rows, :] = jax.nn.sigmoid(g).astype(ref.dtype)

    n = tm // 2
    for r0 in (0, n):
        project(pl.ds(r0, n), n)

    for src, dst in ((late0_ref, cast0_ref), (late1_ref, cast1_ref), (late2_ref, cast2_ref),
                     (late3_ref, cast3_ref), (late4_ref, cast4_ref)):
        dst[...] = src[0].astype(dst.dtype)


_IN_OFFSETS = tuple(int(v) for v in np.cumsum((0,) + IN_WIDTHS))


def _prep_kernel(c_ref, wada_ref, bada_ref, wt_ref, mod_ref, qkv_ref, misc_ref, cq_ref, ckv_ref, g_ref):
    c = c_ref[...]
    sc = c * jax.nn.sigmoid(c)
    mod_ref[...] = _dot(sc.astype(BF16), wada_ref[0].astype(BF16)) + bada_ref[0]

    wt = wt_ref[0]
    cols = wt.shape[1]
    o = _IN_OFFSETS
    qkv_ref[...] = wt[o[0]:o[3], :].T.astype(BF16)
    cq_ref[...] = wt[o[4]:o[5], :].T.astype(BF16)
    ckv_ref[...] = wt[o[5]:o[6], :].T.astype(BF16)
    g_ref[...] = wt[o[7]:o[9], :].T.astype(BF16)
    f = wt[o[3]:o[4], :]
    kr = wt[o[6]:o[7], :]
    misc_t = jnp.concatenate([f, f, f, jnp.zeros((MLA_NOPE - 3 * HEADS, cols), F32), kr,
                              jnp.zeros((LANES - MLA_NOPE - MLA_ROPE, cols), F32)], axis=0)
    misc_ref[...] = misc_t.T.astype(BF16)


def _prep(c, w_ada_all, b_ada_all, w_in_all, layer):
    bsz, d = c.shape
    depth, _, n_mod = w_ada_all.shape
    n_in = w_in_all.shape[2]
    cols = 2 * LANES
    steps = d // cols
    tn = n_mod // steps
    assert d % cols == 0 and n_mod % steps == 0 and tn % LANES == 0
    widths = (3 * FOX_WIDTH, LANES, MLA_Q_LORA, MLA_KV_LORA, 2 * D_MODEL)
    return pl.pallas_call(
        _prep_kernel,
        grid=(steps,),
        in_specs=[pl.BlockSpec((bsz, d), lambda j: (0, 0)),
                  pl.BlockSpec((1, d, tn), lambda j: (layer, 0, j)),
                  pl.BlockSpec((1, 1, tn), lambda j: (layer, 0, j)),
                  pl.BlockSpec((1, n_in, cols), lambda j: (layer, 0, j))],
        out_specs=[pl.BlockSpec((bsz, tn), lambda j: (0, j))]
                  + [pl.BlockSpec((cols, w), lambda j: (j, 0)) for w in widths],
        out_shape=[jax.ShapeDtypeStruct((bsz, n_mod), F32)]
                  + [jax.ShapeDtypeStruct((d, w), BF16) for w in widths],
        name="prep",
    )(c, w_ada_all, b_ada_all.reshape(depth, 1, n_mod), jnp.transpose(w_in_all, (0, 2, 1)))


def _prep_proj_weights(w_in_parts, w_uq, w_ukv, b_forget):
    w_qkv, w_misc, w_cq, w_ckv, w_g = w_in_parts

    r = w_uq.shape[0]
    uq = w_uq.reshape(r, HEADS, MLA_NOPE + MLA_ROPE)
    w_q = jnp.concatenate([uq[:, :, :MLA_NOPE].reshape(r, HEADS * MLA_NOPE),
                           uq[:, :, MLA_NOPE:].reshape(r, HEADS * MLA_ROPE)], axis=1).astype(BF16)

    rk = w_ukv.shape[0]
    ukv = w_ukv.reshape(rk, HEADS, MLA_NOPE + MLA_V)
    w_kv = jnp.concatenate([ukv[:, :, :MLA_NOPE].reshape(rk, HEADS * MLA_NOPE),
                            ukv[:, :, MLA_NOPE:].reshape(rk, HEADS * MLA_V)], axis=1).astype(BF16)

    bf3 = jnp.concatenate([b_forget, b_forget, b_forget,
                           jnp.zeros((LANES - 3 * HEADS,), b_forget.dtype)]).reshape(1, LANES)
    return (w_qkv, w_misc, w_cq, w_ckv, w_g, w_q, w_kv), bf3


def _decay_selector():
    pairs = HEADS // 2
    sel = np.zeros((LANES, 2 * pairs * LANES), np.float32)
    for hh in range(HEADS):
        base_q = (hh // 2) * LANES + (HEAD_DIM if hh % 2 == 0 else 0)
        base_k = (pairs + hh // 2) * LANES + (HEAD_DIM if hh % 2 == 0 else 0)
        for piece in range(3):
            sel[piece * HEADS + hh, base_q + piece] = 1.0
            sel[3 * HEADS, base_q + 3 + piece] = 1.0
            sel[3 * HEADS, base_k + piece] = 1.0
            sel[piece * HEADS + hh, base_k + 3 + piece] = -1.0
    return jnp.asarray(sel, BF16)


def _cast_block_rows(n_rows, n_steps):
    rb = -(-n_rows // n_steps)
    rb += -rb % BF16_ROWS
    while n_rows % rb:
        rb += BF16_ROWS
    return rb


def _token_proj(x, mod, g_pre, positions, weights, bf3, g_q, g_kv, late_weights, layer):
    bsz, seq, d = x.shape
    tm = TOKEN_TILE
    steps_per_batch = seq // tm
    n_steps = bsz * steps_per_batch
    late_in, late_out, late_shapes = [], [], []
    for w in late_weights:
        _, rows, cols = w.shape
        rb = _cast_block_rows(rows, n_steps)
        last = rows // rb - 1
        late_in.append(pl.BlockSpec(
            (1, rb, cols), lambda b, i, last=last: (layer, jnp.minimum(b * steps_per_batch + i, last), 0)))
        late_out.append(pl.BlockSpec(
            (rb, cols), lambda b, i, last=last: (jnp.minimum(b * steps_per_batch + i, last), 0)))
        late_shapes.append(jax.ShapeDtypeStruct((rows, cols), BF16))
    sel = _decay_selector()
    half = MLA_ROPE // 2
    inv_freq = 1.0 / (ROPE_THETA ** (np.arange(0, MLA_ROPE, 2, dtype=np.float32) / MLA_ROPE))
    invf = jnp.asarray(np.tile(inv_freq.astype(np.float32)[:, None], (1, LANES)))
    pos = positions.astype(F32).reshape(bsz, 1, seq)
    row = lambda b, i: (b, i, 0)
    head = lambda b, i: (b, 0, i, 0)
    head_shape = jax.ShapeDtypeStruct((bsz, HEADS, seq, LANES), BF16)
    head_spec = pl.BlockSpec((1, HEADS, tm, LANES), head)
    v_shape = jax.ShapeDtypeStruct((bsz, HEADS, LANES, seq), BF16)
    v_spec = pl.BlockSpec((1, HEADS, LANES, tm), lambda b, i: (b, 0, 0, i))
    g_shape = jax.ShapeDtypeStruct((bsz, seq, d), BF16)
    g_spec = pl.BlockSpec((1, tm, d), row)
    return pl.pallas_call(
        _token_proj_kernel,
        grid=(bsz, seq // tm),
        in_specs=[pl.BlockSpec((1, tm, d), row),
                  _const_spec(mod.shape),
                  _const_spec((1, d)), _const_spec((1, LANES)),
                  _const_spec((1, MLA_Q_LORA)), _const_spec((1, MLA_KV_LORA)),
                  pl.BlockSpec((1, 1, tm), lambda b, i: (b, 0, i)), _const_spec((half, LANES))]
                 + [_const_spec(w.shape) for w in weights] + [_const_spec(sel.shape)] + late_in,
        out_specs=[head_spec, head_spec, v_spec, head_spec, head_spec, v_spec, g_spec, g_spec] + late_out,
        out_shape=[head_shape, head_shape, v_shape, head_shape, head_shape, v_shape, g_shape, g_shape]
                  + late_shapes,
        scratch_shapes=[pltpu.VMEM((SUBLANES, LANES), F32),
                        pltpu.VMEM((tm, LANES), F32),
                        pltpu.VMEM((tm, LANES), F32)],
        compiler_params=pltpu.CompilerParams(dimension_semantics=("arbitrary", "arbitrary"),
                                             vmem_limit_bytes=VMEM_LIMIT),
        name="token_proj",
    )(x, mod, g_pre.reshape(1, d), bf3, g_q.reshape(1, -1), g_kv.reshape(1, -1), pos, invf,
      *weights, sel, *late_weights)


def _attn_kernel(zero_ref, q_ref, k_ref, vt_ref, o_ref, sa_ref, sb_ref, smax_ref, m_ref, acc_ref, *, tile):
    seq = vt_ref.shape[3]
    nq = seq // tile
    n_heads = q_ref.shape[1]
    z = zero_ref[0]
    half = tile // 2
    nt = (((1,), (1,)), ((), ()))
    vrow = lax.broadcasted_iota(jnp.int32, (LANES, tile), 0)
    sum_row = (HEAD_DIM, 0)
    keep_even = vrow < HEAD_DIM
    krow = lax.broadcasted_iota(jnp.int32, (half, half), 0)
    qcol = lax.broadcasted_iota(jnp.int32, (half, half), 1)
    causal_half = krow <= qcol

    def issue_scores(qi, j, s_out, slot, heads=None):
        for hh in (range(n_heads) if heads is None else heads):
            q = q_ref[0, hh, qi * tile:(qi + 1) * tile, :]
            k = k_ref[0, hh, j * tile:(j + 1) * tile, :]
            if j == qi:
                top = lax.dot_general(k[:half], q, nt, preferred_element_type=F32)
                top_left = jnp.where(causal_half, top[:, :half], -jnp.inf)
                bottom = jnp.where(causal_half,
                                   lax.dot_general(k[half:], q[half:], nt, preferred_element_type=F32), -jnp.inf)
                s_out[z + hh, :half, :half] = top_left
                s_out[z + hh, :half, half:] = top[:, half:]
                s_out[z + hh, half:, half:] = bottom
                smax_ref[slot, hh, :, :half] = jnp.max(top_left, axis=0, keepdims=True)
                smax_ref[slot, hh, :, half:] = jnp.maximum(jnp.max(top[:, half:], axis=0, keepdims=True),
                                                           jnp.max(bottom, axis=0, keepdims=True))
            else:
                s = lax.dot_general(k, q, nt, preferred_element_type=F32)
                s_out[z + hh] = s
                smax_ref[slot, hh] = jnp.max(s, axis=0, keepdims=True)

    def softmax_pv(s_in, slot, hh, vth, masked):
        alphas, p_cols = [], []
        for c0 in range(0, tile, LANES):
            n_keys = c0 + LANES if masked else tile
            n_fill = tile if (c0 >= half or not masked) else half
            m_old = m_ref[hh, :, c0:c0 + LANES]
            m_new = jnp.maximum(m_old, smax_ref[slot, hh, :, c0:c0 + LANES])
            parts = [jnp.exp2(s_in[z + hh, r0:r0 + LANES, c0:c0 + LANES] - m_new).astype(BF16)
                     for r0 in range(0, n_keys, LANES)]
            parts += [jnp.zeros((LANES, LANES), BF16)] * ((n_fill - n_keys) // LANES)
            p_cols.append(jnp.concatenate(parts, axis=0))
            m_ref[hh, :, c0:c0 + LANES] = m_new
            alphas.append(jnp.exp2(m_old - m_new))
        alpha = jnp.concatenate(alphas, axis=1)
        n_left = half // LANES
        if masked:
            pv = jnp.concatenate([_dot(vth[:, :half], jnp.concatenate(p_cols[:n_left], axis=1)),
                                  _dot(vth, jnp.concatenate(p_cols[n_left:], axis=1))], axis=1)
        else:
            pv = _dot(vth, jnp.concatenate(p_cols, axis=1))
        acc_ref[z + hh] = alpha * acc_ref[z + hh] + pv

    def reset_state(heads):
        for hh in heads:
            m_ref[hh] = jnp.full(m_ref.shape[1:], -jnp.inf, F32)
            acc_ref[hh] = jnp.zeros(acc_ref.shape[1:], F32)

    blocks = [(qi, j) for qi in range(nq) for j in range(qi + 1)]
    bufs = (sa_ref, sb_ref)
    for pair in range(n_heads // 2):
        heads = (2 * pair, 2 * pair + 1)
        reset_state(heads)
        issue_scores(0, 0, bufs[0], 0, heads)
        for t, (qi, j) in enumerate(blocks):
            s_in, s_out = bufs[t % 2], bufs[(t + 1) % 2]
            for hh in heads:
                if t + 1 < len(blocks):
                    issue_scores(*blocks[t + 1], s_out, (t + 1) % 2, heads=(hh,))
                softmax_pv(s_in, t % 2, hh, vt_ref[0, hh, :, j * tile:(j + 1) * tile], j == qi)
            if j == qi:
                outs = []
                for hh in heads:
                    acc = acc_ref[z + hh]
                    outs.append(acc * (1.0 / acc[sum_row[hh % 2]:sum_row[hh % 2] + 1, :]))
                o_t = jnp.where(keep_even, outs[0], outs[1])
                o_ref[0, qi * tile:(qi + 1) * tile, pair * LANES:(pair + 1) * LANES] = o_t.T.astype(o_ref.dtype)
                if qi + 1 < nq:
                    reset_state(heads)


def _causal_attn(q, k, vt):
    bsz, heads, seq, _ = q.shape
    g = ATTN_PAIRS
    qk_spec = pl.BlockSpec((1, 2 * g, seq, LANES), lambda b, p: (b, p, 0, 0))
    vt_spec = pl.BlockSpec((1, 2 * g, LANES, seq), lambda b, p: (b, p, 0, 0))
    o_spec = pl.BlockSpec((1, seq, g * LANES), lambda b, p: (b, 0, p))
    kern = functools.partial(_attn_kernel, tile=ATTN_TILE)
    return pl.pallas_call(
        kern,
        grid=(bsz, heads // (2 * g)),
        in_specs=[pl.BlockSpec(memory_space=pltpu.SMEM), qk_spec, qk_spec, vt_spec],
        out_specs=o_spec,
        out_shape=jax.ShapeDtypeStruct((bsz, seq, heads * HEAD_DIM), BF16),
        scratch_shapes=[pltpu.VMEM((2 * g, ATTN_TILE, ATTN_TILE), F32),
                        pltpu.VMEM((2 * g, ATTN_TILE, ATTN_TILE), F32),
                        pltpu.VMEM((2, 2 * g, 1, ATTN_TILE), F32),
                        pltpu.VMEM((2 * g, 1, ATTN_TILE), F32),
                        pltpu.VMEM((2 * g, LANES, ATTN_TILE), F32)],
        compiler_params=pltpu.CompilerParams(dimension_semantics=("arbitrary", "arbitrary"),
                                             vmem_limit_bytes=VMEM_LIMIT),
        name="causal_attn",
    )(jnp.zeros((1,), jnp.int32), q, k, vt)


def _mix_ffn_kernel(x_ref, oa_ref, ob_ref, gf_ref, gm_ref, mod_ref, gpost_mix_ref, gpre_ffn_ref,
                    gpost_ffn_ref, wpf_ref, wpm_ref, wout_ref, win_ref, wdown_ref, o_ref):
    tm = x_ref.shape[1]
    mod = lambda k: mod_ref[pl.ds(pl.program_id(0), 1), k * D_MODEL:(k + 1) * D_MODEL]
    gate_mix = mod(2)
    shift = mod(3)
    scale = mod(4)
    gate = mod(5)

    def mix_matmuls(rows):
        pa = _dot(oa_ref[0, rows, :], wpf_ref[...])
        pb = _dot(ob_ref[0, rows, :], wpm_ref[...])
        merged = gf_ref[0, rows, :].astype(F32) * pa + gm_ref[0, rows, :].astype(F32) * pb
        return _dot(merged.astype(BF16), wout_ref[...])

    def mix_residual(rows, y):
        x = x_ref[0, rows, :] + gate_mix * (_rms(y) * gpost_mix_ref[...])
        h = (_rms(x) * gpre_ffn_ref[...] * (1.0 + scale) + shift).astype(BF16)
        return x, h

    chunks = [(lo, min(FFN_CHUNK, D_FF - lo)) for lo in range(0, D_FF, FFN_CHUNK)]

    def up_matmuls(h, lo, w):
        return _dot(h, win_ref[:, lo:lo + w]), _dot(h, win_ref[:, D_FF + lo:D_FF + lo + w])

    def swiglu_matmuls(h):
        y = None
        gu = up_matmuls(h, *chunks[0])
        for c, (lo, w) in enumerate(chunks):
            g, u = gu
            if c + 1 < len(chunks):
                gu = up_matmuls(h, *chunks[c + 1])
            act = (g * jax.nn.sigmoid(g) * u).astype(BF16)
            part = _dot(act, wdown_ref[lo:lo + w, :])
            y = part if y is None else y + part
        return y

    parts = [pl.ds(r0, ROW_PART) for r0 in range(0, tm, ROW_PART)]
    y_mix, xs, y_ffn = {}, {}, {}
    for t in range(len(parts) + 2):
        if t < len(parts):
            y_mix[t] = mix_matmuls(parts[t])
        if 0 <= t - 1 < len(parts):
            xs[t - 1], h = mix_residual(parts[t - 1], y_mix.pop(t - 1))
            y_ffn[t - 1] = swiglu_matmuls(h)
        if 0 <= t - 2 < len(parts):
            o_ref[0, parts[t - 2], :] = xs.pop(t - 2) + gate * (_rms(y_ffn.pop(t - 2)) * gpost_ffn_ref[...])


def _mix_ffn(x, o_a, o_b, gf, gm, mod, g_post_mix, g_pre_ffn, g_post_ffn, w_pf, w_pm, w_out, w_in, w_down):
    bsz, seq, d = x.shape
    tm = MIX_TILE
    row = lambda b, i: (b, i, 0)
    weights = (w_pf, w_pm, w_out, w_in, w_down)
    return pl.pallas_call(
        _mix_ffn_kernel,
        grid=(bsz, seq // tm),
        in_specs=[pl.BlockSpec((1, tm, d), row),
                  pl.BlockSpec((1, tm, FOX_WIDTH), row), pl.BlockSpec((1, tm, FOX_WIDTH), row),
                  pl.BlockSpec((1, tm, d), row), pl.BlockSpec((1, tm, d), row),
                  _const_spec(mod.shape),
                  _const_spec((1, d)), _const_spec((1, d)), _const_spec((1, d))]
                 + [_const_spec(w.shape) for w in weights],
        out_specs=pl.BlockSpec((1, tm, d), row),
        out_shape=jax.ShapeDtypeStruct(x.shape, F32),
        compiler_params=pltpu.CompilerParams(dimension_semantics=("arbitrary", "arbitrary"),
                                             vmem_limit_bytes=VMEM_LIMIT),
        name="mix_ffn",
    )(x, o_a, o_b, gf, gm, mod, g_post_mix.reshape(1, d), g_pre_ffn.reshape(1, d), g_post_ffn.reshape(1, d),
      *weights)


def kernel(x, c, positions, w_ada, b_ada, g_pre_mix, g_post_mix, g_pre_ffn, g_post_ffn, w_in, b_forget,
           g_q_lora, w_uq, g_kv_lora, w_ukv, w_proj_fox, w_proj_mla, w_out, w_ffn_in, w_ffn_out):
    bsz, seq, d = x.shape
    depth = w_ada.shape[0]
    for l in range(depth):
        mod, *w_in_parts = _prep(c, w_ada, b_ada, w_in, l)
        weights, bf3 = _prep_proj_weights(w_in_parts, w_uq[l], w_ukv[l], b_forget[l])
        late = (w_proj_fox, w_proj_mla, w_out, w_ffn_in, w_ffn_out)
        qf, kf, vf, qm, km, vm, gf, gm, *late_bf16 = _token_proj(
            x, mod, g_pre_mix[l], positions, weights, bf3, g_q_lora[l], g_kv_lora[l], late, l)
        o_a = _causal_attn(qf, kf, vf)
        o_b = _causal_attn(qm, km, vm)
        x = _mix_ffn(x, o_a, o_b, gf, gm, mod, g_post_mix[l], g_pre_ffn[l], g_post_ffn[l], *late_bf16)
    return x
```

```python
import functools
import math

import jax
import jax.numpy as jnp
import numpy as np
from jax import lax
from jax.experimental import pallas as pl
from jax.experimental.pallas import tpu as pltpu

D_MODEL = 1024
HEADS = 8
HEAD_DIM = 64
FOX_WIDTH = HEADS * HEAD_DIM
MLA_NOPE = 64
MLA_ROPE = 32
MLA_V = 64
MLA_Q_LORA = 768
MLA_KV_LORA = 256
D_FF = 2816
ROPE_THETA = 10000.0
NORM_EPS = 1e-6
IN_WIDTHS = (FOX_WIDTH, FOX_WIDTH, FOX_WIDTH, HEADS, MLA_Q_LORA, MLA_KV_LORA, MLA_ROPE, D_MODEL, D_MODEL)

LANES = 128
SUBLANES = 8
BF16_ROWS = 16
TOKEN_TILE = 512
MIX_TILE = 512
ROW_PART = 256
ATTN_TILE = 512
ATTN_PAIRS = 2
FFN_CHUNK = 256
VMEM_LIMIT = 56 * 1024 * 1024

_EXP2_FOX = math.log2(math.e) / math.sqrt(HEAD_DIM)
_EXP2_MLA = math.log2(math.e) / math.sqrt(MLA_NOPE + MLA_ROPE)

F32 = jnp.float32
BF16 = jnp.bfloat16


def _const_spec(shape):
    zeros = (0,) * len(shape)
    return pl.BlockSpec(shape, lambda *_: zeros, pipeline_mode=pl.Buffered(1))


def _rms(x):
    return x * lax.rsqrt(jnp.mean(x * x, axis=-1, keepdims=True) + NORM_EPS)


def _dot(a, b):
    return jnp.dot(a, b, preferred_element_type=F32)


def _cumsum_rows(x):
    n = x.shape[0]
    row = lax.broadcasted_iota(jnp.int32, x.shape, 0)
    d = 1
    while d < n:
        x = x + jnp.where(row >= d, pltpu.roll(x, d, axis=0), 0.0)
        d *= 2
    return x


def _rope_tables(pos_row, invf_col):
    n = pos_row.shape[1]
    groups = LANES // MLA_ROPE
    ang = invf_col * pos_row
    c = jnp.cos(ang)
    s = jnp.sin(ang)
    cos_blocks, sin_blocks = [], []
    for t0 in range(0, n, LANES):
        ct = c[:, t0:t0 + LANES]
        st = s[:, t0:t0 + LANES]
        cos_blocks.append(jnp.concatenate([ct, ct] * groups, axis=0).T)
        sin_blocks.append(jnp.concatenate([-st, st] * groups, axis=0).T)
    return jnp.concatenate(cos_blocks, axis=0), jnp.concatenate(sin_blocks, axis=0)


def _rope(x, cos4, sin4, lane):
    half = MLA_ROPE // 2
    swapped = jnp.where((lane & (MLA_ROPE - 1)) < half, pltpu.roll(x, LANES - half, axis=1), pltpu.roll(x, half, axis=1))
    return x * cos4 + swapped * sin4


def _store_values_transposed(vt_ref, rows, v):
    n = v.shape[0]
    row = lax.broadcasted_iota(jnp.int32, (LANES, n), 0)
    for pair in range(HEADS // 2):
        vt = v[:, pair * LANES:(pair + 1) * LANES].T
        even = jnp.where(row < HEAD_DIM, vt, jnp.where(row == HEAD_DIM, 1.0, 0.0))
        odd = jnp.where(row >= HEAD_DIM, vt, jnp.where(row == 0, 1.0, 0.0))
        vt_ref[0, 2 * pair, :, rows] = even.astype(vt_ref.dtype)
        vt_ref[0, 2 * pair + 1, :, rows] = odd.astype(vt_ref.dtype)


def _token_proj_kernel(x_ref, mod_ref, gpre_ref, bf_ref, gq_ref, gkv_ref, pos_ref, invf_ref,
                       wqkv_ref, wmisc_ref, wcq_ref, wckv_ref, wg_ref, wq_ref, wkv_ref, sel_ref,
                       late0_ref, late1_ref, late2_ref, late3_ref, late4_ref,
                       qf_ref, kf_ref, vf_ref, qm_ref, km_ref, vm_ref, gf_ref, gm_ref,
                       cast0_ref, cast1_ref, cast2_ref, cast3_ref, cast4_ref,
                       carry_ref, cos_ref, sin_ref):
    tm = x_ref.shape[1]

    @pl.when(pl.program_id(1) == 0)
    def _():
        carry_ref[...] = jnp.zeros_like(carry_ref)

    cos_ref[...], sin_ref[...] = _rope_tables(pos_ref[0], invf_ref[:, 0:1])

    mod = lambda k: mod_ref[pl.ds(pl.program_id(0), 1), k * D_MODEL:(k + 1) * D_MODEL]
    shift = mod(0)
    scale = mod(1)

    def project(rows, n):
        h = (_rms(x_ref[0, rows, :]) * gpre_ref[...] * (1.0 + scale) + shift).astype(BF16)
        lane = lax.broadcasted_iota(jnp.int32, (n, LANES), 1)
        cos4 = cos_ref[rows, :]
        sin4 = sin_ref[rows, :]
        low = lane < HEAD_DIM

        misc = _dot(h, wmisc_ref[...])
        logit = misc + bf_ref[...]
        logf = jnp.minimum(logit, 0.0) - jnp.log(1.0 + jnp.exp(-jnp.abs(logit)))
        cum = _cumsum_rows(logf) + carry_ref[0:1, :]
        carry_ref[0:1, :] = cum[n - 1:n, :]

        cq = _dot(h, wcq_ref[...])
        ckv = _dot(h, wckv_ref[...])
        nq = (_rms(cq) * gq_ref[...]).astype(BF16)
        nkv = (_rms(ckv) * gkv_ref[...]).astype(BF16)
        p_qkv = _dot(h, wqkv_ref[...])
        _store_values_transposed(vf_ref, rows, p_qkv[:, 2 * FOX_WIDTH:])

        qq = _dot(nq, wq_ref[...]) * _EXP2_MLA
        nope_w = HEADS * MLA_NOPE
        q_rope = [_rope(qq[:, nope_w + g * LANES:nope_w + (g + 1) * LANES], cos4, sin4, lane)
                  for g in range(HEADS * MLA_ROPE // LANES)]
        per_group = LANES // MLA_ROPE
        for hh in range(HEADS):
            pair = hh // 2
            nope = qq[:, pair * LANES:(pair + 1) * LANES]
            src_lane = (hh % per_group) * MLA_ROPE
            dst_lane = HEAD_DIM if hh % 2 == 0 else 0
            rope = q_rope[hh // per_group]
            if src_lane != dst_lane:
                rope = pltpu.roll(rope, (dst_lane - src_lane) % LANES, axis=1)
            in_rope = (lane >= dst_lane) & (lane < dst_lane + MLA_ROPE)
            own = low if hh % 2 == 0 else jnp.logical_not(low)
            qm_ref[0, hh, rows, :] = jnp.where(own, nope, jnp.where(in_rope, rope, 0.0)).astype(qm_ref.dtype)
        kv = _dot(nkv, wkv_ref[...])
        _store_values_transposed(vm_ref, rows, kv[:, nope_w:])
        in_rope = (lane >= MLA_NOPE) & (lane < MLA_NOPE + MLA_ROPE)
        kpe_even = jnp.where(in_rope, _rope(misc, cos4, sin4, lane), 0.0)
        kpe_odd = pltpu.roll(kpe_even, LANES - HEAD_DIM, axis=1)
        for hh in range(HEADS):
            pair = hh // 2
            nope = kv[:, pair * LANES:(pair + 1) * LANES]
            k_full = jnp.where(low, nope, kpe_even) if hh % 2 == 0 else jnp.where(low, kpe_odd, nope)
            km_ref[0, hh, rows, :] = k_full.astype(km_ref.dtype)

        a = cum * math.log2(math.e)
        a_hi = a.astype(BF16).astype(F32)
        r1 = a - a_hi
        a_mid = r1.astype(BF16).astype(F32)
        a_lo = (r1 - a_mid).astype(BF16).astype(F32)
        z = jnp.where(lane < HEADS, a_hi,
                      jnp.where(lane < 2 * HEADS, a_mid,
                                jnp.where(lane < 3 * HEADS, a_lo,
                                          jnp.where(lane == 3 * HEADS, 1.0, 0.0))))
        aug = _dot(z.astype(BF16), sel_ref[...])
        pairs = HEADS // 2
        for hh in range(HEADS):
            pair = hh // 2
            keep = (lane < HEAD_DIM) if hh % 2 == 0 else (lane >= HEAD_DIM)
            xq = p_qkv[:, pair * LANES:(pair + 1) * LANES]
            xk = p_qkv[:, FOX_WIDTH + pair * LANES:FOX_WIDTH + (pair + 1) * LANES]
            aq = aug[:, pair * LANES:(pair + 1) * LANES]
            ak = aug[:, (pairs + pair) * LANES:(pairs + pair + 1) * LANES]
            qf_ref[0, hh, rows, :] = jnp.where(keep, xq * _EXP2_FOX, aq).astype(qf_ref.dtype)
            kf_ref[0, hh, rows, :] = jnp.where(keep, xk, ak).astype(kf_ref.dtype)

        for gi, ref in enumerate((gf_ref, gm_ref)):
            g = _dot(h, wg_ref[:, gi * D_MODEL:(gi + 1) * D_MODEL])
            ref[0, rows, :] = jax.nn.sigmoid(g).astype(ref.dtype)

    n = tm // 2
    for r0 in (0, n):
        project(pl.ds(r0, n), n)

    for src, dst in ((late0_ref, cast0_ref), (late1_ref, cast1_ref), (late2_ref, cast2_ref),
                     (late3_ref, cast3_ref), (late4_ref, cast4_ref)):
        dst[...] = src[0].astype(dst.dtype)


_IN_OFFSETS = tuple(int(v) for v in np.cumsum((0,) + IN_WIDTHS))


def _prep_kernel(c_ref, wada_ref, bada_ref, wt_ref, mod_ref, qkv_ref, misc_ref, cq_ref, ckv_ref, g_ref):
    c = c_ref[...]
    sc = c * jax.nn.sigmoid(c)
    mod_ref[...] = _dot(sc.astype(BF16), wada_ref[0].astype(BF16)) + bada_ref[0]

    wt = wt_ref[0]
    cols = wt.shape[1]
    o = _IN_OFFSETS
    qkv_ref[...] = wt[o[0]:o[3], :].T.astype(BF16)
    cq_ref[...] = wt[o[4]:o[5], :].T.astype(BF16)
    ckv_ref[...] = wt[o[5]:o[6], :].T.astype(BF16)
    g_ref[...] = wt[o[7]:o[9], :].T.astype(BF16)
    f = wt[o[3]:o[4], :]
    kr = wt[o[6]:o[7], :]
    misc_t = jnp.concatenate([f, f, f, jnp.zeros((MLA_NOPE - 3 * HEADS, cols), F32), kr,
                              jnp.zeros((LANES - MLA_NOPE - MLA_ROPE, cols), F32)], axis=0)
    misc_ref[...] = misc_t.T.astype(BF16)


def _prep(c, w_ada_all, b_ada_all, w_in_all, layer):
    bsz, d = c.shape
    depth, _, n_mod = w_ada_all.shape
    n_in = w_in_all.shape[2]
    cols = 2 * LANES
    steps = d // cols
    tn = n_mod // steps
    assert d % cols == 0 and n_mod % steps == 0 and tn % LANES == 0
    widths = (3 * FOX_WIDTH, LANES, MLA_Q_LORA, MLA_KV_LORA, 2 * D_MODEL)
    return pl.pallas_call(
        _prep_kernel,
        grid=(steps,),
        in_specs=[pl.BlockSpec((bsz, d), lambda j: (0, 0)),
                  pl.BlockSpec((1, d, tn), lambda j: (layer, 0, j)),
                  pl.BlockSpec((1, 1, tn), lambda j: (layer, 0, j)),
                  pl.BlockSpec((1, n_in, cols), lambda j: (layer, 0, j))],
        out_specs=[pl.BlockSpec((bsz, tn), lambda j: (0, j))]
                  + [pl.BlockSpec((cols, w), lambda j: (j, 0)) for w in widths],
        out_shape=[jax.ShapeDtypeStruct((bsz, n_mod), F32)]
                  + [jax.ShapeDtypeStruct((d, w), BF16) for w in widths],
        name="prep",
    )(c, w_ada_all, b_ada_all.reshape(depth, 1, n_mod), jnp.transpose(w_in_all, (0, 2, 1)))


def _prep_proj_weights(w_in_parts, w_uq, w_ukv, b_forget):
    w_qkv, w_misc, w_cq, w_ckv, w_g = w_in_parts

    r = w_uq.shape[0]
    uq = w_uq.reshape(r, HEADS, MLA_NOPE + MLA_ROPE)
    w_q = jnp.concatenate([uq[:, :, :MLA_NOPE].reshape(r, HEADS * MLA_NOPE),
                           uq[:, :, MLA_NOPE:].reshape(r, HEADS * MLA_ROPE)], axis=1).astype(BF16)

    rk = w_ukv.shape[0]
    ukv = w_ukv.reshape(rk, HEADS, MLA_NOPE + MLA_V)
    w_kv = jnp.concatenate([ukv[:, :, :MLA_NOPE].reshape(rk, HEADS * MLA_NOPE),
                            ukv[:, :, MLA_NOPE:].reshape(rk, HEADS * MLA_V)], axis=1).astype(BF16)

    bf3 = jnp.concatenate([b_forget, b_forget, b_forget,
                           jnp.zeros((LANES - 3 * HEADS,), b_forget.dtype)]).reshape(1, LANES)
    return (w_qkv, w_misc, w_cq, w_ckv, w_g, w_q, w_kv), bf3


def _decay_selector():
    pairs = HEADS // 2
    sel = np.zeros((LANES, 2 * pairs * LANES), np.float32)
    for hh in range(HEADS):
        base_q = (hh // 2) * LANES + (HEAD_DIM if hh % 2 == 0 else 0)
        base_k = (pairs + hh // 2) * LANES + (HEAD_DIM if hh % 2 == 0 else 0)
        for piece in range(3):
            sel[piece * HEADS + hh, base_q + piece] = 1.0
            sel[3 * HEADS, base_q + 3 + piece] = 1.0
            sel[3 * HEADS, base_k + piece] = 1.0
            sel[piece * HEADS + hh, base_k + 3 + piece] = -1.0
    return jnp.asarray(sel, BF16)


def _cast_block_rows(n_rows, n_steps):
    rb = -(-n_rows // n_steps)
    rb += -rb % BF16_ROWS
    while n_rows % rb:
        rb += BF16_ROWS
    return rb


def _token_proj(x, mod, g_pre, positions, weights, bf3, g_q, g_kv, late_weights, layer):
    bsz, seq, d = x.shape
    tm = TOKEN_TILE
    steps_per_batch = seq // tm
    n_steps = bsz * steps_per_batch
    late_in, late_out, late_shapes = [], [], []
    for w in late_weights:
        _, rows, cols = w.shape
        rb = _cast_block_rows(rows, n_steps)
        last = rows // rb - 1
        late_in.append(pl.BlockSpec(
            (1, rb, cols), lambda b, i, last=last: (layer, jnp.minimum(b * steps_per_batch + i, last), 0)))
        late_out.append(pl.BlockSpec(
            (rb, cols), lambda b, i, last=last: (jnp.minimum(b * steps_per_batch + i, last), 0)))
        late_shapes.append(jax.ShapeDtypeStruct((rows, cols), BF16))
    sel = _decay_selector()
    half = MLA_ROPE // 2
    inv_freq = 1.0 / (ROPE_THETA ** (np.arange(0, MLA_ROPE, 2, dtype=np.float32) / MLA_ROPE))
    invf = jnp.asarray(np.tile(inv_freq.astype(np.float32)[:, None], (1, LANES)))
    pos = positions.astype(F32).reshape(bsz, 1, seq)
    row = lambda b, i: (b, i, 0)
    head = lambda b, i: (b, 0, i, 0)
    head_shape = jax.ShapeDtypeStruct((bsz, HEADS, seq, LANES), BF16)
    head_spec = pl.BlockSpec((1, HEADS, tm, LANES), head)
    v_shape = jax.ShapeDtypeStruct((bsz, HEADS, LANES, seq), BF16)
    v_spec = pl.BlockSpec((1, HEADS, LANES, tm), lambda b, i: (b, 0, 0, i))
    g_shape = jax.ShapeDtypeStruct((bsz, seq, d), BF16)
    g_spec = pl.BlockSpec((1, tm, d), row)
    return pl.pallas_call(
        _token_proj_kernel,
        grid=(bsz, seq // tm),
        in_specs=[pl.BlockSpec((1, tm, d), row),
                  _const_spec(mod.shape),
                  _const_spec((1, d)), _const_spec((1, LANES)),
                  _const_spec((1, MLA_Q_LORA)), _const_spec((1, MLA_KV_LORA)),
                  pl.BlockSpec((1, 1, tm), lambda b, i: (b, 0, i)), _const_spec((half, LANES))]
                 + [_const_spec(w.shape) for w in weights] + [_const_spec(sel.shape)] + late_in,
        out_specs=[head_spec, head_spec, v_spec, head_spec, head_spec, v_spec, g_spec, g_spec] + late_out,
        out_shape=[head_shape, head_shape, v_shape, head_shape, head_shape, v_shape, g_shape, g_shape]
                  + late_shapes,
        scratch_shapes=[pltpu.VMEM((SUBLANES, LANES), F32),
                        pltpu.VMEM((tm, LANES), F32),
                        pltpu.VMEM((tm, LANES), F32)],
        compiler_params=pltpu.CompilerParams(dimension_semantics=("arbitrary", "arbitrary"),
                                             vmem_limit_bytes=VMEM_LIMIT),
        name="token_proj",
    )(x, mod, g_pre.reshape(1, d), bf3, g_q.reshape(1, -1), g_kv.reshape(1, -1), pos, invf,
      *weights, sel, *late_weights)


def _attn_kernel(zero_ref, q_ref, k_ref, vt_ref, o_ref, sa_ref, sb_ref, smax_ref, m_ref, acc_ref, *, tile):
    seq = vt_ref.shape[3]
    nq = seq // tile
    n_heads = q_ref.shape[1]
    z = zero_ref[0]
    half = tile // 2
    nt = (((1,), (1,)), ((), ()))
    vrow = lax.broadcasted_iota(jnp.int32, (LANES, tile), 0)
    sum_row = (HEAD_DIM, 0)
    keep_even = vrow < HEAD_DIM
    krow = lax.broadcasted_iota(jnp.int32, (half, half), 0)
    qcol = lax.broadcasted_iota(jnp.int32, (half, half), 1)
    causal_half = krow <= qcol

    def issue_scores(qi, j, s_out, slot, heads=None):
        for hh in (range(n_heads) if heads is None else heads):
            q = q_ref[0, hh, qi * tile:(qi + 1) * tile, :]
            k = k_ref[0, hh, j * tile:(j + 1) * tile, :]
            if j == qi:
                top = lax.dot_general(k[:half], q, nt, preferred_element_type=F32)
                top_left = jnp.where(causal_half, top[:, :half], -jnp.inf)
                bottom = jnp.where(causal_half,
                                   lax.dot_general(k[half:], q[half:], nt, preferred_element_type=F32), -jnp.inf)
                s_out[z + hh, :half, :half] = top_left
                s_out[z + hh, :half, half:] = top[:, half:]
                s_out[z + hh, half:, half:] = bottom
                smax_ref[slot, hh, :, :half] = jnp.max(top_left, axis=0, keepdims=True)
                smax_ref[slot, hh, :, half:] = jnp.maximum(jnp.max(top[:, half:], axis=0, keepdims=True),
                                                           jnp.max(bottom, axis=0, keepdims=True))
            else:
                s = lax.dot_general(k, q, nt, preferred_element_type=F32)
                s_out[z + hh] = s
                smax_ref[slot, hh] = jnp.max(s, axis=0, keepdims=True)

    def softmax_pv(s_in, slot, hh, vth, masked, first):
        alphas, p_cols = [], []
        for c0 in range(0, tile, LANES):
            n_keys = c0 + LANES if masked else tile
            n_fill = tile if (c0 >= half or not masked) else half
            m_new = smax_ref[slot, hh, :, c0:c0 + LANES]
            if not first:
                m_old = m_ref[hh, :, c0:c0 + LANES]
                m_new = jnp.maximum(m_old, m_new)
                alphas.append(jnp.exp2(m_old - m_new))
            parts = [jnp.exp2(s_in[z + hh, r0:r0 + LANES, c0:c0 + LANES] - m_new).astype(BF16)
                     for r0 in range(0, n_keys, LANES)]
            parts += [jnp.zeros((LANES, LANES), BF16)] * ((n_fill - n_keys) // LANES)
            p_cols.append(jnp.concatenate(parts, axis=0))
            m_ref[hh, :, c0:c0 + LANES] = m_new
        n_left = half // LANES
        if masked:
            pv = jnp.concatenate([_dot(vth[:, :half], jnp.concatenate(p_cols[:n_left], axis=1)),
                                  _dot(vth, jnp.concatenate(p_cols[n_left:], axis=1))], axis=1)
        else:
            pv = _dot(vth, jnp.concatenate(p_cols, axis=1))
        if first:
            acc_ref[z + hh] = pv
        else:
            acc_ref[z + hh] = jnp.concatenate(alphas, axis=1) * acc_ref[z + hh] + pv

    blocks = [(qi, j) for qi in range(nq) for j in range(qi + 1)]
    bufs = (sa_ref, sb_ref)
    for pair in range(n_heads // 2):
        heads = (2 * pair, 2 * pair + 1)
        issue_scores(0, 0, bufs[0], 0, heads)
        for t, (qi, j) in enumerate(blocks):
            s_in, s_out = bufs[t % 2], bufs[(t + 1) % 2]
            for hh in heads:
                if t + 1 < len(blocks):
                    issue_scores(*blocks[t + 1], s_out, (t + 1) % 2, heads=(hh,))
                softmax_pv(s_in, t % 2, hh, vt_ref[0, hh, :, j * tile:(j + 1) * tile], j == qi, j == 0)
            if j == qi:
                outs = []
                for hh in heads:
                    acc = acc_ref[z + hh]
                    outs.append(acc * (1.0 / acc[sum_row[hh % 2]:sum_row[hh % 2] + 1, :]))
                o_t = jnp.where(keep_even, outs[0], outs[1])
                o_ref[0, qi * tile:(qi + 1) * tile, pair * LANES:(pair + 1) * LANES] = o_t.T.astype(o_ref.dtype)


def _causal_attn(q, k, vt):
    bsz, heads, seq, _ = q.shape
    g = ATTN_PAIRS
    qk_spec = pl.BlockSpec((1, 2 * g, seq, LANES), lambda b, p: (b, p, 0, 0))
    vt_spec = pl.BlockSpec((1, 2 * g, LANES, seq), lambda b, p: (b, p, 0, 0))
    o_spec = pl.BlockSpec((1, seq, g * LANES), lambda b, p: (b, 0, p))
    kern = functools.partial(_attn_kernel, tile=ATTN_TILE)
    return pl.pallas_call(
        kern,
        grid=(bsz, heads // (2 * g)),
        in_specs=[pl.BlockSpec(memory_space=pltpu.SMEM), qk_spec, qk_spec, vt_spec],
        out_specs=o_spec,
        out_shape=jax.ShapeDtypeStruct((bsz, seq, heads * HEAD_DIM), BF16),
        scratch_shapes=[pltpu.VMEM((2 * g, ATTN_TILE, ATTN_TILE), F32),
                        pltpu.VMEM((2 * g, ATTN_TILE, ATTN_TILE), F32),
                        pltpu.VMEM((2, 2 * g, 1, ATTN_TILE), F32),
                        pltpu.VMEM((2 * g, 1, ATTN_TILE), F32),
                        pltpu.VMEM((2 * g, LANES, ATTN_TILE), F32)],
        compiler_params=pltpu.CompilerParams(dimension_semantics=("arbitrary", "arbitrary"),
                                             vmem_limit_bytes=VMEM_LIMIT),
        name="causal_attn",
    )(jnp.zeros((1,), jnp.int32), q, k, vt)


def _mix_ffn_kernel(x_ref, oa_ref, ob_ref, gf_ref, gm_ref, mod_ref, gpost_mix_ref, gpre_ffn_ref,
                    gpost_ffn_ref, wpf_ref, wpm_ref, wout_ref, win_ref, wdown_ref, o_ref):
    tm = x_ref.shape[1]
    mod = lambda k: mod_ref[pl.ds(pl.program_id(0), 1), k * D_MODEL:(k + 1) * D_MODEL]
    gate_mix = mod(2)
    shift = mod(3)
    scale = mod(4)
    gate = mod(5)

    def mix_matmuls(rows):
        pa = _dot(oa_ref[0, rows, :], wpf_ref[...])
        pb = _dot(ob_ref[0, rows, :], wpm_ref[...])
        merged = gf_ref[0, rows, :].astype(F32) * pa + gm_ref[0, rows, :].astype(F32) * pb
        return _dot(merged.astype(BF16), wout_ref[...])

    def mix_residual(rows, y):
        x = x_ref[0, rows, :] + gate_mix * (_rms(y) * gpost_mix_ref[...])
        h = (_rms(x) * gpre_ffn_ref[...] * (1.0 + scale) + shift).astype(BF16)
        return x, h

    chunks = [(lo, min(FFN_CHUNK, D_FF - lo)) for lo in range(0, D_FF, FFN_CHUNK)]

    def up_matmuls(h, lo, w):
        return _dot(h, win_ref[:, lo:lo + w]), _dot(h, win_ref[:, D_FF + lo:D_FF + lo + w])

    def swiglu_matmuls(h):
        y = None
        gu = up_matmuls(h, *chunks[0])
        for c, (lo, w) in enumerate(chunks):
            g, u = gu
            if c + 1 < len(chunks):
                gu = up_matmuls(h, *chunks[c + 1])
            act = (g * jax.nn.sigmoid(g) * u).astype(BF16)
            part = _dot(act, wdown_ref[lo:lo + w, :])
            y = part if y is None else y + part
        return y

    parts = [pl.ds(r0, ROW_PART) for r0 in range(0, tm, ROW_PART)]
    y_mix, xs, y_ffn = {}, {}, {}
    for t in range(len(parts) + 2):
        if t < len(parts):
            y_mix[t] = mix_matmuls(parts[t])
        if 0 <= t - 1 < len(parts):
            xs[t - 1], h = mix_residual(parts[t - 1], y_mix.pop(t - 1))
            y_ffn[t - 1] = swiglu_matmuls(h)
        if 0 <= t - 2 < len(parts):
            o_ref[0, parts[t - 2], :] = xs.pop(t - 2) + gate * (_rms(y_ffn.pop(t - 2)) * gpost_ffn_ref[...])


def _mix_ffn(x, o_a, o_b, gf, gm, mod, g_post_mix, g_pre_ffn, g_post_ffn, w_pf, w_pm, w_out, w_in, w_down):
    bsz, seq, d = x.shape
    tm = MIX_TILE
    row = lambda b, i: (b, i, 0)
    weights = (w_pf, w_pm, w_out, w_in, w_down)
    return pl.pallas_call(
        _mix_ffn_kernel,
        grid=(bsz, seq // tm),
        in_specs=[pl.BlockSpec((1, tm, d), row),
                  pl.BlockSpec((1, tm, FOX_WIDTH), row), pl.BlockSpec((1, tm, FOX_WIDTH), row),
                  pl.BlockSpec((1, tm, d), row), pl.BlockSpec((1, tm, d), row),
                  _const_spec(mod.shape),
                  _const_spec((1, d)), _const_spec((1, d)), _const_spec((1, d))]
                 + [_const_spec(w.shape) for w in weights],
        out_specs=pl.BlockSpec((1, tm, d), row),
        out_shape=jax.ShapeDtypeStruct(x.shape, F32),
        compiler_params=pltpu.CompilerParams(dimension_semantics=("arbitrary", "arbitrary"),
                                             vmem_limit_bytes=VMEM_LIMIT),
        name="mix_ffn",
    )(x, o_a, o_b, gf, gm, mod, g_post_mix.reshape(1, d), g_pre_ffn.reshape(1, d), g_post_ffn.reshape(1, d),
      *weights)


def kernel(x, c, positions, w_ada, b_ada, g_pre_mix, g_post_mix, g_pre_ffn, g_post_ffn, w_in, b_forget,
           g_q_lora, w_uq, g_kv_lora, w_ukv, w_proj_fox, w_proj_mla, w_out, w_ffn_in, w_ffn_out):
    bsz, seq, d = x.shape
    depth = w_ada.shape[0]
    for l in range(depth):
        mod, *w_in_parts = _prep(c, w_ada, b_ada, w_in, l)
        weights, bf3 = _prep_proj_weights(w_in_parts, w_uq[l], w_ukv[l], b_forget[l])
        late = (w_proj_fox, w_proj_mla, w_out, w_ffn_in, w_ffn_out)
        qf, kf, vf, qm, km, vm, gf, gm, *late_bf16 = _token_proj(
            x, mod, g_pre_mix[l], positions, weights, bf3, g_q_lora[l], g_kv_lora[l], late, l)
        o_a = _causal_attn(qf, kf, vf)
        o_b = _causal_attn(qm, km, vm)
        x = _mix_ffn(x, o_a, o_b, gf, gm, mod, g_post_mix[l], g_pre_ffn[l], g_post_ffn[l], *late_bf16)
    return x
```

```python
import functools
import math

import jax
import jax.numpy as jnp
import numpy as np
from jax import lax
from jax.experimental import pallas as pl
from jax.experimental.pallas import tpu as pltpu

D_MODEL = 1024
HEADS = 8
HEAD_DIM = 64
FOX_WIDTH = HEADS * HEAD_DIM
MLA_NOPE = 64
MLA_ROPE = 32
MLA_V = 64
MLA_Q_LORA = 768
MLA_KV_LORA = 256
D_FF = 2816
ROPE_THETA = 10000.0
NORM_EPS = 1e-6
IN_WIDTHS = (FOX_WIDTH, FOX_WIDTH, FOX_WIDTH, HEADS, MLA_Q_LORA, MLA_KV_LORA, MLA_ROPE, D_MODEL, D_MODEL)

LANES = 128
SUBLANES = 8
BF16_ROWS = 16
TOKEN_TILE = 512
MIX_TILE = 512
ROW_PART = 256
ATTN_TILE = 512
ATTN_PAIRS = 2
FFN_CHUNK = 256
VMEM_LIMIT = 56 * 1024 * 1024

_EXP2_FOX = math.log2(math.e) / math.sqrt(HEAD_DIM)
_EXP2_MLA = math.log2(math.e) / math.sqrt(MLA_NOPE + MLA_ROPE)

F32 = jnp.float32
BF16 = jnp.bfloat16


def _const_spec(shape):
    zeros = (0,) * len(shape)
    return pl.BlockSpec(shape, lambda *_: zeros, pipeline_mode=pl.Buffered(1))


def _rms(x):
    return x * lax.rsqrt(jnp.mean(x * x, axis=-1, keepdims=True) + NORM_EPS)


def _dot(a, b):
    return jnp.dot(a, b, preferred_element_type=F32)


def _cumsum_rows(x):
    n = x.shape[0]
    row = lax.broadcasted_iota(jnp.int32, x.shape, 0)
    d = 1
    while d < n:
        x = x + jnp.where(row >= d, pltpu.roll(x, d, axis=0), 0.0)
        d *= 2
    return x


def _rope_tables(pos_row, invf_col):
    n = pos_row.shape[1]
    groups = LANES // MLA_ROPE
    ang = invf_col * pos_row
    c = jnp.cos(ang)
    s = jnp.sin(ang)
    cos_blocks, sin_blocks = [], []
    for t0 in range(0, n, LANES):
        ct = c[:, t0:t0 + LANES]
        st = s[:, t0:t0 + LANES]
        cos_blocks.append(jnp.concatenate([ct, ct] * groups, axis=0).T)
        sin_blocks.append(jnp.concatenate([-st, st] * groups, axis=0).T)
    return jnp.concatenate(cos_blocks, axis=0), jnp.concatenate(sin_blocks, axis=0)


def _rope(x, cos4, sin4, lane):
    half = MLA_ROPE // 2
    swapped = jnp.where((lane & (MLA_ROPE - 1)) < half, pltpu.roll(x, LANES - half, axis=1), pltpu.roll(x, half, axis=1))
    return x * cos4 + swapped * sin4


def _store_values_transposed(vt_ref, rows, v):
    n = v.shape[0]
    row = lax.broadcasted_iota(jnp.int32, (LANES, n), 0)
    for pair in range(HEADS // 2):
        vt = v[:, pair * LANES:(pair + 1) * LANES].T
        even = jnp.where(row < HEAD_DIM, vt, jnp.where(row == HEAD_DIM, 1.0, 0.0))
        odd = jnp.where(row >= HEAD_DIM, vt, jnp.where(row == 0, 1.0, 0.0))
        vt_ref[0, 2 * pair, :, rows] = even.astype(vt_ref.dtype)
        vt_ref[0, 2 * pair + 1, :, rows] = odd.astype(vt_ref.dtype)


def _token_proj_kernel(x_ref, mod_ref, gpre_ref, bf_ref, gq_ref, gkv_ref, pos_ref, invf_ref,
                       wqkv_ref, wmisc_ref, wcq_ref, wckv_ref, wg_ref, wq_ref, wkv_ref, sel_ref,
                       late0_ref, late1_ref, late2_ref, late3_ref, late4_ref,
                       qf_ref, kf_ref, vf_ref, qm_ref, km_ref, vm_ref, gf_ref, gm_ref,
                       cast0_ref, cast1_ref, cast2_ref, cast3_ref, cast4_ref,
                       carry_ref, cos_ref, sin_ref):
    tm = x_ref.shape[1]

    @pl.when(pl.program_id(1) == 0)
    def _():
        carry_ref[...] = jnp.zeros_like(carry_ref)

    cos_ref[...], sin_ref[...] = _rope_tables(pos_ref[0], invf_ref[:, 0:1])

    mod = lambda k: mod_ref[pl.ds(pl.program_id(0), 1), k * D_MODEL:(k + 1) * D_MODEL]
    shift = mod(0)
    scale = mod(1)

    def project(rows, n):
        h = (_rms(x_ref[0, rows, :]) * gpre_ref[...] * (1.0 + scale) + shift).astype(BF16)
        lane = lax.broadcasted_iota(jnp.int32, (n, LANES), 1)
        cos4 = cos_ref[rows, :]
        sin4 = sin_ref[rows, :]
        low = lane < HEAD_DIM

        misc = _dot(h, wmisc_ref[...])
        logit = misc + bf_ref[...]
        logf = jnp.minimum(logit, 0.0) - jnp.log(1.0 + jnp.exp(-jnp.abs(logit)))
        cum = _cumsum_rows(logf) + carry_ref[0:1, :]
        carry_ref[0:1, :] = cum[n - 1:n, :]

        cq = _dot(h, wcq_ref[...])
        ckv = _dot(h, wckv_ref[...])
        nq = (_rms(cq) * gq_ref[...]).astype(BF16)
        nkv = (_rms(ckv) * gkv_ref[...]).astype(BF16)
        p_qkv = _dot(h, wqkv_ref[...])
        _store_values_transposed(vf_ref, rows, p_qkv[:, 2 * FOX_WIDTH:])

        qq = _dot(nq, wq_ref[...]) * _EXP2_MLA
        nope_w = HEADS * MLA_NOPE
        q_rope = [_rope(qq[:, nope_w + g * LANES:nope_w + (g + 1) * LANES], cos4, sin4, lane)
                  for g in range(HEADS * MLA_ROPE // LANES)]
        per_group = LANES // MLA_ROPE
        for hh in range(HEADS):
            pair = hh // 2
            nope = qq[:, pair * LANES:(pair + 1) * LANES]
            src_lane = (hh % per_group) * MLA_ROPE
            dst_lane = HEAD_DIM if hh % 2 == 0 else 0
            rope = q_rope[hh // per_group]
            if src_lane != dst_lane:
                rope = pltpu.roll(rope, (dst_lane - src_lane) % LANES, axis=1)
            in_rope = (lane >= dst_lane) & (lane < dst_lane + MLA_ROPE)
            own = low if hh % 2 == 0 else jnp.logical_not(low)
            qm_ref[0, hh, rows, :] = jnp.where(own, nope, jnp.where(in_rope, rope, 0.0)).astype(qm_ref.dtype)
        kv = _dot(nkv, wkv_ref[...])
        _store_values_transposed(vm_ref, rows, kv[:, nope_w:])
        in_rope = (lane >= MLA_NOPE) & (lane < MLA_NOPE + MLA_ROPE)
        kpe_even = jnp.where(in_rope, _rope(misc, cos4, sin4, lane), 0.0)
        kpe_odd = pltpu.roll(kpe_even, LANES - HEAD_DIM, axis=1)
        for hh in range(HEADS):
            pair = hh // 2
            nope = kv[:, pair * LANES:(pair + 1) * LANES]
            k_full = jnp.where(low, nope, kpe_even) if hh % 2 == 0 else jnp.where(low, kpe_odd, nope)
            km_ref[0, hh, rows, :] = k_full.astype(km_ref.dtype)

        a = cum * math.log2(math.e)
        a_hi = a.astype(BF16).astype(F32)
        r1 = a - a_hi
        a_mid = r1.astype(BF16).astype(F32)
        a_lo = (r1 - a_mid).astype(BF16).astype(F32)
        z = jnp.where(lane < HEADS, a_hi,
                      jnp.where(lane < 2 * HEADS, a_mid,
                                jnp.where(lane < 3 * HEADS, a_lo,
                                          jnp.where(lane == 3 * HEADS, 1.0, 0.0))))
        aug = _dot(z.astype(BF16), sel_ref[...])
        pairs = HEADS // 2
        for hh in range(HEADS):
            pair = hh // 2
            keep = (lane < HEAD_DIM) if hh % 2 == 0 else (lane >= HEAD_DIM)
            xq = p_qkv[:, pair * LANES:(pair + 1) * LANES]
            xk = p_qkv[:, FOX_WIDTH + pair * LANES:FOX_WIDTH + (pair + 1) * LANES]
            aq = aug[:, pair * LANES:(pair + 1) * LANES]
            ak = aug[:, (pairs + pair) * LANES:(pairs + pair + 1) * LANES]
            qf_ref[0, hh, rows, :] = jnp.where(keep, xq * _EXP2_FOX, aq).astype(qf_ref.dtype)
            kf_ref[0, hh, rows, :] = jnp.where(keep, xk, ak).astype(kf_ref.dtype)

        for gi, ref in enumerate((gf_ref, gm_ref)):
            g = _dot(h, wg_ref[:, gi * D_MODEL:(gi + 1) * D_MODEL])
            ref[0, rows, :] = jax.nn.sigmoid(g).astype(ref.dtype)

    n = tm // 2
    for r0 in (0, n):
        project(pl.ds(r0, n), n)

    for src, dst in ((late0_ref, cast0_ref), (late1_ref, cast1_ref), (late2_ref, cast2_ref),
                     (late3_ref, cast3_ref), (late4_ref, cast4_ref)):
        dst[...] = src[0].astype(dst.dtype)


_IN_OFFSETS = tuple(int(v) for v in np.cumsum((0,) + IN_WIDTHS))


def _prep_kernel(c_ref, wada_ref, bada_ref, wt_ref, mod_ref, qkv_ref, misc_ref, cq_ref, ckv_ref, g_ref):
    c = c_ref[...]
    sc = c * jax.nn.sigmoid(c)
    mod_ref[...] = _dot(sc.astype(BF16), wada_ref[0].astype(BF16)) + bada_ref[0]

    wt = wt_ref[0]
    cols = wt.shape[1]
    o = _IN_OFFSETS
    qkv_ref[...] = wt[o[0]:o[3], :].T.astype(BF16)
    cq_ref[...] = wt[o[4]:o[5], :].T.astype(BF16)
    ckv_ref[...] = wt[o[5]:o[6], :].T.astype(BF16)
    g_ref[...] = wt[o[7]:o[9], :].T.astype(BF16)
    f = wt[o[3]:o[4], :]
    kr = wt[o[6]:o[7], :]
    misc_t = jnp.concatenate([f, f, f, jnp.zeros((MLA_NOPE - 3 * HEADS, cols), F32), kr,
                              jnp.zeros((LANES - MLA_NOPE - MLA_ROPE, cols), F32)], axis=0)
    misc_ref[...] = misc_t.T.astype(BF16)


def _prep(c, w_ada_all, b_ada_all, w_in_all, layer):
    bsz, d = c.shape
    depth, _, n_mod = w_ada_all.shape
    n_in = w_in_all.shape[2]
    cols = 2 * LANES
    steps = d // cols
    tn = n_mod // steps
    assert d % cols == 0 and n_mod % steps == 0 and tn % LANES == 0
    widths = (3 * FOX_WIDTH, LANES, MLA_Q_LORA, MLA_KV_LORA, 2 * D_MODEL)
    return pl.pallas_call(
        _prep_kernel,
        grid=(steps,),
        in_specs=[pl.BlockSpec((bsz, d), lambda j: (0, 0)),
                  pl.BlockSpec((1, d, tn), lambda j: (layer, 0, j)),
                  pl.BlockSpec((1, 1, tn), lambda j: (layer, 0, j)),
                  pl.BlockSpec((1, n_in, cols), lambda j: (layer, 0, j))],
        out_specs=[pl.BlockSpec((bsz, tn), lambda j: (0, j))]
                  + [pl.BlockSpec((cols, w), lambda j: (j, 0)) for w in widths],
        out_shape=[jax.ShapeDtypeStruct((bsz, n_mod), F32)]
                  + [jax.ShapeDtypeStruct((d, w), BF16) for w in widths],
        name="prep",
    )(c, w_ada_all, b_ada_all.reshape(depth, 1, n_mod), jnp.transpose(w_in_all, (0, 2, 1)))


def _prep_proj_weights(w_in_parts, w_uq, w_ukv, b_forget):
    w_qkv, w_misc, w_cq, w_ckv, w_g = w_in_parts

    r = w_uq.shape[0]
    uq = w_uq.reshape(r, HEADS, MLA_NOPE + MLA_ROPE)
    w_q = jnp.concatenate([uq[:, :, :MLA_NOPE].reshape(r, HEADS * MLA_NOPE),
                           uq[:, :, MLA_NOPE:].reshape(r, HEADS * MLA_ROPE)], axis=1).astype(BF16)

    rk = w_ukv.shape[0]
    ukv = w_ukv.reshape(rk, HEADS, MLA_NOPE + MLA_V)
    w_kv = jnp.concatenate([ukv[:, :, :MLA_NOPE].reshape(rk, HEADS * MLA_NOPE),
                            ukv[:, :, MLA_NOPE:].reshape(rk, HEADS * MLA_V)], axis=1).astype(BF16)

    bf3 = jnp.concatenate([b_forget, b_forget, b_forget,
                           jnp.zeros((LANES - 3 * HEADS,), b_forget.dtype)]).reshape(1, LANES)
    return (w_qkv, w_misc, w_cq, w_ckv, w_g, w_q, w_kv), bf3


def _decay_selector():
    pairs = HEADS // 2
    sel = np.zeros((LANES, 2 * pairs * LANES), np.float32)
    for hh in range(HEADS):
        base_q = (hh // 2) * LANES + (HEAD_DIM if hh % 2 == 0 else 0)
        base_k = (pairs + hh // 2) * LANES + (HEAD_DIM if hh % 2 == 0 else 0)
        for piece in range(3):
            sel[piece * HEADS + hh, base_q + piece] = 1.0
            sel[3 * HEADS, base_q + 3 + piece] = 1.0
            sel[3 * HEADS, base_k + piece] = 1.0
            sel[piece * HEADS + hh, base_k + 3 + piece] = -1.0
    return jnp.asarray(sel, BF16)


def _cast_block_rows(n_rows, n_steps):
    rb = -(-n_rows // n_steps)
    rb += -rb % BF16_ROWS
    while n_rows % rb:
        rb += BF16_ROWS
    return rb


def _token_proj(x, mod, g_pre, positions, weights, bf3, g_q, g_kv, late_weights, layer):
    bsz, seq, d = x.shape
    tm = TOKEN_TILE
    steps_per_batch = seq // tm
    n_steps = bsz * steps_per_batch
    late_in, late_out, late_shapes = [], [], []
    for w in late_weights:
        _, rows, cols = w.shape
        rb = _cast_block_rows(rows, n_steps)
        last = rows // rb - 1
        late_in.append(pl.BlockSpec(
            (1, rb, cols), lambda b, i, last=last: (layer, jnp.minimum(b * steps_per_batch + i, last), 0)))
        late_out.append(pl.BlockSpec(
            (rb, cols), lambda b, i, last=last: (jnp.minimum(b * steps_per_batch + i, last), 0)))
        late_shapes.append(jax.ShapeDtypeStruct((rows, cols), BF16))
    sel = _decay_selector()
    half = MLA_ROPE // 2
    inv_freq = 1.0 / (ROPE_THETA ** (np.arange(0, MLA_ROPE, 2, dtype=np.float32) / MLA_ROPE))
    invf = jnp.asarray(np.tile(inv_freq.astype(np.float32)[:, None], (1, LANES)))
    pos = positions.astype(F32).reshape(bsz, 1, seq)
    row = lambda b, i: (b, i, 0)
    head = lambda b, i: (b, 0, i, 0)
    head_shape = jax.ShapeDtypeStruct((bsz, HEADS, seq, LANES), BF16)
    head_spec = pl.BlockSpec((1, HEADS, tm, LANES), head)
    v_shape = jax.ShapeDtypeStruct((bsz, HEADS, LANES, seq), BF16)
    v_spec = pl.BlockSpec((1, HEADS, LANES, tm), lambda b, i: (b, 0, 0, i))
    g_shape = jax.ShapeDtypeStruct((bsz, seq, d), BF16)
    g_spec = pl.BlockSpec((1, tm, d), row)
    return pl.pallas_call(
        _token_proj_kernel,
        grid=(bsz, seq // tm),
        in_specs=[pl.BlockSpec((1, tm, d), row),
                  _const_spec(mod.shape),
                  _const_spec((1, d)), _const_spec((1, LANES)),
                  _const_spec((1, MLA_Q_LORA)), _const_spec((1, MLA_KV_LORA)),
                  pl.BlockSpec((1, 1, tm), lambda b, i: (b, 0, i)), _const_spec((half, LANES))]
                 + [_const_spec(w.shape) for w in weights] + [_const_spec(sel.shape)] + late_in,
        out_specs=[head_spec, head_spec, v_spec, head_spec, head_spec, v_spec, g_spec, g_spec] + late_out,
        out_shape=[head_shape, head_shape, v_shape, head_shape, head_shape, v_shape, g_shape, g_shape]
                  + late_shapes,
        scratch_shapes=[pltpu.VMEM((SUBLANES, LANES), F32),
                        pltpu.VMEM((tm, LANES), F32),
                        pltpu.VMEM((tm, LANES), F32)],
        compiler_params=pltpu.CompilerParams(dimension_semantics=("arbitrary", "arbitrary"),
                                             vmem_limit_bytes=VMEM_LIMIT),
        name="token_proj",
    )(x, mod, g_pre.reshape(1, d), bf3, g_q.reshape(1, -1), g_kv.reshape(1, -1), pos, invf,
      *weights, sel, *late_weights)


def _attn_kernel(zero_ref, q_ref, k_ref, vt_ref, o_ref, sa_ref, sb_ref, smax_ref, m_ref, acc_ref, *, tile):
    seq = vt_ref.shape[3]
    nq = seq // tile
    n_heads = q_ref.shape[1]
    z = zero_ref[0]
    half = tile // 2
    nt = (((1,), (1,)), ((), ()))
    vrow = lax.broadcasted_iota(jnp.int32, (LANES, tile), 0)
    sum_row = (HEAD_DIM, 0)
    keep_even = vrow < HEAD_DIM
    krow = lax.broadcasted_iota(jnp.int32, (half, half), 0)
    qcol = lax.broadcasted_iota(jnp.int32, (half, half), 1)
    causal_half = krow <= qcol

    def issue_scores(qi, j, s_out, slot, heads=None):
        for hh in (range(n_heads) if heads is None else heads):
            q = q_ref[0, hh, qi * tile:(qi + 1) * tile, :]
            k = k_ref[0, hh, j * tile:(j + 1) * tile, :]
            if j == qi:
                top = lax.dot_general(k[:half], q, nt, preferred_element_type=F32)
                top_left = jnp.where(causal_half, top[:, :half], -jnp.inf)
                bottom = jnp.where(causal_half,
                                   lax.dot_general(k[half:], q[half:], nt, preferred_element_type=F32), -jnp.inf)
                s_out[z + hh, :half, :half] = top_left
                s_out[z + hh, :half, half:] = top[:, half:]
                s_out[z + hh, half:, half:] = bottom
                smax_ref[slot, hh, :, :half] = jnp.max(top_left, axis=0, keepdims=True)
                smax_ref[slot, hh, :, half:] = jnp.maximum(jnp.max(top[:, half:], axis=0, keepdims=True),
                                                           jnp.max(bottom, axis=0, keepdims=True))
            else:
                s = lax.dot_general(k, q, nt, preferred_element_type=F32)
                s_out[z + hh] = s
                smax_ref[slot, hh] = jnp.max(s, axis=0, keepdims=True)

    def softmax_pv(s_in, slot, hh, vth, masked, first):
        alphas, p_cols = [], []
        for c0 in range(0, tile, LANES):
            n_keys = c0 + LANES if masked else tile
            n_fill = tile if (c0 >= half or not masked) else half
            m_new = smax_ref[slot, hh, :, c0:c0 + LANES]
            if not first:
                m_old = m_ref[hh, :, c0:c0 + LANES]
                m_new = jnp.maximum(m_old, m_new)
                alphas.append(jnp.exp2(m_old - m_new))
            parts = [jnp.exp2(s_in[z + hh, r0:r0 + LANES, c0:c0 + LANES] - m_new).astype(BF16)
                     for r0 in range(0, n_keys, LANES)]
            parts += [jnp.zeros((LANES, LANES), BF16)] * ((n_fill - n_keys) // LANES)
            p_cols.append(jnp.concatenate(parts, axis=0))
            if not masked:
                m_ref[hh, :, c0:c0 + LANES] = m_new
        n_left = half // LANES
        if masked:
            pv = jnp.concatenate([_dot(vth[:, :half], jnp.concatenate(p_cols[:n_left], axis=1)),
                                  _dot(vth, jnp.concatenate(p_cols[n_left:], axis=1))], axis=1)
        else:
            pv = _dot(vth, jnp.concatenate(p_cols, axis=1))
        acc = pv if first else jnp.concatenate(alphas, axis=1) * acc_ref[z + hh] + pv
        if masked:
            return acc
        acc_ref[z + hh] = acc
        return None

    blocks = [(qi, j) for qi in range(nq) for j in range(qi + 1)]
    bufs = (sa_ref, sb_ref)
    for pair in range(n_heads // 2):
        heads = (2 * pair, 2 * pair + 1)
        issue_scores(0, 0, bufs[0], 0, heads)
        for t, (qi, j) in enumerate(blocks):
            s_in, s_out = bufs[t % 2], bufs[(t + 1) % 2]
            accs = []
            for hh in heads:
                if t + 1 < len(blocks):
                    issue_scores(*blocks[t + 1], s_out, (t + 1) % 2, heads=(hh,))
                accs.append(softmax_pv(s_in, t % 2, hh, vt_ref[0, hh, :, j * tile:(j + 1) * tile], j == qi, j == 0))
            if j == qi:
                outs = [acc * (1.0 / acc[sum_row[hh % 2]:sum_row[hh % 2] + 1, :]) for hh, acc in zip(heads, accs)]
                o_t = jnp.where(keep_even, outs[0], outs[1])
                o_ref[0, qi * tile:(qi + 1) * tile, pair * LANES:(pair + 1) * LANES] = o_t.T.astype(o_ref.dtype)


def _causal_attn(q, k, vt):
    bsz, heads, seq, _ = q.shape
    g = ATTN_PAIRS
    qk_spec = pl.BlockSpec((1, 2 * g, seq, LANES), lambda b, p: (b, p, 0, 0))
    vt_spec = pl.BlockSpec((1, 2 * g, LANES, seq), lambda b, p: (b, p, 0, 0))
    o_spec = pl.BlockSpec((1, seq, g * LANES), lambda b, p: (b, 0, p))
    kern = functools.partial(_attn_kernel, tile=ATTN_TILE)
    return pl.pallas_call(
        kern,
        grid=(bsz, heads // (2 * g)),
        in_specs=[pl.BlockSpec(memory_space=pltpu.SMEM), qk_spec, qk_spec, vt_spec],
        out_specs=o_spec,
        out_shape=jax.ShapeDtypeStruct((bsz, seq, heads * HEAD_DIM), BF16),
        scratch_shapes=[pltpu.VMEM((2 * g, ATTN_TILE, ATTN_TILE), F32),
                        pltpu.VMEM((2 * g, ATTN_TILE, ATTN_TILE), F32),
                        pltpu.VMEM((2, 2 * g, 1, ATTN_TILE), F32),
                        pltpu.VMEM((2 * g, 1, ATTN_TILE), F32),
                        pltpu.VMEM((2 * g, LANES, ATTN_TILE), F32)],
        compiler_params=pltpu.CompilerParams(dimension_semantics=("arbitrary", "arbitrary"),
                                             vmem_limit_bytes=VMEM_LIMIT),
        name="causal_attn",
    )(jnp.zeros((1,), jnp.int32), q, k, vt)


def _mix_ffn_kernel(x_ref, oa_ref, ob_ref, gf_ref, gm_ref, mod_ref, gpost_mix_ref, gpre_ffn_ref,
                    gpost_ffn_ref, wpf_ref, wpm_ref, wout_ref, win_ref, wdown_ref, o_ref):
    tm = x_ref.shape[1]
    mod = lambda k: mod_ref[pl.ds(pl.program_id(0), 1), k * D_MODEL:(k + 1) * D_MODEL]
    gate_mix = mod(2)
    shift = mod(3)
    scale = mod(4)
    gate = mod(5)

    def mix_matmuls(rows):
        pa = _dot(oa_ref[0, rows, :], wpf_ref[...])
        pb = _dot(ob_ref[0, rows, :], wpm_ref[...])
        merged = gf_ref[0, rows, :].astype(F32) * pa + gm_ref[0, rows, :].astype(F32) * pb
        return _dot(merged.astype(BF16), wout_ref[...])

    def mix_residual(rows, y):
        x = x_ref[0, rows, :] + gate_mix * (_rms(y) * gpost_mix_ref[...])
        h = (_rms(x) * gpre_ffn_ref[...] * (1.0 + scale) + shift).astype(BF16)
        return x, h

    chunks = [(lo, min(FFN_CHUNK, D_FF - lo)) for lo in range(0, D_FF, FFN_CHUNK)]

    def up_matmuls(h, lo, w):
        return _dot(h, win_ref[:, lo:lo + w]), _dot(h, win_ref[:, D_FF + lo:D_FF + lo + w])

    def swiglu_matmuls(h):
        y = None
        gu = up_matmuls(h, *chunks[0])
        for c, (lo, w) in enumerate(chunks):
            g, u = gu
            if c + 1 < len(chunks):
                gu = up_matmuls(h, *chunks[c + 1])
            act = (g * jax.nn.sigmoid(g) * u).astype(BF16)
            part = _dot(act, wdown_ref[lo:lo + w, :])
            y = part if y is None else y + part
        return y

    parts = [pl.ds(r0, ROW_PART) for r0 in range(0, tm, ROW_PART)]
    y_mix, xs, y_ffn = {}, {}, {}
    for t in range(len(parts) + 2):
        if t < len(parts):
            y_mix[t] = mix_matmuls(parts[t])
        if 0 <= t - 1 < len(parts):
            xs[t - 1], h = mix_residual(parts[t - 1], y_mix.pop(t - 1))
            y_ffn[t - 1] = swiglu_matmuls(h)
        if 0 <= t - 2 < len(parts):
            o_ref[0, parts[t - 2], :] = xs.pop(t - 2) + gate * (_rms(y_ffn.pop(t - 2)) * gpost_ffn_ref[...])


def _mix_ffn(x, o_a, o_b, gf, gm, mod, g_post_mix, g_pre_ffn, g_post_ffn, w_pf, w_pm, w_out, w_in, w_down):
    bsz, seq, d = x.shape
    tm = MIX_TILE
    row = lambda b, i: (b, i, 0)
    weights = (w_pf, w_pm, w_out, w_in, w_down)
    return pl.pallas_call(
        _mix_ffn_kernel,
        grid=(bsz, seq // tm),
        in_specs=[pl.BlockSpec((1, tm, d), row),
                  pl.BlockSpec((1, tm, FOX_WIDTH), row), pl.BlockSpec((1, tm, FOX_WIDTH), row),
                  pl.BlockSpec((1, tm, d), row), pl.BlockSpec((1, tm, d), row),
                  _const_spec(mod.shape),
                  _const_spec((1, d)), _const_spec((1, d)), _const_spec((1, d))]
                 + [_const_spec(w.shape) for w in weights],
        out_specs=pl.BlockSpec((1, tm, d), row),
        out_shape=jax.ShapeDtypeStruct(x.shape, F32),
        compiler_params=pltpu.CompilerParams(dimension_semantics=("arbitrary", "arbitrary"),
                                             vmem_limit_bytes=VMEM_LIMIT),
        name="mix_ffn",
    )(x, o_a, o_b, gf, gm, mod, g_post_mix.reshape(1, d), g_pre_ffn.reshape(1, d), g_post_ffn.reshape(1, d),
      *weights)


def kernel(x, c, positions, w_ada, b_ada, g_pre_mix, g_post_mix, g_pre_ffn, g_post_ffn, w_in, b_forget,
           g_q_lora, w_uq, g_kv_lora, w_ukv, w_proj_fox, w_proj_mla, w_out, w_ffn_in, w_ffn_out):
    bsz, seq, d = x.shape
    depth = w_ada.shape[0]
    for l in range(depth):
        mod, *w_in_parts = _prep(c, w_ada, b_ada, w_in, l)
        weights, bf3 = _prep_proj_weights(w_in_parts, w_uq[l], w_ukv[l], b_forget[l])
        late = (w_proj_fox, w_proj_mla, w_out, w_ffn_in, w_ffn_out)
        qf, kf, vf, qm, km, vm, gf, gm, *late_bf16 = _token_proj(
            x, mod, g_pre_mix[l], positions, weights, bf3, g_q_lora[l], g_kv_lora[l], late, l)
        o_a = _causal_attn(qf, kf, vf)
        o_b = _causal_attn(qm, km, vm)
        x = _mix_ffn(x, o_a, o_b, gf, gm, mod, g_post_mix[l], g_pre_ffn[l], g_post_ffn[l], *late_bf16)
    return x
```

```python
import functools
import math

import jax
import jax.numpy as jnp
import numpy as np
from jax import lax
from jax.experimental import pallas as pl
from jax.experimental.pallas import tpu as pltpu

D_MODEL = 1024
HEADS = 8
HEAD_DIM = 64
FOX_WIDTH = HEADS * HEAD_DIM
MLA_NOPE = 64
MLA_ROPE = 32
MLA_V = 64
MLA_Q_LORA = 768
MLA_KV_LORA = 256
D_FF = 2816
ROPE_THETA = 10000.0
NORM_EPS = 1e-6
IN_WIDTHS = (FOX_WIDTH, FOX_WIDTH, FOX_WIDTH, HEADS, MLA_Q_LORA, MLA_KV_LORA, MLA_ROPE, D_MODEL, D_MODEL)

LANES = 128
SUBLANES = 8
BF16_ROWS = 16
TOKEN_TILE = 512
MIX_TILE = 512
ROW_PART = 256
ATTN_TILE = 512
ATTN_PAIRS = 2
FFN_CHUNK = 256
VMEM_LIMIT = 56 * 1024 * 1024

_EXP2_FOX = math.log2(math.e) / math.sqrt(HEAD_DIM)
_EXP2_MLA = math.log2(math.e) / math.sqrt(MLA_NOPE + MLA_ROPE)

F32 = jnp.float32
BF16 = jnp.bfloat16


def _const_spec(shape):
    zeros = (0,) * len(shape)
    return pl.BlockSpec(shape, lambda *_: zeros, pipeline_mode=pl.Buffered(1))


def _rms(x):
    return x * lax.rsqrt(jnp.mean(x * x, axis=-1, keepdims=True) + NORM_EPS)


def _dot(a, b):
    return jnp.dot(a, b, preferred_element_type=F32)


def _cumsum_rows(x):
    n = x.shape[0]
    row = lax.broadcasted_iota(jnp.int32, x.shape, 0)
    d = 1
    while d < n:
        x = x + jnp.where(row >= d, pltpu.roll(x, d, axis=0), 0.0)
        d *= 2
    return x


def _rope_tables(pos_row, invf_col):
    n = pos_row.shape[1]
    groups = LANES // MLA_ROPE
    ang = invf_col * pos_row
    c = jnp.cos(ang)
    s = jnp.sin(ang)
    cos_blocks, sin_blocks = [], []
    for t0 in range(0, n, LANES):
        ct = c[:, t0:t0 + LANES]
        st = s[:, t0:t0 + LANES]
        cos_blocks.append(jnp.concatenate([ct, ct] * groups, axis=0).T)
        sin_blocks.append(jnp.concatenate([-st, st] * groups, axis=0).T)
    return jnp.concatenate(cos_blocks, axis=0), jnp.concatenate(sin_blocks, axis=0)


def _rope(x, cos4, sin4, lane):
    half = MLA_ROPE // 2
    swapped = jnp.where((lane & (MLA_ROPE - 1)) < half, pltpu.roll(x, LANES - half, axis=1), pltpu.roll(x, half, axis=1))
    return x * cos4 + swapped * sin4


def _store_values_transposed(vt_ref, rows, v):
    n = v.shape[0]
    row = lax.broadcasted_iota(jnp.int32, (LANES, n), 0)
    for pair in range(HEADS // 2):
        vt = v[:, pair * LANES:(pair + 1) * LANES].T
        even = jnp.where(row < HEAD_DIM, vt, jnp.where(row == HEAD_DIM, 1.0, 0.0))
        odd = jnp.where(row >= HEAD_DIM, vt, jnp.where(row == 0, 1.0, 0.0))
        vt_ref[0, 2 * pair, :, rows] = even.astype(vt_ref.dtype)
        vt_ref[0, 2 * pair + 1, :, rows] = odd.astype(vt_ref.dtype)


def _token_proj_kernel(x_ref, mod_ref, gpre_ref, bforget_ref, gq_ref, gkv_ref, pos_ref, invf_ref,
                       wqkv_ref, wmisc_ref, wcq_ref, wckv_ref, wg_ref, wq_ref, wkv_ref, sel_ref,
                       late0_ref, late1_ref, late2_ref, late3_ref, late4_ref,
                       qf_ref, kf_ref, vf_ref, qm_ref, km_ref, vm_ref, gf_ref, gm_ref,
                       cast0_ref, cast1_ref, cast2_ref, cast3_ref, cast4_ref,
                       carry_ref, cos_ref, sin_ref, *, layer):
    tm = x_ref.shape[1]

    @pl.when(pl.program_id(1) == 0)
    def _():
        carry_ref[...] = jnp.zeros_like(carry_ref)

    pos_row = pos_ref[pl.ds(pl.program_id(0), 1), :].astype(F32)
    cos_ref[...], sin_ref[...] = _rope_tables(pos_row, invf_ref[:, 0:1])

    lane_row = lax.broadcasted_iota(jnp.int32, (1, LANES), 1)
    forget_bias = jnp.zeros((1, LANES), F32)
    for hh in range(HEADS):
        mine = (lane_row == hh) | (lane_row == HEADS + hh) | (lane_row == 2 * HEADS + hh)
        forget_bias = jnp.where(mine, bforget_ref[layer, hh], forget_bias)

    mod = lambda k: mod_ref[pl.ds(pl.program_id(0), 1), k * D_MODEL:(k + 1) * D_MODEL]
    shift = mod(0)
    scale = mod(1)

    def project(rows, n):
        h = (_rms(x_ref[0, rows, :]) * gpre_ref[...] * (1.0 + scale) + shift).astype(BF16)
        lane = lax.broadcasted_iota(jnp.int32, (n, LANES), 1)
        cos4 = cos_ref[rows, :]
        sin4 = sin_ref[rows, :]
        low = lane < HEAD_DIM

        misc = _dot(h, wmisc_ref[...])
        logit = misc + forget_bias
        logf = jnp.minimum(logit, 0.0) - jnp.log(1.0 + jnp.exp(-jnp.abs(logit)))
        cum = _cumsum_rows(logf) + carry_ref[0:1, :]
        carry_ref[0:1, :] = cum[n - 1:n, :]

        cq = _dot(h, wcq_ref[...])
        ckv = _dot(h, wckv_ref[...])
        nq = (_rms(cq) * gq_ref[...]).astype(BF16)
        nkv = (_rms(ckv) * gkv_ref[...]).astype(BF16)
        p_qkv = _dot(h, wqkv_ref[...])
        _store_values_transposed(vf_ref, rows, p_qkv[:, 2 * FOX_WIDTH:])

        qq = _dot(nq, wq_ref[...]) * _EXP2_MLA
        nope_w = HEADS * MLA_NOPE
        q_rope = [_rope(qq[:, nope_w + g * LANES:nope_w + (g + 1) * LANES], cos4, sin4, lane)
                  for g in range(HEADS * MLA_ROPE // LANES)]
        per_group = LANES // MLA_ROPE
        for hh in range(HEADS):
            pair = hh // 2
            nope = qq[:, pair * LANES:(pair + 1) * LANES]
            src_lane = (hh % per_group) * MLA_ROPE
            dst_lane = HEAD_DIM if hh % 2 == 0 else 0
            rope = q_rope[hh // per_group]
            if src_lane != dst_lane:
                rope = pltpu.roll(rope, (dst_lane - src_lane) % LANES, axis=1)
            in_rope = (lane >= dst_lane) & (lane < dst_lane + MLA_ROPE)
            own = low if hh % 2 == 0 else jnp.logical_not(low)
            qm_ref[0, hh, rows, :] = jnp.where(own, nope, jnp.where(in_rope, rope, 0.0)).astype(qm_ref.dtype)
        kv = _dot(nkv, wkv_ref[...])
        _store_values_transposed(vm_ref, rows, kv[:, nope_w:])
        in_rope = (lane >= MLA_NOPE) & (lane < MLA_NOPE + MLA_ROPE)
        kpe_even = jnp.where(in_rope, _rope(misc, cos4, sin4, lane), 0.0)
        kpe_odd = pltpu.roll(kpe_even, LANES - HEAD_DIM, axis=1)
        for hh in range(HEADS):
            pair = hh // 2
            nope = kv[:, pair * LANES:(pair + 1) * LANES]
            k_full = jnp.where(low, nope, kpe_even) if hh % 2 == 0 else jnp.where(low, kpe_odd, nope)
            km_ref[0, hh, rows, :] = k_full.astype(km_ref.dtype)

        a = cum * math.log2(math.e)
        a_hi = a.astype(BF16).astype(F32)
        r1 = a - a_hi
        a_mid = r1.astype(BF16).astype(F32)
        a_lo = (r1 - a_mid).astype(BF16).astype(F32)
        z = jnp.where(lane < HEADS, a_hi,
                      jnp.where(lane < 2 * HEADS, a_mid,
                                jnp.where(lane < 3 * HEADS, a_lo,
                                          jnp.where(lane == 3 * HEADS, 1.0, 0.0))))
        aug = _dot(z.astype(BF16), sel_ref[...])
        pairs = HEADS // 2
        for hh in range(HEADS):
            pair = hh // 2
            keep = (lane < HEAD_DIM) if hh % 2 == 0 else (lane >= HEAD_DIM)
            xq = p_qkv[:, pair * LANES:(pair + 1) * LANES]
            xk = p_qkv[:, FOX_WIDTH + pair * LANES:FOX_WIDTH + (pair + 1) * LANES]
            aq = aug[:, pair * LANES:(pair + 1) * LANES]
            ak = aug[:, (pairs + pair) * LANES:(pairs + pair + 1) * LANES]
            qf_ref[0, hh, rows, :] = jnp.where(keep, xq * _EXP2_FOX, aq).astype(qf_ref.dtype)
            kf_ref[0, hh, rows, :] = jnp.where(keep, xk, ak).astype(kf_ref.dtype)

        for gi, ref in enumerate((gf_ref, gm_ref)):
            g = _dot(h, wg_ref[:, gi * D_MODEL:(gi + 1) * D_MODEL])
            ref[0, rows, :] = jax.nn.sigmoid(g).astype(ref.dtype)

    n = tm // 2
    for r0 in (0, n):
        project(pl.ds(r0, n), n)

    for src, dst in ((late0_ref, cast0_ref), (late1_ref, cast1_ref), (late2_ref, cast2_ref),
                     (late3_ref, cast3_ref), (late4_ref, cast4_ref)):
        dst[...] = src[0].astype(dst.dtype)


_IN_OFFSETS = tuple(int(v) for v in np.cumsum((0,) + IN_WIDTHS))


def _regroup_head_columns(w, first, second):
    n = w.shape[1]
    assert n == HEADS * (first + second) and first & (first - 1) == 0 and second & (second - 1) == 0
    col = lax.broadcasted_iota(jnp.int32, (1, n), 1)
    split = HEADS * first
    k = col - split
    src_first = (col >> int(math.log2(first))) * (first + second) + (col & (first - 1))
    src_second = (k >> int(math.log2(second))) * (first + second) + first + (k & (second - 1))
    src = jnp.where(col < split, src_first, src_second)
    perm = jnp.where(lax.broadcasted_iota(jnp.int32, (n, n), 0) == src, 1.0, 0.0).astype(BF16)
    return _dot(w.astype(BF16), perm).astype(BF16)


def _prep_kernel(c_ref, wada_ref, bada_ref, wt_ref, wuq_ref, wukv_ref,
                 mod_ref, qkv_ref, misc_ref, cq_ref, ckv_ref, g_ref, wq_ref, wkv_ref):
    c = c_ref[...]
    sc = c * jax.nn.sigmoid(c)
    mod_ref[...] = _dot(sc.astype(BF16), wada_ref[0].astype(BF16)) + bada_ref[0]

    wt = wt_ref[0]
    cols = wt.shape[1]
    o = _IN_OFFSETS
    qkv_ref[...] = wt[o[0]:o[3], :].T.astype(BF16)
    cq_ref[...] = wt[o[4]:o[5], :].T.astype(BF16)
    ckv_ref[...] = wt[o[5]:o[6], :].T.astype(BF16)
    g_ref[...] = wt[o[7]:o[9], :].T.astype(BF16)
    f = wt[o[3]:o[4], :]
    kr = wt[o[6]:o[7], :]
    misc_t = jnp.concatenate([f, f, f, jnp.zeros((MLA_NOPE - 3 * HEADS, cols), F32), kr,
                              jnp.zeros((LANES - MLA_NOPE - MLA_ROPE, cols), F32)], axis=0)
    misc_ref[...] = misc_t.T.astype(BF16)

    wq_ref[...] = _regroup_head_columns(wuq_ref[0], MLA_NOPE, MLA_ROPE)
    wkv_ref[...] = _regroup_head_columns(wukv_ref[0], MLA_NOPE, MLA_V)


def _prep(c, w_ada_all, b_ada_all, w_in_all, w_uq_all, w_ukv_all, layer):
    bsz, d = c.shape
    depth, _, n_mod = w_ada_all.shape
    n_in = w_in_all.shape[2]
    cols = 2 * LANES
    steps = d // cols
    tn = n_mod // steps
    assert d % cols == 0 and n_mod % steps == 0 and tn % LANES == 0
    widths = (3 * FOX_WIDTH, LANES, MLA_Q_LORA, MLA_KV_LORA, 2 * D_MODEL)
    up_proj = (w_uq_all, w_ukv_all)
    up_rows = [w.shape[1] // steps for w in up_proj]
    assert all(w.shape[1] % steps == 0 and r % BF16_ROWS == 0 for w, r in zip(up_proj, up_rows))
    return pl.pallas_call(
        _prep_kernel,
        grid=(steps,),
        in_specs=[pl.BlockSpec((bsz, d), lambda j: (0, 0)),
                  pl.BlockSpec((1, d, tn), lambda j: (layer, 0, j)),
                  pl.BlockSpec((1, 1, tn), lambda j: (layer, 0, j)),
                  pl.BlockSpec((1, n_in, cols), lambda j: (layer, 0, j))]
                 + [pl.BlockSpec((1, r, w.shape[2]), lambda j: (layer, j, 0)) for w, r in zip(up_proj, up_rows)],
        out_specs=[pl.BlockSpec((bsz, tn), lambda j: (0, j))]
                  + [pl.BlockSpec((cols, w), lambda j: (j, 0)) for w in widths]
                  + [pl.BlockSpec((r, w.shape[2]), lambda j: (j, 0)) for w, r in zip(up_proj, up_rows)],
        out_shape=[jax.ShapeDtypeStruct((bsz, n_mod), F32)]
                  + [jax.ShapeDtypeStruct((d, w), BF16) for w in widths]
                  + [jax.ShapeDtypeStruct(w.shape[1:], BF16) for w in up_proj],
        name="prep",
    )(c, w_ada_all, b_ada_all.reshape(depth, 1, n_mod), jnp.transpose(w_in_all, (0, 2, 1)), *up_proj)


def _decay_selector():
    pairs = HEADS // 2
    sel = np.zeros((LANES, 2 * pairs * LANES), np.float32)
    for hh in range(HEADS):
        base_q = (hh // 2) * LANES + (HEAD_DIM if hh % 2 == 0 else 0)
        base_k = (pairs + hh // 2) * LANES + (HEAD_DIM if hh % 2 == 0 else 0)
        for piece in range(3):
            sel[piece * HEADS + hh, base_q + piece] = 1.0
            sel[3 * HEADS, base_q + 3 + piece] = 1.0
            sel[3 * HEADS, base_k + piece] = 1.0
            sel[piece * HEADS + hh, base_k + 3 + piece] = -1.0
    return jnp.asarray(sel, BF16)


def _cast_block_rows(n_rows, n_steps):
    rb = -(-n_rows // n_steps)
    rb += -rb % BF16_ROWS
    while n_rows % rb:
        rb += BF16_ROWS
    return rb


def _token_proj(x, mod, g_pre, positions, weights, b_forget, g_q, g_kv, late_weights, layer):
    bsz, seq, d = x.shape
    tm = TOKEN_TILE
    steps_per_batch = seq // tm
    n_steps = bsz * steps_per_batch
    late_in, late_out, late_shapes = [], [], []
    for w in late_weights:
        _, rows, cols = w.shape
        rb = _cast_block_rows(rows, n_steps)
        last = rows // rb - 1
        late_in.append(pl.BlockSpec(
            (1, rb, cols), lambda b, i, last=last: (layer, jnp.minimum(b * steps_per_batch + i, last), 0)))
        late_out.append(pl.BlockSpec(
            (rb, cols), lambda b, i, last=last: (jnp.minimum(b * steps_per_batch + i, last), 0)))
        late_shapes.append(jax.ShapeDtypeStruct((rows, cols), BF16))
    sel = _decay_selector()
    half = MLA_ROPE // 2
    inv_freq = 1.0 / (ROPE_THETA ** (np.arange(0, MLA_ROPE, 2, dtype=np.float32) / MLA_ROPE))
    invf = jnp.asarray(np.tile(inv_freq.astype(np.float32)[:, None], (1, LANES)))
    row = lambda b, i: (b, i, 0)
    head = lambda b, i: (b, 0, i, 0)
    head_shape = jax.ShapeDtypeStruct((bsz, HEADS, seq, LANES), BF16)
    head_spec = pl.BlockSpec((1, HEADS, tm, LANES), head)
    v_shape = jax.ShapeDtypeStruct((bsz, HEADS, LANES, seq), BF16)
    v_spec = pl.BlockSpec((1, HEADS, LANES, tm), lambda b, i: (b, 0, 0, i))
    g_shape = jax.ShapeDtypeStruct((bsz, seq, d), BF16)
    g_spec = pl.BlockSpec((1, tm, d), row)
    return pl.pallas_call(
        functools.partial(_token_proj_kernel, layer=layer),
        grid=(bsz, seq // tm),
        in_specs=[pl.BlockSpec((1, tm, d), row),
                  _const_spec(mod.shape),
                  _const_spec((1, d)), pl.BlockSpec(memory_space=pltpu.SMEM),
                  _const_spec((1, MLA_Q_LORA)), _const_spec((1, MLA_KV_LORA)),
                  pl.BlockSpec((bsz, tm), lambda b, i: (0, i)), _const_spec((half, LANES))]
                 + [_const_spec(w.shape) for w in weights] + [_const_spec(sel.shape)] + late_in,
        out_specs=[head_spec, head_spec, v_spec, head_spec, head_spec, v_spec, g_spec, g_spec] + late_out,
        out_shape=[head_shape, head_shape, v_shape, head_shape, head_shape, v_shape, g_shape, g_shape]
                  + late_shapes,
        scratch_shapes=[pltpu.VMEM((SUBLANES, LANES), F32),
                        pltpu.VMEM((tm, LANES), F32),
                        pltpu.VMEM((tm, LANES), F32)],
        compiler_params=pltpu.CompilerParams(dimension_semantics=("arbitrary", "arbitrary"),
                                             vmem_limit_bytes=VMEM_LIMIT),
        name="token_proj",
    )(x, mod, g_pre.reshape(1, d), b_forget, g_q.reshape(1, -1), g_kv.reshape(1, -1), positions, invf,
      *weights, sel, *late_weights)


def _attn_kernel(zero_ref, q_ref, k_ref, vt_ref, o_ref, sa_ref, sb_ref, smax_ref, m_ref, acc_ref, *, tile):
    seq = vt_ref.shape[3]
    nq = seq // tile
    n_heads = q_ref.shape[1]
    z = zero_ref[0]
    half = tile // 2
    nt = (((1,), (1,)), ((), ()))
    vrow = lax.broadcasted_iota(jnp.int32, (LANES, tile), 0)
    sum_row = (HEAD_DIM, 0)
    keep_even = vrow < HEAD_DIM
    krow = lax.broadcasted_iota(jnp.int32, (half, half), 0)
    qcol = lax.broadcasted_iota(jnp.int32, (half, half), 1)
    causal_half = krow <= qcol

    def issue_scores(qi, j, s_out, slot, heads=None):
        for hh in (range(n_heads) if heads is None else heads):
            q = q_ref[0, hh, qi * tile:(qi + 1) * tile, :]
            k = k_ref[0, hh, j * tile:(j + 1) * tile, :]
            if j == qi:
                top = lax.dot_general(k[:half], q, nt, preferred_element_type=F32)
                top_left = jnp.where(causal_half, top[:, :half], -jnp.inf)
                bottom = jnp.where(causal_half,
                                   lax.dot_general(k[half:], q[half:], nt, preferred_element_type=F32), -jnp.inf)
                s_out[z + hh, :half, :half] = top_left
                s_out[z + hh, :half, half:] = top[:, half:]
                s_out[z + hh, half:, half:] = bottom
                smax_ref[slot, hh, :, :half] = jnp.max(top_left, axis=0, keepdims=True)
                smax_ref[slot, hh, :, half:] = jnp.maximum(jnp.max(top[:, half:], axis=0, keepdims=True),
                                                           jnp.max(bottom, axis=0, keepdims=True))
            else:
                s = lax.dot_general(k, q, nt, preferred_element_type=F32)
                s_out[z + hh] = s
                smax_ref[slot, hh] = jnp.max(s, axis=0, keepdims=True)

    def softmax_pv(s_in, slot, hh, vth, masked, first):
        alphas, p_cols = [], []
        for c0 in range(0, tile, LANES):
            n_keys = c0 + LANES if masked else tile
            n_fill = tile if (c0 >= half or not masked) else half
            m_new = smax_ref[slot, hh, :, c0:c0 + LANES]
            if not first:
                m_old = m_ref[hh, :, c0:c0 + LANES]
                m_new = jnp.maximum(m_old, m_new)
                alphas.append(jnp.exp2(m_old - m_new))
            parts = [jnp.exp2(s_in[z + hh, r0:r0 + LANES, c0:c0 + LANES] - m_new).astype(BF16)
                     for r0 in range(0, n_keys, LANES)]
            parts += [jnp.zeros((LANES, LANES), BF16)] * ((n_fill - n_keys) // LANES)
            p_cols.append(jnp.concatenate(parts, axis=0))
            if not masked:
                m_ref[hh, :, c0:c0 + LANES] = m_new
        n_left = half // LANES
        if masked:
            pv = jnp.concatenate([_dot(vth[:, :half], jnp.concatenate(p_cols[:n_left], axis=1)),
                                  _dot(vth, jnp.concatenate(p_cols[n_left:], axis=1))], axis=1)
        else:
            pv = _dot(vth, jnp.concatenate(p_cols, axis=1))
        acc = pv if first else jnp.concatenate(alphas, axis=1) * acc_ref[z + hh] + pv
        if masked:
            return acc
        acc_ref[z + hh] = acc
        return None

    blocks = [(qi, j) for qi in range(nq) for j in range(qi + 1)]
    bufs = (sa_ref, sb_ref)
    for pair in range(n_heads // 2):
        heads = (2 * pair, 2 * pair + 1)
        issue_scores(0, 0, bufs[0], 0, heads)
        for t, (qi, j) in enumerate(blocks):
            s_in, s_out = bufs[t % 2], bufs[(t + 1) % 2]
            accs = []
            for hh in heads:
                if t + 1 < len(blocks):
                    issue_scores(*blocks[t + 1], s_out, (t + 1) % 2, heads=(hh,))
                accs.append(softmax_pv(s_in, t % 2, hh, vt_ref[0, hh, :, j * tile:(j + 1) * tile], j == qi, j == 0))
            if j == qi:
                outs = [acc * (1.0 / acc[sum_row[hh % 2]:sum_row[hh % 2] + 1, :]) for hh, acc in zip(heads, accs)]
                o_t = jnp.where(keep_even, outs[0], outs[1])
                o_ref[0, qi * tile:(qi + 1) * tile, pair * LANES:(pair + 1) * LANES] = o_t.T.astype(o_ref.dtype)


def _causal_attn(q, k, vt):
    bsz, heads, seq, _ = q.shape
    g = ATTN_PAIRS
    qk_spec = pl.BlockSpec((1, 2 * g, seq, LANES), lambda b, p: (b, p, 0, 0))
    vt_spec = pl.BlockSpec((1, 2 * g, LANES, seq), lambda b, p: (b, p, 0, 0))
    o_spec = pl.BlockSpec((1, seq, g * LANES), lambda b, p: (b, 0, p))
    kern = functools.partial(_attn_kernel, tile=ATTN_TILE)
    return pl.pallas_call(
        kern,
        grid=(bsz, heads // (2 * g)),
        in_specs=[pl.BlockSpec(memory_space=pltpu.SMEM), qk_spec, qk_spec, vt_spec],
        out_specs=o_spec,
        out_shape=jax.ShapeDtypeStruct((bsz, seq, heads * HEAD_DIM), BF16),
        scratch_shapes=[pltpu.VMEM((2 * g, ATTN_TILE, ATTN_TILE), F32),
                        pltpu.VMEM((2 * g, ATTN_TILE, ATTN_TILE), F32),
                        pltpu.VMEM((2, 2 * g, 1, ATTN_TILE), F32),
                        pltpu.VMEM((2 * g, 1, ATTN_TILE), F32),
                        pltpu.VMEM((2 * g, LANES, ATTN_TILE), F32)],
        compiler_params=pltpu.CompilerParams(dimension_semantics=("arbitrary", "arbitrary"),
                                             vmem_limit_bytes=VMEM_LIMIT),
        name="causal_attn",
    )(jnp.zeros((1,), jnp.int32), q, k, vt)


def _mix_ffn_kernel(x_ref, oa_ref, ob_ref, gf_ref, gm_ref, mod_ref, gpost_mix_ref, gpre_ffn_ref,
                    gpost_ffn_ref, wpf_ref, wpm_ref, wout_ref, win_ref, wdown_ref, o_ref):
    tm = x_ref.shape[1]
    mod = lambda k: mod_ref[pl.ds(pl.program_id(0), 1), k * D_MODEL:(k + 1) * D_MODEL]
    shift = mod(3)
    mix_gain = mod(2) * gpost_mix_ref[...]
    ffn_in_gain = gpre_ffn_ref[...] * (1.0 + mod(4))
    ffn_gain = mod(5) * gpost_ffn_ref[...]

    def merge_matmuls(rows):
        pa = _dot(oa_ref[0, rows, :], wpf_ref[...])
        pb = _dot(ob_ref[0, rows, :], wpm_ref[...])
        merged = gf_ref[0, rows, :].astype(F32) * pa + gm_ref[0, rows, :].astype(F32) * pb
        return merged.astype(BF16)

    def mix_residual(rows, y):
        x = x_ref[0, rows, :] + _rms(y) * mix_gain
        h = (_rms(x) * ffn_in_gain + shift).astype(BF16)
        return x, h

    chunks = [(lo, min(FFN_CHUNK, D_FF - lo)) for lo in range(0, D_FF, FFN_CHUNK)]

    def up_matmuls(h, lo, w):
        return _dot(h, win_ref[:, lo:lo + w]), _dot(h, win_ref[:, D_FF + lo:D_FF + lo + w])

    def swiglu_matmuls(h):
        y = None
        gu = up_matmuls(h, *chunks[0])
        for c, (lo, w) in enumerate(chunks):
            g, u = gu
            if c + 1 < len(chunks):
                gu = up_matmuls(h, *chunks[c + 1])
            act = (g * jax.nn.sigmoid(g) * u).astype(BF16)
            part = _dot(act, wdown_ref[lo:lo + w, :])
            y = part if y is None else y + part
        return y

    def store_output(rows, x, y):
        o_ref[0, rows, :] = x + _rms(y) * ffn_gain

    parts = [pl.ds(r0, ROW_PART) for r0 in range(0, tm, ROW_PART)]
    merged = [merge_matmuls(rows) for rows in parts]
    y_mix = [_dot(m, wout_ref[...]) for m in merged]
    pending = None
    for rows, y in zip(parts, y_mix):
        x, h = mix_residual(rows, y)
        y_ffn = swiglu_matmuls(h)
        if pending is not None:
            store_output(*pending)
        pending = (rows, x, y_ffn)
    store_output(*pending)


def _mix_ffn(x, o_a, o_b, gf, gm, mod, g_post_mix, g_pre_ffn, g_post_ffn, w_pf, w_pm, w_out, w_in, w_down):
    bsz, seq, d = x.shape
    tm = MIX_TILE
    row = lambda b, i: (b, i, 0)
    weights = (w_pf, w_pm, w_out, w_in, w_down)
    return pl.pallas_call(
        _mix_ffn_kernel,
        grid=(bsz, seq // tm),
        in_specs=[pl.BlockSpec((1, tm, d), row),
                  pl.BlockSpec((1, tm, FOX_WIDTH), row), pl.BlockSpec((1, tm, FOX_WIDTH), row),
                  pl.BlockSpec((1, tm, d), row), pl.BlockSpec((1, tm, d), row),
                  _const_spec(mod.shape),
                  _const_spec((1, d)), _const_spec((1, d)), _const_spec((1, d))]
                 + [_const_spec(w.shape) for w in weights],
        out_specs=pl.BlockSpec((1, tm, d), row),
        out_shape=jax.ShapeDtypeStruct(x.shape, F32),
        compiler_params=pltpu.CompilerParams(dimension_semantics=("arbitrary", "arbitrary"),
                                             vmem_limit_bytes=VMEM_LIMIT),
        name="mix_ffn",
    )(x, o_a, o_b, gf, gm, mod, g_post_mix.reshape(1, d), g_pre_ffn.reshape(1, d), g_post_ffn.reshape(1, d),
      *weights)


def kernel(x, c, positions, w_ada, b_ada, g_pre_mix, g_post_mix, g_pre_ffn, g_post_ffn, w_in, b_forget,
           g_q_lora, w_uq, g_kv_lora, w_ukv, w_proj_fox, w_proj_mla, w_out, w_ffn_in, w_ffn_out):
    bsz, seq, d = x.shape
    depth = w_ada.shape[0]
    for l in range(depth):
        mod, *weights = _prep(c, w_ada, b_ada, w_in, w_uq, w_ukv, l)
        late = (w_proj_fox, w_proj_mla, w_out, w_ffn_in, w_ffn_out)
        qf, kf, vf, qm, km, vm, gf, gm, *late_bf16 = _token_proj(
            x, mod, g_pre_mix[l], positions, weights, b_forget, g_q_lora[l], g_kv_lora[l], late, l)
        o_a = _causal_attn(qf, kf, vf)
        o_b = _causal_attn(qm, km, vm)
        x = _mix_ffn(x, o_a, o_b, gf, gm, mod, g_post_mix[l], g_pre_ffn[l], g_post_ffn[l], *late_bf16)
    return x
```

```python
import functools
import math

import jax
import jax.numpy as jnp
import numpy as np
from jax import lax
from jax.experimental import pallas as pl
from jax.experimental.pallas import tpu as pltpu

D_MODEL = 1024
HEADS = 8
HEAD_DIM = 64
FOX_WIDTH = HEADS * HEAD_DIM
MLA_NOPE = 64
MLA_ROPE = 32
MLA_V = 64
MLA_Q_LORA = 768
MLA_KV_LORA = 256
D_FF = 2816
ROPE_THETA = 10000.0
NORM_EPS = 1e-6
IN_WIDTHS = (FOX_WIDTH, FOX_WIDTH, FOX_WIDTH, HEADS, MLA_Q_LORA, MLA_KV_LORA, MLA_ROPE, D_MODEL, D_MODEL)

LANES = 128
SUBLANES = 8
BF16_ROWS = 16
TOKEN_TILE = 512
MIX_TILE = 512
ROW_PART = 256
ATTN_TILE = 512
ATTN_PAIRS = 2
FFN_CHUNK = 256
VMEM_LIMIT = 56 * 1024 * 1024

_EXP2_FOX = math.log2(math.e) / math.sqrt(HEAD_DIM)
_EXP2_MLA = math.log2(math.e) / math.sqrt(MLA_NOPE + MLA_ROPE)

F32 = jnp.float32
BF16 = jnp.bfloat16


def _const_spec(shape):
    zeros = (0,) * len(shape)
    return pl.BlockSpec(shape, lambda *_: zeros, pipeline_mode=pl.Buffered(1))


def _rms(x):
    return x * lax.rsqrt(jnp.mean(x * x, axis=-1, keepdims=True) + NORM_EPS)


def _dot(a, b):
    return jnp.dot(a, b, preferred_element_type=F32)


def _cumsum_rows(x):
    n = x.shape[0]
    row = lax.broadcasted_iota(jnp.int32, x.shape, 0)
    d = 1
    while d < n:
        x = x + jnp.where(row >= d, pltpu.roll(x, d, axis=0), 0.0)
        d *= 2
    return x


def _rope_tables(pos_row, invf_col):
    n = pos_row.shape[1]
    groups = LANES // MLA_ROPE
    ang = invf_col * pos_row
    c = jnp.cos(ang)
    s = jnp.sin(ang)
    cos_blocks, sin_blocks = [], []
    for t0 in range(0, n, LANES):
        ct = c[:, t0:t0 + LANES]
        st = s[:, t0:t0 + LANES]
        cos_blocks.append(jnp.concatenate([ct, ct] * groups, axis=0).T)
        sin_blocks.append(jnp.concatenate([-st, st] * groups, axis=0).T)
    return jnp.concatenate(cos_blocks, axis=0), jnp.concatenate(sin_blocks, axis=0)


def _rope(x, cos4, sin4, lane):
    half = MLA_ROPE // 2
    swapped = jnp.where((lane & (MLA_ROPE - 1)) < half, pltpu.roll(x, LANES - half, axis=1), pltpu.roll(x, half, axis=1))
    return x * cos4 + swapped * sin4


def _store_values_transposed(vt_ref, rows, v):
    n = v.shape[0]
    row = lax.broadcasted_iota(jnp.int32, (LANES, n), 0)
    for pair in range(HEADS // 2):
        vt = v[:, pair * LANES:(pair + 1) * LANES].T
        even = jnp.where(row < HEAD_DIM, vt, jnp.where(row == HEAD_DIM, 1.0, 0.0))
        odd = jnp.where(row >= HEAD_DIM, vt, jnp.where(row == 0, 1.0, 0.0))
        vt_ref[0, 2 * pair, :, rows] = even.astype(vt_ref.dtype)
        vt_ref[0, 2 * pair + 1, :, rows] = odd.astype(vt_ref.dtype)


def _token_proj_kernel(x_ref, mod_ref, gpre_ref, bforget_ref, gq_ref, gkv_ref, pos_ref, invf_ref,
                       wqkv_ref, wmisc_ref, wcq_ref, wckv_ref, wg_ref, wq_ref, wkv_ref, sel_ref,
                       late0_ref, late1_ref, late2_ref, late3_ref, late4_ref,
                       qf_ref, kf_ref, vf_ref, qm_ref, km_ref, vm_ref, gf_ref, gm_ref,
                       cast0_ref, cast1_ref, cast2_ref, cast3_ref, cast4_ref,
                       carry_ref, cos_ref, sin_ref, *, layer):
    tm = x_ref.shape[1]

    @pl.when(pl.program_id(1) == 0)
    def _():
        carry_ref[...] = jnp.zeros_like(carry_ref)

    pos_row = pos_ref[pl.ds(pl.program_id(0), 1), :].astype(F32)
    cos_ref[...], sin_ref[...] = _rope_tables(pos_row, invf_ref[:, 0:1])

    lane_row = lax.broadcasted_iota(jnp.int32, (1, LANES), 1)
    forget_bias = jnp.zeros((1, LANES), F32)
    for hh in range(HEADS):
        mine = (lane_row == hh) | (lane_row == HEADS + hh) | (lane_row == 2 * HEADS + hh)
        forget_bias = jnp.where(mine, bforget_ref[layer, hh], forget_bias)

    mod = lambda k: mod_ref[pl.ds(pl.program_id(0), 1), k * D_MODEL:(k + 1) * D_MODEL]
    shift = mod(0)
    gain = gpre_ref[...] * (1.0 + mod(1))

    def project(rows, n):
        h = (_rms(x_ref[0, rows, :]) * gain + shift).astype(BF16)
        lane = lax.broadcasted_iota(jnp.int32, (n, LANES), 1)
        cos4 = cos_ref[rows, :]
        sin4 = sin_ref[rows, :]
        low = lane < HEAD_DIM

        misc = _dot(h, wmisc_ref[...])
        logit = misc + forget_bias
        logf = jnp.minimum(logit, 0.0) - jnp.log(1.0 + jnp.exp(-jnp.abs(logit)))
        cum = _cumsum_rows(logf) + carry_ref[0:1, :]
        carry_ref[0:1, :] = cum[n - 1:n, :]

        cq = _dot(h, wcq_ref[...])
        ckv = _dot(h, wckv_ref[...])
        nq = (_rms(cq) * gq_ref[...]).astype(BF16)
        nkv = (_rms(ckv) * gkv_ref[...]).astype(BF16)
        p_qkv = _dot(h, wqkv_ref[...])
        _store_values_transposed(vf_ref, rows, p_qkv[:, 2 * FOX_WIDTH:])

        qq = _dot(nq, wq_ref[...]) * _EXP2_MLA
        nope_w = HEADS * MLA_NOPE
        q_rope = [_rope(qq[:, nope_w + g * LANES:nope_w + (g + 1) * LANES], cos4, sin4, lane)
                  for g in range(HEADS * MLA_ROPE // LANES)]
        per_group = LANES // MLA_ROPE
        for hh in range(HEADS):
            pair = hh // 2
            nope = qq[:, pair * LANES:(pair + 1) * LANES]
            src_lane = (hh % per_group) * MLA_ROPE
            dst_lane = HEAD_DIM if hh % 2 == 0 else 0
            rope = q_rope[hh // per_group]
            if src_lane != dst_lane:
                rope = pltpu.roll(rope, (dst_lane - src_lane) % LANES, axis=1)
            in_rope = (lane >= dst_lane) & (lane < dst_lane + MLA_ROPE)
            own = low if hh % 2 == 0 else jnp.logical_not(low)
            qm_ref[0, hh, rows, :] = jnp.where(own, nope, jnp.where(in_rope, rope, 0.0)).astype(qm_ref.dtype)
        kv = _dot(nkv, wkv_ref[...])
        _store_values_transposed(vm_ref, rows, kv[:, nope_w:])
        in_rope = (lane >= MLA_NOPE) & (lane < MLA_NOPE + MLA_ROPE)
        kpe_even = jnp.where(in_rope, _rope(misc, cos4, sin4, lane), 0.0)
        kpe_odd = pltpu.roll(kpe_even, LANES - HEAD_DIM, axis=1)
        for hh in range(HEADS):
            pair = hh // 2
            nope = kv[:, pair * LANES:(pair + 1) * LANES]
            k_full = jnp.where(low, nope, kpe_even) if hh % 2 == 0 else jnp.where(low, kpe_odd, nope)
            km_ref[0, hh, rows, :] = k_full.astype(km_ref.dtype)

        a = cum * math.log2(math.e)
        a_hi = a.astype(BF16).astype(F32)
        r1 = a - a_hi
        a_mid = r1.astype(BF16).astype(F32)
        a_lo = (r1 - a_mid).astype(BF16).astype(F32)
        z = jnp.where(lane < HEADS, a_hi,
                      jnp.where(lane < 2 * HEADS, a_mid,
                                jnp.where(lane < 3 * HEADS, a_lo,
                                          jnp.where(lane == 3 * HEADS, 1.0, 0.0))))
        aug = _dot(z.astype(BF16), sel_ref[...])
        pairs = HEADS // 2
        for hh in range(HEADS):
            pair = hh // 2
            keep = (lane < HEAD_DIM) if hh % 2 == 0 else (lane >= HEAD_DIM)
            xq = p_qkv[:, pair * LANES:(pair + 1) * LANES]
            xk = p_qkv[:, FOX_WIDTH + pair * LANES:FOX_WIDTH + (pair + 1) * LANES]
            aq = aug[:, pair * LANES:(pair + 1) * LANES]
            ak = aug[:, (pairs + pair) * LANES:(pairs + pair + 1) * LANES]
            qf_ref[0, hh, rows, :] = jnp.where(keep, xq * _EXP2_FOX, aq).astype(qf_ref.dtype)
            kf_ref[0, hh, rows, :] = jnp.where(keep, xk, ak).astype(kf_ref.dtype)

        for gi, ref in enumerate((gf_ref, gm_ref)):
            g = _dot(h, wg_ref[:, gi * D_MODEL:(gi + 1) * D_MODEL])
            ref[0, rows, :] = jax.nn.sigmoid(g).astype(ref.dtype)

    n = tm // 2
    for r0 in (0, n):
        project(pl.ds(r0, n), n)

    for src, dst in ((late0_ref, cast0_ref), (late1_ref, cast1_ref), (late2_ref, cast2_ref),
                     (late3_ref, cast3_ref), (late4_ref, cast4_ref)):
        dst[...] = src[0].astype(dst.dtype)


_IN_OFFSETS = tuple(int(v) for v in np.cumsum((0,) + IN_WIDTHS))


def _regroup_head_columns(w, first, second):
    n = w.shape[1]
    assert n == HEADS * (first + second) and first & (first - 1) == 0 and second & (second - 1) == 0
    col = lax.broadcasted_iota(jnp.int32, (1, n), 1)
    split = HEADS * first
    k = col - split
    src_first = (col >> int(math.log2(first))) * (first + second) + (col & (first - 1))
    src_second = (k >> int(math.log2(second))) * (first + second) + first + (k & (second - 1))
    src = jnp.where(col < split, src_first, src_second)
    perm = jnp.where(lax.broadcasted_iota(jnp.int32, (n, n), 0) == src, 1.0, 0.0).astype(BF16)
    return _dot(w.astype(BF16), perm).astype(BF16)


def _prep_kernel(c_ref, wada_ref, bada_ref, wt_ref, wuq_ref, wukv_ref,
                 mod_ref, qkv_ref, misc_ref, cq_ref, ckv_ref, g_ref, wq_ref, wkv_ref):
    c = c_ref[...]
    sc = c * jax.nn.sigmoid(c)
    mod_ref[...] = _dot(sc.astype(BF16), wada_ref[0].astype(BF16)) + bada_ref[0]

    wt = wt_ref[0]
    cols = wt.shape[1]
    o = _IN_OFFSETS
    qkv_ref[...] = wt[o[0]:o[3], :].T.astype(BF16)
    cq_ref[...] = wt[o[4]:o[5], :].T.astype(BF16)
    ckv_ref[...] = wt[o[5]:o[6], :].T.astype(BF16)
    g_ref[...] = wt[o[7]:o[9], :].T.astype(BF16)
    f = wt[o[3]:o[4], :]
    kr = wt[o[6]:o[7], :]
    misc_t = jnp.concatenate([f, f, f, jnp.zeros((MLA_NOPE - 3 * HEADS, cols), F32), kr,
                              jnp.zeros((LANES - MLA_NOPE - MLA_ROPE, cols), F32)], axis=0)
    misc_ref[...] = misc_t.T.astype(BF16)

    wq_ref[...] = _regroup_head_columns(wuq_ref[0], MLA_NOPE, MLA_ROPE)
    wkv_ref[...] = _regroup_head_columns(wukv_ref[0], MLA_NOPE, MLA_V)


def _prep(c, w_ada_all, b_ada_all, w_in_all, w_uq_all, w_ukv_all, layer):
    bsz, d = c.shape
    depth, _, n_mod = w_ada_all.shape
    n_in = w_in_all.shape[2]
    cols = 2 * LANES
    steps = d // cols
    tn = n_mod // steps
    assert d % cols == 0 and n_mod % steps == 0 and tn % LANES == 0
    widths = (3 * FOX_WIDTH, LANES, MLA_Q_LORA, MLA_KV_LORA, 2 * D_MODEL)
    up_proj = (w_uq_all, w_ukv_all)
    up_rows = [w.shape[1] // steps for w in up_proj]
    assert all(w.shape[1] % steps == 0 and r % BF16_ROWS == 0 for w, r in zip(up_proj, up_rows))
    return pl.pallas_call(
        _prep_kernel,
        grid=(steps,),
        in_specs=[pl.BlockSpec((bsz, d), lambda j: (0, 0)),
                  pl.BlockSpec((1, d, tn), lambda j: (layer, 0, j)),
                  pl.BlockSpec((1, 1, tn), lambda j: (layer, 0, j)),
                  pl.BlockSpec((1, n_in, cols), lambda j: (layer, 0, j))]
                 + [pl.BlockSpec((1, r, w.shape[2]), lambda j: (layer, j, 0)) for w, r in zip(up_proj, up_rows)],
        out_specs=[pl.BlockSpec((bsz, tn), lambda j: (0, j))]
                  + [pl.BlockSpec((cols, w), lambda j: (j, 0)) for w in widths]
                  + [pl.BlockSpec((r, w.shape[2]), lambda j: (j, 0)) for w, r in zip(up_proj, up_rows)],
        out_shape=[jax.ShapeDtypeStruct((bsz, n_mod), F32)]
                  + [jax.ShapeDtypeStruct((d, w), BF16) for w in widths]
                  + [jax.ShapeDtypeStruct(w.shape[1:], BF16) for w in up_proj],
        name="prep",
    )(c, w_ada_all, b_ada_all.reshape(depth, 1, n_mod), jnp.transpose(w_in_all, (0, 2, 1)), *up_proj)


def _decay_selector():
    pairs = HEADS // 2
    sel = np.zeros((LANES, 2 * pairs * LANES), np.float32)
    for hh in range(HEADS):
        base_q = (hh // 2) * LANES + (HEAD_DIM if hh % 2 == 0 else 0)
        base_k = (pairs + hh // 2) * LANES + (HEAD_DIM if hh % 2 == 0 else 0)
        for piece in range(3):
            sel[piece * HEADS + hh, base_q + piece] = 1.0
            sel[3 * HEADS, base_q + 3 + piece] = 1.0
            sel[3 * HEADS, base_k + piece] = 1.0
            sel[piece * HEADS + hh, base_k + 3 + piece] = -1.0
    return jnp.asarray(sel, BF16)


def _cast_block_rows(n_rows, n_steps):
    rb = -(-n_rows // n_steps)
    rb += -rb % BF16_ROWS
    while n_rows % rb:
        rb += BF16_ROWS
    return rb


def _token_proj(x, mod, g_pre, positions, weights, b_forget, g_q, g_kv, late_weights, layer):
    bsz, seq, d = x.shape
    tm = TOKEN_TILE
    steps_per_batch = seq // tm
    n_steps = bsz * steps_per_batch
    late_in, late_out, late_shapes = [], [], []
    for w in late_weights:
        _, rows, cols = w.shape
        rb = _cast_block_rows(rows, n_steps)
        last = rows // rb - 1
        late_in.append(pl.BlockSpec(
            (1, rb, cols), lambda b, i, last=last: (layer, jnp.minimum(b * steps_per_batch + i, last), 0)))
        late_out.append(pl.BlockSpec(
            (rb, cols), lambda b, i, last=last: (jnp.minimum(b * steps_per_batch + i, last), 0)))
        late_shapes.append(jax.ShapeDtypeStruct((rows, cols), BF16))
    sel = _decay_selector()
    half = MLA_ROPE // 2
    inv_freq = 1.0 / (ROPE_THETA ** (np.arange(0, MLA_ROPE, 2, dtype=np.float32) / MLA_ROPE))
    invf = jnp.asarray(np.tile(inv_freq.astype(np.float32)[:, None], (1, LANES)))
    row = lambda b, i: (b, i, 0)
    head = lambda b, i: (b, 0, i, 0)
    head_shape = jax.ShapeDtypeStruct((bsz, HEADS, seq, LANES), BF16)
    head_spec = pl.BlockSpec((1, HEADS, tm, LANES), head)
    v_shape = jax.ShapeDtypeStruct((bsz, HEADS, LANES, seq), BF16)
    v_spec = pl.BlockSpec((1, HEADS, LANES, tm), lambda b, i: (b, 0, 0, i))
    g_shape = jax.ShapeDtypeStruct((bsz, seq, d), BF16)
    g_spec = pl.BlockSpec((1, tm, d), row)
    return pl.pallas_call(
        functools.partial(_token_proj_kernel, layer=layer),
        grid=(bsz, seq // tm),
        in_specs=[pl.BlockSpec((1, tm, d), row),
                  _const_spec(mod.shape),
                  _const_spec((1, d)), pl.BlockSpec(memory_space=pltpu.SMEM),
                  _const_spec((1, MLA_Q_LORA)), _const_spec((1, MLA_KV_LORA)),
                  pl.BlockSpec((bsz, tm), lambda b, i: (0, i)), _const_spec((half, LANES))]
                 + [_const_spec(w.shape) for w in weights] + [_const_spec(sel.shape)] + late_in,
        out_specs=[head_spec, head_spec, v_spec, head_spec, head_spec, v_spec, g_spec, g_spec] + late_out,
        out_shape=[head_shape, head_shape, v_shape, head_shape, head_shape, v_shape, g_shape, g_shape]
                  + late_shapes,
        scratch_shapes=[pltpu.VMEM((SUBLANES, LANES), F32),
                        pltpu.VMEM((tm, LANES), F32),
                        pltpu.VMEM((tm, LANES), F32)],
        compiler_params=pltpu.CompilerParams(dimension_semantics=("arbitrary", "arbitrary"),
                                             vmem_limit_bytes=VMEM_LIMIT),
        name="token_proj",
    )(x, mod, g_pre.reshape(1, d), b_forget, g_q.reshape(1, -1), g_kv.reshape(1, -1), positions, invf,
      *weights, sel, *late_weights)


def _attn_kernel(zero_ref, q_ref, k_ref, vt_ref, o_ref, sa_ref, sb_ref, smax_ref, m_ref, acc_ref, *, tile):
    seq = vt_ref.shape[3]
    nq = seq // tile
    n_heads = q_ref.shape[1]
    z = zero_ref[0]
    half = tile // 2
    nt = (((1,), (1,)), ((), ()))
    vrow = lax.broadcasted_iota(jnp.int32, (LANES, tile), 0)
    sum_row = (HEAD_DIM, 0)
    keep_even = vrow < HEAD_DIM
    krow = lax.broadcasted_iota(jnp.int32, (half, half), 0)
    qcol = lax.broadcasted_iota(jnp.int32, (half, half), 1)
    causal_half = krow <= qcol

    def issue_scores(qi, j, s_out, slot, heads=None):
        for hh in (range(n_heads) if heads is None else heads):
            q = q_ref[0, hh, qi * tile:(qi + 1) * tile, :]
            k = k_ref[0, hh, j * tile:(j + 1) * tile, :]
            if j == qi:
                top = lax.dot_general(k[:half], q, nt, preferred_element_type=F32)
                top_left = jnp.where(causal_half, top[:, :half], -jnp.inf)
                bottom = jnp.where(causal_half,
                                   lax.dot_general(k[half:], q[half:], nt, preferred_element_type=F32), -jnp.inf)
                s_out[z + hh, :half, :half] = top_left
                s_out[z + hh, :half, half:] = top[:, half:]
                s_out[z + hh, half:, half:] = bottom
                smax_ref[slot, hh, :, :half] = jnp.max(top_left, axis=0, keepdims=True)
                smax_ref[slot, hh, :, half:] = jnp.maximum(jnp.max(top[:, half:], axis=0, keepdims=True),
                                                           jnp.max(bottom, axis=0, keepdims=True))
            else:
                s = lax.dot_general(k, q, nt, preferred_element_type=F32)
                s_out[z + hh] = s
                smax_ref[slot, hh] = jnp.max(s, axis=0, keepdims=True)

    def softmax_pv(s_in, slot, hh, vth, masked, first):
        alphas, p_cols = [], []
        for c0 in range(0, tile, LANES):
            n_keys = c0 + LANES if masked else tile
            n_fill = tile if (c0 >= half or not masked) else half
            m_new = smax_ref[slot, hh, :, c0:c0 + LANES]
            if not first:
                m_old = m_ref[hh, :, c0:c0 + LANES]
                m_new = jnp.maximum(m_old, m_new)
                alphas.append(jnp.exp2(m_old - m_new))
            parts = [jnp.exp2(s_in[z + hh, r0:r0 + LANES, c0:c0 + LANES] - m_new).astype(BF16)
                     for r0 in range(0, n_keys, LANES)]
            parts += [jnp.zeros((LANES, LANES), BF16)] * ((n_fill - n_keys) // LANES)
            p_cols.append(jnp.concatenate(parts, axis=0))
            if not masked:
                m_ref[hh, :, c0:c0 + LANES] = m_new
        n_left = half // LANES
        if masked:
            pv = jnp.concatenate([_dot(vth[:, :half], jnp.concatenate(p_cols[:n_left], axis=1)),
                                  _dot(vth, jnp.concatenate(p_cols[n_left:], axis=1))], axis=1)
        else:
            pv = _dot(vth, jnp.concatenate(p_cols, axis=1))
        acc = pv if first else jnp.concatenate(alphas, axis=1) * acc_ref[z + hh] + pv
        if masked:
            return acc
        acc_ref[z + hh] = acc
        return None

    blocks = [(qi, j) for qi in range(nq) for j in range(qi + 1)]
    bufs = (sa_ref, sb_ref)
    for pair in range(n_heads // 2):
        heads = (2 * pair, 2 * pair + 1)
        issue_scores(0, 0, bufs[0], 0, heads)
        for t, (qi, j) in enumerate(blocks):
            s_in, s_out = bufs[t % 2], bufs[(t + 1) % 2]
            accs = []
            for hh in heads:
                if t + 1 < len(blocks):
                    issue_scores(*blocks[t + 1], s_out, (t + 1) % 2, heads=(hh,))
                accs.append(softmax_pv(s_in, t % 2, hh, vt_ref[0, hh, :, j * tile:(j + 1) * tile], j == qi, j == 0))
            if j == qi:
                outs = [acc * (1.0 / acc[sum_row[hh % 2]:sum_row[hh % 2] + 1, :]) for hh, acc in zip(heads, accs)]
                o_t = jnp.where(keep_even, outs[0], outs[1])
                o_ref[0, qi * tile:(qi + 1) * tile, pair * LANES:(pair + 1) * LANES] = o_t.T.astype(o_ref.dtype)


def _causal_attn(q, k, vt):
    bsz, heads, seq, _ = q.shape
    g = ATTN_PAIRS
    qk_spec = pl.BlockSpec((1, 2 * g, seq, LANES), lambda b, p: (b, p, 0, 0))
    vt_spec = pl.BlockSpec((1, 2 * g, LANES, seq), lambda b, p: (b, p, 0, 0))
    o_spec = pl.BlockSpec((1, seq, g * LANES), lambda b, p: (b, 0, p))
    kern = functools.partial(_attn_kernel, tile=ATTN_TILE)
    return pl.pallas_call(
        kern,
        grid=(bsz, heads // (2 * g)),
        in_specs=[pl.BlockSpec(memory_space=pltpu.SMEM), qk_spec, qk_spec, vt_spec],
        out_specs=o_spec,
        out_shape=jax.ShapeDtypeStruct((bsz, seq, heads * HEAD_DIM), BF16),
        scratch_shapes=[pltpu.VMEM((2 * g, ATTN_TILE, ATTN_TILE), F32),
                        pltpu.VMEM((2 * g, ATTN_TILE, ATTN_TILE), F32),
                        pltpu.VMEM((2, 2 * g, 1, ATTN_TILE), F32),
                        pltpu.VMEM((2 * g, 1, ATTN_TILE), F32),
                        pltpu.VMEM((2 * g, LANES, ATTN_TILE), F32)],
        compiler_params=pltpu.CompilerParams(dimension_semantics=("arbitrary", "arbitrary"),
                                             vmem_limit_bytes=VMEM_LIMIT),
        name="causal_attn",
    )(jnp.zeros((1,), jnp.int32), q, k, vt)


def _mix_ffn_kernel(x_ref, oa_ref, ob_ref, gf_ref, gm_ref, mod_ref, gpost_mix_ref, gpre_ffn_ref,
                    gpost_ffn_ref, wpf_ref, wpm_ref, wout_ref, win_ref, wdown_ref, o_ref):
    tm = x_ref.shape[1]
    mod = lambda k: mod_ref[pl.ds(pl.program_id(0), 1), k * D_MODEL:(k + 1) * D_MODEL]
    shift = mod(3)
    mix_gain = mod(2) * gpost_mix_ref[...]
    ffn_in_gain = gpre_ffn_ref[...] * (1.0 + mod(4))
    ffn_gain = mod(5) * gpost_ffn_ref[...]

    def merge_matmuls(rows):
        pa = _dot(oa_ref[0, rows, :], wpf_ref[...])
        pb = _dot(ob_ref[0, rows, :], wpm_ref[...])
        merged = gf_ref[0, rows, :].astype(F32) * pa + gm_ref[0, rows, :].astype(F32) * pb
        return merged.astype(BF16)

    def mix_residual(rows, y):
        x = x_ref[0, rows, :] + _rms(y) * mix_gain
        h = (_rms(x) * ffn_in_gain + shift).astype(BF16)
        return x, h

    chunks = [(lo, min(FFN_CHUNK, D_FF - lo)) for lo in range(0, D_FF, FFN_CHUNK)]

    def up_matmuls(h, lo, w):
        return _dot(h, win_ref[:, lo:lo + w]), _dot(h, win_ref[:, D_FF + lo:D_FF + lo + w])

    def swiglu_matmuls(h):
        y = None
        gu = up_matmuls(h, *chunks[0])
        for c, (lo, w) in enumerate(chunks):
            g, u = gu
            if c + 1 < len(chunks):
                gu = up_matmuls(h, *chunks[c + 1])
            act = (g * jax.nn.sigmoid(g) * u).astype(BF16)
            part = _dot(act, wdown_ref[lo:lo + w, :])
            y = part if y is None else y + part
        return y

    def store_output(rows, x, y):
        o_ref[0, rows, :] = x + _rms(y) * ffn_gain

    parts = [pl.ds(r0, ROW_PART) for r0 in range(0, tm, ROW_PART)]
    merged = [merge_matmuls(rows) for rows in parts]
    y_mix = [_dot(m, wout_ref[...]) for m in merged]
    pending = None
    for rows, y in zip(parts, y_mix):
        x, h = mix_residual(rows, y)
        y_ffn = swiglu_matmuls(h)
        if pending is not None:
            store_output(*pending)
        pending = (rows, x, y_ffn)
    store_output(*pending)


def _mix_ffn(x, o_a, o_b, gf, gm, mod, g_post_mix, g_pre_ffn, g_post_ffn, w_pf, w_pm, w_out, w_in, w_down):
    bsz, seq, d = x.shape
    tm = MIX_TILE
    row = lambda b, i: (b, i, 0)
    weights = (w_pf, w_pm, w_out, w_in, w_down)
    return pl.pallas_call(
        _mix_ffn_kernel,
        grid=(bsz, seq // tm),
        in_specs=[pl.BlockSpec((1, tm, d), row),
                  pl.BlockSpec((1, tm, FOX_WIDTH), row), pl.BlockSpec((1, tm, FOX_WIDTH), row),
                  pl.BlockSpec((1, tm, d), row), pl.BlockSpec((1, tm, d), row),
                  _const_spec(mod.shape),
                  _const_spec((1, d)), _const_spec((1, d)), _const_spec((1, d))]
                 + [_const_spec(w.shape) for w in weights],
        out_specs=pl.BlockSpec((1, tm, d), row),
        out_shape=jax.ShapeDtypeStruct(x.shape, F32),
        compiler_params=pltpu.CompilerParams(dimension_semantics=("arbitrary", "arbitrary"),
                                             vmem_limit_bytes=VMEM_LIMIT),
        name="mix_ffn",
    )(x, o_a, o_b, gf, gm, mod, g_post_mix.reshape(1, d), g_pre_ffn.reshape(1, d), g_post_ffn.reshape(1, d),
      *weights)


def kernel(x, c, positions, w_ada, b_ada, g_pre_mix, g_post_mix, g_pre_ffn, g_post_ffn, w_in, b_forget,
           g_q_lora, w_uq, g_kv_lora, w_ukv, w_proj_fox, w_proj_mla, w_out, w_ffn_in, w_ffn_out):
    bsz, seq, d = x.shape
    depth = w_ada.shape[0]
    for l in range(depth):
        mod, *weights = _prep(c, w_ada, b_ada, w_in, w_uq, w_ukv, l)
        late = (w_proj_fox, w_proj_mla, w_out, w_ffn_in, w_ffn_out)
        qf, kf, vf, qm, km, vm, gf, gm, *late_bf16 = _token_proj(
            x, mod, g_pre_mix[l], positions, weights, b_forget, g_q_lora[l], g_kv_lora[l], late, l)
        o_a = _causal_attn(qf, kf, vf)
        o_b = _causal_attn(qm, km, vm)
        x = _mix_ffn(x, o_a, o_b, gf, gm, mod, g_post_mix[l], g_pre_ffn[l], g_post_ffn[l], *late_bf16)
    return x
```

```python
import functools
import math

import jax
import jax.numpy as jnp
import numpy as np
from jax import lax
from jax.experimental import pallas as pl
from jax.experimental.pallas import tpu as pltpu

D_MODEL = 1024
HEADS = 8
HEAD_DIM = 64
FOX_WIDTH = HEADS * HEAD_DIM
MLA_NOPE = 64
MLA_ROPE = 32
MLA_V = 64
MLA_Q_LORA = 768
MLA_KV_LORA = 256
D_FF = 2816
ROPE_THETA = 10000.0
NORM_EPS = 1e-6
IN_WIDTHS = (FOX_WIDTH, FOX_WIDTH, FOX_WIDTH, HEADS, MLA_Q_LORA, MLA_KV_LORA, MLA_ROPE, D_MODEL, D_MODEL)

LANES = 128
SUBLANES = 8
BF16_ROWS = 16
TOKEN_TILE = 512
MIX_TILE = 512
ROW_PART = 256
ATTN_TILE = 512
ATTN_PAIRS = 2
FFN_CHUNK = 256
VMEM_LIMIT = 56 * 1024 * 1024

_EXP2_FOX = math.log2(math.e) / math.sqrt(HEAD_DIM)
_EXP2_MLA = math.log2(math.e) / math.sqrt(MLA_NOPE + MLA_ROPE)

F32 = jnp.float32
BF16 = jnp.bfloat16


def _const_spec(shape):
    zeros = (0,) * len(shape)
    return pl.BlockSpec(shape, lambda *_: zeros, pipeline_mode=pl.Buffered(1))


def _rms(x):
    return x * lax.rsqrt(jnp.mean(x * x, axis=-1, keepdims=True) + NORM_EPS)


def _dot(a, b):
    return jnp.dot(a, b, preferred_element_type=F32)


def _cumsum_rows(x):
    n = x.shape[0]
    row = lax.broadcasted_iota(jnp.int32, x.shape, 0)
    d = 1
    while d < n:
        x = x + jnp.where(row >= d, pltpu.roll(x, d, axis=0), 0.0)
        d *= 2
    return x


def _rope_tables(pos_row, invf_col):
    n = pos_row.shape[1]
    groups = LANES // MLA_ROPE
    ang = invf_col * pos_row
    c = jnp.cos(ang)
    s = jnp.sin(ang)
    cos_blocks, sin_blocks = [], []
    for t0 in range(0, n, LANES):
        ct = c[:, t0:t0 + LANES]
        st = s[:, t0:t0 + LANES]
        cos_blocks.append(jnp.concatenate([ct, ct] * groups, axis=0).T)
        sin_blocks.append(jnp.concatenate([-st, st] * groups, axis=0).T)
    return jnp.concatenate(cos_blocks, axis=0), jnp.concatenate(sin_blocks, axis=0)


def _rope(x, cos4, sin4, lane):
    half = MLA_ROPE // 2
    swapped = jnp.where((lane & (MLA_ROPE - 1)) < half, pltpu.roll(x, LANES - half, axis=1), pltpu.roll(x, half, axis=1))
    return x * cos4 + swapped * sin4


def _store_values_transposed(vt_ref, rows, v):
    n = v.shape[0]
    row = lax.broadcasted_iota(jnp.int32, (LANES, n), 0)
    for pair in range(HEADS // 2):
        vt = v[:, pair * LANES:(pair + 1) * LANES].T
        even = jnp.where(row < HEAD_DIM, vt, jnp.where(row == HEAD_DIM, 1.0, 0.0))
        odd = jnp.where(row >= HEAD_DIM, vt, jnp.where(row == 0, 1.0, 0.0))
        vt_ref[0, 2 * pair, :, rows] = even.astype(vt_ref.dtype)
        vt_ref[0, 2 * pair + 1, :, rows] = odd.astype(vt_ref.dtype)


def _token_proj_kernel(x_ref, mod_ref, gpre_ref, bforget_ref, gq_ref, gkv_ref, pos_ref, invf_ref,
                       wqkv_ref, wmisc_ref, wcq_ref, wckv_ref, wg_ref, wq_ref, wkv_ref, sel_ref,
                       late0_ref, late1_ref, late2_ref, late3_ref, late4_ref,
                       qf_ref, kf_ref, vf_ref, qm_ref, km_ref, vm_ref, gf_ref, gm_ref,
                       cast0_ref, cast1_ref, cast2_ref, cast3_ref, cast4_ref,
                       carry_ref, cos_ref, sin_ref, *, layer):
    tm = x_ref.shape[1]

    @pl.when(pl.program_id(1) == 0)
    def _():
        carry_ref[...] = jnp.zeros_like(carry_ref)

    pos_row = pos_ref[pl.ds(pl.program_id(0), 1), :].astype(F32)
    cos_ref[...], sin_ref[...] = _rope_tables(pos_row, invf_ref[:, 0:1])

    lane_row = lax.broadcasted_iota(jnp.int32, (1, LANES), 1)
    forget_bias = jnp.zeros((1, LANES), F32)
    for hh in range(HEADS):
        mine = (lane_row == hh) | (lane_row == HEADS + hh) | (lane_row == 2 * HEADS + hh)
        forget_bias = jnp.where(mine, bforget_ref[layer, hh], forget_bias)

    mod = lambda k: mod_ref[pl.ds(pl.program_id(0), 1), k * D_MODEL:(k + 1) * D_MODEL]
    shift = mod(0)
    gain = gpre_ref[...] * (1.0 + mod(1))

    def project(rows, n):
        h = (_rms(x_ref[0, rows, :]) * gain + shift).astype(BF16)
        lane = lax.broadcasted_iota(jnp.int32, (n, LANES), 1)
        cos4 = cos_ref[rows, :]
        sin4 = sin_ref[rows, :]
        low = lane < HEAD_DIM

        misc = _dot(h, wmisc_ref[...])
        logit = misc + forget_bias
        logf = jnp.minimum(logit, 0.0) - jnp.log(1.0 + jnp.exp(-jnp.abs(logit)))
        cum = _cumsum_rows(logf) + carry_ref[0:1, :]
        carry_ref[0:1, :] = cum[n - 1:n, :]

        cq = _dot(h, wcq_ref[...])
        ckv = _dot(h, wckv_ref[...])
        nq = (_rms(cq) * gq_ref[...]).astype(BF16)
        nkv = (_rms(ckv) * gkv_ref[...]).astype(BF16)
        p_qkv = _dot(h, wqkv_ref[...])
        _store_values_transposed(vf_ref, rows, p_qkv[:, 2 * FOX_WIDTH:])

        qq = _dot(nq, wq_ref[...]) * _EXP2_MLA
        nope_w = HEADS * MLA_NOPE
        q_rope = [_rope(qq[:, nope_w + g * LANES:nope_w + (g + 1) * LANES], cos4, sin4, lane)
                  for g in range(HEADS * MLA_ROPE // LANES)]
        per_group = LANES // MLA_ROPE
        for hh in range(HEADS):
            pair = hh // 2
            nope = qq[:, pair * LANES:(pair + 1) * LANES]
            src_lane = (hh % per_group) * MLA_ROPE
            dst_lane = HEAD_DIM if hh % 2 == 0 else 0
            rope = q_rope[hh // per_group]
            if src_lane != dst_lane:
                rope = pltpu.roll(rope, (dst_lane - src_lane) % LANES, axis=1)
            in_rope = (lane >= dst_lane) & (lane < dst_lane + MLA_ROPE)
            own = low if hh % 2 == 0 else jnp.logical_not(low)
            qm_ref[0, hh, rows, :] = jnp.where(own, nope, jnp.where(in_rope, rope, 0.0)).astype(qm_ref.dtype)
        gf_ref[0, rows, :] = jax.nn.sigmoid(_dot(h, wg_ref[:, :D_MODEL])).astype(gf_ref.dtype)
        kv = _dot(nkv, wkv_ref[...])
        _store_values_transposed(vm_ref, rows, kv[:, nope_w:])
        in_rope = (lane >= MLA_NOPE) & (lane < MLA_NOPE + MLA_ROPE)
        kpe_even = jnp.where(in_rope, _rope(misc, cos4, sin4, lane), 0.0)
        kpe_odd = pltpu.roll(kpe_even, LANES - HEAD_DIM, axis=1)
        for hh in range(HEADS):
            pair = hh // 2
            nope = kv[:, pair * LANES:(pair + 1) * LANES]
            k_full = jnp.where(low, nope, kpe_even) if hh % 2 == 0 else jnp.where(low, kpe_odd, nope)
            km_ref[0, hh, rows, :] = k_full.astype(km_ref.dtype)

        a = cum * math.log2(math.e)
        a_hi = a.astype(BF16).astype(F32)
        r1 = a - a_hi
        a_mid = r1.astype(BF16).astype(F32)
        a_lo = (r1 - a_mid).astype(BF16).astype(F32)
        z = jnp.where(lane < HEADS, a_hi,
                      jnp.where(lane < 2 * HEADS, a_mid,
                                jnp.where(lane < 3 * HEADS, a_lo,
                                          jnp.where(lane == 3 * HEADS, 1.0, 0.0))))
        aug = _dot(z.astype(BF16), sel_ref[...])
        pairs = HEADS // 2
        for hh in range(HEADS):
            pair = hh // 2
            keep = (lane < HEAD_DIM) if hh % 2 == 0 else (lane >= HEAD_DIM)
            xq = p_qkv[:, pair * LANES:(pair + 1) * LANES]
            xk = p_qkv[:, FOX_WIDTH + pair * LANES:FOX_WIDTH + (pair + 1) * LANES]
            aq = aug[:, pair * LANES:(pair + 1) * LANES]
            ak = aug[:, (pairs + pair) * LANES:(pairs + pair + 1) * LANES]
            qf_ref[0, hh, rows, :] = jnp.where(keep, xq * _EXP2_FOX, aq).astype(qf_ref.dtype)
            kf_ref[0, hh, rows, :] = jnp.where(keep, xk, ak).astype(kf_ref.dtype)

        gm_ref[0, rows, :] = jax.nn.sigmoid(_dot(h, wg_ref[:, D_MODEL:])).astype(gm_ref.dtype)

    n = tm // 2
    for r0 in (0, n):
        project(pl.ds(r0, n), n)

    for src, dst in ((late0_ref, cast0_ref), (late1_ref, cast1_ref), (late2_ref, cast2_ref),
                     (late3_ref, cast3_ref), (late4_ref, cast4_ref)):
        dst[...] = src[0].astype(dst.dtype)


_IN_OFFSETS = tuple(int(v) for v in np.cumsum((0,) + IN_WIDTHS))


def _regroup_head_columns(w, first, second):
    n = w.shape[1]
    assert n == HEADS * (first + second) and first & (first - 1) == 0 and second & (second - 1) == 0
    col = lax.broadcasted_iota(jnp.int32, (1, n), 1)
    split = HEADS * first
    k = col - split
    src_first = (col >> int(math.log2(first))) * (first + second) + (col & (first - 1))
    src_second = (k >> int(math.log2(second))) * (first + second) + first + (k & (second - 1))
    src = jnp.where(col < split, src_first, src_second)
    perm = jnp.where(lax.broadcasted_iota(jnp.int32, (n, n), 0) == src, 1.0, 0.0).astype(BF16)
    return _dot(w.astype(BF16), perm).astype(BF16)


def _prep_kernel(c_ref, wada_ref, bada_ref, wt_ref, wuq_ref, wukv_ref,
                 mod_ref, qkv_ref, misc_ref, cq_ref, ckv_ref, g_ref, wq_ref, wkv_ref):
    c = c_ref[...]
    sc = c * jax.nn.sigmoid(c)
    mod_ref[...] = _dot(sc.astype(BF16), wada_ref[0].astype(BF16)) + bada_ref[0]

    wt = wt_ref[0]
    cols = wt.shape[1]
    o = _IN_OFFSETS
    qkv_ref[...] = wt[o[0]:o[3], :].T.astype(BF16)
    cq_ref[...] = wt[o[4]:o[5], :].T.astype(BF16)
    ckv_ref[...] = wt[o[5]:o[6], :].T.astype(BF16)
    g_ref[...] = wt[o[7]:o[9], :].T.astype(BF16)
    f = wt[o[3]:o[4], :]
    kr = wt[o[6]:o[7], :]
    misc_t = jnp.concatenate([f, f, f, jnp.zeros((MLA_NOPE - 3 * HEADS, cols), F32), kr,
                              jnp.zeros((LANES - MLA_NOPE - MLA_ROPE, cols), F32)], axis=0)
    misc_ref[...] = misc_t.T.astype(BF16)

    wq_ref[...] = _regroup_head_columns(wuq_ref[0], MLA_NOPE, MLA_ROPE)
    wkv_ref[...] = _regroup_head_columns(wukv_ref[0], MLA_NOPE, MLA_V)


def _prep(c, w_ada_all, b_ada_all, w_in_all, w_uq_all, w_ukv_all, layer):
    bsz, d = c.shape
    depth, _, n_mod = w_ada_all.shape
    n_in = w_in_all.shape[2]
    cols = 2 * LANES
    steps = d // cols
    tn = n_mod // steps
    assert d % cols == 0 and n_mod % steps == 0 and tn % LANES == 0
    widths = (3 * FOX_WIDTH, LANES, MLA_Q_LORA, MLA_KV_LORA, 2 * D_MODEL)
    up_proj = (w_uq_all, w_ukv_all)
    up_rows = [w.shape[1] // steps for w in up_proj]
    assert all(w.shape[1] % steps == 0 and r % BF16_ROWS == 0 for w, r in zip(up_proj, up_rows))
    return pl.pallas_call(
        _prep_kernel,
        grid=(steps,),
        in_specs=[pl.BlockSpec((bsz, d), lambda j: (0, 0)),
                  pl.BlockSpec((1, d, tn), lambda j: (layer, 0, j)),
                  pl.BlockSpec((1, 1, tn), lambda j: (layer, 0, j)),
                  pl.BlockSpec((1, n_in, cols), lambda j: (layer, 0, j))]
                 + [pl.BlockSpec((1, r, w.shape[2]), lambda j: (layer, j, 0)) for w, r in zip(up_proj, up_rows)],
        out_specs=[pl.BlockSpec((bsz, tn), lambda j: (0, j))]
                  + [pl.BlockSpec((cols, w), lambda j: (j, 0)) for w in widths]
                  + [pl.BlockSpec((r, w.shape[2]), lambda j: (j, 0)) for w, r in zip(up_proj, up_rows)],
        out_shape=[jax.ShapeDtypeStruct((bsz, n_mod), F32)]
                  + [jax.ShapeDtypeStruct((d, w), BF16) for w in widths]
                  + [jax.ShapeDtypeStruct(w.shape[1:], BF16) for w in up_proj],
        name="prep",
    )(c, w_ada_all, b_ada_all.reshape(depth, 1, n_mod), jnp.transpose(w_in_all, (0, 2, 1)), *up_proj)


def _decay_selector():
    pairs = HEADS // 2
    sel = np.zeros((LANES, 2 * pairs * LANES), np.float32)
    for hh in range(HEADS):
        base_q = (hh // 2) * LANES + (HEAD_DIM if hh % 2 == 0 else 0)
        base_k = (pairs + hh // 2) * LANES + (HEAD_DIM if hh % 2 == 0 else 0)
        for piece in range(3):
            sel[piece * HEADS + hh, base_q + piece] = 1.0
            sel[3 * HEADS, base_q + 3 + piece] = 1.0
            sel[3 * HEADS, base_k + piece] = 1.0
            sel[piece * HEADS + hh, base_k + 3 + piece] = -1.0
    return jnp.asarray(sel, BF16)


def _cast_block_rows(n_rows, n_steps):
    rb = -(-n_rows // n_steps)
    rb += -rb % BF16_ROWS
    while n_rows % rb:
        rb += BF16_ROWS
    return rb


def _token_proj(x, mod, g_pre, positions, weights, b_forget, g_q, g_kv, late_weights, layer):
    bsz, seq, d = x.shape
    tm = TOKEN_TILE
    steps_per_batch = seq // tm
    n_steps = bsz * steps_per_batch
    late_in, late_out, late_shapes = [], [], []
    for w in late_weights:
        _, rows, cols = w.shape
        rb = _cast_block_rows(rows, n_steps)
        last = rows // rb - 1
        late_in.append(pl.BlockSpec(
            (1, rb, cols), lambda b, i, last=last: (layer, jnp.minimum(b * steps_per_batch + i, last), 0)))
        late_out.append(pl.BlockSpec(
            (rb, cols), lambda b, i, last=last: (jnp.minimum(b * steps_per_batch + i, last), 0)))
        late_shapes.append(jax.ShapeDtypeStruct((rows, cols), BF16))
    sel = _decay_selector()
    half = MLA_ROPE // 2
    inv_freq = 1.0 / (ROPE_THETA ** (np.arange(0, MLA_ROPE, 2, dtype=np.float32) / MLA_ROPE))
    invf = jnp.asarray(np.tile(inv_freq.astype(np.float32)[:, None], (1, LANES)))
    row = lambda b, i: (b, i, 0)
    head = lambda b, i: (b, 0, i, 0)
    head_shape = jax.ShapeDtypeStruct((bsz, HEADS, seq, LANES), BF16)
    head_spec = pl.BlockSpec((1, HEADS, tm, LANES), head)
    v_shape = jax.ShapeDtypeStruct((bsz, HEADS, LANES, seq), BF16)
    v_spec = pl.BlockSpec((1, HEADS, LANES, tm), lambda b, i: (b, 0, 0, i))
    g_shape = jax.ShapeDtypeStruct((bsz, seq, d), BF16)
    g_spec = pl.BlockSpec((1, tm, d), row)
    return pl.pallas_call(
        functools.partial(_token_proj_kernel, layer=layer),
        grid=(bsz, seq // tm),
        in_specs=[pl.BlockSpec((1, tm, d), row),
                  _const_spec(mod.shape),
                  _const_spec((1, d)), pl.BlockSpec(memory_space=pltpu.SMEM),
                  _const_spec((1, MLA_Q_LORA)), _const_spec((1, MLA_KV_LORA)),
                  pl.BlockSpec((bsz, tm), lambda b, i: (0, i)), _const_spec((half, LANES))]
                 + [_const_spec(w.shape) for w in weights] + [_const_spec(sel.shape)] + late_in,
        out_specs=[head_spec, head_spec, v_spec, head_spec, head_spec, v_spec, g_spec, g_spec] + late_out,
        out_shape=[head_shape, head_shape, v_shape, head_shape, head_shape, v_shape, g_shape, g_shape]
                  + late_shapes,
        scratch_shapes=[pltpu.VMEM((SUBLANES, LANES), F32),
                        pltpu.VMEM((tm, LANES), F32),
                        pltpu.VMEM((tm, LANES), F32)],
        compiler_params=pltpu.CompilerParams(dimension_semantics=("arbitrary", "arbitrary"),
                                             vmem_limit_bytes=VMEM_LIMIT),
        name="token_proj",
    )(x, mod, g_pre.reshape(1, d), b_forget, g_q.reshape(1, -1), g_kv.reshape(1, -1), positions, invf,
      *weights, sel, *late_weights)


def _attn_kernel(zero_ref, q_ref, k_ref, vt_ref, o_ref, sa_ref, sb_ref, smax_ref, m_ref, acc_ref, *, tile):
    seq = vt_ref.shape[3]
    nq = seq // tile
    n_heads = q_ref.shape[1]
    z = zero_ref[0]
    half = tile // 2
    nt = (((1,), (1,)), ((), ()))
    vrow = lax.broadcasted_iota(jnp.int32, (LANES, tile), 0)
    sum_row = (HEAD_DIM, 0)
    keep_even = vrow < HEAD_DIM
    krow = lax.broadcasted_iota(jnp.int32, (half, half), 0)
    qcol = lax.broadcasted_iota(jnp.int32, (half, half), 1)
    causal_half = krow <= qcol

    def issue_scores(qi, j, s_out, slot, heads=None):
        for hh in (range(n_heads) if heads is None else heads):
            q = q_ref[0, hh, qi * tile:(qi + 1) * tile, :]
            k = k_ref[0, hh, j * tile:(j + 1) * tile, :]
            if j == qi:
                top = lax.dot_general(k[:half], q, nt, preferred_element_type=F32)
                top_left = jnp.where(causal_half, top[:, :half], -jnp.inf)
                bottom = jnp.where(causal_half,
                                   lax.dot_general(k[half:], q[half:], nt, preferred_element_type=F32), -jnp.inf)
                s_out[z + hh, :half, :half] = top_left
                s_out[z + hh, :half, half:] = top[:, half:]
                s_out[z + hh, half:, half:] = bottom
                smax_ref[slot, hh, :, :half] = jnp.max(top_left, axis=0, keepdims=True)
                smax_ref[slot, hh, :, half:] = jnp.maximum(jnp.max(top[:, half:], axis=0, keepdims=True),
                                                           jnp.max(bottom, axis=0, keepdims=True))
            else:
                s = lax.dot_general(k, q, nt, preferred_element_type=F32)
                s_out[z + hh] = s
                smax_ref[slot, hh] = jnp.max(s, axis=0, keepdims=True)

    def softmax_pv(s_in, slot, hh, vth, masked, first):
        alphas, p_cols = [], []
        for c0 in range(0, tile, LANES):
            n_keys = c0 + LANES if masked else tile
            n_fill = tile if (c0 >= half or not masked) else half
            m_new = smax_ref[slot, hh, :, c0:c0 + LANES]
            if not first:
                m_old = m_ref[hh, :, c0:c0 + LANES]
                m_new = jnp.maximum(m_old, m_new)
                alphas.append(jnp.exp2(m_old - m_new))
            parts = [jnp.exp2(s_in[z + hh, r0:r0 + LANES, c0:c0 + LANES] - m_new).astype(BF16)
                     for r0 in range(0, n_keys, LANES)]
            parts += [jnp.zeros((LANES, LANES), BF16)] * ((n_fill - n_keys) // LANES)
            p_cols.append(jnp.concatenate(parts, axis=0))
            if not masked:
                m_ref[hh, :, c0:c0 + LANES] = m_new
        n_left = half // LANES
        if masked:
            pv = jnp.concatenate([_dot(vth[:, :half], jnp.concatenate(p_cols[:n_left], axis=1)),
                                  _dot(vth, jnp.concatenate(p_cols[n_left:], axis=1))], axis=1)
        else:
            pv = _dot(vth, jnp.concatenate(p_cols, axis=1))
        acc = pv if first else jnp.concatenate(alphas, axis=1) * acc_ref[z + hh] + pv
        if masked:
            return acc
        acc_ref[z + hh] = acc
        return None

    blocks = [(qi, j) for qi in range(nq) for j in range(qi + 1)]
    bufs = (sa_ref, sb_ref)
    for pair in range(n_heads // 2):
        heads = (2 * pair, 2 * pair + 1)
        issue_scores(0, 0, bufs[0], 0, heads)
        for t, (qi, j) in enumerate(blocks):
            s_in, s_out = bufs[t % 2], bufs[(t + 1) % 2]
            accs = []
            for hh in heads:
                if t + 1 < len(blocks):
                    issue_scores(*blocks[t + 1], s_out, (t + 1) % 2, heads=(hh,))
                accs.append(softmax_pv(s_in, t % 2, hh, vt_ref[0, hh, :, j * tile:(j + 1) * tile], j == qi, j == 0))
            if j == qi:
                outs = [acc * (1.0 / acc[sum_row[hh % 2]:sum_row[hh % 2] + 1, :]) for hh, acc in zip(heads, accs)]
                o_t = jnp.where(keep_even, outs[0], outs[1])
                o_ref[0, qi * tile:(qi + 1) * tile, pair * LANES:(pair + 1) * LANES] = o_t.T.astype(o_ref.dtype)


def _causal_attn(q, k, vt):
    bsz, heads, seq, _ = q.shape
    g = ATTN_PAIRS
    qk_spec = pl.BlockSpec((1, 2 * g, seq, LANES), lambda b, p: (b, p, 0, 0))
    vt_spec = pl.BlockSpec((1, 2 * g, LANES, seq), lambda b, p: (b, p, 0, 0))
    o_spec = pl.BlockSpec((1, seq, g * LANES), lambda b, p: (b, 0, p))
    kern = functools.partial(_attn_kernel, tile=ATTN_TILE)
    return pl.pallas_call(
        kern,
        grid=(bsz, heads // (2 * g)),
        in_specs=[pl.BlockSpec(memory_space=pltpu.SMEM), qk_spec, qk_spec, vt_spec],
        out_specs=o_spec,
        out_shape=jax.ShapeDtypeStruct((bsz, seq, heads * HEAD_DIM), BF16),
        scratch_shapes=[pltpu.VMEM((2 * g, ATTN_TILE, ATTN_TILE), F32),
                        pltpu.VMEM((2 * g, ATTN_TILE, ATTN_TILE), F32),
                        pltpu.VMEM((2, 2 * g, 1, ATTN_TILE), F32),
                        pltpu.VMEM((2 * g, 1, ATTN_TILE), F32),
                        pltpu.VMEM((2 * g, LANES, ATTN_TILE), F32)],
        compiler_params=pltpu.CompilerParams(dimension_semantics=("arbitrary", "arbitrary"),
                                             vmem_limit_bytes=VMEM_LIMIT),
        name="causal_attn",
    )(jnp.zeros((1,), jnp.int32), q, k, vt)


def _mix_ffn_kernel(x_ref, oa_ref, ob_ref, gf_ref, gm_ref, mod_ref, gpost_mix_ref, gpre_ffn_ref,
                    gpost_ffn_ref, wpf_ref, wpm_ref, wout_ref, win_ref, wdown_ref, o_ref):
    tm = x_ref.shape[1]
    mod = lambda k: mod_ref[pl.ds(pl.program_id(0), 1), k * D_MODEL:(k + 1) * D_MODEL]
    shift = mod(3)
    mix_gain = mod(2) * gpost_mix_ref[...]
    ffn_in_gain = gpre_ffn_ref[...] * (1.0 + mod(4))
    ffn_gain = mod(5) * gpost_ffn_ref[...]

    def merge_matmuls(rows):
        pa = _dot(oa_ref[0, rows, :], wpf_ref[...])
        pb = _dot(ob_ref[0, rows, :], wpm_ref[...])
        merged = gf_ref[0, rows, :].astype(F32) * pa + gm_ref[0, rows, :].astype(F32) * pb
        return merged.astype(BF16)

    def mix_residual(rows, y):
        x = x_ref[0, rows, :] + _rms(y) * mix_gain
        h = (_rms(x) * ffn_in_gain + shift).astype(BF16)
        return x, h

    chunks = [(lo, min(FFN_CHUNK, D_FF - lo)) for lo in range(0, D_FF, FFN_CHUNK)]

    def up_matmuls(h, lo, w):
        return _dot(h, win_ref[:, lo:lo + w]), _dot(h, win_ref[:, D_FF + lo:D_FF + lo + w])

    def swiglu_matmuls(h):
        y = None
        gu = up_matmuls(h, *chunks[0])
        for c, (lo, w) in enumerate(chunks):
            g, u = gu
            if c + 1 < len(chunks):
                gu = up_matmuls(h, *chunks[c + 1])
            act = (g * jax.nn.sigmoid(g) * u).astype(BF16)
            part = _dot(act, wdown_ref[lo:lo + w, :])
            y = part if y is None else y + part
        return y

    def store_output(rows, x, y):
        o_ref[0, rows, :] = x + _rms(y) * ffn_gain

    parts = [pl.ds(r0, ROW_PART) for r0 in range(0, tm, ROW_PART)]
    merged = [merge_matmuls(rows) for rows in parts]
    y_mix = [_dot(m, wout_ref[...]) for m in merged]
    pending = None
    for rows, y in zip(parts, y_mix):
        x, h = mix_residual(rows, y)
        y_ffn = swiglu_matmuls(h)
        if pending is not None:
            store_output(*pending)
        pending = (rows, x, y_ffn)
    store_output(*pending)


def _mix_ffn(x, o_a, o_b, gf, gm, mod, g_post_mix, g_pre_ffn, g_post_ffn, w_pf, w_pm, w_out, w_in, w_down):
    bsz, seq, d = x.shape
    tm = MIX_TILE
    row = lambda b, i: (b, i, 0)
    weights = (w_pf, w_pm, w_out, w_in, w_down)
    return pl.pallas_call(
        _mix_ffn_kernel,
        grid=(bsz, seq // tm),
        in_specs=[pl.BlockSpec((1, tm, d), row),
                  pl.BlockSpec((1, tm, FOX_WIDTH), row), pl.BlockSpec((1, tm, FOX_WIDTH), row),
                  pl.BlockSpec((1, tm, d), row), pl.BlockSpec((1, tm, d), row),
                  _const_spec(mod.shape),
                  _const_spec((1, d)), _const_spec((1, d)), _const_spec((1, d))]
                 + [_const_spec(w.shape) for w in weights],
        out_specs=pl.BlockSpec((1, tm, d), row),
        out_shape=jax.ShapeDtypeStruct(x.shape, F32),
        compiler_params=pltpu.CompilerParams(dimension_semantics=("arbitrary", "arbitrary"),
                                             vmem_limit_bytes=VMEM_LIMIT),
        name="mix_ffn",
    )(x, o_a, o_b, gf, gm, mod, g_post_mix.reshape(1, d), g_pre_ffn.reshape(1, d), g_post_ffn.reshape(1, d),
      *weights)


def kernel(x, c, positions, w_ada, b_ada, g_pre_mix, g_post_mix, g_pre_ffn, g_post_ffn, w_in, b_forget,
           g_q_lora, w_uq, g_kv_lora, w_ukv, w_proj_fox, w_proj_mla, w_out, w_ffn_in, w_ffn_out):
    bsz, seq, d = x.shape
    depth = w_ada.shape[0]
    for l in range(depth):
        mod, *weights = _prep(c, w_ada, b_ada, w_in, w_uq, w_ukv, l)
        late = (w_proj_fox, w_proj_mla, w_out, w_ffn_in, w_ffn_out)
        qf, kf, vf, qm, km, vm, gf, gm, *late_bf16 = _token_proj(
            x, mod, g_pre_mix[l], positions, weights, b_forget, g_q_lora[l], g_kv_lora[l], late, l)
        o_a = _causal_attn(qf, kf, vf)
        o_b = _causal_attn(qm, km, vm)
        x = _mix_ffn(x, o_a, o_b, gf, gm, mod, g_post_mix[l], g_pre_ffn[l], g_post_ffn[l], *late_bf16)
    return x
```

```python
import functools
import math

import jax
import jax.numpy as jnp
import numpy as np
from jax import lax
from jax.experimental import pallas as pl
from jax.experimental.pallas import tpu as pltpu

D_MODEL = 1024
HEADS = 8
HEAD_DIM = 64
FOX_WIDTH = HEADS * HEAD_DIM
MLA_NOPE = 64
MLA_ROPE = 32
MLA_V = 64
MLA_Q_LORA = 768
MLA_KV_LORA = 256
D_FF = 2816
ROPE_THETA = 10000.0
NORM_EPS = 1e-6
IN_WIDTHS = (FOX_WIDTH, FOX_WIDTH, FOX_WIDTH, HEADS, MLA_Q_LORA, MLA_KV_LORA, MLA_ROPE, D_MODEL, D_MODEL)

LANES = 128
SUBLANES = 8
BF16_ROWS = 16
TOKEN_TILE = 512
MIX_TILE = 512
ROW_PART = 256
ATTN_TILE = 512
ATTN_PAIRS = 2
FFN_CHUNK = 256
VMEM_LIMIT = 56 * 1024 * 1024

_EXP2_FOX = math.log2(math.e) / math.sqrt(HEAD_DIM)
_EXP2_MLA = math.log2(math.e) / math.sqrt(MLA_NOPE + MLA_ROPE)

F32 = jnp.float32
BF16 = jnp.bfloat16


def _const_spec(shape):
    zeros = (0,) * len(shape)
    return pl.BlockSpec(shape, lambda *_: zeros, pipeline_mode=pl.Buffered(1))


def _rms(x):
    return x * lax.rsqrt(jnp.mean(x * x, axis=-1, keepdims=True) + NORM_EPS)


def _dot(a, b):
    return jnp.dot(a, b, preferred_element_type=F32)


def _cumsum_rows(x):
    n = x.shape[0]
    row = lax.broadcasted_iota(jnp.int32, x.shape, 0)
    d = 1
    while d < n:
        x = x + jnp.where(row >= d, pltpu.roll(x, d, axis=0), 0.0)
        d *= 2
    return x


def _rope_tables(pos_row, invf_col):
    n = pos_row.shape[1]
    groups = LANES // MLA_ROPE
    ang = invf_col * pos_row
    c = jnp.cos(ang)
    s = jnp.sin(ang)
    cos_blocks, sin_blocks = [], []
    for t0 in range(0, n, LANES):
        ct = c[:, t0:t0 + LANES]
        st = s[:, t0:t0 + LANES]
        cos_blocks.append(jnp.concatenate([ct, ct] * groups, axis=0).T)
        sin_blocks.append(jnp.concatenate([-st, st] * groups, axis=0).T)
    return jnp.concatenate(cos_blocks, axis=0), jnp.concatenate(sin_blocks, axis=0)


def _rope(x, cos4, sin4, lane):
    half = MLA_ROPE // 2
    swapped = jnp.where((lane & (MLA_ROPE - 1)) < half, pltpu.roll(x, LANES - half, axis=1), pltpu.roll(x, half, axis=1))
    return x * cos4 + swapped * sin4


def _store_values_transposed(vt_ref, rows, v):
    n = v.shape[0]
    row = lax.broadcasted_iota(jnp.int32, (LANES, n), 0)
    for pair in range(HEADS // 2):
        vt = v[:, pair * LANES:(pair + 1) * LANES].T
        even = jnp.where(row < HEAD_DIM, vt, jnp.where(row == HEAD_DIM, 1.0, 0.0))
        odd = jnp.where(row >= HEAD_DIM, vt, jnp.where(row == 0, 1.0, 0.0))
        vt_ref[0, 2 * pair, :, rows] = even.astype(vt_ref.dtype)
        vt_ref[0, 2 * pair + 1, :, rows] = odd.astype(vt_ref.dtype)


def _token_proj_kernel(x_ref, mod_ref, gpre_ref, bforget_ref, gq_ref, gkv_ref, pos_ref, invf_ref,
                       wqkv_ref, wmisc_ref, wcq_ref, wckv_ref, wg_ref, wq_ref, wkv_ref, sel_ref,
                       late0_ref, late1_ref, late2_ref, late3_ref, late4_ref,
                       qf_ref, kf_ref, vf_ref, qm_ref, km_ref, vm_ref, gf_ref, gm_ref,
                       cast0_ref, cast1_ref, cast2_ref, cast3_ref, cast4_ref,
                       carry_ref, cos_ref, sin_ref, *, layer):
    tm = x_ref.shape[1]

    @pl.when(pl.program_id(1) == 0)
    def _():
        carry_ref[...] = jnp.zeros_like(carry_ref)

    pos_row = pos_ref[pl.ds(pl.program_id(0), 1), :].astype(F32)
    cos_ref[...], sin_ref[...] = _rope_tables(pos_row, invf_ref[:, 0:1])

    lane_row = lax.broadcasted_iota(jnp.int32, (1, LANES), 1)
    forget_bias = jnp.zeros((1, LANES), F32)
    for hh in range(HEADS):
        mine = (lane_row == hh) | (lane_row == HEADS + hh) | (lane_row == 2 * HEADS + hh)
        forget_bias = jnp.where(mine, bforget_ref[layer, hh], forget_bias)

    mod = lambda k: mod_ref[pl.ds(pl.program_id(0), 1), k * D_MODEL:(k + 1) * D_MODEL]
    shift = mod(0)
    gain = gpre_ref[...] * (1.0 + mod(1))

    def project(rows, n):
        h = (_rms(x_ref[0, rows, :]) * gain + shift).astype(BF16)
        lane = lax.broadcasted_iota(jnp.int32, (n, LANES), 1)
        cos4 = cos_ref[rows, :]
        sin4 = sin_ref[rows, :]
        low = lane < HEAD_DIM

        misc = _dot(h, wmisc_ref[...])
        logit = misc + forget_bias
        logf = jnp.minimum(logit, 0.0) - jnp.log(1.0 + jnp.exp(-jnp.abs(logit)))
        cum = _cumsum_rows(logf) + carry_ref[0:1, :]
        carry_ref[0:1, :] = cum[n - 1:n, :]

        cq = _dot(h, wcq_ref[...])
        ckv = _dot(h, wckv_ref[...])
        nq = (_rms(cq) * gq_ref[...]).astype(BF16)
        nkv = (_rms(ckv) * gkv_ref[...]).astype(BF16)
        p_qkv = _dot(h, wqkv_ref[...])
        _store_values_transposed(vf_ref, rows, p_qkv[:, 2 * FOX_WIDTH:])

        qq = _dot(nq, wq_ref[...]) * _EXP2_MLA
        nope_w = HEADS * MLA_NOPE
        q_rope = [_rope(qq[:, nope_w + g * LANES:nope_w + (g + 1) * LANES], cos4, sin4, lane)
                  for g in range(HEADS * MLA_ROPE // LANES)]
        per_group = LANES // MLA_ROPE
        for hh in range(HEADS):
            pair = hh // 2
            nope = qq[:, pair * LANES:(pair + 1) * LANES]
            src_lane = (hh % per_group) * MLA_ROPE
            dst_lane = HEAD_DIM if hh % 2 == 0 else 0
            rope = q_rope[hh // per_group]
            if src_lane != dst_lane:
                rope = pltpu.roll(rope, (dst_lane - src_lane) % LANES, axis=1)
            in_rope = (lane >= dst_lane) & (lane < dst_lane + MLA_ROPE)
            own = low if hh % 2 == 0 else jnp.logical_not(low)
            qm_ref[0, hh, rows, :] = jnp.where(own, nope, jnp.where(in_rope, rope, 0.0)).astype(qm_ref.dtype)
        kv = _dot(nkv, wkv_ref[...])
        _store_values_transposed(vm_ref, rows, kv[:, nope_w:])
        in_rope = (lane >= MLA_NOPE) & (lane < MLA_NOPE + MLA_ROPE)
        kpe_even = jnp.where(in_rope, _rope(misc, cos4, sin4, lane), 0.0)
        kpe_odd = pltpu.roll(kpe_even, LANES - HEAD_DIM, axis=1)
        for hh in range(HEADS):
            pair = hh // 2
            nope = kv[:, pair * LANES:(pair + 1) * LANES]
            k_full = jnp.where(low, nope, kpe_even) if hh % 2 == 0 else jnp.where(low, kpe_odd, nope)
            km_ref[0, hh, rows, :] = k_full.astype(km_ref.dtype)

        a = cum * math.log2(math.e)
        a_hi = a.astype(BF16).astype(F32)
        r1 = a - a_hi
        a_mid = r1.astype(BF16).astype(F32)
        a_lo = (r1 - a_mid).astype(BF16).astype(F32)
        z = jnp.where(lane < HEADS, a_hi,
                      jnp.where(lane < 2 * HEADS, a_mid,
                                jnp.where(lane < 3 * HEADS, a_lo,
                                          jnp.where(lane == 3 * HEADS, 1.0, 0.0))))
        aug = _dot(z.astype(BF16), sel_ref[...])
        pairs = HEADS // 2
        for hh in range(HEADS):
            pair = hh // 2
            keep = (lane < HEAD_DIM) if hh % 2 == 0 else (lane >= HEAD_DIM)
            xq = p_qkv[:, pair * LANES:(pair + 1) * LANES]
            xk = p_qkv[:, FOX_WIDTH + pair * LANES:FOX_WIDTH + (pair + 1) * LANES]
            aq = aug[:, pair * LANES:(pair + 1) * LANES]
            ak = aug[:, (pairs + pair) * LANES:(pairs + pair + 1) * LANES]
            qf_ref[0, hh, rows, :] = jnp.where(keep, xq * _EXP2_FOX, aq).astype(qf_ref.dtype)
            kf_ref[0, hh, rows, :] = jnp.where(keep, xk, ak).astype(kf_ref.dtype)

        for gi, ref in enumerate((gf_ref, gm_ref)):
            g = _dot(h, wg_ref[:, gi * D_MODEL:(gi + 1) * D_MODEL])
            ref[0, rows, :] = jax.nn.sigmoid(g).astype(ref.dtype)

    n = tm // 2
    for r0 in (0, n):
        project(pl.ds(r0, n), n)

    for src, dst in ((late0_ref, cast0_ref), (late1_ref, cast1_ref), (late2_ref, cast2_ref),
                     (late3_ref, cast3_ref), (late4_ref, cast4_ref)):
        dst[...] = src[0].astype(dst.dtype)


_IN_OFFSETS = tuple(int(v) for v in np.cumsum((0,) + IN_WIDTHS))


def _regroup_head_columns(w, first, second):
    n = w.shape[1]
    assert n == HEADS * (first + second) and first & (first - 1) == 0 and second & (second - 1) == 0
    col = lax.broadcasted_iota(jnp.int32, (1, n), 1)
    split = HEADS * first
    k = col - split
    src_first = (col >> int(math.log2(first))) * (first + second) + (col & (first - 1))
    src_second = (k >> int(math.log2(second))) * (first + second) + first + (k & (second - 1))
    src = jnp.where(col < split, src_first, src_second)
    perm = jnp.where(lax.broadcasted_iota(jnp.int32, (n, n), 0) == src, 1.0, 0.0).astype(BF16)
    return _dot(w.astype(BF16), perm).astype(BF16)


def _prep_kernel(c_ref, wada_ref, bada_ref, wt_ref, wuq_ref, wukv_ref,
                 mod_ref, qkv_ref, misc_ref, cq_ref, ckv_ref, g_ref, wq_ref, wkv_ref):
    c = c_ref[...]
    sc = c * jax.nn.sigmoid(c)
    mod_ref[...] = _dot(sc.astype(BF16), wada_ref[0].astype(BF16)) + bada_ref[0]

    wt = wt_ref[0]
    cols = wt.shape[1]
    o = _IN_OFFSETS
    qkv_ref[...] = wt[o[0]:o[3], :].T.astype(BF16)
    cq_ref[...] = wt[o[4]:o[5], :].T.astype(BF16)
    ckv_ref[...] = wt[o[5]:o[6], :].T.astype(BF16)
    g_ref[...] = wt[o[7]:o[9], :].T.astype(BF16)
    f = wt[o[3]:o[4], :]
    kr = wt[o[6]:o[7], :]
    misc_t = jnp.concatenate([f, f, f, jnp.zeros((MLA_NOPE - 3 * HEADS, cols), F32), kr,
                              jnp.zeros((LANES - MLA_NOPE - MLA_ROPE, cols), F32)], axis=0)
    misc_ref[...] = misc_t.T.astype(BF16)

    wq_ref[...] = _regroup_head_columns(wuq_ref[0], MLA_NOPE, MLA_ROPE)
    wkv_ref[...] = _regroup_head_columns(wukv_ref[0], MLA_NOPE, MLA_V)


def _prep(c, w_ada_all, b_ada_all, w_in_all, w_uq_all, w_ukv_all, layer):
    bsz, d = c.shape
    depth, _, n_mod = w_ada_all.shape
    n_in = w_in_all.shape[2]
    cols = 2 * LANES
    steps = d // cols
    tn = n_mod // steps
    assert d % cols == 0 and n_mod % steps == 0 and tn % LANES == 0
    widths = (3 * FOX_WIDTH, LANES, MLA_Q_LORA, MLA_KV_LORA, 2 * D_MODEL)
    up_proj = (w_uq_all, w_ukv_all)
    up_rows = [w.shape[1] // steps for w in up_proj]
    assert all(w.shape[1] % steps == 0 and r % BF16_ROWS == 0 for w, r in zip(up_proj, up_rows))
    return pl.pallas_call(
        _prep_kernel,
        grid=(steps,),
        in_specs=[pl.BlockSpec((bsz, d), lambda j: (0, 0)),
                  pl.BlockSpec((1, d, tn), lambda j: (layer, 0, j)),
                  pl.BlockSpec((1, 1, tn), lambda j: (layer, 0, j)),
                  pl.BlockSpec((1, n_in, cols), lambda j: (layer, 0, j))]
                 + [pl.BlockSpec((1, r, w.shape[2]), lambda j: (layer, j, 0)) for w, r in zip(up_proj, up_rows)],
        out_specs=[pl.BlockSpec((bsz, tn), lambda j: (0, j))]
                  + [pl.BlockSpec((cols, w), lambda j: (j, 0)) for w in widths]
                  + [pl.BlockSpec((r, w.shape[2]), lambda j: (j, 0)) for w, r in zip(up_proj, up_rows)],
        out_shape=[jax.ShapeDtypeStruct((bsz, n_mod), F32)]
                  + [jax.ShapeDtypeStruct((d, w), BF16) for w in widths]
                  + [jax.ShapeDtypeStruct(w.shape[1:], BF16) for w in up_proj],
        name="prep",
    )(c, w_ada_all, b_ada_all.reshape(depth, 1, n_mod), jnp.transpose(w_in_all, (0, 2, 1)), *up_proj)


def _decay_selector():
    pairs = HEADS // 2
    sel = np.zeros((LANES, 2 * pairs * LANES), np.float32)
    for hh in range(HEADS):
        base_q = (hh // 2) * LANES + (HEAD_DIM if hh % 2 == 0 else 0)
        base_k = (pairs + hh // 2) * LANES + (HEAD_DIM if hh % 2 == 0 else 0)
        for piece in range(3):
            sel[piece * HEADS + hh, base_q + piece] = 1.0
            sel[3 * HEADS, base_q + 3 + piece] = 1.0
            sel[3 * HEADS, base_k + piece] = 1.0
            sel[piece * HEADS + hh, base_k + 3 + piece] = -1.0
    return jnp.asarray(sel, BF16)


def _cast_block_rows(n_rows, n_steps):
    rb = -(-n_rows // n_steps)
    rb += -rb % BF16_ROWS
    while n_rows % rb:
        rb += BF16_ROWS
    return rb


def _token_proj(x, mod, g_pre, positions, weights, b_forget, g_q, g_kv, late_weights, layer):
    bsz, seq, d = x.shape
    tm = TOKEN_TILE
    steps_per_batch = seq // tm
    n_steps = bsz * steps_per_batch
    late_in, late_out, late_shapes = [], [], []
    for w in late_weights:
        _, rows, cols = w.shape
        rb = _cast_block_rows(rows, n_steps)
        last = rows // rb - 1
        late_in.append(pl.BlockSpec(
            (1, rb, cols), lambda b, i, last=last: (layer, jnp.minimum(b * steps_per_batch + i, last), 0)))
        late_out.append(pl.BlockSpec(
            (rb, cols), lambda b, i, last=last: (jnp.minimum(b * steps_per_batch + i, last), 0)))
        late_shapes.append(jax.ShapeDtypeStruct((rows, cols), BF16))
    sel = _decay_selector()
    half = MLA_ROPE // 2
    inv_freq = 1.0 / (ROPE_THETA ** (np.arange(0, MLA_ROPE, 2, dtype=np.float32) / MLA_ROPE))
    invf = jnp.asarray(np.tile(inv_freq.astype(np.float32)[:, None], (1, LANES)))
    row = lambda b, i: (b, i, 0)
    head = lambda b, i: (b, 0, i, 0)
    head_shape = jax.ShapeDtypeStruct((bsz, HEADS, seq, LANES), BF16)
    head_spec = pl.BlockSpec((1, HEADS, tm, LANES), head)
    v_shape = jax.ShapeDtypeStruct((bsz, HEADS, LANES, seq), BF16)
    v_spec = pl.BlockSpec((1, HEADS, LANES, tm), lambda b, i: (b, 0, 0, i))
    g_shape = jax.ShapeDtypeStruct((bsz, seq, d), BF16)
    g_spec = pl.BlockSpec((1, tm, d), row)
    return pl.pallas_call(
        functools.partial(_token_proj_kernel, layer=layer),
        grid=(bsz, seq // tm),
        in_specs=[pl.BlockSpec((1, tm, d), row),
                  _const_spec(mod.shape),
                  _const_spec((1, d)), pl.BlockSpec(memory_space=pltpu.SMEM),
                  _const_spec((1, MLA_Q_LORA)), _const_spec((1, MLA_KV_LORA)),
                  pl.BlockSpec((bsz, tm), lambda b, i: (0, i)), _const_spec((half, LANES))]
                 + [_const_spec(w.shape) for w in weights] + [_const_spec(sel.shape)] + late_in,
        out_specs=[head_spec, head_spec, v_spec, head_spec, head_spec, v_spec, g_spec, g_spec] + late_out,
        out_shape=[head_shape, head_shape, v_shape, head_shape, head_shape, v_shape, g_shape, g_shape]
                  + late_shapes,
        scratch_shapes=[pltpu.VMEM((SUBLANES, LANES), F32),
                        pltpu.VMEM((tm, LANES), F32),
                        pltpu.VMEM((tm, LANES), F32)],
        compiler_params=pltpu.CompilerParams(dimension_semantics=("arbitrary", "arbitrary"),
                                             vmem_limit_bytes=VMEM_LIMIT),
        name="token_proj",
    )(x, mod, g_pre.reshape(1, d), b_forget, g_q.reshape(1, -1), g_kv.reshape(1, -1), positions, invf,
      *weights, sel, *late_weights)


def _attn_kernel(zero_ref, q_ref, k_ref, vt_ref, o_ref, sa_ref, sb_ref, smax_ref, m_ref, acc_ref, *, tile):
    seq = vt_ref.shape[3]
    nq = seq // tile
    n_heads = q_ref.shape[1]
    z = zero_ref[0]
    half = tile // 2
    nt = (((1,), (1,)), ((), ()))
    vrow = lax.broadcasted_iota(jnp.int32, (LANES, tile), 0)
    sum_row = (HEAD_DIM, 0)
    keep_even = vrow < HEAD_DIM
    krow = lax.broadcasted_iota(jnp.int32, (half, half), 0)
    qcol = lax.broadcasted_iota(jnp.int32, (half, half), 1)
    causal_half = krow <= qcol

    def issue_scores(qi, j, s_out, slot, heads=None):
        for hh in (range(n_heads) if heads is None else heads):
            q = q_ref[0, hh, qi * tile:(qi + 1) * tile, :]
            k = k_ref[0, hh, j * tile:(j + 1) * tile, :]
            if j == qi:
                top = lax.dot_general(k[:half], q, nt, preferred_element_type=F32)
                top_left = jnp.where(causal_half, top[:, :half], -jnp.inf)
                bottom = jnp.where(causal_half,
                                   lax.dot_general(k[half:], q[half:], nt, preferred_element_type=F32), -jnp.inf)
                s_out[z + hh, :half, :half] = top_left
                s_out[z + hh, :half, half:] = top[:, half:]
                s_out[z + hh, half:, half:] = bottom
                smax_ref[slot, hh, :, :half] = jnp.max(top_left, axis=0, keepdims=True)
                smax_ref[slot, hh, :, half:] = jnp.maximum(jnp.max(top[:, half:], axis=0, keepdims=True),
                                                           jnp.max(bottom, axis=0, keepdims=True))
            else:
                s = lax.dot_general(k, q, nt, preferred_element_type=F32)
                s_out[z + hh] = s
                smax_ref[slot, hh] = jnp.max(s, axis=0, keepdims=True)

    def softmax_pv(s_in, slot, hh, vth, masked, first):
        alphas, p_cols = [], []
        for c0 in range(0, tile, LANES):
            n_keys = c0 + LANES if masked else tile
            n_fill = tile if (c0 >= half or not masked) else half
            m_new = smax_ref[slot, hh, :, c0:c0 + LANES]
            if not first:
                m_old = m_ref[hh, :, c0:c0 + LANES]
                m_new = jnp.maximum(m_old, m_new)
                alphas.append(jnp.exp2(m_old - m_new))
            parts = [jnp.exp2(s_in[z + hh, r0:r0 + LANES, c0:c0 + LANES] - m_new).astype(BF16)
                     for r0 in range(0, n_keys, LANES)]
            parts += [jnp.zeros((LANES, LANES), BF16)] * ((n_fill - n_keys) // LANES)
            p_cols.append(jnp.concatenate(parts, axis=0))
            if not masked:
                m_ref[hh, :, c0:c0 + LANES] = m_new
        n_left = half // LANES
        if masked:
            pv = jnp.concatenate([_dot(vth[:, :half], jnp.concatenate(p_cols[:n_left], axis=1)),
                                  _dot(vth, jnp.concatenate(p_cols[n_left:], axis=1))], axis=1)
        else:
            pv = _dot(vth, jnp.concatenate(p_cols, axis=1))
        acc = pv if first else jnp.concatenate(alphas, axis=1) * acc_ref[z + hh] + pv
        if masked:
            return acc
        acc_ref[z + hh] = acc
        return None

    blocks = [(qi, j) for qi in range(nq) for j in range(qi + 1)]
    bufs = (sa_ref, sb_ref)
    for pair in range(n_heads // 2):
        heads = (2 * pair, 2 * pair + 1)
        issue_scores(0, 0, bufs[0], 0, heads)
        for t, (qi, j) in enumerate(blocks):
            s_in, s_out = bufs[t % 2], bufs[(t + 1) % 2]
            accs = []
            for hh in heads:
                if t + 1 < len(blocks):
                    issue_scores(*blocks[t + 1], s_out, (t + 1) % 2, heads=(hh,))
                accs.append(softmax_pv(s_in, t % 2, hh, vt_ref[0, hh, :, j * tile:(j + 1) * tile], j == qi, j == 0))
            if j == qi:
                outs = [acc * (1.0 / acc[sum_row[hh % 2]:sum_row[hh % 2] + 1, :]) for hh, acc in zip(heads, accs)]
                o_t = jnp.where(keep_even, outs[0], outs[1])
                o_ref[0, qi * tile:(qi + 1) * tile, pair * LANES:(pair + 1) * LANES] = o_t.T.astype(o_ref.dtype)


def _causal_attn(q, k, vt):
    bsz, heads, seq, _ = q.shape
    g = ATTN_PAIRS
    qk_spec = pl.BlockSpec((1, 2 * g, seq, LANES), lambda b, p: (b, p, 0, 0))
    vt_spec = pl.BlockSpec((1, 2 * g, LANES, seq), lambda b, p: (b, p, 0, 0))
    o_spec = pl.BlockSpec((1, seq, g * LANES), lambda b, p: (b, 0, p))
    kern = functools.partial(_attn_kernel, tile=ATTN_TILE)
    return pl.pallas_call(
        kern,
        grid=(bsz, heads // (2 * g)),
        in_specs=[pl.BlockSpec(memory_space=pltpu.SMEM), qk_spec, qk_spec, vt_spec],
        out_specs=o_spec,
        out_shape=jax.ShapeDtypeStruct((bsz, seq, heads * HEAD_DIM), BF16),
        scratch_shapes=[pltpu.VMEM((2 * g, ATTN_TILE, ATTN_TILE), F32),
                        pltpu.VMEM((2 * g, ATTN_TILE, ATTN_TILE), F32),
                        pltpu.VMEM((2, 2 * g, 1, ATTN_TILE), F32),
                        pltpu.VMEM((2 * g, 1, ATTN_TILE), F32),
                        pltpu.VMEM((2 * g, LANES, ATTN_TILE), F32)],
        compiler_params=pltpu.CompilerParams(dimension_semantics=("arbitrary", "arbitrary"),
                                             vmem_limit_bytes=VMEM_LIMIT),
        name="causal_attn",
    )(jnp.zeros((1,), jnp.int32), q, k, vt)


def _ordering_zero(value, zero):
    bits = lax.bitcast_convert_type(value, jnp.uint32)
    cols = functools.reduce(jnp.bitwise_or, [bits[:, c:c + LANES] for c in range(0, bits.shape[1], LANES)])
    tile = functools.reduce(jnp.bitwise_or, [cols[r:r + SUBLANES] for r in range(0, cols.shape[0], SUBLANES)])
    return lax.bitcast_convert_type(tile & zero.astype(jnp.uint32), F32)


def _mix_ffn_kernel(zero_ref, x_ref, oa_ref, ob_ref, gf_ref, gm_ref, mod_ref, gpost_mix_ref, gpre_ffn_ref,
                    gpost_ffn_ref, wpf_ref, wpm_ref, wout_ref, win_ref, wdown_ref, o_ref):
    tm = x_ref.shape[1]
    mod = lambda k: mod_ref[pl.ds(pl.program_id(0), 1), k * D_MODEL:(k + 1) * D_MODEL]
    shift = mod(3)
    mix_gain = mod(2) * gpost_mix_ref[...]
    ffn_in_gain = gpre_ffn_ref[...] * (1.0 + mod(4))
    ffn_gain = mod(5) * gpost_ffn_ref[...]

    def merge_matmuls(rows):
        pa = _dot(oa_ref[0, rows, :], wpf_ref[...])
        pb = _dot(ob_ref[0, rows, :], wpm_ref[...])
        merged = gf_ref[0, rows, :].astype(F32) * pa + gm_ref[0, rows, :].astype(F32) * pb
        return merged.astype(BF16)

    def mix_residual(rows, y):
        x = x_ref[0, rows, :] + _rms(y) * mix_gain
        h = (_rms(x) * ffn_in_gain + shift).astype(BF16)
        return x, h

    chunks = [(lo, min(FFN_CHUNK, D_FF - lo)) for lo in range(0, D_FF, FFN_CHUNK)]

    def up_matmuls(h, lo, w):
        return _dot(h, win_ref[:, lo:lo + w]), _dot(h, win_ref[:, D_FF + lo:D_FF + lo + w])

    def swiglu_matmuls(h, after=None):
        y = None
        gu = up_matmuls(h, *chunks[0])
        for c, (lo, w) in enumerate(chunks):
            g, u = gu
            if c + 1 < len(chunks):
                gu = up_matmuls(h, *chunks[c + 1])
            elif after is not None:
                top = jnp.concatenate([g[:SUBLANES, :LANES] + after, g[:SUBLANES, LANES:]], axis=1)
                g = jnp.concatenate([top, g[SUBLANES:]], axis=0)
            act = (g * jax.nn.sigmoid(g) * u).astype(BF16)
            part = _dot(act, wdown_ref[lo:lo + w, :])
            y = part if y is None else y + part
        return y

    def store_output(rows, x, y):
        out = x + _rms(y) * ffn_gain
        o_ref[0, rows, :] = out
        return out

    parts = [pl.ds(r0, ROW_PART) for r0 in range(0, tm, ROW_PART)]
    merged = [merge_matmuls(rows) for rows in parts]
    y_mix = [_dot(m, wout_ref[...]) for m in merged]
    pending = None
    for t, (rows, y) in enumerate(zip(parts, y_mix)):
        x, h = mix_residual(rows, y)
        out = store_output(*pending) if pending is not None else None
        pin = t + 1 == len(parts) and out is not None
        y_ffn = swiglu_matmuls(h, _ordering_zero(out, zero_ref[0]) if pin else None)
        pending = (rows, x, y_ffn)
    store_output(*pending)


def _mix_ffn(x, o_a, o_b, gf, gm, mod, g_post_mix, g_pre_ffn, g_post_ffn, w_pf, w_pm, w_out, w_in, w_down):
    bsz, seq, d = x.shape
    tm = MIX_TILE
    row = lambda b, i: (b, i, 0)
    weights = (w_pf, w_pm, w_out, w_in, w_down)
    return pl.pallas_call(
        _mix_ffn_kernel,
        grid=(bsz, seq // tm),
        in_specs=[pl.BlockSpec(memory_space=pltpu.SMEM),
                  pl.BlockSpec((1, tm, d), row),
                  pl.BlockSpec((1, tm, FOX_WIDTH), row), pl.BlockSpec((1, tm, FOX_WIDTH), row),
                  pl.BlockSpec((1, tm, d), row), pl.BlockSpec((1, tm, d), row),
                  _const_spec(mod.shape),
                  _const_spec((1, d)), _const_spec((1, d)), _const_spec((1, d))]
                 + [_const_spec(w.shape) for w in weights],
        out_specs=pl.BlockSpec((1, tm, d), row),
        out_shape=jax.ShapeDtypeStruct(x.shape, F32),
        compiler_params=pltpu.CompilerParams(dimension_semantics=("arbitrary", "arbitrary"),
                                             vmem_limit_bytes=VMEM_LIMIT),
        name="mix_ffn",
    )(jnp.zeros((1,), jnp.int32), x, o_a, o_b, gf, gm, mod, g_post_mix.reshape(1, d), g_pre_ffn.reshape(1, d), g_post_ffn.reshape(1, d),
      *weights)


def kernel(x, c, positions, w_ada, b_ada, g_pre_mix, g_post_mix, g_pre_ffn, g_post_ffn, w_in, b_forget,
           g_q_lora, w_uq, g_kv_lora, w_ukv, w_proj_fox, w_proj_mla, w_out, w_ffn_in, w_ffn_out):
    bsz, seq, d = x.shape
    depth = w_ada.shape[0]
    for l in range(depth):
        mod, *weights = _prep(c, w_ada, b_ada, w_in, w_uq, w_ukv, l)
        late = (w_proj_fox, w_proj_mla, w_out, w_ffn_in, w_ffn_out)
        qf, kf, vf, qm, km, vm, gf, gm, *late_bf16 = _token_proj(
            x, mod, g_pre_mix[l], positions, weights, b_forget, g_q_lora[l], g_kv_lora[l], late, l)
        o_a = _causal_attn(qf, kf, vf)
        o_b = _causal_attn(qm, km, vm)
        x = _mix_ffn(x, o_a, o_b, gf, gm, mod, g_post_mix[l], g_pre_ffn[l], g_post_ffn[l], *late_bf16)
    return x
```
